```python
import math
import jax, jax.numpy as jnp
from jax import lax
import numpy as np

D_MODEL = 2048
BATCH = 1
SEQ = 8192
DEPTH = 2

HEAD_DIM = 128
MOBA_HEADS = 8
MOBA_WIDTH = MOBA_HEADS * HEAD_DIM
MOBA_BLOCK = 256
MOBA_TOPK = 3
MOBA_Q_CHUNK = 64
SSM_D_INNER = D_MODEL
SSM_HEAD_DIM = 64
SSM_HEADS = SSM_D_INNER // SSM_HEAD_DIM
SSM_STATE = 128
SSM_GROUPS = 8
SSM_CONV = 4
SSM_CHUNK = 256
SSM_CONV_CH = SSM_D_INNER + 2 * SSM_GROUPS * SSM_STATE
NSA_HEADS = 8
NSA_KV_GROUPS = 2
NSA_WIDTH = NSA_HEADS * HEAD_DIM
NSA_KV_WIDTH = NSA_KV_GROUPS * HEAD_DIM
NSA_CMP_LEN = 32
NSA_CMP_STRIDE = 16
NSA_SEL_BLOCK = 64
NSA_TOPN = 16
NSA_WINDOW = 512
NSA_WIN_QBLOCK = 128
NSA_Q_CHUNK = 64
N_BRANCHES = 3
N_EXPERTS = 64
N_EXPERT_GROUPS = 8
TOPK_GROUPS = 4
TOP_K = 8
D_EXPERT = 512
D_SHARED = 512
ROUTED_SCALE = 2.5
MOE_ROW_BLOCK = 128
DEEPNORM_ALPHA = (2 * DEPTH) ** 0.25
DEEPNORM_BETA = (8 * DEPTH) ** -0.25
LN_EPS = 1e-5
RMS_EPS = 1e-5
IN_SPLIT_SIZES = (MOBA_WIDTH, MOBA_WIDTH, MOBA_WIDTH,
                  SSM_D_INNER, SSM_CONV_CH, SSM_HEADS,
                  NSA_WIDTH, 6 * NSA_KV_WIDTH, N_BRANCHES * NSA_HEADS,
                  N_BRANCHES * D_MODEL)
IN_COLS = sum(IN_SPLIT_SIZES)

kernel_name = 'hybrid_moba_ssd_nsa_moe_deepnorm'

F32 = jnp.float32


def _layer_norm(x, g, b):
    xf = x.astype(F32)
    xc = xf - jnp.mean(xf, axis=-1, keepdims=True)
    var = jnp.mean(xc * xc, axis=-1, keepdims=True)
    return (xc * lax.rsqrt(var + LN_EPS) * g.astype(F32) + b.astype(F32)).astype(x.dtype)


def _masked_softmax(s, mask):
    s = jnp.where(mask, s, -jnp.inf)
    m = jnp.max(s, axis=-1, keepdims=True)
    m = jnp.where(jnp.isfinite(m), m, 0.0)
    e = jnp.exp(s - m)
    den = jnp.sum(e, axis=-1, keepdims=True)
    return e / jnp.where(den > 0, den, 1.0)


def _swiglu(x, w_gate, w_up, w_down):
    return (jax.nn.silu(x @ w_gate) * (x @ w_up)) @ w_down


def moba_attention(q, k, v):
    B, H, S, hd = q.shape
    s_pad = -(-S // MOBA_BLOCK) * MOBA_BLOCK
    pad = ((0, 0), (0, 0), (0, s_pad - S), (0, 0))
    q = jnp.pad(q, pad)
    k = jnp.pad(k, pad)
    v = jnp.pad(v, pad)
    nb = s_pad // MOBA_BLOCK
    topk = min(MOBA_TOPK, nb)
    scale = hd ** -0.5
    kb = k.reshape(B, H, nb, MOBA_BLOCK, hd)
    vb = v.reshape(B, H, nb, MOBA_BLOCK, hd)
    k_mean = jnp.mean(kb.astype(F32), axis=3)
    gate = jnp.einsum('bhsd,bhnd->bhsn', q.astype(F32), k_mean)
    q_blk = jnp.arange(s_pad) // MOBA_BLOCK
    past = jnp.arange(nb)[None, :] < q_blk[:, None]
    gate = jnp.where(past, gate, -jnp.inf)
    _, sel = lax.top_k(gate, topk)
    sel_ok = sel < q_blk[:, None]
    b_idx = jnp.arange(B)[:, None, None, None]
    h_idx = jnp.arange(H)[None, :, None, None]
    q_off = jnp.arange(MOBA_Q_CHUNK)
    k_off = jnp.arange(MOBA_BLOCK)

    def chunk(c):
        start = c * MOBA_Q_CHUNK
        qc = lax.dynamic_slice_in_dim(q, start, MOBA_Q_CHUNK, axis=2)
        sc = lax.dynamic_slice_in_dim(sel, start, MOBA_Q_CHUNK, axis=2)
        okc = lax.dynamic_slice_in_dim(sel_ok, start, MOBA_Q_CHUNK, axis=2)
        own = start // MOBA_BLOCK
        k_own = lax.dynamic_index_in_dim(kb, own, axis=2, keepdims=False)
        v_own = lax.dynamic_index_in_dim(vb, own, axis=2, keepdims=False)
        k_sel = kb[b_idx, h_idx, sc]
        v_sel = vb[b_idx, h_idx, sc]
        s_sel = jnp.einsum('bhqd,bhqnkd->bhqnk', qc, k_sel, preferred_element_type=F32) * scale
        s_sel = jnp.where(okc[..., None], s_sel, -jnp.inf)
        s_own = jnp.einsum('bhqd,bhkd->bhqk', qc, k_own, preferred_element_type=F32) * scale
        causal = (own * MOBA_BLOCK + k_off)[None, :] <= (start + q_off)[:, None]
        s_own = jnp.where(causal, s_own, -jnp.inf)
        logits = jnp.concatenate([s_sel.reshape(B, H, MOBA_Q_CHUNK, topk * MOBA_BLOCK), s_own], axis=-1)
        p = jax.nn.softmax(logits, axis=-1).astype(v.dtype)
        p_sel = p[..., :topk * MOBA_BLOCK].reshape(B, H, MOBA_Q_CHUNK, topk, MOBA_BLOCK)
        p_own = p[..., topk * MOBA_BLOCK:]
        return (jnp.einsum('bhqnk,bhqnkd->bhqd', p_sel, v_sel)
                + jnp.einsum('bhqk,bhkd->bhqd', p_own, v_own))

    out = lax.map(chunk, jnp.arange(s_pad // MOBA_Q_CHUNK))
    out = jnp.moveaxis(out, 0, 2).reshape(B, H, s_pad, hd)
    return out[:, :, :S]


def _causal_depthwise_conv(x, w, b):
    y = lax.conv_general_dilated(x, w[:, None, :].astype(x.dtype), window_strides=(1,),
                                 padding=((SSM_CONV - 1, 0),),
                                 dimension_numbers=('NWC', 'WIO', 'NWC'),
                                 feature_group_count=x.shape[-1])
    return y + b.astype(x.dtype)


def _ssd_chunked(x, a, bm, cm):
    Bsz, S, H, P = x.shape
    G, N = bm.shape[2], bm.shape[3]
    J = H // G
    L = SSM_CHUNK
    s_pad = -(-S // L) * L
    nc = s_pad // L
    x = jnp.pad(x, ((0, 0), (0, s_pad - S), (0, 0), (0, 0))).reshape(Bsz, nc, L, G, J, P)
    bm = jnp.pad(bm, ((0, 0), (0, s_pad - S), (0, 0), (0, 0))).reshape(Bsz, nc, L, G, N)
    cm = jnp.pad(cm, ((0, 0), (0, s_pad - S), (0, 0), (0, 0))).reshape(Bsz, nc, L, G, N)
    a = jnp.pad(a, ((0, 0), (0, s_pad - S), (0, 0))).reshape(Bsz, nc, L, G, J).transpose(0, 3, 4, 1, 2)
    a_cs = jnp.cumsum(a, axis=-1)
    tril = jnp.tril(jnp.ones((L, L), dtype=bool))
    seg = a_cs[..., :, None] - a_cs[..., None, :]
    decay_in = jnp.exp(jnp.where(tril, seg, -jnp.inf))
    cb = jnp.einsum('bclgn,bcsgn->bgcls', cm, bm)
    y_diag = jnp.einsum('bgjcls,bcsgjp->bclgjp', cb[:, :, None] * decay_in, x)
    decay_to_end = jnp.exp(a_cs[..., -1:] - a_cs).transpose(0, 3, 4, 1, 2)
    chunk_states = jnp.einsum('bclgn,bclgjp->bcgjpn', bm, x * decay_to_end[..., None])
    chunk_decay = jnp.exp(a_cs[..., -1]).transpose(3, 0, 1, 2)

    def step(state, inp):
        dec, new = inp
        return state * dec[..., None, None] + new, state

    init = jnp.zeros((Bsz, G, J, P, N), x.dtype)
    _, states_in = lax.scan(step, init, (chunk_decay, jnp.moveaxis(chunk_states, 1, 0)))
    decay_from_start = jnp.exp(a_cs).transpose(0, 3, 4, 1, 2)
    y_off = jnp.einsum('bclgn,cbgjpn->bclgjp', cm, states_in) * decay_from_start[..., None]
    y = (y_diag + y_off).reshape(Bsz, s_pad, H, P)
    return y[:, :S]


def mamba2_mixer(z, xbc, dt_raw, conv_w, conv_b, dt_bias, a_log, d_skip, norm_g):
    Bsz, S, _ = z.shape
    xbc = jax.nn.silu(_causal_depthwise_conv(xbc, conv_w, conv_b))
    xs, bm, cm = jnp.split(xbc, [SSM_D_INNER, SSM_D_INNER + SSM_GROUPS * SSM_STATE], axis=-1)
    xs = xs.reshape(Bsz, S, SSM_HEADS, SSM_HEAD_DIM).astype(F32)
    bm = bm.reshape(Bsz, S, SSM_GROUPS, SSM_STATE).astype(F32)
    cm = cm.reshape(Bsz, S, SSM_GROUPS, SSM_STATE).astype(F32)
    dt = jax.nn.softplus(dt_raw.astype(F32) + dt_bias.astype(F32))
    a = -jnp.exp(a_log.astype(F32))
    y = _ssd_chunked(xs * dt[..., None], dt * a, bm, cm) + xs * d_skip.astype(F32)[:, None]
    y = y.reshape(Bsz, S, SSM_D_INNER) * jax.nn.silu(z.astype(F32))
    yg = y.reshape(Bsz, S, SSM_GROUPS, SSM_D_INNER // SSM_GROUPS)
    yg = yg * lax.rsqrt(jnp.mean(yg * yg, axis=-1, keepdims=True) + RMS_EPS)
    return (yg.reshape(Bsz, S, SSM_D_INNER) * norm_g.astype(F32)).astype(z.dtype)


def nsa_attention(q, k_cmp, v_cmp, k_sel, v_sel, k_win, v_win, gates,
                  cmp_pos_k, cmp_w1_k, cmp_w2_k, cmp_pos_v, cmp_w1_v, cmp_w2_v):
    Bsz, S, H, hd = q.shape
    G = NSA_KV_GROUPS
    J = H // G
    scale = hd ** -0.5
    qg = q.reshape(Bsz, S, G, J, hd).transpose(0, 2, 3, 1, 4)
    pos = jnp.arange(S)

    def to_g(t):
        return t.transpose(0, 2, 1, 3)

    n_str = S // NSA_CMP_STRIDE
    ratio = NSA_CMP_LEN // NSA_CMP_STRIDE
    n_cmp = n_str - ratio + 1

    def compress(t, pos_emb, w1, w2):
        tr = to_g(t).reshape(Bsz, G, n_str, NSA_CMP_STRIDE, hd)
        blocks = jnp.concatenate([tr[:, :, i:i + n_cmp] for i in range(ratio)], axis=3) + pos_emb
        flat = blocks.reshape(Bsz, G, n_cmp, NSA_CMP_LEN * hd)
        return jax.nn.gelu(flat @ w1) @ w2

    kc = compress(k_cmp, cmp_pos_k, cmp_w1_k, cmp_w2_k)
    vc = compress(v_cmp, cmp_pos_v, cmp_w1_v, cmp_w2_v)
    cmp_end = jnp.arange(n_cmp) * NSA_CMP_STRIDE + NSA_CMP_LEN - 1
    s_c = jnp.einsum('bgjsd,bgnd->bgjsn', qg, kc, preferred_element_type=F32) * scale
    p_c = _masked_softmax(s_c, cmp_end[None, :] <= pos[:, None])
    o_cmp = jnp.einsum('bgjsn,bgnd->bgjsd', p_c.astype(vc.dtype), vc)

    n_sel = S // NSA_SEL_BLOCK
    c_start = jnp.arange(n_cmp) * NSA_CMP_STRIDE
    s_start = jnp.arange(n_sel) * NSA_SEL_BLOCK
    overlap = ((c_start[:, None] < s_start[None, :] + NSA_SEL_BLOCK)
               & (c_start[:, None] + NSA_CMP_LEN > s_start[None, :])).astype(F32)
    imp = jnp.einsum('bgjsn,nm->bgsm', p_c, overlap)
    cur = pos // NSA_SEL_BLOCK
    blk = jnp.arange(n_sel)
    allowed = blk[None, :] <= cur[:, None]
    forced = (blk[None, :] == 0) | (blk[None, :] == cur[:, None]) | (blk[None, :] == cur[:, None] - 1)
    imp = jnp.where(forced, jnp.inf, jnp.where(allowed, imp, -jnp.inf))
    topn = min(NSA_TOPN, n_sel)
    _, sel = lax.top_k(imp, topn)
    sel_ok = sel <= cur[:, None]
    ksb = to_g(k_sel).reshape(Bsz, G, n_sel, NSA_SEL_BLOCK, hd)
    vsb = to_g(v_sel).reshape(Bsz, G, n_sel, NSA_SEL_BLOCK, hd)
    b_idx = jnp.arange(Bsz)[:, None, None, None]
    g_idx = jnp.arange(G)[None, :, None, None]
    in_blk = jnp.arange(NSA_SEL_BLOCK)
    q_off = jnp.arange(NSA_Q_CHUNK)

    def sel_chunk(c):
        start = c * NSA_Q_CHUNK
        qc = lax.dynamic_slice_in_dim(qg, start, NSA_Q_CHUNK, axis=3)
        sc = lax.dynamic_slice_in_dim(sel, start, NSA_Q_CHUNK, axis=2)
        okc = lax.dynamic_slice_in_dim(sel_ok, start, NSA_Q_CHUNK, axis=2)
        kg = ksb[b_idx, g_idx, sc]
        vg = vsb[b_idx, g_idx, sc]
        s = jnp.einsum('bgjqd,bgqnkd->bgjqnk', qc, kg, preferred_element_type=F32) * scale
        kpos = sc[..., None] * NSA_SEL_BLOCK + in_blk
        mask = okc[..., None] & (kpos <= (start + q_off)[:, None, None])
        s = jnp.where(mask[:, :, None], s, -jnp.inf)
        p = jax.nn.softmax(s.reshape(Bsz, G, J, NSA_Q_CHUNK, topn * NSA_SEL_BLOCK), axis=-1)
        p = p.reshape(s.shape).astype(vg.dtype)
        return jnp.einsum('bgjqnk,bgqnkd->bgjqd', p, vg)

    o_sel = lax.map(sel_chunk, jnp.arange(S // NSA_Q_CHUNK))
    o_sel = jnp.moveaxis(o_sel, 0, 3).reshape(Bsz, G, J, S, hd)

    qb = NSA_WIN_QBLOCK
    n_wb = S // qb
    halo = NSA_WINDOW // qb

    def band(t):
        tp = jnp.pad(to_g(t), ((0, 0), (0, 0), (NSA_WINDOW, 0), (0, 0))).reshape(Bsz, G, n_wb + halo, qb, hd)
        return jnp.concatenate([tp[:, :, i:i + n_wb] for i in range(halo + 1)], axis=3)

    kw = band(k_win)
    vw = band(v_win)
    qw = qg.reshape(Bsz, G, J, n_wb, qb, hd)
    s_w = jnp.einsum('bgjwqd,bgwkd->bgjwqk', qw, kw, preferred_element_type=F32) * scale
    qpos = jnp.arange(n_wb)[:, None] * qb + jnp.arange(qb)[None, :]
    kpos = jnp.arange(n_wb)[:, None] * qb - NSA_WINDOW + jnp.arange((halo + 1) * qb)[None, :]
    diff = qpos[:, :, None] - kpos[:, None, :]
    wmask = (diff >= 0) & (diff < NSA_WINDOW) & (kpos[:, None, :] >= 0)
    p_w = jax.nn.softmax(jnp.where(wmask, s_w, -jnp.inf), axis=-1).astype(vw.dtype)
    o_win = jnp.einsum('bgjwqk,bgwkd->bgjwqd', p_w, vw).reshape(Bsz, G, J, S, hd)

    gt = gates.reshape(Bsz, S, G, J, N_BRANCHES).transpose(0, 2, 3, 1, 4)
    o = gt[..., 0:1] * o_cmp + gt[..., 1:2] * o_sel + gt[..., 2:3] * o_win
    return o.transpose(0, 3, 1, 2, 4).reshape(Bsz, S, H * hd)


def hybrid_mixer(h, w_in, conv_w, conv_b, dt_bias, a_log, d_skip, ssm_norm_g,
                 cmp_pos_k, cmp_w1_k, cmp_w2_k, cmp_pos_v, cmp_w1_v, cmp_w2_v,
                 w_br_a, w_br_b, w_br_c, w_out):
    Bsz, S, D = h.shape
    proj = h @ w_in
    split_at = np.cumsum(IN_SPLIT_SIZES)[:-1].tolist()
    qa, ka, va, z, xbc, dt_raw, q_n, kv_n, g_n, g_m = jnp.split(proj, split_at, axis=-1)

    def moba_heads(t):
        return t.reshape(Bsz, S, MOBA_HEADS, HEAD_DIM).transpose(0, 2, 1, 3)

    y_a = moba_attention(moba_heads(qa), moba_heads(ka), moba_heads(va))
    y_a = y_a.transpose(0, 2, 1, 3).reshape(Bsz, S, MOBA_WIDTH)

    y_b = mamba2_mixer(z, xbc, dt_raw, conv_w, conv_b, dt_bias, a_log, d_skip, ssm_norm_g)

    kc, vc, ks, vs, kw, vw = [t.reshape(Bsz, S, NSA_KV_GROUPS, HEAD_DIM) for t in jnp.split(kv_n, 6, axis=-1)]
    nsa_gates = jax.nn.sigmoid(g_n.astype(F32)).reshape(Bsz, S, NSA_HEADS, N_BRANCHES).astype(h.dtype)
    y_c = nsa_attention(q_n.reshape(Bsz, S, NSA_HEADS, HEAD_DIM), kc, vc, ks, vs, kw, vw, nsa_gates,
                        cmp_pos_k, cmp_w1_k, cmp_w2_k, cmp_pos_v, cmp_w1_v, cmp_w2_v)

    gm = jax.nn.sigmoid(g_m.astype(F32)).reshape(Bsz, S, N_BRANCHES, D).astype(h.dtype)
    merged = (gm[:, :, 0] * (y_a @ w_br_a) + gm[:, :, 1] * (y_b @ w_br_b)
              + gm[:, :, 2] * (y_c @ w_br_c))
    return merged @ w_out


def moe_ffn(h, w_router, router_bias, w_exp_gate, w_exp_up, w_exp_down, w_sh_gate, w_sh_up, w_sh_down):
    Bsz, S, D = h.shape
    n_tok = Bsz * S
    x = h.reshape(n_tok, D)
    scores = jax.nn.sigmoid(jnp.einsum('nd,de->ne', x, w_router, preferred_element_type=F32))
    biased = scores + router_bias.astype(F32)
    per_group = N_EXPERTS // N_EXPERT_GROUPS
    grp_score = jnp.sum(lax.top_k(biased.reshape(n_tok, N_EXPERT_GROUPS, per_group), 2)[0], axis=-1)
    _, top_grp = lax.top_k(grp_score, TOPK_GROUPS)
    grp_keep = jnp.any(top_grp[..., None] == jnp.arange(N_EXPERT_GROUPS), axis=1)
    exp_keep = jnp.repeat(grp_keep, per_group, axis=1)
    _, top_e = lax.top_k(jnp.where(exp_keep, biased, -jnp.inf), TOP_K)
    w = jnp.take_along_axis(scores, top_e, axis=1)
    w = w / jnp.sum(w, axis=-1, keepdims=True) * ROUTED_SCALE

    n_asg = n_tok * TOP_K
    flat_e = top_e.reshape(-1)
    flat_tok = jnp.repeat(jnp.arange(n_tok, dtype=jnp.int32), TOP_K)
    flat_w = w.reshape(-1)
    order = jnp.argsort(flat_e)
    sorted_e = flat_e[order]
    counts = jnp.bincount(flat_e, length=N_EXPERTS)
    padded = (counts + MOE_ROW_BLOCK - 1) // MOE_ROW_BLOCK * MOE_ROW_BLOCK
    pad_end = jnp.cumsum(padded)
    dest = (pad_end - padded)[sorted_e] + jnp.arange(n_asg) - (jnp.cumsum(counts) - counts)[sorted_e]
    n_rows = (-(-n_asg // MOE_ROW_BLOCK) + N_EXPERTS) * MOE_ROW_BLOCK
    n_blk = n_rows // MOE_ROW_BLOCK
    row_tok = jnp.full((n_rows,), n_tok, jnp.int32).at[dest].set(flat_tok[order])
    row_w = jnp.zeros((n_rows,), F32).at[dest].set(flat_w[order])
    blk_e = jnp.minimum(jnp.searchsorted(pad_end, jnp.arange(n_blk) * MOE_ROW_BLOCK, side='right'),
                        N_EXPERTS - 1)
    x_pad = jnp.concatenate([x, jnp.zeros((1, D), x.dtype)], axis=0)

    def expert_block(inp):
        e, tok, wt = inp
        xb = x_pad[tok]
        yb = _swiglu(xb, w_exp_gate[e], w_exp_up[e], w_exp_down[e])
        return yb * wt[:, None].astype(x.dtype)

    y_rows = lax.map(expert_block, (blk_e, row_tok.reshape(n_blk, MOE_ROW_BLOCK),
                                    row_w.reshape(n_blk, MOE_ROW_BLOCK)))
    routed = jax.ops.segment_sum(y_rows.reshape(n_rows, D), row_tok, num_segments=n_tok + 1)[:n_tok]
    shared = _swiglu(x, w_sh_gate, w_sh_up, w_sh_down)
    return (routed + shared).reshape(Bsz, S, D)


def setup_inputs(seed: int = 0) -> dict:
    key = jax.random.key(seed)
    ks = jax.random.split(key, 30)
    L = DEPTH
    D = D_MODEL

    def nrm(k, shape, scale):
        return jax.random.normal(k, shape, F32) * scale

    dt0 = jnp.exp(jax.random.uniform(ks[4], (L, SSM_HEADS), F32, math.log(1e-3), math.log(1e-1)))
    return {
        'x': nrm(ks[0], (BATCH, SEQ, D), 1.0),
        'w_in': nrm(ks[1], (L, D, IN_COLS), D ** -0.5),
        'conv_w': nrm(ks[2], (L, SSM_CONV, SSM_CONV_CH), SSM_CONV ** -0.5),
        'conv_b': nrm(ks[3], (L, SSM_CONV_CH), 0.01),
        'dt_bias': dt0 + jnp.log(-jnp.expm1(-dt0)),
        'a_log': jnp.log(jax.random.uniform(ks[5], (L, SSM_HEADS), F32, 1.0, 16.0)),
        'd_skip': 1.0 + nrm(ks[6], (L, SSM_HEADS), 0.01),
        'ssm_norm_g': 1.0 + nrm(ks[7], (L, SSM_D_INNER), 0.01),
        'cmp_pos_k': nrm(ks[8], (L, NSA_CMP_LEN, HEAD_DIM), 0.02),
        'cmp_w1_k': nrm(ks[9], (L, NSA_CMP_LEN * HEAD_DIM, HEAD_DIM), (NSA_CMP_LEN * HEAD_DIM) ** -0.5),
        'cmp_w2_k': nrm(ks[10], (L, HEAD_DIM, HEAD_DIM), HEAD_DIM ** -0.5),
        'cmp_pos_v': nrm(ks[11], (L, NSA_CMP_LEN, HEAD_DIM), 0.02),
        'cmp_w1_v': nrm(ks[12], (L, NSA_CMP_LEN * HEAD_DIM, HEAD_DIM), (NSA_CMP_LEN * HEAD_DIM) ** -0.5),
        'cmp_w2_v': nrm(ks[13], (L, HEAD_DIM, HEAD_DIM), HEAD_DIM ** -0.5),
        'w_br_a': nrm(ks[14], (L, MOBA_WIDTH, D), MOBA_WIDTH ** -0.5),
        'w_br_b': nrm(ks[15], (L, SSM_D_INNER, D), SSM_D_INNER ** -0.5),
        'w_br_c': nrm(ks[16], (L, NSA_WIDTH, D), NSA_WIDTH ** -0.5),
        'w_out': nrm(ks[17], (L, D, D), D ** -0.5 * DEEPNORM_BETA),
        'ln1_g': 1.0 + nrm(ks[18], (L, D), 0.01),
        'ln1_b': nrm(ks[19], (L, D), 0.01),
        'w_router': nrm(ks[20], (L, D, N_EXPERTS), D ** -0.5),
        'router_bias': nrm(ks[21], (L, N_EXPERTS), 0.01),
        'w_exp_gate': nrm(ks[22], (L, N_EXPERTS, D, D_EXPERT), D ** -0.5),
        'w_exp_up': nrm(ks[23], (L, N_EXPERTS, D, D_EXPERT), D ** -0.5),
        'w_exp_down': nrm(ks[24], (L, N_EXPERTS, D_EXPERT, D), D_EXPERT ** -0.5 * DEEPNORM_BETA),
        'w_sh_gate': nrm(ks[25], (L, D, D_SHARED), D ** -0.5),
        'w_sh_up': nrm(ks[26], (L, D, D_SHARED), D ** -0.5),
        'w_sh_down': nrm(ks[27], (L, D_SHARED, D), D_SHARED ** -0.5 * DEEPNORM_BETA),
        'ln2_g': 1.0 + nrm(ks[28], (L, D), 0.01),
        'ln2_b': nrm(ks[29], (L, D), 0.01),
    }


def reference(x, w_in, conv_w, conv_b, dt_bias, a_log, d_skip, ssm_norm_g,
              cmp_pos_k, cmp_w1_k, cmp_w2_k, cmp_pos_v, cmp_w1_v, cmp_w2_v,
              w_br_a, w_br_b, w_br_c, w_out, ln1_g, ln1_b,
              w_router, router_bias, w_exp_gate, w_exp_up, w_exp_down,
              w_sh_gate, w_sh_up, w_sh_down, ln2_g, ln2_b):
    h = x
    for l in range(DEPTH):
        mix = hybrid_mixer(h, w_in[l], conv_w[l], conv_b[l], dt_bias[l], a_log[l], d_skip[l], ssm_norm_g[l],
                           cmp_pos_k[l], cmp_w1_k[l], cmp_w2_k[l], cmp_pos_v[l], cmp_w1_v[l], cmp_w2_v[l],
                           w_br_a[l], w_br_b[l], w_br_c[l], w_out[l])
        h = _layer_norm(DEEPNORM_ALPHA * h + mix, ln1_g[l], ln1_b[l])
        ffn = moe_ffn(h, w_router[l], router_bias[l], w_exp_gate[l], w_exp_up[l], w_exp_down[l],
                      w_sh_gate[l], w_sh_up[l], w_sh_down[l])
        h = _layer_norm(DEEPNORM_ALPHA * h + ffn, ln2_g[l], ln2_b[l])
    return h
```

```python
import functools

import numpy as np
import jax
import jax.numpy as jnp
from jax import lax
from jax.experimental import pallas as pl
from jax.experimental.pallas import tpu as pltpu

F32 = jnp.float32
MXU = jnp.bfloat16

D_MODEL = 2048
DEPTH = 2
HEAD_DIM = 128
MOBA_HEADS = 8
MOBA_WIDTH = MOBA_HEADS * HEAD_DIM
MOBA_BLOCK = 256
MOBA_TOPK = 3
SSM_D_INNER = D_MODEL
SSM_HEAD_DIM = 64
SSM_HEADS = SSM_D_INNER // SSM_HEAD_DIM
SSM_STATE = 128
SSM_GROUPS = 8
SSM_HEADS_PER_GROUP = SSM_HEADS // SSM_GROUPS
SSM_GROUP_WIDTH = SSM_D_INNER // SSM_GROUPS
SSM_CONV = 4
SSM_CHUNK = 256
SSM_CONV_CH = SSM_D_INNER + 2 * SSM_GROUPS * SSM_STATE
NSA_HEADS = 8
NSA_KV_GROUPS = 2
NSA_HEADS_PER_GROUP = NSA_HEADS // NSA_KV_GROUPS
NSA_WIDTH = NSA_HEADS * HEAD_DIM
NSA_KV_WIDTH = NSA_KV_GROUPS * HEAD_DIM
NSA_CMP_LEN = 32
NSA_CMP_STRIDE = 16
NSA_SEL_BLOCK = 64
NSA_TOPN = 16
NSA_WINDOW = 512
N_BRANCHES = 3
N_EXPERTS = 64
N_EXPERT_GROUPS = 8
EXPERTS_PER_GROUP = N_EXPERTS // N_EXPERT_GROUPS
TOPK_GROUPS = 4
TOP_K = 8
D_EXPERT = 512
ROUTED_SCALE = 2.5
DEEPNORM_ALPHA = (2 * DEPTH) ** 0.25
LN_EPS = 1e-5
RMS_EPS = 1e-5
IN_SPLIT_SIZES = (MOBA_WIDTH, MOBA_WIDTH, MOBA_WIDTH,
                  SSM_D_INNER, SSM_CONV_CH, SSM_HEADS,
                  NSA_WIDTH, 6 * NSA_KV_WIDTH, N_BRANCHES * NSA_HEADS,
                  N_BRANCHES * D_MODEL)

LANES = 128
SUBLANES = 8
ATT_TILE = 256
MOE_TILE = 256
ATT_SCALE = HEAD_DIM ** -0.5
MASK_BIAS = -2.0 ** 30
NEG = -1e30

NT = (((1,), (1,)), ((), ()))

P1_QA, P1_KA, P1_VA, P1_QN, P1_KVN = 0, 1024, 2048, 3072, 4096
P1_COLS = 5632
P2_Z, P2_XBC, P2_GM = 0, 2048, 6144
P2_COLS = 12288
P3_COLS = (SSM_GROUPS + NSA_KV_GROUPS) * LANES


def _cparams(semantics, vmem_mb=48):
    return pltpu.CompilerParams(dimension_semantics=semantics, vmem_limit_bytes=vmem_mb * 1024 * 1024)


def _sigmoid(x):
    return 1.0 / (1.0 + jnp.exp(-x))


def _silu(x):
    return x * _sigmoid(x)


def _split3(x):
    hi = x.astype(jnp.bfloat16)
    r1 = x - hi.astype(F32)
    mid = r1.astype(jnp.bfloat16)
    lo = (r1 - mid.astype(F32)).astype(jnp.bfloat16)
    return hi, mid, lo


def _dot3(a_exact, x, left=True):
    acc = None
    for part in _split3(x):
        t = (jnp.dot(a_exact, part, preferred_element_type=F32) if left
             else jnp.dot(part, a_exact, preferred_element_type=F32))
        acc = t if acc is None else acc + t
    return acc


def _mm_kernel(a_ref, b_ref, o_ref):
    o_ref[...] = jnp.dot(a_ref[...].astype(MXU), b_ref[...].astype(MXU),
                         preferred_element_type=F32).astype(o_ref.dtype)


def _matmul(a, b, *, tm, tn, out_dtype):
    m, k = a.shape
    n = b.shape[1]
    assert m % tm == 0 and n % tn == 0
    return pl.pallas_call(
        _mm_kernel,
        grid=(n // tn, m // tm),
        in_specs=[pl.BlockSpec((tm, k), lambda j, i: (i, 0)),
                  pl.BlockSpec((k, tn), lambda j, i: (0, j))],
        out_specs=pl.BlockSpec((tm, tn), lambda j, i: (i, j)),
        out_shape=jax.ShapeDtypeStruct((m, n), out_dtype),
        compiler_params=_cparams(("arbitrary", "arbitrary"), 56),
        name="proj_matmul",
    )(a, b)


def _relayout_w_in(w):
    off = np.concatenate([[0], np.cumsum(IN_SPLIT_SIZES)])
    qa, ka, va, z, xbc, dt, qn, kvn, gn, gm = [w[:, off[i]:off[i + 1]] for i in range(len(IN_SPLIT_SIZES))]
    d = w.shape[0]
    w1 = jnp.concatenate([qa, ka, va, qn, kvn], axis=1).astype(MXU)
    w2 = jnp.concatenate([z, xbc, gm], axis=1).astype(MXU)
    dtp = jnp.pad(dt.reshape(d, SSM_GROUPS, SSM_HEADS_PER_GROUP),
                  ((0, 0), (0, 0), (0, LANES - SSM_HEADS_PER_GROUP))).reshape(d, SSM_GROUPS * LANES)
    ng = NSA_HEADS_PER_GROUP * N_BRANCHES
    gnp = jnp.pad(gn.reshape(d, NSA_KV_GROUPS, ng), ((0, 0), (0, 0), (0, LANES - ng))).reshape(d, NSA_KV_GROUPS * LANES)
    w3 = jnp.concatenate([dtp, gnp], axis=1).astype(MXU)
    return w1, w2, w3


def _flash_update(carry, s, v):
    m, l, acc = carry
    m_new = jnp.maximum(m, jnp.max(s, axis=1, keepdims=True))
    alpha = jnp.exp(m - m_new)
    p = jnp.exp(s - m_new)
    l = alpha * l + jnp.sum(p, axis=1, keepdims=True)
    acc = alpha * acc + jnp.dot(p.astype(MXU), v, preferred_element_type=F32)
    return m_new, l, acc


def _flash_init(rows):
    return (jnp.full((rows, 1), NEG, F32), jnp.zeros((rows, 1), F32), jnp.zeros((rows, HEAD_DIM), F32))


def _moba_kernel(q_ref, k_ref, v_ref, o_ref, kaug_ref, kmean_ref, *, nb):
    blk = MOBA_BLOCK
    qi = pl.program_id(1)

    @pl.when(qi == 0)
    def _build_keys():
        kmean_ref[...] = jnp.zeros_like(kmean_ref)
        lane = lax.broadcasted_iota(jnp.int32, (blk, LANES), 1)

        def body(j, c):
            r0 = pl.multiple_of(j * blk, blk)
            kb = k_ref[pl.ds(r0, blk), :]
            kaug_ref[pl.ds(r0, blk), 0:HEAD_DIM] = kb.astype(kaug_ref.dtype)
            kaug_ref[pl.ds(r0, blk), HEAD_DIM:2 * HEAD_DIM] = (lane == j).astype(kaug_ref.dtype)
            kmean_ref[pl.ds(j, 1), :] = jnp.sum(kb.astype(F32), axis=0, keepdims=True) * (1.0 / blk)
            return c

        lax.fori_loop(0, nb, body, 0)

    q = q_ref[...].astype(MXU)
    gate = lax.dot_general(q, kmean_ref[...].astype(MXU), NT, preferred_element_type=F32)
    lane = lax.broadcasted_iota(jnp.int32, (blk, LANES), 1).astype(F32)
    own = qi.astype(F32)
    past = lane < own
    gate = jnp.where(past, gate, -jnp.inf)
    sel = lane == own
    for _ in range(MOBA_TOPK):
        mx = jnp.max(gate, axis=1, keepdims=True)
        idx = jnp.min(jnp.where((gate == mx) & past, lane, float(LANES)), axis=1, keepdims=True)
        pick = lane == idx
        sel = sel | pick
        gate = jnp.where(pick, -jnp.inf, gate)
    bias = jnp.where(sel, 0.0, MASK_BIAS).astype(MXU)
    qaug = jnp.concatenate([q, bias], axis=1)

    def scores(j):
        r0 = pl.multiple_of(j * blk, blk)
        s = lax.dot_general(qaug, kaug_ref[pl.ds(r0, blk), :], NT, preferred_element_type=F32) * ATT_SCALE
        return s, v_ref[pl.ds(r0, blk), :].astype(MXU)

    def past_step(j, carry):
        s, v = scores(j)
        return _flash_update(carry, s, v)

    carry = lax.fori_loop(0, qi, past_step, _flash_init(blk))
    s, v = scores(qi)
    row = lax.broadcasted_iota(jnp.int32, (blk, blk), 0)
    col = lax.broadcasted_iota(jnp.int32, (blk, blk), 1)
    s = jnp.where(col <= row, s, NEG)
    _, l, acc = _flash_update(carry, s, v)
    o_ref[...] = (acc / l).astype(o_ref.dtype)


def _moba(p1):
    s = p1.shape[0]
    nb = s // MOBA_BLOCK
    assert nb <= LANES
    kcol, vcol = P1_KA // HEAD_DIM, P1_VA // HEAD_DIM
    return pl.pallas_call(
        functools.partial(_moba_kernel, nb=nb),
        grid=(MOBA_HEADS, nb),
        in_specs=[pl.BlockSpec((MOBA_BLOCK, HEAD_DIM), lambda h, i: (i, h)),
                  pl.BlockSpec((s, HEAD_DIM), lambda h, i: (0, kcol + h)),
                  pl.BlockSpec((s, HEAD_DIM), lambda h, i: (0, vcol + h))],
        out_specs=pl.BlockSpec((MOBA_BLOCK, HEAD_DIM), lambda h, i: (i, h)),
        out_shape=jax.ShapeDtypeStruct((s, MOBA_WIDTH), MXU),
        scratch_shapes=[pltpu.VMEM((s, 2 * HEAD_DIM), MXU), pltpu.VMEM((LANES, HEAD_DIM), F32)],
        compiler_params=_cparams(("arbitrary", "arbitrary")),
        name="moba_attention",
    )(p1, p1, p1)


def _ssd_kernel(dtb_ref, alog_ref, dskip_ref,
                xs_ref, bm_ref, cm_ref, z_ref, dt_ref, cwx_ref, cwb_ref, cwc_ref, cbx_ref, cbb_ref, cbc_ref,
                ng_ref, o_ref, xbuf, bbuf, cbuf, state_ref, ybuf):
    L = SSM_CHUNK
    W = SSM_GROUP_WIDTH
    J = SSM_HEADS_PER_GROUP
    P = SSM_HEAD_DIM
    g = pl.program_id(0)
    c = pl.program_id(1)

    @pl.when(c == 0)
    def _reset():
        xbuf[0:SUBLANES, :] = jnp.zeros((SUBLANES, W), F32)
        bbuf[0:SUBLANES, :] = jnp.zeros((SUBLANES, SSM_STATE), F32)
        cbuf[0:SUBLANES, :] = jnp.zeros((SUBLANES, SSM_STATE), F32)
        state_ref[...] = jnp.zeros_like(state_ref)

    def conv_silu(buf, raw_ref, w_ref, b_ref):
        buf[SUBLANES:SUBLANES + L, :] = raw_ref[...]
        acc = b_ref[...]
        for i in range(SSM_CONV):
            lo = SUBLANES - (SSM_CONV - 1) + i
            acc = acc + w_ref[i:i + 1, :] * buf[lo:lo + L, :]
        buf[0:SUBLANES, :] = buf[L:L + SUBLANES, :]
        return _silu(acc)

    xs = conv_silu(xbuf, xs_ref, cwx_ref, cbx_ref)
    bm = conv_silu(bbuf, bm_ref, cwb_ref, cbb_ref)
    cm = conv_silu(cbuf, cm_ref, cwc_ref, cbc_ref)

    lane = lax.broadcasted_iota(jnp.int32, (1, LANES), 1)
    dtb = jnp.zeros((1, LANES), F32)
    alog = jnp.full((1, LANES), -jnp.inf, F32)
    for j in range(J):
        dtb = jnp.where(lane == j, dtb_ref[g * J + j], dtb)
        alog = jnp.where(lane == j, alog_ref[g * J + j], alog)
    x = dt_ref[...] + dtb
    dt = jnp.maximum(x, 0.0) + jnp.log(1.0 + jnp.exp(-jnp.abs(x)))
    dt = jnp.where(lane < J, dt, 0.0)
    a = dt * (-jnp.exp(alog))

    er = lax.broadcasted_iota(jnp.int32, (LANES, W), 0)
    ec = lax.broadcasted_iota(jnp.int32, (LANES, W), 1)
    expand = ((ec >> 6) == er).astype(jnp.bfloat16)
    tr = lax.broadcasted_iota(jnp.int32, (L, L), 0)
    tc = lax.broadcasted_iota(jnp.int32, (L, L), 1)
    tril = tr >= tc
    tril_b = tril.astype(jnp.bfloat16)
    dt_e = _dot3(expand, dt, left=False)
    a_e = _dot3(expand, a, left=False)
    acs = _dot3(tril_b, a_e, left=True)
    acs_t = acs.T
    a_last = acs[L - 1:L, :]

    xdt = xs * dt_e
    cb = lax.dot_general(cm.astype(MXU), bm.astype(MXU), NT, preferred_element_type=F32)
    for j in range(J):
        colv = acs[:, j * P:j * P + 1]
        rowv = acs_t[j * P:j * P + 1, :]
        dec = jnp.exp(jnp.where(tril, colv - rowv, -jnp.inf))
        ybuf[:, j * P:(j + 1) * P] = jnp.dot((cb * dec).astype(MXU), xdt[:, j * P:(j + 1) * P].astype(MXU),
                                             preferred_element_type=F32)
    st_old = state_ref[...]
    xdte = (xdt * jnp.exp(a_last - acs)).astype(MXU)
    st_new = jnp.dot(bm.T.astype(MXU), xdte, preferred_element_type=F32)
    y_off = jnp.dot(cm.astype(MXU), st_old.astype(MXU), preferred_element_type=F32) * jnp.exp(acs)
    state_ref[...] = st_old * jnp.exp(a_last) + st_new

    lane_w = lax.broadcasted_iota(jnp.int32, (1, W), 1)
    dsk = jnp.zeros((1, W), F32)
    for j in range(J):
        dsk = jnp.where((lane_w >> 6) == j, dskip_ref[g * J + j], dsk)
    y = ybuf[...] + y_off + xs * dsk
    y = y * _silu(z_ref[...])
    y = y * lax.rsqrt(jnp.mean(y * y, axis=1, keepdims=True) + RMS_EPS)
    o_ref[...] = (y * ng_ref[...]).astype(o_ref.dtype)


def _ssd(p2, p3, conv_w, conv_b, dt_bias, a_log, d_skip, norm_g):
    s = p2.shape[0]
    L, W, N = SSM_CHUNK, SSM_GROUP_WIDTH, SSM_STATE
    xs0 = P2_XBC // W
    bm0 = (P2_XBC + SSM_D_INNER) // N
    cm0 = bm0 + SSM_GROUPS
    z0 = P2_Z // W
    cb2 = conv_b.reshape(1, SSM_CONV_CH)
    ng2 = norm_g.reshape(1, SSM_D_INNER)
    grid_spec = pltpu.PrefetchScalarGridSpec(
        num_scalar_prefetch=3,
        grid=(SSM_GROUPS, s // L),
        in_specs=[
            pl.BlockSpec((L, W), lambda g, c, *_: (c, xs0 + g)),
            pl.BlockSpec((L, N), lambda g, c, *_: (c, bm0 + g)),
            pl.BlockSpec((L, N), lambda g, c, *_: (c, cm0 + g)),
            pl.BlockSpec((L, W), lambda g, c, *_: (c, z0 + g)),
            pl.BlockSpec((L, LANES), lambda g, c, *_: (c, g)),
            pl.BlockSpec((SSM_CONV, W), lambda g, c, *_: (0, g)),
            pl.BlockSpec((SSM_CONV, N), lambda g, c, *_: (0, SSM_D_INNER // N + g)),
            pl.BlockSpec((SSM_CONV, N), lambda g, c, *_: (0, SSM_D_INNER // N + SSM_GROUPS + g)),
            pl.BlockSpec((1, W), lambda g, c, *_: (0, g)),
            pl.BlockSpec((1, N), lambda g, c, *_: (0, SSM_D_INNER // N + g)),
            pl.BlockSpec((1, N), lambda g, c, *_: (0, SSM_D_INNER // N + SSM_GROUPS + g)),
            pl.BlockSpec((1, W), lambda g, c, *_: (0, g)),
        ],
        out_specs=pl.BlockSpec((L, W), lambda g, c, *_: (c, g)),
        scratch_shapes=[pltpu.VMEM((SUBLANES + L, W), F32), pltpu.VMEM((SUBLANES + L, N), F32),
                        pltpu.VMEM((SUBLANES + L, N), F32), pltpu.VMEM((N, W), F32), pltpu.VMEM((L, W), F32)],
    )
    return pl.pallas_call(
        _ssd_kernel,
        grid_spec=grid_spec,
        out_shape=jax.ShapeDtypeStruct((s, SSM_D_INNER), MXU),
        compiler_params=_cparams(("arbitrary", "arbitrary")),
        name="ssd_mixer",
    )(dt_bias, a_log, d_skip, p2, p2, p2, p2, p3, conv_w, conv_w, conv_w, cb2, cb2, cb2, ng2)


def _nsa_compress_kernel(t_ref, pos_ref, w1_ref, w2_ref, o_ref):
    half = NSA_CMP_STRIDE * HEAD_DIM
    t = t_ref[0].astype(F32)
    pos = pos_ref[0]
    lo = (t + pos[:, :half]).astype(MXU)
    hi = (t + pos[:, half:]).astype(MXU)
    w1 = w1_ref[0]
    a = jnp.dot(lo, w1[:half].astype(MXU), preferred_element_type=F32)
    b = jnp.dot(hi, w1[half:].astype(MXU), preferred_element_type=F32)
    n = t.shape[0]
    pre = a + pltpu.roll(b, n - 1, 0)
    act = jax.nn.gelu(pre, approximate=True)
    o_ref[0] = jnp.dot(act.astype(MXU), w2_ref[0].astype(MXU), preferred_element_type=F32).astype(o_ref.dtype)


def _nsa_compress(p1, pos_k, w1_k, w2_k, pos_v, w1_v, w2_v):
    s = p1.shape[0]
    n_str = s // NSA_CMP_STRIDE
    kv = p1[:, P1_KVN:P1_KVN + 2 * NSA_KV_WIDTH]
    t4 = kv.reshape(s, 4, HEAD_DIM).transpose(1, 0, 2).reshape(4, n_str, NSA_CMP_STRIDE * HEAD_DIM)
    pos = jnp.stack([pos_k, pos_v]).reshape(2, 1, NSA_CMP_LEN * HEAD_DIM)
    w1 = jnp.stack([w1_k, w1_v])
    w2 = jnp.stack([w2_k, w2_v])
    return pl.pallas_call(
        _nsa_compress_kernel,
        grid=(4,),
        in_specs=[pl.BlockSpec((1, n_str, NSA_CMP_STRIDE * HEAD_DIM), lambda i: (i, 0, 0)),
                  pl.BlockSpec((1, 1, NSA_CMP_LEN * HEAD_DIM), lambda i: (i // 2, 0, 0)),
                  pl.BlockSpec((1, NSA_CMP_LEN * HEAD_DIM, HEAD_DIM), lambda i: (i // 2, 0, 0)),
                  pl.BlockSpec((1, HEAD_DIM, HEAD_DIM), lambda i: (i // 2, 0, 0))],
        out_specs=pl.BlockSpec((1, n_str, HEAD_DIM), lambda i: (i, 0, 0)),
        out_shape=jax.ShapeDtypeStruct((4, n_str, HEAD_DIM), MXU),
        compiler_params=_cparams(("arbitrary",)),
        name="nsa_compress",
    )(t4, pos, w1, w2)


def _nsa_cmp_kernel(q_ref, kc_ref, vc_ref, gn_ref, o_ref, selb_ref, *, n_sel):
    tq = ATT_TILE
    qi = pl.program_id(1)
    kc = kc_ref[0].astype(MXU)
    vc = vc_ref[0].astype(MXU)
    nc = kc.shape[0]
    pos = qi * tq + lax.broadcasted_iota(jnp.int32, (tq, nc), 0)
    cidx = lax.broadcasted_iota(jnp.int32, (tq, nc), 1)
    valid = (cidx * NSA_CMP_STRIDE + NSA_CMP_LEN - 1 <= pos) & (cidx < nc - 1)
    orow = lax.broadcasted_iota(jnp.int32, (nc, LANES), 0) * NSA_CMP_STRIDE
    ocol = lax.broadcasted_iota(jnp.int32, (nc, LANES), 1) * NSA_SEL_BLOCK
    overlap = ((orow < ocol + NSA_SEL_BLOCK) & (orow + NSA_CMP_LEN > ocol)).astype(MXU)
    gates = _sigmoid(gn_ref[...])
    imp = jnp.zeros((tq, LANES), F32)
    for j in range(NSA_HEADS_PER_GROUP):
        q = q_ref[:, j * HEAD_DIM:(j + 1) * HEAD_DIM].astype(MXU)
        s = lax.dot_general(q, kc, NT, preferred_element_type=F32) * ATT_SCALE
        s = jnp.where(valid, s, -jnp.inf)
        m = jnp.max(s, axis=1, keepdims=True)
        m = jnp.where(m > -jnp.inf, m, 0.0)
        e = jnp.exp(s - m)
        den = jnp.sum(e, axis=1, keepdims=True)
        p = (e / jnp.where(den > 0, den, 1.0)).astype(MXU)
        o = jnp.dot(p, vc, preferred_element_type=F32)
        imp = imp + jnp.dot(p, overlap, preferred_element_type=F32)
        o_ref[:, j * HEAD_DIM:(j + 1) * HEAD_DIM] = o * gates[:, N_BRANCHES * j:N_BRANCHES * j + 1]

    lane = lax.broadcasted_iota(jnp.int32, (tq, LANES), 1).astype(F32)
    cur = ((qi * tq + lax.broadcasted_iota(jnp.int32, (tq, LANES), 0)) >> 6).astype(F32)
    allowed = lane <= cur
    forced = (lane == 0.0) | (lane == cur) | (lane == cur - 1.0)
    val = jnp.where(forced, jnp.inf, jnp.where(allowed, imp, -jnp.inf))
    val = jnp.where(lane < float(n_sel), val, -jnp.inf)

    def pick_round(_, c):
        val, sel = c
        mx = jnp.max(val, axis=1, keepdims=True)
        idx = jnp.min(jnp.where(val == mx, lane, float(LANES)), axis=1, keepdims=True)
        pick = lane == idx
        sel = jnp.where(pick & allowed, 1.0, sel)
        val = jnp.where(pick, -jnp.inf, val)
        return val, sel

    _, sel = lax.fori_loop(0, min(NSA_TOPN, n_sel), pick_round, (val, jnp.zeros((tq, LANES), F32)))
    selb_ref[0] = jnp.where(sel > 0, 0.0, MASK_BIAS).astype(selb_ref.dtype)


def _nsa_cmp(p1, p3, kvc):
    s = p1.shape[0]
    n_str = s // NSA_CMP_STRIDE
    n_sel = s // NSA_SEL_BLOCK
    assert n_sel <= LANES
    gw = NSA_HEADS_PER_GROUP * HEAD_DIM
    q0 = P1_QN // gw
    return pl.pallas_call(
        functools.partial(_nsa_cmp_kernel, n_sel=n_sel),
        grid=(NSA_KV_GROUPS, s // ATT_TILE),
        in_specs=[pl.BlockSpec((ATT_TILE, gw), lambda g, i: (i, q0 + g)),
                  pl.BlockSpec((1, n_str, HEAD_DIM), lambda g, i: (g, 0, 0)),
                  pl.BlockSpec((1, n_str, HEAD_DIM), lambda g, i: (NSA_KV_GROUPS + g, 0, 0)),
                  pl.BlockSpec((ATT_TILE, LANES), lambda g, i: (i, SSM_GROUPS + g))],
        out_specs=[pl.BlockSpec((ATT_TILE, gw), lambda g, i: (i, g)),
                   pl.BlockSpec((1, ATT_TILE, LANES), lambda g, i: (g, i, 0))],
        out_shape=[jax.ShapeDtypeStruct((s, NSA_WIDTH), F32),
                   jax.ShapeDtypeStruct((NSA_KV_GROUPS, s, LANES), MXU)],
        compiler_params=_cparams(("arbitrary", "arbitrary")),
        name="nsa_compressed_attention",
    )(p1, kvc, kvc, p3)


def _nsa_sel_kernel(q_ref, selb_ref, k_ref, v_ref, gn_ref, prev_ref, o_ref, kaug_ref, qaug_ref, *, n_tiles):
    tq = ATT_TILE
    J = NSA_HEADS_PER_GROUP
    qi = pl.program_id(1)

    @pl.when(qi == 0)
    def _build_keys():
        lane = lax.broadcasted_iota(jnp.int32, (tq, LANES), 1)
        rowi = lax.broadcasted_iota(jnp.int32, (tq, LANES), 0)

        def body(t, c):
            r0 = pl.multiple_of(t * tq, tq)
            kaug_ref[pl.ds(r0, tq), 0:HEAD_DIM] = k_ref[pl.ds(r0, tq), :].astype(kaug_ref.dtype)
            kaug_ref[pl.ds(r0, tq), HEAD_DIM:2 * HEAD_DIM] = (
                lane == ((t * tq + rowi) >> 6)).astype(kaug_ref.dtype)
            return c

        lax.fori_loop(0, n_tiles, body, 0)

    selb = selb_ref[0]
    for j in range(J):
        qaug_ref[j * tq:(j + 1) * tq, 0:HEAD_DIM] = q_ref[:, j * HEAD_DIM:(j + 1) * HEAD_DIM].astype(qaug_ref.dtype)
        qaug_ref[j * tq:(j + 1) * tq, HEAD_DIM:2 * HEAD_DIM] = selb
    qaug = qaug_ref[...]

    def scores(t):
        r0 = pl.multiple_of(t * tq, tq)
        s = lax.dot_general(qaug, kaug_ref[pl.ds(r0, tq), :], NT, preferred_element_type=F32) * ATT_SCALE
        return s, v_ref[pl.ds(r0, tq), :].astype(MXU)

    def past_step(t, carry):
        s, v = scores(t)
        return _flash_update(carry, s, v)

    carry = lax.fori_loop(0, qi, past_step, _flash_init(J * tq))
    s, v = scores(qi)
    row = lax.broadcasted_iota(jnp.int32, (J * tq, tq), 0) & (tq - 1)
    col = lax.broadcasted_iota(jnp.int32, (J * tq, tq), 1)
    s = jnp.where(col <= row, s, NEG)
    _, l, acc = _flash_update(carry, s, v)
    o = acc / l
    gates = _sigmoid(gn_ref[...])
    for j in range(J):
        o_ref[:, j * HEAD_DIM:(j + 1) * HEAD_DIM] = (
            prev_ref[:, j * HEAD_DIM:(j + 1) * HEAD_DIM]
            + o[j * tq:(j + 1) * tq] * gates[:, N_BRANCHES * j + 1:N_BRANCHES * j + 2])


def _nsa_sel(p1, p3, selb, prev):
    s = p1.shape[0]
    gw = NSA_HEADS_PER_GROUP * HEAD_DIM
    q0 = P1_QN // gw
    k0 = (P1_KVN + 2 * NSA_KV_WIDTH) // HEAD_DIM
    v0 = k0 + NSA_KV_GROUPS
    return pl.pallas_call(
        functools.partial(_nsa_sel_kernel, n_tiles=s // ATT_TILE),
        grid=(NSA_KV_GROUPS, s // ATT_TILE),
        in_specs=[pl.BlockSpec((ATT_TILE, gw), lambda g, i: (i, q0 + g)),
                  pl.BlockSpec((1, ATT_TILE, LANES), lambda g, i: (g, i, 0)),
                  pl.BlockSpec((s, HEAD_DIM), lambda g, i: (0, k0 + g)),
                  pl.BlockSpec((s, HEAD_DIM), lambda g, i: (0, v0 + g)),
                  pl.BlockSpec((ATT_TILE, LANES), lambda g, i: (i, SSM_GROUPS + g)),
                  pl.BlockSpec((ATT_TILE, gw), lambda g, i: (i, g))],
        out_specs=pl.BlockSpec((ATT_TILE, gw), lambda g, i: (i, g)),
        out_shape=jax.ShapeDtypeStruct((s, NSA_WIDTH), F32),
        scratch_shapes=[pltpu.VMEM((s, 2 * HEAD_DIM), MXU),
                        pltpu.VMEM((NSA_HEADS_PER_GROUP * ATT_TILE, 2 * HEAD_DIM), MXU)],
        compiler_params=_cparams(("arbitrary", "arbitrary")),
        name="nsa_selected_attention",
    )(p1, selb, p1, p1, p3, prev)


def _nsa_win_kernel(q_ref, k_ref, v_ref, gn_ref, prev_ref, o_ref, qst_ref):
    tq = ATT_TILE
    J = NSA_HEADS_PER_GROUP
    halo = NSA_WINDOW // tq
    qi = pl.program_id(1)
    for j in range(J):
        qst_ref[j * tq:(j + 1) * tq, :] = q_ref[:, j * HEAD_DIM:(j + 1) * HEAD_DIM].astype(qst_ref.dtype)
    q = qst_ref[...]
    row = lax.broadcasted_iota(jnp.int32, (J * tq, tq), 0) & (tq - 1)
    col = lax.broadcasted_iota(jnp.int32, (J * tq, tq), 1)

    def step(t, carry):
        r0 = pl.multiple_of(t * tq, tq)
        s = lax.dot_general(q, k_ref[pl.ds(r0, tq), :].astype(MXU), NT, preferred_element_type=F32) * ATT_SCALE
        diff = (qi - t) * tq + row - col
        s = jnp.where((diff >= 0) & (diff < NSA_WINDOW), s, NEG)
        return _flash_update(carry, s, v_ref[pl.ds(r0, tq), :].astype(MXU))

    _, l, acc = lax.fori_loop(jnp.maximum(qi - halo, 0), qi + 1, step, _flash_init(J * tq))
    o = acc / l
    gates = _sigmoid(gn_ref[...])
    for j in range(J):
        o_ref[:, j * HEAD_DIM:(j + 1) * HEAD_DIM] = (
            prev_ref[:, j * HEAD_DIM:(j + 1) * HEAD_DIM]
            + o[j * tq:(j + 1) * tq] * gates[:, N_BRANCHES * j + 2:N_BRANCHES * j + 3])


def _nsa_win(p1, p3, prev):
    s = p1.shape[0]
    gw = NSA_HEADS_PER_GROUP * HEAD_DIM
    q0 = P1_QN // gw
    k0 = (P1_KVN + 4 * NSA_KV_WIDTH) // HEAD_DIM
    v0 = k0 + NSA_KV_GROUPS
    return pl.pallas_call(
        _nsa_win_kernel,
        grid=(NSA_KV_GROUPS, s // ATT_TILE),
        in_specs=[pl.BlockSpec((ATT_TILE, gw), lambda g, i: (i, q0 + g)),
                  pl.BlockSpec((s, HEAD_DIM), lambda g, i: (0, k0 + g)),
                  pl.BlockSpec((s, HEAD_DIM), lambda g, i: (0, v0 + g)),
                  pl.BlockSpec((ATT_TILE, LANES), lambda g, i: (i, SSM_GROUPS + g)),
                  pl.BlockSpec((ATT_TILE, gw), lambda g, i: (i, g))],
        out_specs=pl.BlockSpec((ATT_TILE, gw), lambda g, i: (i, g)),
        out_shape=jax.ShapeDtypeStruct((s, NSA_WIDTH), F32),
        scratch_shapes=[pltpu.VMEM((NSA_HEADS_PER_GROUP * ATT_TILE, HEAD_DIM), MXU)],
        compiler_params=_cparams(("arbitrary", "arbitrary")),
        name="nsa_window_attention",
    )(p1, p1, p1, p3, prev)


def _merge_kernel(ya_ref, yb_ref, yc_ref, g0_ref, g1_ref, g2_ref, wa_ref, wb_ref, wc_ref, o_ref):
    def branch(y_ref, w_ref, g_ref):
        prod = jnp.dot(y_ref[...].astype(MXU), w_ref[...].astype(MXU), preferred_element_type=F32)
        return _sigmoid(g_ref[...]) * prod

    o_ref[...] = (branch(ya_ref, wa_ref, g0_ref) + branch(yb_ref, wb_ref, g1_ref)
                  + branch(yc_ref, wc_ref, g2_ref)).astype(o_ref.dtype)


def _merge(ya, yb, yc, p2, wa, wb, wc, *, tm=512, tn=512):
    s = ya.shape[0]
    d = wa.shape[1]
    g0 = P2_GM // tn
    gstep = d // tn
    return pl.pallas_call(
        _merge_kernel,
        grid=(d // tn, s // tm),
        in_specs=[pl.BlockSpec((tm, ya.shape[1]), lambda j, i: (i, 0)),
                  pl.BlockSpec((tm, yb.shape[1]), lambda j, i: (i, 0)),
                  pl.BlockSpec((tm, yc.shape[1]), lambda j, i: (i, 0)),
                  pl.BlockSpec((tm, tn), lambda j, i: (i, g0 + j)),
                  pl.BlockSpec((tm, tn), lambda j, i: (i, g0 + gstep + j)),
                  pl.BlockSpec((tm, tn), lambda j, i: (i, g0 + 2 * gstep + j)),
                  pl.BlockSpec((wa.shape[0], tn), lambda j, i: (0, j)),
                  pl.BlockSpec((wb.shape[0], tn), lambda j, i: (0, j)),
                  pl.BlockSpec((wc.shape[0], tn), lambda j, i: (0, j))],
        out_specs=pl.BlockSpec((tm, tn), lambda j, i: (i, j)),
        out_shape=jax.ShapeDtypeStruct((s, d), MXU),
        compiler_params=_cparams(("arbitrary", "arbitrary")),
        name="branch_merge",
    )(ya, yb, yc, p2, p2, p2, wa, wb, wc)


def _layer_norm_rows(x, g, b):
    xc = x - jnp.mean(x, axis=1, keepdims=True)
    var = jnp.mean(xc * xc, axis=1, keepdims=True)
    return xc * lax.rsqrt(var + LN_EPS) * g + b


def _wout_ln_kernel(m_ref, w_ref, h_ref, g_ref, b_ref, o_ref, ob_ref):
    acc = jnp.dot(m_ref[...].astype(MXU), w_ref[...].astype(MXU), preferred_element_type=F32)
    y = _layer_norm_rows(DEEPNORM_ALPHA * h_ref[...] + acc, g_ref[...], b_ref[...])
    o_ref[...] = y
    ob_ref[...] = y.astype(ob_ref.dtype)


def _wout_ln(merged, w_out, h, ln_g, ln_b, *, tm=256):
    s, d = h.shape
    return pl.pallas_call(
        _wout_ln_kernel,
        grid=(s // tm,),
        in_specs=[pl.BlockSpec((tm, d), lambda i: (i, 0)),
                  pl.BlockSpec((d, d), lambda i: (0, 0)),
                  pl.BlockSpec((tm, d), lambda i: (i, 0)),
                  pl.BlockSpec((1, d), lambda i: (0, 0)),
                  pl.BlockSpec((1, d), lambda i: (0, 0))],
        out_specs=[pl.BlockSpec((tm, d), lambda i: (i, 0)), pl.BlockSpec((tm, d), lambda i: (i, 0))],
        out_shape=[jax.ShapeDtypeStruct((s, d), F32), jax.ShapeDtypeStruct((s, d), MXU)],
        compiler_params=_cparams(("arbitrary",)),
        name="out_proj_layernorm",
    )(merged, w_out.astype(MXU), h, ln_g.reshape(1, d), ln_b.reshape(1, d))


def _ffn_ln_kernel(h_ref, r_ref, s_ref, g_ref, b_ref, o_ref, ob_ref):
    y = _layer_norm_rows(DEEPNORM_ALPHA * h_ref[...] + (r_ref[...] + s_ref[...]), g_ref[...], b_ref[...])
    o_ref[...] = y
    ob_ref[...] = y.astype(ob_ref.dtype)


def _ffn_ln(h, routed, shared, ln_g, ln_b, *, tm=256):
    s, d = h.shape
    row = pl.BlockSpec((tm, d), lambda i: (i, 0))
    vec = pl.BlockSpec((1, d), lambda i: (0, 0))
    return pl.pallas_call(
        _ffn_ln_kernel,
        grid=(s // tm,),
        in_specs=[row, row, row, vec, vec],
        out_specs=[row, row],
        out_shape=[jax.ShapeDtypeStruct((s, d), F32), jax.ShapeDtypeStruct((s, d), MXU)],
        compiler_params=_cparams(("arbitrary",)),
        name="ffn_residual_layernorm",
    )(h, routed, shared, ln_g.reshape(1, d), ln_b.reshape(1, d))


def _router_kernel(h_ref, wr_ref, rb_ref, e_ref, w_ref, pos_ref, cnt_ref, carry_ref):
    tq = h_ref.shape[0]
    i = pl.program_id(0)

    @pl.when(i == 0)
    def _reset():
        carry_ref[...] = jnp.zeros_like(carry_ref)

    logits = lax.dot_general(wr_ref[...].astype(MXU), h_ref[...].astype(MXU), NT, preferred_element_type=F32)
    scores = _sigmoid(logits)
    biased = scores + rb_ref[...]
    G, PG = N_EXPERT_GROUPS, EXPERTS_PER_GROUP
    sub = lax.broadcasted_iota(jnp.int32, (PG, tq), 0).astype(F32)
    gi = lax.broadcasted_iota(jnp.int32, (G, tq), 0).astype(F32)
    gs = jnp.zeros((G, tq), F32)
    for g in range(G):
        blk = biased[g * PG:(g + 1) * PG, :]
        m1 = jnp.max(blk, axis=0, keepdims=True)
        i1 = jnp.min(jnp.where(blk == m1, sub, float(PG)), axis=0, keepdims=True)
        m2 = jnp.max(jnp.where(sub == i1, -jnp.inf, blk), axis=0, keepdims=True)
        gs = jnp.where(gi == float(g), m1 + m2, gs)
    keep = jnp.zeros((G, tq), F32)
    for _ in range(TOPK_GROUPS):
        mx = jnp.max(gs, axis=0, keepdims=True)
        idx = jnp.min(jnp.where(gs == mx, gi, float(G)), axis=0, keepdims=True)
        pick = gi == idx
        keep = jnp.where(pick, 1.0, keep)
        gs = jnp.where(pick, -jnp.inf, gs)
    val = jnp.concatenate(
        [jnp.where(keep[g:g + 1, :] > 0, biased[g * PG:(g + 1) * PG, :], -jnp.inf) for g in range(G)], axis=0)
    ei = lax.broadcasted_iota(jnp.int32, (N_EXPERTS, tq), 0).astype(F32)
    picks, svals = [], []
    sel = jnp.zeros((N_EXPERTS, tq), F32)
    for r in range(TOP_K):
        mx = jnp.max(val, axis=0, keepdims=True)
        idx = jnp.min(jnp.where(val == mx, ei, float(N_EXPERTS)), axis=0, keepdims=True)
        pick = ei == idx
        picks.append(pick)
        svals.append(jnp.sum(jnp.where(pick, scores, 0.0), axis=0, keepdims=True))
        e_ref[r:r + 1, :] = idx.astype(jnp.int32)
        sel = jnp.where(pick, 1.0, sel)
        val = jnp.where(pick, -jnp.inf, val)
    wsum = svals[0]
    for r in range(1, TOP_K):
        wsum = wsum + svals[r]
    tr = lax.broadcasted_iota(jnp.int32, (tq, tq), 0)
    tc = lax.broadcasted_iota(jnp.int32, (tq, tq), 1)
    before = (tr < tc).astype(jnp.bfloat16)
    prefix = jnp.dot(sel.astype(jnp.bfloat16), before, preferred_element_type=F32)
    pos = carry_ref[:, 0:1] + prefix
    for r in range(TOP_K):
        w_ref[r:r + 1, :] = svals[r] / wsum * ROUTED_SCALE
        pos_ref[r:r + 1, :] = jnp.sum(jnp.where(picks[r], pos, 0.0), axis=0, keepdims=True).astype(jnp.int32)
    total = carry_ref[...] + jnp.sum(sel, axis=1, keepdims=True)
    carry_ref[...] = total
    cnt_ref[...] = total


def _router(hb, w_router, router_bias, *, tq=256):
    s, d = hb.shape
    row = pl.BlockSpec((TOP_K, tq), lambda i: (0, i))
    return pl.pallas_call(
        _router_kernel,
        grid=(s // tq,),
        in_specs=[pl.BlockSpec((tq, d), lambda i: (i, 0)),
                  pl.BlockSpec((N_EXPERTS, d), lambda i: (0, 0)),
                  pl.BlockSpec((N_EXPERTS, 1), lambda i: (0, 0))],
        out_specs=[row, row, row, pl.BlockSpec((N_EXPERTS, LANES), lambda i: (0, 0))],
        out_shape=[jax.ShapeDtypeStruct((TOP_K, s), jnp.int32), jax.ShapeDtypeStruct((TOP_K, s), F32),
                   jax.ShapeDtypeStruct((TOP_K, s), jnp.int32), jax.ShapeDtypeStruct((N_EXPERTS, LANES), F32)],
        scratch_shapes=[pltpu.VMEM((N_EXPERTS, LANES), F32)],
        compiler_params=_cparams(("arbitrary",)),
        name="moe_router",
    )(hb, w_router.T, router_bias.reshape(N_EXPERTS, 1))


def _expert_kernel(te_ref, nu_ref, x_ref, wg_ref, wu_ref, wd_ref, o_ref):
    i = pl.program_id(0)

    @pl.when(i < nu_ref[0])
    def _compute():
        x = x_ref[...].astype(MXU)
        hg = jnp.dot(x, wg_ref[0].astype(MXU), preferred_element_type=F32)
        hu = jnp.dot(x, wu_ref[0].astype(MXU), preferred_element_type=F32)
        act = (_silu(hg) * hu).astype(MXU)
        o_ref[...] = jnp.dot(act, wd_ref[0].astype(MXU), preferred_element_type=F32).astype(o_ref.dtype)

    @pl.when(i >= nu_ref[0])
    def _unused():
        o_ref[...] = jnp.zeros_like(o_ref)


def _experts(x_rows, tile_e, n_used, w_gate, w_up, w_down, *, tm):
    n_rows, d = x_rows.shape
    f = w_gate.shape[2]
    grid_spec = pltpu.PrefetchScalarGridSpec(
        num_scalar_prefetch=2,
        grid=(n_rows // tm,),
        in_specs=[pl.BlockSpec((tm, d), lambda i, te, nu: (i, 0)),
                  pl.BlockSpec((1, d, f), lambda i, te, nu: (te[i], 0, 0)),
                  pl.BlockSpec((1, d, f), lambda i, te, nu: (te[i], 0, 0)),
                  pl.BlockSpec((1, f, d), lambda i, te, nu: (te[i], 0, 0))],
        out_specs=pl.BlockSpec((tm, d), lambda i, te, nu: (i, 0)),
    )
    return pl.pallas_call(
        _expert_kernel,
        grid_spec=grid_spec,
        out_shape=jax.ShapeDtypeStruct((n_rows, d), F32),
        compiler_params=_cparams(("arbitrary",), 56),
        name="moe_experts",
    )(tile_e, n_used, x_rows, w_gate, w_up, w_down)


def _moe(h, hb, w_router, router_bias, w_exp_gate, w_exp_up, w_exp_down, w_sh_gate, w_sh_up, w_sh_down):
    s, d = h.shape
    tm = MOE_TILE
    top_e, top_w, top_pos, counts = _router(hb, w_router, router_bias)
    cnt = counts[:, 0].astype(jnp.int32)
    tiles_e = (cnt + tm - 1) // tm
    tile_end = jnp.cumsum(tiles_e)
    row_start = (tile_end - tiles_e) * tm
    dest = row_start[top_e] + top_pos
    n_tiles = s * TOP_K // tm + N_EXPERTS
    tok = jnp.broadcast_to(jnp.arange(s, dtype=jnp.int32)[None, :], (TOP_K, s))
    row_tok = jnp.zeros((n_tiles * tm,), jnp.int32).at[dest.reshape(-1)].set(tok.reshape(-1))
    tile_e = jnp.minimum(jnp.searchsorted(tile_end, jnp.arange(n_tiles, dtype=jnp.int32), side="right"),
                         N_EXPERTS - 1).astype(jnp.int32)
    n_used = tile_end[-1:].astype(jnp.int32)
    x_rows = jnp.take(hb, row_tok, axis=0)
    y_rows = _experts(x_rows, tile_e, n_used, w_exp_gate, w_exp_up, w_exp_down, tm=tm)
    routed = jnp.einsum("ks,ksd->sd", top_w, jnp.take(y_rows, dest, axis=0))
    shared = _experts(hb, jnp.zeros((s // tm,), jnp.int32), jnp.full((1,), s // tm, jnp.int32),
                      w_sh_gate[None], w_sh_up[None], w_sh_down[None], tm=tm)
    return routed, shared


def _mixer(hb, w_in, conv_w, conv_b, dt_bias, a_log, d_skip, ssm_norm_g,
           cmp_pos_k, cmp_w1_k, cmp_w2_k, cmp_pos_v, cmp_w1_v, cmp_w2_v, w_br_a, w_br_b, w_br_c):
    w1, w2, w3 = _relayout_w_in(w_in)
    p1 = _matmul(hb, w1, tm=1024, tn=1408, out_dtype=MXU)
    p2 = _matmul(hb, w2, tm=1024, tn=1536, out_dtype=F32)
    p3 = _matmul(hb, w3, tm=1024, tn=P3_COLS, out_dtype=F32)
    y_a = _moba(p1)
    y_b = _ssd(p2, p3, conv_w, conv_b, dt_bias, a_log, d_skip, ssm_norm_g)
    kvc = _nsa_compress(p1, cmp_pos_k, cmp_w1_k, cmp_w2_k, cmp_pos_v, cmp_w1_v, cmp_w2_v)
    y_c, selb = _nsa_cmp(p1, p3, kvc)
    y_c = _nsa_sel(p1, p3, selb, y_c)
    y_c = _nsa_win(p1, p3, y_c)
    return _merge(y_a, y_b, y_c, p2, w_br_a, w_br_b, w_br_c)


def kernel(x, w_in, conv_w, conv_b, dt_bias, a_log, d_skip, ssm_norm_g, cmp_pos_k, cmp_w1_k, cmp_w2_k, cmp_pos_v, cmp_w1_v, cmp_w2_v, w_br_a, w_br_b, w_br_c, w_out, ln1_g, ln1_b, w_router, router_bias, w_exp_gate, w_exp_up, w_exp_down, w_sh_gate, w_sh_up, w_sh_down, ln2_g, ln2_b):
    bsz, s, d = x.shape
    assert bsz == 1
    h = x.reshape(s, d)
    hb = h.astype(MXU)
    for l in range(w_in.shape[0]):
        merged = _mixer(hb, w_in[l], conv_w[l], conv_b[l], dt_bias[l], a_log[l], d_skip[l], ssm_norm_g[l],
                        cmp_pos_k[l], cmp_w1_k[l], cmp_w2_k[l], cmp_pos_v[l], cmp_w1_v[l], cmp_w2_v[l],
                        w_br_a[l], w_br_b[l], w_br_c[l])
        h, hb = _wout_ln(merged, w_out[l], h, ln1_g[l], ln1_b[l])
        routed, shared = _moe(h, hb, w_router[l], router_bias[l], w_exp_gate[l], w_exp_up[l], w_exp_down[l],
                              w_sh_gate[l], w_sh_up[l], w_sh_down[l])
        h, hb = _ffn_ln(h, routed, shared, ln2_g[l], ln2_b[l])
    return h.reshape(bsz, s, d)
```

```python
import functools

import numpy as np
import jax
import jax.numpy as jnp
from jax import lax
from jax.experimental import pallas as pl
from jax.experimental.pallas import tpu as pltpu

F32 = jnp.float32
MXU = jnp.bfloat16

D_MODEL = 2048
DEPTH = 2
HEAD_DIM = 128
MOBA_HEADS = 8
MOBA_WIDTH = MOBA_HEADS * HEAD_DIM
MOBA_BLOCK = 256
MOBA_TOPK = 3
SSM_D_INNER = D_MODEL
SSM_HEAD_DIM = 64
SSM_HEADS = SSM_D_INNER // SSM_HEAD_DIM
SSM_STATE = 128
SSM_GROUPS = 8
SSM_HEADS_PER_GROUP = SSM_HEADS // SSM_GROUPS
SSM_GROUP_WIDTH = SSM_D_INNER // SSM_GROUPS
SSM_CONV = 4
SSM_CHUNK = 256
SSM_CONV_CH = SSM_D_INNER + 2 * SSM_GROUPS * SSM_STATE
NSA_HEADS = 8
NSA_KV_GROUPS = 2
NSA_HEADS_PER_GROUP = NSA_HEADS // NSA_KV_GROUPS
NSA_WIDTH = NSA_HEADS * HEAD_DIM
NSA_KV_WIDTH = NSA_KV_GROUPS * HEAD_DIM
NSA_CMP_LEN = 32
NSA_CMP_STRIDE = 16
NSA_SEL_BLOCK = 64
NSA_TOPN = 16
NSA_WINDOW = 512
N_BRANCHES = 3
N_EXPERTS = 64
N_EXPERT_GROUPS = 8
EXPERTS_PER_GROUP = N_EXPERTS // N_EXPERT_GROUPS
TOPK_GROUPS = 4
TOP_K = 8
D_EXPERT = 512
ROUTED_SCALE = 2.5
DEEPNORM_ALPHA = (2 * DEPTH) ** 0.25
LN_EPS = 1e-5
RMS_EPS = 1e-5
IN_SPLIT_SIZES = (MOBA_WIDTH, MOBA_WIDTH, MOBA_WIDTH,
                  SSM_D_INNER, SSM_CONV_CH, SSM_HEADS,
                  NSA_WIDTH, 6 * NSA_KV_WIDTH, N_BRANCHES * NSA_HEADS,
                  N_BRANCHES * D_MODEL)

LANES = 128
SUBLANES = 8
ATT_TILE = 256
MOE_TILE = 256
ATT_SCALE = HEAD_DIM ** -0.5
MASK_BIAS = -2.0 ** 30
NEG = -1e30

NT = (((1,), (1,)), ((), ()))

P1_QA, P1_KA, P1_VA, P1_QN, P1_KVN = 0, 1024, 2048, 3072, 4096
P1_COLS = 5632
P2_Z, P2_XBC, P2_GM = 0, 2048, 6144
P2_COLS = 12288
P3_COLS = (SSM_GROUPS + NSA_KV_GROUPS) * LANES


def _cparams(semantics, vmem_mb=48):
    return pltpu.CompilerParams(dimension_semantics=semantics, vmem_limit_bytes=vmem_mb * 1024 * 1024)


def _sigmoid(x):
    return 1.0 / (1.0 + jnp.exp(-x))


def _silu(x):
    return x * _sigmoid(x)


def _split3(x):
    hi = x.astype(jnp.bfloat16)
    r1 = x - hi.astype(F32)
    mid = r1.astype(jnp.bfloat16)
    lo = (r1 - mid.astype(F32)).astype(jnp.bfloat16)
    return hi, mid, lo


def _dot3(a_exact, x, left=True):
    acc = None
    for part in _split3(x):
        t = (jnp.dot(a_exact, part, preferred_element_type=F32) if left
             else jnp.dot(part, a_exact, preferred_element_type=F32))
        acc = t if acc is None else acc + t
    return acc


def _mm_kernel(a_ref, b_ref, o_ref):
    o_ref[...] = jnp.dot(a_ref[...].astype(MXU), b_ref[...].astype(MXU),
                         preferred_element_type=F32).astype(o_ref.dtype)


def _matmul(a, b, *, tm, tn, out_dtype):
    m, k = a.shape
    n = b.shape[1]
    assert m % tm == 0 and n % tn == 0
    return pl.pallas_call(
        _mm_kernel,
        grid=(n // tn, m // tm),
        in_specs=[pl.BlockSpec((tm, k), lambda j, i: (i, 0)),
                  pl.BlockSpec((k, tn), lambda j, i: (0, j))],
        out_specs=pl.BlockSpec((tm, tn), lambda j, i: (i, j)),
        out_shape=jax.ShapeDtypeStruct((m, n), out_dtype),
        compiler_params=_cparams(("arbitrary", "arbitrary"), 56),
        name="proj_matmul",
    )(a, b)


def _relayout_w_in(w):
    off = np.concatenate([[0], np.cumsum(IN_SPLIT_SIZES)])
    qa, ka, va, z, xbc, dt, qn, kvn, gn, gm = [w[:, off[i]:off[i + 1]] for i in range(len(IN_SPLIT_SIZES))]
    d = w.shape[0]
    w1 = jnp.concatenate([qa, ka, va, qn, kvn], axis=1).astype(MXU)
    w2 = jnp.concatenate([z, xbc, gm], axis=1).astype(MXU)
    dtp = jnp.pad(dt.reshape(d, SSM_GROUPS, SSM_HEADS_PER_GROUP),
                  ((0, 0), (0, 0), (0, LANES - SSM_HEADS_PER_GROUP))).reshape(d, SSM_GROUPS * LANES)
    ng = NSA_HEADS_PER_GROUP * N_BRANCHES
    gnp = jnp.pad(gn.reshape(d, NSA_KV_GROUPS, ng), ((0, 0), (0, 0), (0, LANES - ng))).reshape(d, NSA_KV_GROUPS * LANES)
    w3 = jnp.concatenate([dtp, gnp], axis=1).astype(MXU)
    return w1, w2, w3


def _flash_init(m_ref, acc_ref):
    m_ref[...] = jnp.full(m_ref.shape, NEG, F32)
    acc_ref[...] = jnp.zeros(acc_ref.shape, F32)


def _flash_step(s, v_aug, m_ref, acc_ref):
    hd = HEAD_DIM
    m_prev = m_ref[...]
    m_new = jnp.maximum(m_prev, jnp.max(s, axis=1, keepdims=True))
    alpha = jnp.exp(m_prev - m_new)
    p = jnp.concatenate([jnp.exp(s[:, c * hd:(c + 1) * hd] - m_new) for c in range(s.shape[1] // hd)], axis=1)
    pv = jnp.dot(p.astype(MXU), v_aug, preferred_element_type=F32)
    acc_ref[:, 0:hd] = alpha * acc_ref[:, 0:hd] + pv[:, 0:hd]
    acc_ref[:, hd:2 * hd] = alpha * acc_ref[:, hd:2 * hd] + pv[:, hd:2 * hd]
    m_ref[...] = m_new


def _flash_out(acc_ref):
    return acc_ref[:, 0:HEAD_DIM] / acc_ref[:, HEAD_DIM:2 * HEAD_DIM]


MOBA_KV_GROUP = 4


def _moba_kernel(q_ref, k_ref, v_ref, o_ref, kaug_ref, vaug_ref, kmean_ref, m_ref, acc_ref, *, nb):
    blk = MOBA_BLOCK
    grp = MOBA_KV_GROUP * blk
    qi = pl.program_id(1)

    @pl.when(qi == 0)
    def _build_keys():
        kmean_ref[...] = jnp.zeros_like(kmean_ref)
        lane = lax.broadcasted_iota(jnp.int32, (blk, LANES), 1)
        ones = jnp.ones((blk, HEAD_DIM), vaug_ref.dtype)

        def body(j, c):
            r0 = pl.multiple_of(j * blk, blk)
            kb = k_ref[pl.ds(r0, blk), :]
            kaug_ref[pl.ds(r0, blk), 0:HEAD_DIM] = kb.astype(kaug_ref.dtype)
            kaug_ref[pl.ds(r0, blk), HEAD_DIM:2 * HEAD_DIM] = (lane == j).astype(kaug_ref.dtype)
            vaug_ref[pl.ds(r0, blk), 0:HEAD_DIM] = v_ref[pl.ds(r0, blk), :].astype(vaug_ref.dtype)
            vaug_ref[pl.ds(r0, blk), HEAD_DIM:2 * HEAD_DIM] = ones
            kmean_ref[pl.ds(j, 1), :] = jnp.sum(kb.astype(F32), axis=0, keepdims=True) * (1.0 / blk)
            return c

        lax.fori_loop(0, nb, body, 0)

    q = q_ref[...].astype(MXU)
    gate = lax.dot_general(q, kmean_ref[...].astype(MXU), NT, preferred_element_type=F32)
    lane = lax.broadcasted_iota(jnp.int32, (blk, LANES), 1).astype(F32)
    past = lane < qi.astype(F32)
    gate = jnp.where(past, gate, -jnp.inf)
    sel = jnp.zeros((blk, LANES), jnp.bool_)
    for _ in range(MOBA_TOPK):
        mx = jnp.max(gate, axis=1, keepdims=True)
        idx = jnp.min(jnp.where((gate == mx) & past, lane, float(LANES)), axis=1, keepdims=True)
        pick = lane == idx
        sel = sel | pick
        gate = jnp.where(pick, -jnp.inf, gate)
    bias = jnp.where(sel, 0.0, MASK_BIAS).astype(MXU)
    qaug = jnp.concatenate([q, bias], axis=1)

    _flash_init(m_ref, acc_ref)

    def past_step(g, c):
        r0 = pl.multiple_of(g * grp, grp)
        s = lax.dot_general(qaug, kaug_ref[pl.ds(r0, grp), :], NT, preferred_element_type=F32) * ATT_SCALE
        _flash_step(s, vaug_ref[pl.ds(r0, grp), :], m_ref, acc_ref)
        return c

    lax.fori_loop(0, (qi + MOBA_KV_GROUP - 1) // MOBA_KV_GROUP, past_step, 0)
    r0 = pl.multiple_of(qi * blk, blk)
    s = lax.dot_general(q, k_ref[pl.ds(r0, blk), :].astype(MXU), NT, preferred_element_type=F32) * ATT_SCALE
    row = lax.broadcasted_iota(jnp.int32, (blk, blk), 0)
    col = lax.broadcasted_iota(jnp.int32, (blk, blk), 1)
    _flash_step(jnp.where(col <= row, s, NEG), vaug_ref[pl.ds(r0, blk), :], m_ref, acc_ref)
    o_ref[...] = _flash_out(acc_ref).astype(o_ref.dtype)


def _moba(p1):
    s = p1.shape[0]
    nb = s // MOBA_BLOCK
    assert nb <= LANES and nb % MOBA_KV_GROUP == 0
    kcol, vcol = P1_KA // HEAD_DIM, P1_VA // HEAD_DIM
    return pl.pallas_call(
        functools.partial(_moba_kernel, nb=nb),
        grid=(MOBA_HEADS, nb),
        in_specs=[pl.BlockSpec((MOBA_BLOCK, HEAD_DIM), lambda h, i: (i, h)),
                  pl.BlockSpec((s, HEAD_DIM), lambda h, i: (0, kcol + h)),
                  pl.BlockSpec((s, HEAD_DIM), lambda h, i: (0, vcol + h))],
        out_specs=pl.BlockSpec((MOBA_BLOCK, HEAD_DIM), lambda h, i: (i, h)),
        out_shape=jax.ShapeDtypeStruct((s, MOBA_WIDTH), MXU),
        scratch_shapes=[pltpu.VMEM((s, 2 * HEAD_DIM), MXU), pltpu.VMEM((s, 2 * HEAD_DIM), MXU),
                        pltpu.VMEM((LANES, HEAD_DIM), F32),
                        pltpu.VMEM((MOBA_BLOCK, HEAD_DIM), F32), pltpu.VMEM((MOBA_BLOCK, 2 * HEAD_DIM), F32)],
        compiler_params=_cparams(("arbitrary", "arbitrary")),
        name="moba_attention",
    )(p1, p1, p1)


def _ssd_kernel(dtb_ref, alog_ref, dskip_ref,
                xs_ref, bm_ref, cm_ref, z_ref, dt_ref, cwx_ref, cwb_ref, cwc_ref, cbx_ref, cbb_ref, cbc_ref,
                ng_ref, o_ref, xbuf, bbuf, cbuf, state_ref, ybuf):
    L = SSM_CHUNK
    W = SSM_GROUP_WIDTH
    J = SSM_HEADS_PER_GROUP
    P = SSM_HEAD_DIM
    g = pl.program_id(0)
    c = pl.program_id(1)

    @pl.when(c == 0)
    def _reset():
        xbuf[0:SUBLANES, :] = jnp.zeros((SUBLANES, W), F32)
        bbuf[0:SUBLANES, :] = jnp.zeros((SUBLANES, SSM_STATE), F32)
        cbuf[0:SUBLANES, :] = jnp.zeros((SUBLANES, SSM_STATE), F32)
        state_ref[...] = jnp.zeros_like(state_ref)

    def conv_silu(buf, raw_ref, w_ref, b_ref):
        buf[SUBLANES:SUBLANES + L, :] = raw_ref[...]
        acc = b_ref[...]
        for i in range(SSM_CONV):
            lo = SUBLANES - (SSM_CONV - 1) + i
            acc = acc + w_ref[i:i + 1, :] * buf[lo:lo + L, :]
        buf[0:SUBLANES, :] = buf[L:L + SUBLANES, :]
        return _silu(acc)

    xs = conv_silu(xbuf, xs_ref, cwx_ref, cbx_ref)
    bm = conv_silu(bbuf, bm_ref, cwb_ref, cbb_ref)
    cm = conv_silu(cbuf, cm_ref, cwc_ref, cbc_ref)

    lane = lax.broadcasted_iota(jnp.int32, (1, LANES), 1)
    dtb = jnp.zeros((1, LANES), F32)
    alog = jnp.full((1, LANES), -jnp.inf, F32)
    for j in range(J):
        dtb = jnp.where(lane == j, dtb_ref[g * J + j], dtb)
        alog = jnp.where(lane == j, alog_ref[g * J + j], alog)
    x = dt_ref[...] + dtb
    dt = jnp.maximum(x, 0.0) + jnp.log(1.0 + jnp.exp(-jnp.abs(x)))
    dt = jnp.where(lane < J, dt, 0.0)
    a = dt * (-jnp.exp(alog))

    er = lax.broadcasted_iota(jnp.int32, (LANES, W), 0)
    ec = lax.broadcasted_iota(jnp.int32, (LANES, W), 1)
    expand = ((ec >> 6) == er).astype(jnp.bfloat16)
    tr = lax.broadcasted_iota(jnp.int32, (L, L), 0)
    tc = lax.broadcasted_iota(jnp.int32, (L, L), 1)
    tril = tr >= tc
    tril_b = tril.astype(jnp.bfloat16)
    dt_e = _dot3(expand, dt, left=False)
    a_e = _dot3(expand, a, left=False)
    acs = _dot3(tril_b, a_e, left=True)
    acs_t = acs.T
    a_last = acs[L - 1:L, :]

    xdt = xs * dt_e
    cb = lax.dot_general(cm.astype(MXU), bm.astype(MXU), NT, preferred_element_type=F32)
    for j in range(J):
        colv = acs[:, j * P:j * P + 1]
        rowv = acs_t[j * P:j * P + 1, :]
        dec = jnp.exp(jnp.where(tril, colv - rowv, -jnp.inf))
        ybuf[:, j * P:(j + 1) * P] = jnp.dot((cb * dec).astype(MXU), xdt[:, j * P:(j + 1) * P].astype(MXU),
                                             preferred_element_type=F32)
    st_old = state_ref[...]
    xdte = (xdt * jnp.exp(a_last - acs)).astype(MXU)
    st_new = jnp.dot(bm.T.astype(MXU), xdte, preferred_element_type=F32)
    y_off = jnp.dot(cm.astype(MXU), st_old.astype(MXU), preferred_element_type=F32) * jnp.exp(acs)
    state_ref[...] = st_old * jnp.exp(a_last) + st_new

    lane_w = lax.broadcasted_iota(jnp.int32, (1, W), 1)
    dsk = jnp.zeros((1, W), F32)
    for j in range(J):
        dsk = jnp.where((lane_w >> 6) == j, dskip_ref[g * J + j], dsk)
    y = ybuf[...] + y_off + xs * dsk
    y = y * _silu(z_ref[...])
    y = y * lax.rsqrt(jnp.mean(y * y, axis=1, keepdims=True) + RMS_EPS)
    o_ref[...] = (y * ng_ref[...]).astype(o_ref.dtype)


def _ssd(p2, p3, conv_w, conv_b, dt_bias, a_log, d_skip, norm_g):
    s = p2.shape[0]
    L, W, N = SSM_CHUNK, SSM_GROUP_WIDTH, SSM_STATE
    xs0 = P2_XBC // W
    bm0 = (P2_XBC + SSM_D_INNER) // N
    cm0 = bm0 + SSM_GROUPS
    z0 = P2_Z // W
    cb2 = conv_b.reshape(1, SSM_CONV_CH)
    ng2 = norm_g.reshape(1, SSM_D_INNER)
    grid_spec = pltpu.PrefetchScalarGridSpec(
        num_scalar_prefetch=3,
        grid=(SSM_GROUPS, s // L),
        in_specs=[
            pl.BlockSpec((L, W), lambda g, c, *_: (c, xs0 + g)),
            pl.BlockSpec((L, N), lambda g, c, *_: (c, bm0 + g)),
            pl.BlockSpec((L, N), lambda g, c, *_: (c, cm0 + g)),
            pl.BlockSpec((L, W), lambda g, c, *_: (c, z0 + g)),
            pl.BlockSpec((L, LANES), lambda g, c, *_: (c, g)),
            pl.BlockSpec((SSM_CONV, W), lambda g, c, *_: (0, g)),
            pl.BlockSpec((SSM_CONV, N), lambda g, c, *_: (0, SSM_D_INNER // N + g)),
            pl.BlockSpec((SSM_CONV, N), lambda g, c, *_: (0, SSM_D_INNER // N + SSM_GROUPS + g)),
            pl.BlockSpec((1, W), lambda g, c, *_: (0, g)),
            pl.BlockSpec((1, N), lambda g, c, *_: (0, SSM_D_INNER // N + g)),
            pl.BlockSpec((1, N), lambda g, c, *_: (0, SSM_D_INNER // N + SSM_GROUPS + g)),
            pl.BlockSpec((1, W), lambda g, c, *_: (0, g)),
        ],
        out_specs=pl.BlockSpec((L, W), lambda g, c, *_: (c, g)),
        scratch_shapes=[pltpu.VMEM((SUBLANES + L, W), F32), pltpu.VMEM((SUBLANES + L, N), F32),
                        pltpu.VMEM((SUBLANES + L, N), F32), pltpu.VMEM((N, W), F32), pltpu.VMEM((L, W), F32)],
    )
    return pl.pallas_call(
        _ssd_kernel,
        grid_spec=grid_spec,
        out_shape=jax.ShapeDtypeStruct((s, SSM_D_INNER), MXU),
        compiler_params=_cparams(("arbitrary", "arbitrary")),
        name="ssd_mixer",
    )(dt_bias, a_log, d_skip, p2, p2, p2, p2, p3, conv_w, conv_w, conv_w, cb2, cb2, cb2, ng2)


def _nsa_compress_kernel(t_ref, pos_ref, w1_ref, w2_ref, o_ref):
    half = NSA_CMP_STRIDE * HEAD_DIM
    t = t_ref[0].astype(F32)
    pos = pos_ref[0]
    lo = (t + pos[:, :half]).astype(MXU)
    hi = (t + pos[:, half:]).astype(MXU)
    w1 = w1_ref[0]
    a = jnp.dot(lo, w1[:half].astype(MXU), preferred_element_type=F32)
    b = jnp.dot(hi, w1[half:].astype(MXU), preferred_element_type=F32)
    n = t.shape[0]
    pre = a + pltpu.roll(b, n - 1, 0)
    act = jax.nn.gelu(pre, approximate=True)
    o_ref[0] = jnp.dot(act.astype(MXU), w2_ref[0].astype(MXU), preferred_element_type=F32).astype(o_ref.dtype)


def _nsa_compress(p1, pos_k, w1_k, w2_k, pos_v, w1_v, w2_v):
    s = p1.shape[0]
    n_str = s // NSA_CMP_STRIDE
    kv = p1[:, P1_KVN:P1_KVN + 2 * NSA_KV_WIDTH]
    t4 = kv.reshape(s, 4, HEAD_DIM).transpose(1, 0, 2).reshape(4, n_str, NSA_CMP_STRIDE * HEAD_DIM)
    pos = jnp.stack([pos_k, pos_v]).reshape(2, 1, NSA_CMP_LEN * HEAD_DIM)
    w1 = jnp.stack([w1_k, w1_v])
    w2 = jnp.stack([w2_k, w2_v])
    return pl.pallas_call(
        _nsa_compress_kernel,
        grid=(4,),
        in_specs=[pl.BlockSpec((1, n_str, NSA_CMP_STRIDE * HEAD_DIM), lambda i: (i, 0, 0)),
                  pl.BlockSpec((1, 1, NSA_CMP_LEN * HEAD_DIM), lambda i: (i // 2, 0, 0)),
                  pl.BlockSpec((1, NSA_CMP_LEN * HEAD_DIM, HEAD_DIM), lambda i: (i // 2, 0, 0)),
                  pl.BlockSpec((1, HEAD_DIM, HEAD_DIM), lambda i: (i // 2, 0, 0))],
        out_specs=pl.BlockSpec((1, n_str, HEAD_DIM), lambda i: (i, 0, 0)),
        out_shape=jax.ShapeDtypeStruct((4, n_str, HEAD_DIM), MXU),
        compiler_params=_cparams(("arbitrary",)),
        name="nsa_compress",
    )(t4, pos, w1, w2)


def _nsa_cmp_kernel(q_ref, kc_ref, vc_ref, gn_ref, o_ref, selb_ref, *, n_sel):
    tq = ATT_TILE
    qi = pl.program_id(1)
    kc = kc_ref[0].astype(MXU)
    vc = vc_ref[0].astype(MXU)
    nc = kc.shape[0]
    pos = qi * tq + lax.broadcasted_iota(jnp.int32, (tq, nc), 0)
    cidx = lax.broadcasted_iota(jnp.int32, (tq, nc), 1)
    valid = (cidx * NSA_CMP_STRIDE + NSA_CMP_LEN - 1 <= pos) & (cidx < nc - 1)
    orow = lax.broadcasted_iota(jnp.int32, (nc, LANES), 0) * NSA_CMP_STRIDE
    ocol = lax.broadcasted_iota(jnp.int32, (nc, LANES), 1) * NSA_SEL_BLOCK
    overlap = ((orow < ocol + NSA_SEL_BLOCK) & (orow + NSA_CMP_LEN > ocol)).astype(MXU)
    gates = _sigmoid(gn_ref[...])
    imp = jnp.zeros((tq, LANES), F32)
    for j in range(NSA_HEADS_PER_GROUP):
        q = q_ref[:, j * HEAD_DIM:(j + 1) * HEAD_DIM].astype(MXU)
        s = lax.dot_general(q, kc, NT, preferred_element_type=F32) * ATT_SCALE
        s = jnp.where(valid, s, -jnp.inf)
        m = jnp.max(s, axis=1, keepdims=True)
        m = jnp.where(m > -jnp.inf, m, 0.0)
        e = jnp.exp(s - m)
        den = jnp.sum(e, axis=1, keepdims=True)
        p = (e / jnp.where(den > 0, den, 1.0)).astype(MXU)
        o = jnp.dot(p, vc, preferred_element_type=F32)
        imp = imp + jnp.dot(p, overlap, preferred_element_type=F32)
        o_ref[:, j * HEAD_DIM:(j + 1) * HEAD_DIM] = o * gates[:, N_BRANCHES * j:N_BRANCHES * j + 1]

    lane = lax.broadcasted_iota(jnp.int32, (tq, LANES), 1).astype(F32)
    cur = ((qi * tq + lax.broadcasted_iota(jnp.int32, (tq, LANES), 0)) >> 6).astype(F32)
    allowed = lane <= cur
    forced = (lane == 0.0) | (lane == cur) | (lane == cur - 1.0)
    val = jnp.where(forced, jnp.inf, jnp.where(allowed, imp, -jnp.inf))
    val = jnp.where(lane < float(n_sel), val, -jnp.inf)

    def pick_round(_, c):
        val, sel = c
        mx = jnp.max(val, axis=1, keepdims=True)
        idx = jnp.min(jnp.where(val == mx, lane, float(LANES)), axis=1, keepdims=True)
        pick = lane == idx
        sel = jnp.where(pick & allowed, 1.0, sel)
        val = jnp.where(pick, -jnp.inf, val)
        return val, sel

    _, sel = lax.fori_loop(0, min(NSA_TOPN, n_sel), pick_round, (val, jnp.zeros((tq, LANES), F32)))
    selb_ref[0] = jnp.where(sel > 0, 0.0, MASK_BIAS).astype(selb_ref.dtype)


def _nsa_cmp(p1, p3, kvc):
    s = p1.shape[0]
    n_str = s // NSA_CMP_STRIDE
    n_sel = s // NSA_SEL_BLOCK
    assert n_sel <= LANES
    gw = NSA_HEADS_PER_GROUP * HEAD_DIM
    q0 = P1_QN // gw
    return pl.pallas_call(
        functools.partial(_nsa_cmp_kernel, n_sel=n_sel),
        grid=(NSA_KV_GROUPS, s // ATT_TILE),
        in_specs=[pl.BlockSpec((ATT_TILE, gw), lambda g, i: (i, q0 + g)),
                  pl.BlockSpec((1, n_str, HEAD_DIM), lambda g, i: (g, 0, 0)),
                  pl.BlockSpec((1, n_str, HEAD_DIM), lambda g, i: (NSA_KV_GROUPS + g, 0, 0)),
                  pl.BlockSpec((ATT_TILE, LANES), lambda g, i: (i, SSM_GROUPS + g))],
        out_specs=[pl.BlockSpec((ATT_TILE, gw), lambda g, i: (i, g)),
                   pl.BlockSpec((1, ATT_TILE, LANES), lambda g, i: (g, i, 0))],
        out_shape=[jax.ShapeDtypeStruct((s, NSA_WIDTH), F32),
                   jax.ShapeDtypeStruct((NSA_KV_GROUPS, s, LANES), MXU)],
        compiler_params=_cparams(("arbitrary", "arbitrary")),
        name="nsa_compressed_attention",
    )(p1, kvc, kvc, p3)


def _nsa_sel_kernel(q_ref, selb_ref, k_ref, v_ref, gn_ref, prev_ref, o_ref, kaug_ref, vaug_ref, qaug_ref,
                    m_ref, acc_ref, *, n_tiles):
    tq = ATT_TILE
    J = NSA_HEADS_PER_GROUP
    qi = pl.program_id(1)

    @pl.when(qi == 0)
    def _build_keys():
        lane = lax.broadcasted_iota(jnp.int32, (tq, LANES), 1)
        rowi = lax.broadcasted_iota(jnp.int32, (tq, LANES), 0)
        ones = jnp.ones((tq, HEAD_DIM), vaug_ref.dtype)

        def body(t, c):
            r0 = pl.multiple_of(t * tq, tq)
            kaug_ref[pl.ds(r0, tq), 0:HEAD_DIM] = k_ref[pl.ds(r0, tq), :].astype(kaug_ref.dtype)
            kaug_ref[pl.ds(r0, tq), HEAD_DIM:2 * HEAD_DIM] = (
                lane == ((t * tq + rowi) >> 6)).astype(kaug_ref.dtype)
            vaug_ref[pl.ds(r0, tq), 0:HEAD_DIM] = v_ref[pl.ds(r0, tq), :].astype(vaug_ref.dtype)
            vaug_ref[pl.ds(r0, tq), HEAD_DIM:2 * HEAD_DIM] = ones
            return c

        lax.fori_loop(0, n_tiles, body, 0)

    selb = selb_ref[0]
    for j in range(J):
        qaug_ref[j * tq:(j + 1) * tq, 0:HEAD_DIM] = q_ref[:, j * HEAD_DIM:(j + 1) * HEAD_DIM].astype(qaug_ref.dtype)
        qaug_ref[j * tq:(j + 1) * tq, HEAD_DIM:2 * HEAD_DIM] = selb
    _flash_init(m_ref, acc_ref)

    def scores(t):
        r0 = pl.multiple_of(t * tq, tq)
        return lax.dot_general(qaug_ref[...], kaug_ref[pl.ds(r0, tq), :], NT,
                               preferred_element_type=F32) * ATT_SCALE, vaug_ref[pl.ds(r0, tq), :]

    def past_step(t, c):
        s, v = scores(t)
        _flash_step(s, v, m_ref, acc_ref)
        return c

    lax.fori_loop(0, qi, past_step, 0)
    s, v = scores(qi)
    row = lax.broadcasted_iota(jnp.int32, (J * tq, tq), 0) & (tq - 1)
    col = lax.broadcasted_iota(jnp.int32, (J * tq, tq), 1)
    _flash_step(jnp.where(col <= row, s, NEG), v, m_ref, acc_ref)
    o = _flash_out(acc_ref)
    gates = _sigmoid(gn_ref[...])
    for j in range(J):
        o_ref[:, j * HEAD_DIM:(j + 1) * HEAD_DIM] = (
            prev_ref[:, j * HEAD_DIM:(j + 1) * HEAD_DIM]
            + o[j * tq:(j + 1) * tq] * gates[:, N_BRANCHES * j + 1:N_BRANCHES * j + 2])


def _nsa_sel(p1, p3, selb, prev):
    s = p1.shape[0]
    gw = NSA_HEADS_PER_GROUP * HEAD_DIM
    q0 = P1_QN // gw
    k0 = (P1_KVN + 2 * NSA_KV_WIDTH) // HEAD_DIM
    v0 = k0 + NSA_KV_GROUPS
    return pl.pallas_call(
        functools.partial(_nsa_sel_kernel, n_tiles=s // ATT_TILE),
        grid=(NSA_KV_GROUPS, s // ATT_TILE),
        in_specs=[pl.BlockSpec((ATT_TILE, gw), lambda g, i: (i, q0 + g)),
                  pl.BlockSpec((1, ATT_TILE, LANES), lambda g, i: (g, i, 0)),
                  pl.BlockSpec((s, HEAD_DIM), lambda g, i: (0, k0 + g)),
                  pl.BlockSpec((s, HEAD_DIM), lambda g, i: (0, v0 + g)),
                  pl.BlockSpec((ATT_TILE, LANES), lambda g, i: (i, SSM_GROUPS + g)),
                  pl.BlockSpec((ATT_TILE, gw), lambda g, i: (i, g))],
        out_specs=pl.BlockSpec((ATT_TILE, gw), lambda g, i: (i, g)),
        out_shape=jax.ShapeDtypeStruct((s, NSA_WIDTH), F32),
        scratch_shapes=[pltpu.VMEM((s, 2 * HEAD_DIM), MXU), pltpu.VMEM((s, 2 * HEAD_DIM), MXU),
                        pltpu.VMEM((NSA_HEADS_PER_GROUP * ATT_TILE, 2 * HEAD_DIM), MXU),
                        pltpu.VMEM((NSA_HEADS_PER_GROUP * ATT_TILE, HEAD_DIM), F32),
                        pltpu.VMEM((NSA_HEADS_PER_GROUP * ATT_TILE, 2 * HEAD_DIM), F32)],
        compiler_params=_cparams(("arbitrary", "arbitrary")),
        name="nsa_selected_attention",
    )(p1, selb, p1, p1, p3, prev)


def _nsa_win_kernel(q_ref, k_ref, v_ref, gn_ref, prev_ref, o_ref, qst_ref, m_ref, acc_ref):
    tq = ATT_TILE
    J = NSA_HEADS_PER_GROUP
    halo = NSA_WINDOW // tq
    qi = pl.program_id(1)
    for j in range(J):
        qst_ref[j * tq:(j + 1) * tq, :] = q_ref[:, j * HEAD_DIM:(j + 1) * HEAD_DIM].astype(qst_ref.dtype)
    row = lax.broadcasted_iota(jnp.int32, (J * tq, tq), 0) & (tq - 1)
    col = lax.broadcasted_iota(jnp.int32, (J * tq, tq), 1)
    ones = jnp.ones((tq, HEAD_DIM), MXU)
    _flash_init(m_ref, acc_ref)

    def step(t, c):
        r0 = pl.multiple_of(t * tq, tq)
        s = lax.dot_general(qst_ref[...], k_ref[pl.ds(r0, tq), :].astype(MXU), NT,
                            preferred_element_type=F32) * ATT_SCALE
        diff = (qi - t) * tq + row - col
        s = jnp.where((diff >= 0) & (diff < NSA_WINDOW), s, NEG)
        v_aug = jnp.concatenate([v_ref[pl.ds(r0, tq), :].astype(MXU), ones], axis=1)
        _flash_step(s, v_aug, m_ref, acc_ref)
        return c

    lax.fori_loop(jnp.maximum(qi - halo, 0), qi + 1, step, 0)
    o = _flash_out(acc_ref)
    gates = _sigmoid(gn_ref[...])
    for j in range(J):
        o_ref[:, j * HEAD_DIM:(j + 1) * HEAD_DIM] = (
            prev_ref[:, j * HEAD_DIM:(j + 1) * HEAD_DIM]
            + o[j * tq:(j + 1) * tq] * gates[:, N_BRANCHES * j + 2:N_BRANCHES * j + 3])


def _nsa_win(p1, p3, prev):
    s = p1.shape[0]
    gw = NSA_HEADS_PER_GROUP * HEAD_DIM
    q0 = P1_QN // gw
    k0 = (P1_KVN + 4 * NSA_KV_WIDTH) // HEAD_DIM
    v0 = k0 + NSA_KV_GROUPS
    return pl.pallas_call(
        _nsa_win_kernel,
        grid=(NSA_KV_GROUPS, s // ATT_TILE),
        in_specs=[pl.BlockSpec((ATT_TILE, gw), lambda g, i: (i, q0 + g)),
                  pl.BlockSpec((s, HEAD_DIM), lambda g, i: (0, k0 + g)),
                  pl.BlockSpec((s, HEAD_DIM), lambda g, i: (0, v0 + g)),
                  pl.BlockSpec((ATT_TILE, LANES), lambda g, i: (i, SSM_GROUPS + g)),
                  pl.BlockSpec((ATT_TILE, gw), lambda g, i: (i, g))],
        out_specs=pl.BlockSpec((ATT_TILE, gw), lambda g, i: (i, g)),
        out_shape=jax.ShapeDtypeStruct((s, NSA_WIDTH), F32),
        scratch_shapes=[pltpu.VMEM((NSA_HEADS_PER_GROUP * ATT_TILE, HEAD_DIM), MXU),
                        pltpu.VMEM((NSA_HEADS_PER_GROUP * ATT_TILE, HEAD_DIM), F32),
                        pltpu.VMEM((NSA_HEADS_PER_GROUP * ATT_TILE, 2 * HEAD_DIM), F32)],
        compiler_params=_cparams(("arbitrary", "arbitrary")),
        name="nsa_window_attention",
    )(p1, p1, p1, p3, prev)


def _merge_kernel(ya_ref, yb_ref, yc_ref, g0_ref, g1_ref, g2_ref, wa_ref, wb_ref, wc_ref, o_ref):
    def branch(y_ref, w_ref, g_ref):
        prod = jnp.dot(y_ref[...].astype(MXU), w_ref[...].astype(MXU), preferred_element_type=F32)
        return _sigmoid(g_ref[...]) * prod

    o_ref[...] = (branch(ya_ref, wa_ref, g0_ref) + branch(yb_ref, wb_ref, g1_ref)
                  + branch(yc_ref, wc_ref, g2_ref)).astype(o_ref.dtype)


def _merge(ya, yb, yc, p2, wa, wb, wc, *, tm=512, tn=512):
    s = ya.shape[0]
    d = wa.shape[1]
    g0 = P2_GM // tn
    gstep = d // tn
    return pl.pallas_call(
        _merge_kernel,
        grid=(d // tn, s // tm),
        in_specs=[pl.BlockSpec((tm, ya.shape[1]), lambda j, i: (i, 0)),
                  pl.BlockSpec((tm, yb.shape[1]), lambda j, i: (i, 0)),
                  pl.BlockSpec((tm, yc.shape[1]), lambda j, i: (i, 0)),
                  pl.BlockSpec((tm, tn), lambda j, i: (i, g0 + j)),
                  pl.BlockSpec((tm, tn), lambda j, i: (i, g0 + gstep + j)),
                  pl.BlockSpec((tm, tn), lambda j, i: (i, g0 + 2 * gstep + j)),
                  pl.BlockSpec((wa.shape[0], tn), lambda j, i: (0, j)),
                  pl.BlockSpec((wb.shape[0], tn), lambda j, i: (0, j)),
                  pl.BlockSpec((wc.shape[0], tn), lambda j, i: (0, j))],
        out_specs=pl.BlockSpec((tm, tn), lambda j, i: (i, j)),
        out_shape=jax.ShapeDtypeStruct((s, d), MXU),
        compiler_params=_cparams(("arbitrary", "arbitrary")),
        name="branch_merge",
    )(ya, yb, yc, p2, p2, p2, wa, wb, wc)


def _layer_norm_rows(x, g, b):
    xc = x - jnp.mean(x, axis=1, keepdims=True)
    var = jnp.mean(xc * xc, axis=1, keepdims=True)
    return xc * lax.rsqrt(var + LN_EPS) * g + b


def _pack_halves(y):
    half = y.shape[1] // 2
    bits = lax.bitcast_convert_type(y.astype(jnp.bfloat16).astype(F32), jnp.int32)
    return ((bits[:, :half] >> 16) & jnp.int32(0xFFFF)) | (bits[:, half:] & jnp.int32(-65536))


def _unpack_halves(w):
    lo = lax.bitcast_convert_type(w << 16, F32)
    hi = lax.bitcast_convert_type(w & jnp.int32(-65536), F32)
    return lo.astype(MXU), hi.astype(MXU)


def _wout_ln_kernel(m_ref, w_ref, h_ref, g_ref, b_ref, o_ref, ob_ref, op_ref):
    acc = jnp.dot(m_ref[...].astype(MXU), w_ref[...].astype(MXU), preferred_element_type=F32)
    y = _layer_norm_rows(DEEPNORM_ALPHA * h_ref[...] + acc, g_ref[...], b_ref[...])
    o_ref[...] = y
    ob_ref[...] = y.astype(ob_ref.dtype)
    op_ref[...] = _pack_halves(y)


def _wout_ln(merged, w_out, h, ln_g, ln_b, *, tm=256):
    s, d = h.shape
    row = pl.BlockSpec((tm, d), lambda i: (i, 0))
    vec = pl.BlockSpec((1, d), lambda i: (0, 0))
    return pl.pallas_call(
        _wout_ln_kernel,
        grid=(s // tm,),
        in_specs=[row, pl.BlockSpec((d, d), lambda i: (0, 0)), row, vec, vec],
        out_specs=[row, row, pl.BlockSpec((tm, d // 2), lambda i: (i, 0))],
        out_shape=[jax.ShapeDtypeStruct((s, d), F32), jax.ShapeDtypeStruct((s, d), MXU),
                   jax.ShapeDtypeStruct((s, d // 2), jnp.int32)],
        compiler_params=_cparams(("arbitrary",)),
        name="out_proj_layernorm",
    )(merged, w_out.astype(MXU), h, ln_g.reshape(1, d), ln_b.reshape(1, d))


def _dispatch_kernel(dest_ref, hp_ref, init_ref, o_ref, sem):
    del init_ref
    tt = dest_ref.shape[1]
    i = pl.program_id(0)
    n = pl.num_programs(0)
    slot = i % 2

    def row_copy(t, d, sl):
        return pltpu.make_async_copy(hp_ref.at[pl.ds(t, 1)], o_ref.at[pl.ds(d, 1)], sem.at[sl])

    def issue(r, c):
        for k in range(TOP_K):
            row_copy(i * tt + r, dest_ref[k, r], slot).start()
        return c

    lax.fori_loop(0, tt, issue, 0)

    def drain(sl):
        def body(r, c):
            row_copy(0, 0, sl).wait()
            return c

        lax.fori_loop(0, tt * TOP_K, body, 0)

    @pl.when(i > 0)
    def _():
        drain(1 - slot)

    @pl.when(i == n - 1)
    def _():
        drain(slot)


def _dispatch(hp, dest, n_rows, *, tt=256):
    s, half = hp.shape
    return pl.pallas_call(
        _dispatch_kernel,
        grid=(s // tt,),
        in_specs=[pl.BlockSpec((TOP_K, tt), lambda i: (0, i), memory_space=pltpu.SMEM),
                  pl.BlockSpec(memory_space=pl.ANY),
                  pl.BlockSpec(memory_space=pl.ANY)],
        out_specs=pl.BlockSpec(memory_space=pl.ANY),
        out_shape=jax.ShapeDtypeStruct((n_rows, half), jnp.int32),
        scratch_shapes=[pltpu.SemaphoreType.DMA((2,))],
        input_output_aliases={2: 0},
        compiler_params=_cparams(("arbitrary",)),
        name="moe_dispatch",
    )(dest, hp, jnp.zeros((n_rows, half), jnp.int32))


def _combine_kernel(dcur_ref, dnxt_ref, w_ref, h_ref, sh_ref, g_ref, b_ref, y_ref, o_ref, ob_ref, buf, sem):
    tt = dcur_ref.shape[1]
    i = pl.program_id(0)
    n = pl.num_programs(0)
    slot = i % 2

    def row_copy(d, k, r, sl):
        return pltpu.make_async_copy(y_ref.at[pl.ds(d, 1)], buf.at[sl, k, pl.ds(r, 1)], sem.at[sl])

    def issue(d_ref, sl):
        def body(r, c):
            for k in range(TOP_K):
                row_copy(d_ref[k, r], k, r, sl).start()
            return c

        lax.fori_loop(0, tt, body, 0)

    @pl.when(i == 0)
    def _():
        issue(dcur_ref, slot)

    @pl.when(i + 1 < n)
    def _():
        issue(dnxt_ref, 1 - slot)

    def drain(r, c):
        row_copy(0, 0, 0, slot).wait()
        return c

    lax.fori_loop(0, tt * TOP_K, drain, 0)
    w = w_ref[...]
    routed = w[:, 0:1] * buf[slot, 0]
    for k in range(1, TOP_K):
        routed = routed + w[:, k:k + 1] * buf[slot, k]
    y = _layer_norm_rows(DEEPNORM_ALPHA * h_ref[...] + (routed + sh_ref[...]), g_ref[...], b_ref[...])
    o_ref[...] = y
    ob_ref[...] = y.astype(ob_ref.dtype)


def _combine_ln(h, y_rows, dest, top_w, shared, ln_g, ln_b, *, tt=128):
    s, d = h.shape
    n = s // tt
    row = pl.BlockSpec((tt, d), lambda i: (i, 0))
    vec = pl.BlockSpec((1, d), lambda i: (0, 0))
    return pl.pallas_call(
        _combine_kernel,
        grid=(n,),
        in_specs=[pl.BlockSpec((TOP_K, tt), lambda i: (0, i), memory_space=pltpu.SMEM),
                  pl.BlockSpec((TOP_K, tt), lambda i: (0, jnp.minimum(i + 1, n - 1)), memory_space=pltpu.SMEM),
                  pl.BlockSpec((tt, TOP_K), lambda i: (i, 0)),
                  row, row, vec, vec,
                  pl.BlockSpec(memory_space=pl.ANY)],
        out_specs=[row, row],
        out_shape=[jax.ShapeDtypeStruct((s, d), F32), jax.ShapeDtypeStruct((s, d), MXU)],
        scratch_shapes=[pltpu.VMEM((2, TOP_K, tt, d), F32), pltpu.SemaphoreType.DMA((2,))],
        compiler_params=_cparams(("arbitrary",)),
        name="moe_combine_layernorm",
    )(dest, dest, top_w.T, h, shared, ln_g.reshape(1, d), ln_b.reshape(1, d), y_rows)


def _router_kernel(h_ref, wr_ref, rb_ref, e_ref, w_ref, pos_ref, cnt_ref, carry_ref):
    tq = h_ref.shape[0]
    i = pl.program_id(0)

    @pl.when(i == 0)
    def _reset():
        carry_ref[...] = jnp.zeros_like(carry_ref)

    logits = lax.dot_general(wr_ref[...].astype(MXU), h_ref[...].astype(MXU), NT, preferred_element_type=F32)
    scores = _sigmoid(logits)
    biased = scores + rb_ref[...]
    G, PG = N_EXPERT_GROUPS, EXPERTS_PER_GROUP
    sub = lax.broadcasted_iota(jnp.int32, (PG, tq), 0).astype(F32)
    gi = lax.broadcasted_iota(jnp.int32, (G, tq), 0).astype(F32)
    gs = jnp.zeros((G, tq), F32)
    for g in range(G):
        blk = biased[g * PG:(g + 1) * PG, :]
        m1 = jnp.max(blk, axis=0, keepdims=True)
        i1 = jnp.min(jnp.where(blk == m1, sub, float(PG)), axis=0, keepdims=True)
        m2 = jnp.max(jnp.where(sub == i1, -jnp.inf, blk), axis=0, keepdims=True)
        gs = jnp.where(gi == float(g), m1 + m2, gs)
    keep = jnp.zeros((G, tq), F32)
    for _ in range(TOPK_GROUPS):
        mx = jnp.max(gs, axis=0, keepdims=True)
        idx = jnp.min(jnp.where(gs == mx, gi, float(G)), axis=0, keepdims=True)
        pick = gi == idx
        keep = jnp.where(pick, 1.0, keep)
        gs = jnp.where(pick, -jnp.inf, gs)
    val = jnp.concatenate(
        [jnp.where(keep[g:g + 1, :] > 0, biased[g * PG:(g + 1) * PG, :], -jnp.inf) for g in range(G)], axis=0)
    ei = lax.broadcasted_iota(jnp.int32, (N_EXPERTS, tq), 0).astype(F32)
    picks, svals = [], []
    sel = jnp.zeros((N_EXPERTS, tq), F32)
    for r in range(TOP_K):
        mx = jnp.max(val, axis=0, keepdims=True)
        idx = jnp.min(jnp.where(val == mx, ei, float(N_EXPERTS)), axis=0, keepdims=True)
        pick = ei == idx
        picks.append(pick)
        svals.append(jnp.sum(jnp.where(pick, scores, 0.0), axis=0, keepdims=True))
        e_ref[r:r + 1, :] = idx.astype(jnp.int32)
        sel = jnp.where(pick, 1.0, sel)
        val = jnp.where(pick, -jnp.inf, val)
    wsum = svals[0]
    for r in range(1, TOP_K):
        wsum = wsum + svals[r]
    tr = lax.broadcasted_iota(jnp.int32, (tq, tq), 0)
    tc = lax.broadcasted_iota(jnp.int32, (tq, tq), 1)
    before = (tr < tc).astype(jnp.bfloat16)
    prefix = jnp.dot(sel.astype(jnp.bfloat16), before, preferred_element_type=F32)
    pos = carry_ref[:, 0:1] + prefix
    for r in range(TOP_K):
        w_ref[r:r + 1, :] = svals[r] / wsum * ROUTED_SCALE
        pos_ref[r:r + 1, :] = jnp.sum(jnp.where(picks[r], pos, 0.0), axis=0, keepdims=True).astype(jnp.int32)
    total = carry_ref[...] + jnp.sum(sel, axis=1, keepdims=True)
    carry_ref[...] = total
    cnt_ref[...] = total


def _router(hb, w_router, router_bias, *, tq=256):
    s, d = hb.shape
    row = pl.BlockSpec((TOP_K, tq), lambda i: (0, i))
    return pl.pallas_call(
        _router_kernel,
        grid=(s // tq,),
        in_specs=[pl.BlockSpec((tq, d), lambda i: (i, 0)),
                  pl.BlockSpec((N_EXPERTS, d), lambda i: (0, 0)),
                  pl.BlockSpec((N_EXPERTS, 1), lambda i: (0, 0))],
        out_specs=[row, row, row, pl.BlockSpec((N_EXPERTS, LANES), lambda i: (0, 0))],
        out_shape=[jax.ShapeDtypeStruct((TOP_K, s), jnp.int32), jax.ShapeDtypeStruct((TOP_K, s), F32),
                   jax.ShapeDtypeStruct((TOP_K, s), jnp.int32), jax.ShapeDtypeStruct((N_EXPERTS, LANES), F32)],
        scratch_shapes=[pltpu.VMEM((N_EXPERTS, LANES), F32)],
        compiler_params=_cparams(("arbitrary",)),
        name="moe_router",
    )(hb, w_router.T, router_bias.reshape(N_EXPERTS, 1))


def _expert_kernel(te_ref, nu_ref, x_ref, wg_ref, wu_ref, wd_ref, o_ref):
    i = pl.program_id(0)

    @pl.when(i < nu_ref[0])
    def _compute():
        xa, xb = _unpack_halves(x_ref[...])
        half = xa.shape[1]

        def up(w_ref):
            w = w_ref[0]
            return (jnp.dot(xa, w[:half].astype(MXU), preferred_element_type=F32)
                    + jnp.dot(xb, w[half:].astype(MXU), preferred_element_type=F32))

        hg = up(wg_ref)
        hu = up(wu_ref)
        act = (_silu(hg) * hu).astype(MXU)
        o_ref[...] = jnp.dot(act, wd_ref[0].astype(MXU), preferred_element_type=F32).astype(o_ref.dtype)

    @pl.when(i >= nu_ref[0])
    def _unused():
        o_ref[...] = jnp.zeros_like(o_ref)


def _experts(x_rows, tile_e, n_used, w_gate, w_up, w_down, *, tm):
    n_rows = x_rows.shape[0]
    d, f = w_gate.shape[1:]
    grid_spec = pltpu.PrefetchScalarGridSpec(
        num_scalar_prefetch=2,
        grid=(n_rows // tm,),
        in_specs=[pl.BlockSpec((tm, d // 2), lambda i, te, nu: (jnp.minimum(i, nu[0] - 1), 0)),
                  pl.BlockSpec((1, d, f), lambda i, te, nu: (te[i], 0, 0)),
                  pl.BlockSpec((1, d, f), lambda i, te, nu: (te[i], 0, 0)),
                  pl.BlockSpec((1, f, d), lambda i, te, nu: (te[i], 0, 0))],
        out_specs=pl.BlockSpec((tm, d), lambda i, te, nu: (i, 0)),
    )
    return pl.pallas_call(
        _expert_kernel,
        grid_spec=grid_spec,
        out_shape=jax.ShapeDtypeStruct((n_rows, d), F32),
        compiler_params=_cparams(("arbitrary",), 56),
        name="moe_experts",
    )(tile_e, n_used, x_rows, w_gate, w_up, w_down)


def _moe_ln(h, hb, hp, w_router, router_bias, w_exp_gate, w_exp_up, w_exp_down, w_sh_gate, w_sh_up, w_sh_down,
            ln_g, ln_b):
    s, d = h.shape
    tm = MOE_TILE
    top_e, top_w, top_pos, counts = _router(hb, w_router, router_bias)
    cnt = counts[:, 0].astype(jnp.int32)
    tiles_e = (cnt + tm - 1) // tm
    tile_end = jnp.cumsum(tiles_e)
    row_start = (tile_end - tiles_e) * tm
    experts = jnp.arange(N_EXPERTS, dtype=jnp.int32)
    dest = top_pos + jnp.sum(jnp.where(top_e[..., None] == experts, row_start, 0), axis=-1)
    n_tiles = s * TOP_K // tm + N_EXPERTS
    tile_e = jnp.minimum(jnp.sum(tile_end[None, :] <= jnp.arange(n_tiles, dtype=jnp.int32)[:, None], axis=1),
                         N_EXPERTS - 1).astype(jnp.int32)
    n_used = tile_end[-1:].astype(jnp.int32)
    x_rows = _dispatch(hp, dest, n_tiles * tm)
    y_rows = _experts(x_rows, tile_e, n_used, w_exp_gate, w_exp_up, w_exp_down, tm=tm)
    shared = _experts(hp, jnp.zeros((s // tm,), jnp.int32), jnp.full((1,), s // tm, jnp.int32),
                      w_sh_gate[None], w_sh_up[None], w_sh_down[None], tm=tm)
    return _combine_ln(h, y_rows, dest, top_w, shared, ln_g, ln_b)


def _mixer(hb, w_in, conv_w, conv_b, dt_bias, a_log, d_skip, ssm_norm_g,
           cmp_pos_k, cmp_w1_k, cmp_w2_k, cmp_pos_v, cmp_w1_v, cmp_w2_v, w_br_a, w_br_b, w_br_c):
    w1, w2, w3 = _relayout_w_in(w_in)
    p1 = _matmul(hb, w1, tm=1024, tn=1408, out_dtype=MXU)
    p2 = _matmul(hb, w2, tm=1024, tn=1536, out_dtype=F32)
    p3 = _matmul(hb, w3, tm=1024, tn=P3_COLS, out_dtype=F32)
    y_a = _moba(p1)
    y_b = _ssd(p2, p3, conv_w, conv_b, dt_bias, a_log, d_skip, ssm_norm_g)
    kvc = _nsa_compress(p1, cmp_pos_k, cmp_w1_k, cmp_w2_k, cmp_pos_v, cmp_w1_v, cmp_w2_v)
    y_c, selb = _nsa_cmp(p1, p3, kvc)
    y_c = _nsa_sel(p1, p3, selb, y_c)
    y_c = _nsa_win(p1, p3, y_c)
    return _merge(y_a, y_b, y_c, p2, w_br_a, w_br_b, w_br_c)


def kernel(x, w_in, conv_w, conv_b, dt_bias, a_log, d_skip, ssm_norm_g, cmp_pos_k, cmp_w1_k, cmp_w2_k, cmp_pos_v, cmp_w1_v, cmp_w2_v, w_br_a, w_br_b, w_br_c, w_out, ln1_g, ln1_b, w_router, router_bias, w_exp_gate, w_exp_up, w_exp_down, w_sh_gate, w_sh_up, w_sh_down, ln2_g, ln2_b):
    bsz, s, d = x.shape
    assert bsz == 1
    h = x.reshape(s, d)
    hb = h.astype(MXU)
    for l in range(w_in.shape[0]):
        merged = _mixer(hb, w_in[l], conv_w[l], conv_b[l], dt_bias[l], a_log[l], d_skip[l], ssm_norm_g[l],
                        cmp_pos_k[l], cmp_w1_k[l], cmp_w2_k[l], cmp_pos_v[l], cmp_w1_v[l], cmp_w2_v[l],
                        w_br_a[l], w_br_b[l], w_br_c[l])
        h, hb, hp = _wout_ln(merged, w_out[l], h, ln1_g[l], ln1_b[l])
        h, hb = _moe_ln(h, hb, hp, w_router[l], router_bias[l], w_exp_gate[l], w_exp_up[l], w_exp_down[l],
                        w_sh_gate[l], w_sh_up[l], w_sh_down[l], ln2_g[l], ln2_b[l])
    return h.reshape(bsz, s, d)
```

```python
import functools

import numpy as np
import jax
import jax.numpy as jnp
from jax import lax
from jax.experimental import pallas as pl
from jax.experimental.pallas import tpu as pltpu

F32 = jnp.float32
MXU = jnp.bfloat16

D_MODEL = 2048
DEPTH = 2
HEAD_DIM = 128
MOBA_HEADS = 8
MOBA_WIDTH = MOBA_HEADS * HEAD_DIM
MOBA_BLOCK = 256
MOBA_TOPK = 3
SSM_D_INNER = D_MODEL
SSM_HEAD_DIM = 64
SSM_HEADS = SSM_D_INNER // SSM_HEAD_DIM
SSM_STATE = 128
SSM_GROUPS = 8
SSM_HEADS_PER_GROUP = SSM_HEADS // SSM_GROUPS
SSM_GROUP_WIDTH = SSM_D_INNER // SSM_GROUPS
SSM_CONV = 4
SSM_CHUNK = 256
SSM_CONV_CH = SSM_D_INNER + 2 * SSM_GROUPS * SSM_STATE
NSA_HEADS = 8
NSA_KV_GROUPS = 2
NSA_HEADS_PER_GROUP = NSA_HEADS // NSA_KV_GROUPS
NSA_WIDTH = NSA_HEADS * HEAD_DIM
NSA_KV_WIDTH = NSA_KV_GROUPS * HEAD_DIM
NSA_CMP_LEN = 32
NSA_CMP_STRIDE = 16
NSA_SEL_BLOCK = 64
NSA_TOPN = 16
NSA_WINDOW = 512
N_BRANCHES = 3
N_EXPERTS = 64
N_EXPERT_GROUPS = 8
EXPERTS_PER_GROUP = N_EXPERTS // N_EXPERT_GROUPS
TOPK_GROUPS = 4
TOP_K = 8
D_EXPERT = 512
ROUTED_SCALE = 2.5
DEEPNORM_ALPHA = (2 * DEPTH) ** 0.25
LN_EPS = 1e-5
RMS_EPS = 1e-5
IN_SPLIT_SIZES = (MOBA_WIDTH, MOBA_WIDTH, MOBA_WIDTH,
                  SSM_D_INNER, SSM_CONV_CH, SSM_HEADS,
                  NSA_WIDTH, 6 * NSA_KV_WIDTH, N_BRANCHES * NSA_HEADS,
                  N_BRANCHES * D_MODEL)

LANES = 128
SUBLANES = 8
ATT_TILE = 256
MOE_TILE = 256
ATT_SCALE = HEAD_DIM ** -0.5
MASK_BIAS = -2.0 ** 30
NEG = -1e30

NT = (((1,), (1,)), ((), ()))

W_IN_BULK = 3 * MOBA_WIDTH + SSM_D_INNER + SSM_CONV_CH
P1_QA, P1_KA, P1_VA = 0, 1024, 2048
P1_QN, P1_KVN = 0, 1024
P2_Z, P2_XBC = 0, 2048
P2_GM = 0
P3_COLS = (SSM_GROUPS + NSA_KV_GROUPS) * LANES


def _cparams(semantics, vmem_mb=48):
    return pltpu.CompilerParams(dimension_semantics=semantics, vmem_limit_bytes=vmem_mb * 1024 * 1024)


def _sigmoid(x):
    return 1.0 / (1.0 + jnp.exp(-x))


def _silu(x):
    return x * _sigmoid(x)


def _split3(x):
    hi = x.astype(jnp.bfloat16)
    r1 = x - hi.astype(F32)
    mid = r1.astype(jnp.bfloat16)
    lo = (r1 - mid.astype(F32)).astype(jnp.bfloat16)
    return hi, mid, lo


def _dot3(a_exact, x, left=True):
    acc = None
    for part in _split3(x):
        t = (jnp.dot(a_exact, part, preferred_element_type=F32) if left
             else jnp.dot(part, a_exact, preferred_element_type=F32))
        acc = t if acc is None else acc + t
    return acc


def _mm_kernel(a_ref, b_ref, o_ref):
    o_ref[...] = jnp.dot(a_ref[...].astype(MXU), b_ref[...].astype(MXU),
                         preferred_element_type=F32).astype(o_ref.dtype)


def _matmul(a, b, *, tm, tn, out_dtype):
    m, k = a.shape
    n = b.shape[1]
    assert m % tm == 0 and n % tn == 0
    return pl.pallas_call(
        _mm_kernel,
        grid=(n // tn, m // tm),
        in_specs=[pl.BlockSpec((tm, k), lambda j, i: (i, 0)),
                  pl.BlockSpec((k, tn), lambda j, i: (0, j))],
        out_specs=pl.BlockSpec((tm, tn), lambda j, i: (i, j)),
        out_shape=jax.ShapeDtypeStruct((m, n), out_dtype),
        compiler_params=_cparams(("arbitrary", "arbitrary"), 56),
        name="proj_matmul",
    )(a, b)


def _mm_bulk_kernel(a_ref, b_ref, o_ref, wb_ref):
    @pl.when(pl.program_id(1) == 0)
    def _cast_weights():
        wb_ref[...] = b_ref[...].astype(MXU)

    o_ref[...] = jnp.dot(a_ref[...].astype(MXU), wb_ref[...], preferred_element_type=F32).astype(o_ref.dtype)


def _proj_bulk(a, w_all, layer, *, col0, ncols, tm, tn, out_dtype):
    m, k = a.shape
    assert m % tm == 0 and ncols % tn == 0 and col0 % tn == 0
    return pl.pallas_call(
        _mm_bulk_kernel,
        grid=(ncols // tn, m // tm),
        in_specs=[pl.BlockSpec((tm, k), lambda j, i: (i, 0)),
                  pl.BlockSpec((None, k, tn), lambda j, i: (layer, 0, col0 // tn + j))],
        out_specs=pl.BlockSpec((tm, tn), lambda j, i: (i, j)),
        out_shape=jax.ShapeDtypeStruct((m, ncols), out_dtype),
        scratch_shapes=[pltpu.VMEM((k, tn), MXU)],
        compiler_params=_cparams(("arbitrary", "arbitrary"), 56),
        name="proj_bulk_matmul",
    )(a, w_all)


def _relayout_w_tail(w_tail):
    off = np.concatenate([[0], np.cumsum(IN_SPLIT_SIZES[5:])])
    dt, qn, kvn, gn, gm = [w_tail[:, off[i]:off[i + 1]] for i in range(5)]
    d = w_tail.shape[0]
    wn = jnp.concatenate([qn, kvn], axis=1).astype(MXU)
    dtp = jnp.pad(dt.reshape(d, SSM_GROUPS, SSM_HEADS_PER_GROUP),
                  ((0, 0), (0, 0), (0, LANES - SSM_HEADS_PER_GROUP))).reshape(d, SSM_GROUPS * LANES)
    ng = NSA_HEADS_PER_GROUP * N_BRANCHES
    gnp = jnp.pad(gn.reshape(d, NSA_KV_GROUPS, ng), ((0, 0), (0, 0), (0, LANES - ng))).reshape(d, NSA_KV_GROUPS * LANES)
    w3 = jnp.concatenate([dtp, gnp], axis=1).astype(MXU)
    return wn, gm.astype(MXU), w3


def _flash_init(m_ref, acc_ref):
    m_ref[...] = jnp.full(m_ref.shape, NEG, F32)
    acc_ref[...] = jnp.zeros(acc_ref.shape, F32)


def _flash_step(s, v_aug, m_ref, acc_ref):
    hd = HEAD_DIM
    m_prev = m_ref[...]
    m_new = jnp.maximum(m_prev, jnp.max(s, axis=1, keepdims=True))
    alpha = jnp.exp(m_prev - m_new)
    p = jnp.concatenate([jnp.exp(s[:, c * hd:(c + 1) * hd] - m_new) for c in range(s.shape[1] // hd)], axis=1)
    pv = jnp.dot(p.astype(MXU), v_aug, preferred_element_type=F32)
    acc_ref[:, 0:hd] = alpha * acc_ref[:, 0:hd] + pv[:, 0:hd]
    acc_ref[:, hd:2 * hd] = alpha * acc_ref[:, hd:2 * hd] + pv[:, hd:2 * hd]
    m_ref[...] = m_new


def _flash_out(acc_ref):
    return acc_ref[:, 0:HEAD_DIM] / acc_ref[:, HEAD_DIM:2 * HEAD_DIM]


MOBA_KV_GROUP = 4


def _moba_kernel(q_ref, k_ref, v_ref, o_ref, kaug_ref, vaug_ref, kmean_ref, m_ref, acc_ref, *, nb):
    blk = MOBA_BLOCK
    grp = MOBA_KV_GROUP * blk
    qi = pl.program_id(1)

    @pl.when(qi == 0)
    def _build_keys():
        kmean_ref[...] = jnp.zeros_like(kmean_ref)
        lane = lax.broadcasted_iota(jnp.int32, (blk, LANES), 1)
        ones = jnp.ones((blk, HEAD_DIM), vaug_ref.dtype)

        def body(j, c):
            r0 = pl.multiple_of(j * blk, blk)
            kb = k_ref[pl.ds(r0, blk), :]
            kaug_ref[pl.ds(r0, blk), 0:HEAD_DIM] = kb.astype(kaug_ref.dtype)
            kaug_ref[pl.ds(r0, blk), HEAD_DIM:2 * HEAD_DIM] = (lane == j).astype(kaug_ref.dtype)
            vaug_ref[pl.ds(r0, blk), 0:HEAD_DIM] = v_ref[pl.ds(r0, blk), :].astype(vaug_ref.dtype)
            vaug_ref[pl.ds(r0, blk), HEAD_DIM:2 * HEAD_DIM] = ones
            kmean_ref[pl.ds(j, 1), :] = jnp.sum(kb.astype(F32), axis=0, keepdims=True) * (1.0 / blk)
            return c

        lax.fori_loop(0, nb, body, 0)

    q = q_ref[...].astype(MXU)
    gate = lax.dot_general(q, kmean_ref[...].astype(MXU), NT, preferred_element_type=F32)
    lane = lax.broadcasted_iota(jnp.int32, (blk, LANES), 1).astype(F32)
    past = lane < qi.astype(F32)
    gate = jnp.where(past, gate, -jnp.inf)
    sel = jnp.zeros((blk, LANES), jnp.bool_)
    for _ in range(MOBA_TOPK):
        mx = jnp.max(gate, axis=1, keepdims=True)
        idx = jnp.min(jnp.where((gate == mx) & past, lane, float(LANES)), axis=1, keepdims=True)
        pick = lane == idx
        sel = sel | pick
        gate = jnp.where(pick, -jnp.inf, gate)
    bias = jnp.where(sel, 0.0, MASK_BIAS).astype(MXU)
    qaug = jnp.concatenate([q, bias], axis=1)

    _flash_init(m_ref, acc_ref)

    def past_step(g, c):
        r0 = pl.multiple_of(g * grp, grp)
        s = lax.dot_general(qaug, kaug_ref[pl.ds(r0, grp), :], NT, preferred_element_type=F32) * ATT_SCALE
        _flash_step(s, vaug_ref[pl.ds(r0, grp), :], m_ref, acc_ref)
        return c

    lax.fori_loop(0, (qi + MOBA_KV_GROUP - 1) // MOBA_KV_GROUP, past_step, 0)
    r0 = pl.multiple_of(qi * blk, blk)
    s = lax.dot_general(q, k_ref[pl.ds(r0, blk), :].astype(MXU), NT, preferred_element_type=F32) * ATT_SCALE
    row = lax.broadcasted_iota(jnp.int32, (blk, blk), 0)
    col = lax.broadcasted_iota(jnp.int32, (blk, blk), 1)
    _flash_step(jnp.where(col <= row, s, NEG), vaug_ref[pl.ds(r0, blk), :], m_ref, acc_ref)
    o_ref[...] = _flash_out(acc_ref).astype(o_ref.dtype)


def _moba(p1):
    s = p1.shape[0]
    nb = s // MOBA_BLOCK
    assert nb <= LANES and nb % MOBA_KV_GROUP == 0
    kcol, vcol = P1_KA // HEAD_DIM, P1_VA // HEAD_DIM
    return pl.pallas_call(
        functools.partial(_moba_kernel, nb=nb),
        grid=(MOBA_HEADS, nb),
        in_specs=[pl.BlockSpec((MOBA_BLOCK, HEAD_DIM), lambda h, i: (i, h)),
                  pl.BlockSpec((s, HEAD_DIM), lambda h, i: (0, kcol + h)),
                  pl.BlockSpec((s, HEAD_DIM), lambda h, i: (0, vcol + h))],
        out_specs=pl.BlockSpec((MOBA_BLOCK, HEAD_DIM), lambda h, i: (i, h)),
        out_shape=jax.ShapeDtypeStruct((s, MOBA_WIDTH), MXU),
        scratch_shapes=[pltpu.VMEM((s, 2 * HEAD_DIM), MXU), pltpu.VMEM((s, 2 * HEAD_DIM), MXU),
                        pltpu.VMEM((LANES, HEAD_DIM), F32),
                        pltpu.VMEM((MOBA_BLOCK, HEAD_DIM), F32), pltpu.VMEM((MOBA_BLOCK, 2 * HEAD_DIM), F32)],
        compiler_params=_cparams(("arbitrary", "arbitrary")),
        name="moba_attention",
    )(p1, p1, p1)


def _ssd_kernel(dtb_ref, alog_ref, dskip_ref,
                xs_ref, bm_ref, cm_ref, z_ref, dt_ref, cwx_ref, cwb_ref, cwc_ref, cbx_ref, cbb_ref, cbc_ref,
                ng_ref, o_ref, xbuf, bbuf, cbuf, state_ref, ybuf):
    L = SSM_CHUNK
    W = SSM_GROUP_WIDTH
    J = SSM_HEADS_PER_GROUP
    P = SSM_HEAD_DIM
    g = pl.program_id(0)
    c = pl.program_id(1)

    @pl.when(c == 0)
    def _reset():
        xbuf[0:SUBLANES, :] = jnp.zeros((SUBLANES, W), F32)
        bbuf[0:SUBLANES, :] = jnp.zeros((SUBLANES, SSM_STATE), F32)
        cbuf[0:SUBLANES, :] = jnp.zeros((SUBLANES, SSM_STATE), F32)
        state_ref[...] = jnp.zeros_like(state_ref)

    def conv_silu(buf, raw_ref, w_ref, b_ref):
        buf[SUBLANES:SUBLANES + L, :] = raw_ref[...]
        acc = b_ref[...]
        for i in range(SSM_CONV):
            lo = SUBLANES - (SSM_CONV - 1) + i
            acc = acc + w_ref[i:i + 1, :] * buf[lo:lo + L, :]
        buf[0:SUBLANES, :] = buf[L:L + SUBLANES, :]
        return _silu(acc)

    xs = conv_silu(xbuf, xs_ref, cwx_ref, cbx_ref)
    bm = conv_silu(bbuf, bm_ref, cwb_ref, cbb_ref)
    cm = conv_silu(cbuf, cm_ref, cwc_ref, cbc_ref)

    lane = lax.broadcasted_iota(jnp.int32, (1, LANES), 1)
    dtb = jnp.zeros((1, LANES), F32)
    alog = jnp.full((1, LANES), -jnp.inf, F32)
    for j in range(J):
        dtb = jnp.where(lane == j, dtb_ref[g * J + j], dtb)
        alog = jnp.where(lane == j, alog_ref[g * J + j], alog)
    x = dt_ref[...] + dtb
    dt = jnp.maximum(x, 0.0) + jnp.log(1.0 + jnp.exp(-jnp.abs(x)))
    dt = jnp.where(lane < J, dt, 0.0)
    a = dt * (-jnp.exp(alog))

    er = lax.broadcasted_iota(jnp.int32, (LANES, W), 0)
    ec = lax.broadcasted_iota(jnp.int32, (LANES, W), 1)
    expand = ((ec >> 6) == er).astype(jnp.bfloat16)
    tr = lax.broadcasted_iota(jnp.int32, (L, L), 0)
    tc = lax.broadcasted_iota(jnp.int32, (L, L), 1)
    tril = tr >= tc
    tril_b = tril.astype(jnp.bfloat16)
    dt_e = _dot3(expand, dt, left=False)
    a_e = _dot3(expand, a, left=False)
    acs = _dot3(tril_b, a_e, left=True)
    acs_t = acs.T
    a_last = acs[L - 1:L, :]

    xdt = xs * dt_e
    cb = lax.dot_general(cm.astype(MXU), bm.astype(MXU), NT, preferred_element_type=F32)
    for j in range(J):
        colv = acs[:, j * P:j * P + 1]
        rowv = acs_t[j * P:j * P + 1, :]
        dec = jnp.exp(jnp.where(tril, colv - rowv, -jnp.inf))
        ybuf[:, j * P:(j + 1) * P] = jnp.dot((cb * dec).astype(MXU), xdt[:, j * P:(j + 1) * P].astype(MXU),
                                             preferred_element_type=F32)
    st_old = state_ref[...]
    xdte = (xdt * jnp.exp(a_last - acs)).astype(MXU)
    st_new = jnp.dot(bm.T.astype(MXU), xdte, preferred_element_type=F32)
    y_off = jnp.dot(cm.astype(MXU), st_old.astype(MXU), preferred_element_type=F32) * jnp.exp(acs)
    state_ref[...] = st_old * jnp.exp(a_last) + st_new

    lane_w = lax.broadcasted_iota(jnp.int32, (1, W), 1)
    dsk = jnp.zeros((1, W), F32)
    for j in range(J):
        dsk = jnp.where((lane_w >> 6) == j, dskip_ref[g * J + j], dsk)
    y = ybuf[...] + y_off + xs * dsk
    y = y * _silu(z_ref[...])
    y = y * lax.rsqrt(jnp.mean(y * y, axis=1, keepdims=True) + RMS_EPS)
    o_ref[...] = (y * ng_ref[...]).astype(o_ref.dtype)


def _ssd(p2, p3, conv_w, conv_b, dt_bias, a_log, d_skip, norm_g):
    s = p2.shape[0]
    L, W, N = SSM_CHUNK, SSM_GROUP_WIDTH, SSM_STATE
    xs0 = P2_XBC // W
    bm0 = (P2_XBC + SSM_D_INNER) // N
    cm0 = bm0 + SSM_GROUPS
    z0 = P2_Z // W
    cb2 = conv_b.reshape(1, SSM_CONV_CH)
    ng2 = norm_g.reshape(1, SSM_D_INNER)
    grid_spec = pltpu.PrefetchScalarGridSpec(
        num_scalar_prefetch=3,
        grid=(SSM_GROUPS, s // L),
        in_specs=[
            pl.BlockSpec((L, W), lambda g, c, *_: (c, xs0 + g)),
            pl.BlockSpec((L, N), lambda g, c, *_: (c, bm0 + g)),
            pl.BlockSpec((L, N), lambda g, c, *_: (c, cm0 + g)),
            pl.BlockSpec((L, W), lambda g, c, *_: (c, z0 + g)),
            pl.BlockSpec((L, LANES), lambda g, c, *_: (c, g)),
            pl.BlockSpec((SSM_CONV, W), lambda g, c, *_: (0, g)),
            pl.BlockSpec((SSM_CONV, N), lambda g, c, *_: (0, SSM_D_INNER // N + g)),
            pl.BlockSpec((SSM_CONV, N), lambda g, c, *_: (0, SSM_D_INNER // N + SSM_GROUPS + g)),
            pl.BlockSpec((1, W), lambda g, c, *_: (0, g)),
            pl.BlockSpec((1, N), lambda g, c, *_: (0, SSM_D_INNER // N + g)),
            pl.BlockSpec((1, N), lambda g, c, *_: (0, SSM_D_INNER // N + SSM_GROUPS + g)),
            pl.BlockSpec((1, W), lambda g, c, *_: (0, g)),
        ],
        out_specs=pl.BlockSpec((L, W), lambda g, c, *_: (c, g)),
        scratch_shapes=[pltpu.VMEM((SUBLANES + L, W), F32), pltpu.VMEM((SUBLANES + L, N), F32),
                        pltpu.VMEM((SUBLANES + L, N), F32), pltpu.VMEM((N, W), F32), pltpu.VMEM((L, W), F32)],
    )
    return pl.pallas_call(
        _ssd_kernel,
        grid_spec=grid_spec,
        out_shape=jax.ShapeDtypeStruct((s, SSM_D_INNER), MXU),
        compiler_params=_cparams(("arbitrary", "arbitrary")),
        name="ssd_mixer",
    )(dt_bias, a_log, d_skip, p2, p2, p2, p2, p3, conv_w, conv_w, conv_w, cb2, cb2, cb2, ng2)


def _nsa_compress_kernel(t_ref, pos_ref, w1_ref, w2_ref, o_ref):
    half = NSA_CMP_STRIDE * HEAD_DIM
    t = t_ref[0].astype(F32)
    pos = pos_ref[0]
    lo = (t + pos[:, :half]).astype(MXU)
    hi = (t + pos[:, half:]).astype(MXU)
    w1 = w1_ref[0]
    a = jnp.dot(lo, w1[:half].astype(MXU), preferred_element_type=F32)
    b = jnp.dot(hi, w1[half:].astype(MXU), preferred_element_type=F32)
    n = t.shape[0]
    pre = a + pltpu.roll(b, n - 1, 0)
    act = jax.nn.gelu(pre, approximate=True)
    o_ref[0] = jnp.dot(act.astype(MXU), w2_ref[0].astype(MXU), preferred_element_type=F32).astype(o_ref.dtype)


def _nsa_compress(p1, pos_k, w1_k, w2_k, pos_v, w1_v, w2_v):
    s = p1.shape[0]
    n_str = s // NSA_CMP_STRIDE
    kv = p1[:, P1_KVN:P1_KVN + 2 * NSA_KV_WIDTH]
    t4 = kv.reshape(s, 4, HEAD_DIM).transpose(1, 0, 2).reshape(4, n_str, NSA_CMP_STRIDE * HEAD_DIM)
    pos = jnp.stack([pos_k, pos_v]).reshape(2, 1, NSA_CMP_LEN * HEAD_DIM)
    w1 = jnp.stack([w1_k, w1_v])
    w2 = jnp.stack([w2_k, w2_v])
    return pl.pallas_call(
        _nsa_compress_kernel,
        grid=(4,),
        in_specs=[pl.BlockSpec((1, n_str, NSA_CMP_STRIDE * HEAD_DIM), lambda i: (i, 0, 0)),
                  pl.BlockSpec((1, 1, NSA_CMP_LEN * HEAD_DIM), lambda i: (i // 2, 0, 0)),
                  pl.BlockSpec((1, NSA_CMP_LEN * HEAD_DIM, HEAD_DIM), lambda i: (i // 2, 0, 0)),
                  pl.BlockSpec((1, HEAD_DIM, HEAD_DIM), lambda i: (i // 2, 0, 0))],
        out_specs=pl.BlockSpec((1, n_str, HEAD_DIM), lambda i: (i, 0, 0)),
        out_shape=jax.ShapeDtypeStruct((4, n_str, HEAD_DIM), MXU),
        compiler_params=_cparams(("arbitrary",)),
        name="nsa_compress",
    )(t4, pos, w1, w2)


def _nsa_cmp_kernel(q_ref, kc_ref, vc_ref, gn_ref, o_ref, selb_ref, *, n_sel):
    tq = ATT_TILE
    qi = pl.program_id(1)
    kc = kc_ref[0].astype(MXU)
    vc = vc_ref[0].astype(MXU)
    nc = kc.shape[0]
    pos = qi * tq + lax.broadcasted_iota(jnp.int32, (tq, nc), 0)
    cidx = lax.broadcasted_iota(jnp.int32, (tq, nc), 1)
    valid = (cidx * NSA_CMP_STRIDE + NSA_CMP_LEN - 1 <= pos) & (cidx < nc - 1)
    orow = lax.broadcasted_iota(jnp.int32, (nc, LANES), 0) * NSA_CMP_STRIDE
    ocol = lax.broadcasted_iota(jnp.int32, (nc, LANES), 1) * NSA_SEL_BLOCK
    overlap = ((orow < ocol + NSA_SEL_BLOCK) & (orow + NSA_CMP_LEN > ocol)).astype(MXU)
    gates = _sigmoid(gn_ref[...])
    imp = jnp.zeros((tq, LANES), F32)
    for j in range(NSA_HEADS_PER_GROUP):
        q = q_ref[:, j * HEAD_DIM:(j + 1) * HEAD_DIM].astype(MXU)
        s = lax.dot_general(q, kc, NT, preferred_element_type=F32) * ATT_SCALE
        s = jnp.where(valid, s, -jnp.inf)
        m = jnp.max(s, axis=1, keepdims=True)
        m = jnp.where(m > -jnp.inf, m, 0.0)
        e = jnp.exp(s - m)
        den = jnp.sum(e, axis=1, keepdims=True)
        p = (e / jnp.where(den > 0, den, 1.0)).astype(MXU)
        o = jnp.dot(p, vc, preferred_element_type=F32)
        imp = imp + jnp.dot(p, overlap, preferred_element_type=F32)
        o_ref[:, j * HEAD_DIM:(j + 1) * HEAD_DIM] = o * gates[:, N_BRANCHES * j:N_BRANCHES * j + 1]

    lane = lax.broadcasted_iota(jnp.int32, (tq, LANES), 1).astype(F32)
    cur = ((qi * tq + lax.broadcasted_iota(jnp.int32, (tq, LANES), 0)) >> 6).astype(F32)
    allowed = lane <= cur
    forced = (lane == 0.0) | (lane == cur) | (lane == cur - 1.0)
    val = jnp.where(forced, jnp.inf, jnp.where(allowed, imp, -jnp.inf))
    val = jnp.where(lane < float(n_sel), val, -jnp.inf)

    def pick_round(_, c):
        val, sel = c
        mx = jnp.max(val, axis=1, keepdims=True)
        idx = jnp.min(jnp.where(val == mx, lane, float(LANES)), axis=1, keepdims=True)
        pick = lane == idx
        sel = jnp.where(pick & allowed, 1.0, sel)
        val = jnp.where(pick, -jnp.inf, val)
        return val, sel

    _, sel = lax.fori_loop(0, min(NSA_TOPN, n_sel), pick_round, (val, jnp.zeros((tq, LANES), F32)))
    selb_ref[0] = jnp.where(sel > 0, 0.0, MASK_BIAS).astype(selb_ref.dtype)


def _nsa_cmp(p1, p3, kvc):
    s = p1.shape[0]
    n_str = s // NSA_CMP_STRIDE
    n_sel = s // NSA_SEL_BLOCK
    assert n_sel <= LANES
    gw = NSA_HEADS_PER_GROUP * HEAD_DIM
    q0 = P1_QN // gw
    return pl.pallas_call(
        functools.partial(_nsa_cmp_kernel, n_sel=n_sel),
        grid=(NSA_KV_GROUPS, s // ATT_TILE),
        in_specs=[pl.BlockSpec((ATT_TILE, gw), lambda g, i: (i, q0 + g)),
                  pl.BlockSpec((1, n_str, HEAD_DIM), lambda g, i: (g, 0, 0)),
                  pl.BlockSpec((1, n_str, HEAD_DIM), lambda g, i: (NSA_KV_GROUPS + g, 0, 0)),
                  pl.BlockSpec((ATT_TILE, LANES), lambda g, i: (i, SSM_GROUPS + g))],
        out_specs=[pl.BlockSpec((ATT_TILE, gw), lambda g, i: (i, g)),
                   pl.BlockSpec((1, ATT_TILE, LANES), lambda g, i: (g, i, 0))],
        out_shape=[jax.ShapeDtypeStruct((s, NSA_WIDTH), F32),
                   jax.ShapeDtypeStruct((NSA_KV_GROUPS, s, LANES), MXU)],
        compiler_params=_cparams(("arbitrary", "arbitrary")),
        name="nsa_compressed_attention",
    )(p1, kvc, kvc, p3)


def _nsa_sel_kernel(q_ref, selb_ref, k_ref, v_ref, gn_ref, prev_ref, o_ref, kaug_ref, vaug_ref, qaug_ref,
                    m_ref, acc_ref, *, n_tiles):
    tq = ATT_TILE
    J = NSA_HEADS_PER_GROUP
    qi = pl.program_id(1)

    @pl.when(qi == 0)
    def _build_keys():
        lane = lax.broadcasted_iota(jnp.int32, (tq, LANES), 1)
        rowi = lax.broadcasted_iota(jnp.int32, (tq, LANES), 0)
        ones = jnp.ones((tq, HEAD_DIM), vaug_ref.dtype)

        def body(t, c):
            r0 = pl.multiple_of(t * tq, tq)
            kaug_ref[pl.ds(r0, tq), 0:HEAD_DIM] = k_ref[pl.ds(r0, tq), :].astype(kaug_ref.dtype)
            kaug_ref[pl.ds(r0, tq), HEAD_DIM:2 * HEAD_DIM] = (
                lane == ((t * tq + rowi) >> 6)).astype(kaug_ref.dtype)
            vaug_ref[pl.ds(r0, tq), 0:HEAD_DIM] = v_ref[pl.ds(r0, tq), :].astype(vaug_ref.dtype)
            vaug_ref[pl.ds(r0, tq), HEAD_DIM:2 * HEAD_DIM] = ones
            return c

        lax.fori_loop(0, n_tiles, body, 0)

    selb = selb_ref[0]
    for j in range(J):
        qaug_ref[j * tq:(j + 1) * tq, 0:HEAD_DIM] = q_ref[:, j * HEAD_DIM:(j + 1) * HEAD_DIM].astype(qaug_ref.dtype)
        qaug_ref[j * tq:(j + 1) * tq, HEAD_DIM:2 * HEAD_DIM] = selb
    _flash_init(m_ref, acc_ref)

    def scores(t):
        r0 = pl.multiple_of(t * tq, tq)
        return lax.dot_general(qaug_ref[...], kaug_ref[pl.ds(r0, tq), :], NT,
                               preferred_element_type=F32) * ATT_SCALE, vaug_ref[pl.ds(r0, tq), :]

    def past_step(t, c):
        s, v = scores(t)
        _flash_step(s, v, m_ref, acc_ref)
        return c

    lax.fori_loop(0, qi, past_step, 0)
    s, v = scores(qi)
    row = lax.broadcasted_iota(jnp.int32, (J * tq, tq), 0) & (tq - 1)
    col = lax.broadcasted_iota(jnp.int32, (J * tq, tq), 1)
    _flash_step(jnp.where(col <= row, s, NEG), v, m_ref, acc_ref)
    o = _flash_out(acc_ref)
    gates = _sigmoid(gn_ref[...])
    for j in range(J):
        o_ref[:, j * HEAD_DIM:(j + 1) * HEAD_DIM] = (
            prev_ref[:, j * HEAD_DIM:(j + 1) * HEAD_DIM]
            + o[j * tq:(j + 1) * tq] * gates[:, N_BRANCHES * j + 1:N_BRANCHES * j + 2])


def _nsa_sel(p1, p3, selb, prev):
    s = p1.shape[0]
    gw = NSA_HEADS_PER_GROUP * HEAD_DIM
    q0 = P1_QN // gw
    k0 = (P1_KVN + 2 * NSA_KV_WIDTH) // HEAD_DIM
    v0 = k0 + NSA_KV_GROUPS
    return pl.pallas_call(
        functools.partial(_nsa_sel_kernel, n_tiles=s // ATT_TILE),
        grid=(NSA_KV_GROUPS, s // ATT_TILE),
        in_specs=[pl.BlockSpec((ATT_TILE, gw), lambda g, i: (i, q0 + g)),
                  pl.BlockSpec((1, ATT_TILE, LANES), lambda g, i: (g, i, 0)),
                  pl.BlockSpec((s, HEAD_DIM), lambda g, i: (0, k0 + g)),
                  pl.BlockSpec((s, HEAD_DIM), lambda g, i: (0, v0 + g)),
                  pl.BlockSpec((ATT_TILE, LANES), lambda g, i: (i, SSM_GROUPS + g)),
                  pl.BlockSpec((ATT_TILE, gw), lambda g, i: (i, g))],
        out_specs=pl.BlockSpec((ATT_TILE, gw), lambda g, i: (i, g)),
        out_shape=jax.ShapeDtypeStruct((s, NSA_WIDTH), F32),
        scratch_shapes=[pltpu.VMEM((s, 2 * HEAD_DIM), MXU), pltpu.VMEM((s, 2 * HEAD_DIM), MXU),
                        pltpu.VMEM((NSA_HEADS_PER_GROUP * ATT_TILE, 2 * HEAD_DIM), MXU),
                        pltpu.VMEM((NSA_HEADS_PER_GROUP * ATT_TILE, HEAD_DIM), F32),
                        pltpu.VMEM((NSA_HEADS_PER_GROUP * ATT_TILE, 2 * HEAD_DIM), F32)],
        compiler_params=_cparams(("arbitrary", "arbitrary")),
        name="nsa_selected_attention",
    )(p1, selb, p1, p1, p3, prev)


def _nsa_win_kernel(q_ref, k_ref, v_ref, gn_ref, prev_ref, o_ref, qst_ref, m_ref, acc_ref):
    tq = ATT_TILE
    J = NSA_HEADS_PER_GROUP
    halo = NSA_WINDOW // tq
    qi = pl.program_id(1)
    for j in range(J):
        qst_ref[j * tq:(j + 1) * tq, :] = q_ref[:, j * HEAD_DIM:(j + 1) * HEAD_DIM].astype(qst_ref.dtype)
    row = lax.broadcasted_iota(jnp.int32, (J * tq, tq), 0) & (tq - 1)
    col = lax.broadcasted_iota(jnp.int32, (J * tq, tq), 1)
    ones = jnp.ones((tq, HEAD_DIM), MXU)
    _flash_init(m_ref, acc_ref)

    def step(t, c):
        r0 = pl.multiple_of(t * tq, tq)
        s = lax.dot_general(qst_ref[...], k_ref[pl.ds(r0, tq), :].astype(MXU), NT,
                            preferred_element_type=F32) * ATT_SCALE
        diff = (qi - t) * tq + row - col
        s = jnp.where((diff >= 0) & (diff < NSA_WINDOW), s, NEG)
        v_aug = jnp.concatenate([v_ref[pl.ds(r0, tq), :].astype(MXU), ones], axis=1)
        _flash_step(s, v_aug, m_ref, acc_ref)
        return c

    lax.fori_loop(jnp.maximum(qi - halo, 0), qi + 1, step, 0)
    o = _flash_out(acc_ref)
    gates = _sigmoid(gn_ref[...])
    for j in range(J):
        o_ref[:, j * HEAD_DIM:(j + 1) * HEAD_DIM] = (
            prev_ref[:, j * HEAD_DIM:(j + 1) * HEAD_DIM]
            + o[j * tq:(j + 1) * tq] * gates[:, N_BRANCHES * j + 2:N_BRANCHES * j + 3])


def _nsa_win(p1, p3, prev):
    s = p1.shape[0]
    gw = NSA_HEADS_PER_GROUP * HEAD_DIM
    q0 = P1_QN // gw
    k0 = (P1_KVN + 4 * NSA_KV_WIDTH) // HEAD_DIM
    v0 = k0 + NSA_KV_GROUPS
    return pl.pallas_call(
        _nsa_win_kernel,
        grid=(NSA_KV_GROUPS, s // ATT_TILE),
        in_specs=[pl.BlockSpec((ATT_TILE, gw), lambda g, i: (i, q0 + g)),
                  pl.BlockSpec((s, HEAD_DIM), lambda g, i: (0, k0 + g)),
                  pl.BlockSpec((s, HEAD_DIM), lambda g, i: (0, v0 + g)),
                  pl.BlockSpec((ATT_TILE, LANES), lambda g, i: (i, SSM_GROUPS + g)),
                  pl.BlockSpec((ATT_TILE, gw), lambda g, i: (i, g))],
        out_specs=pl.BlockSpec((ATT_TILE, gw), lambda g, i: (i, g)),
        out_shape=jax.ShapeDtypeStruct((s, NSA_WIDTH), F32),
        scratch_shapes=[pltpu.VMEM((NSA_HEADS_PER_GROUP * ATT_TILE, HEAD_DIM), MXU),
                        pltpu.VMEM((NSA_HEADS_PER_GROUP * ATT_TILE, HEAD_DIM), F32),
                        pltpu.VMEM((NSA_HEADS_PER_GROUP * ATT_TILE, 2 * HEAD_DIM), F32)],
        compiler_params=_cparams(("arbitrary", "arbitrary")),
        name="nsa_window_attention",
    )(p1, p1, p1, p3, prev)


def _merge_kernel(ya_ref, yb_ref, yc_ref, g0_ref, g1_ref, g2_ref, wa_ref, wb_ref, wc_ref, o_ref):
    def branch(y_ref, w_ref, g_ref):
        prod = jnp.dot(y_ref[...].astype(MXU), w_ref[...].astype(MXU), preferred_element_type=F32)
        return _sigmoid(g_ref[...]) * prod

    o_ref[...] = (branch(ya_ref, wa_ref, g0_ref) + branch(yb_ref, wb_ref, g1_ref)
                  + branch(yc_ref, wc_ref, g2_ref)).astype(o_ref.dtype)


def _merge(ya, yb, yc, p2, wa, wb, wc, *, tm=512, tn=512):
    s = ya.shape[0]
    d = wa.shape[1]
    g0 = P2_GM // tn
    gstep = d // tn
    return pl.pallas_call(
        _merge_kernel,
        grid=(d // tn, s // tm),
        in_specs=[pl.BlockSpec((tm, ya.shape[1]), lambda j, i: (i, 0)),
                  pl.BlockSpec((tm, yb.shape[1]), lambda j, i: (i, 0)),
                  pl.BlockSpec((tm, yc.shape[1]), lambda j, i: (i, 0)),
                  pl.BlockSpec((tm, tn), lambda j, i: (i, g0 + j)),
                  pl.BlockSpec((tm, tn), lambda j, i: (i, g0 + gstep + j)),
                  pl.BlockSpec((tm, tn), lambda j, i: (i, g0 + 2 * gstep + j)),
                  pl.BlockSpec((wa.shape[0], tn), lambda j, i: (0, j)),
                  pl.BlockSpec((wb.shape[0], tn), lambda j, i: (0, j)),
                  pl.BlockSpec((wc.shape[0], tn), lambda j, i: (0, j))],
        out_specs=pl.BlockSpec((tm, tn), lambda j, i: (i, j)),
        out_shape=jax.ShapeDtypeStruct((s, d), MXU),
        compiler_params=_cparams(("arbitrary", "arbitrary")),
        name="branch_merge",
    )(ya, yb, yc, p2, p2, p2, wa, wb, wc)


def _layer_norm_rows(x, g, b):
    xc = x - jnp.mean(x, axis=1, keepdims=True)
    var = jnp.mean(xc * xc, axis=1, keepdims=True)
    return xc * lax.rsqrt(var + LN_EPS) * g + b


def _pack_halves(y):
    half = y.shape[1] // 2
    bits = lax.bitcast_convert_type(y.astype(jnp.bfloat16).astype(F32), jnp.int32)
    return ((bits[:, :half] >> 16) & jnp.int32(0xFFFF)) | (bits[:, half:] & jnp.int32(-65536))


def _unpack_halves(w):
    lo = lax.bitcast_convert_type(w << 16, F32)
    hi = lax.bitcast_convert_type(w & jnp.int32(-65536), F32)
    return lo.astype(MXU), hi.astype(MXU)


def _wout_ln_kernel(m_ref, w_ref, h_ref, g_ref, b_ref, o_ref, ob_ref, op_ref):
    acc = jnp.dot(m_ref[...].astype(MXU), w_ref[...].astype(MXU), preferred_element_type=F32)
    y = _layer_norm_rows(DEEPNORM_ALPHA * h_ref[...] + acc, g_ref[...], b_ref[...])
    o_ref[...] = y
    ob_ref[...] = y.astype(ob_ref.dtype)
    op_ref[...] = _pack_halves(y)


def _wout_ln(merged, w_out, h, ln_g, ln_b, *, tm=256):
    s, d = h.shape
    row = pl.BlockSpec((tm, d), lambda i: (i, 0))
    vec = pl.BlockSpec((1, d), lambda i: (0, 0))
    return pl.pallas_call(
        _wout_ln_kernel,
        grid=(s // tm,),
        in_specs=[row, pl.BlockSpec((d, d), lambda i: (0, 0)), row, vec, vec],
        out_specs=[row, row, pl.BlockSpec((tm, d // 2), lambda i: (i, 0))],
        out_shape=[jax.ShapeDtypeStruct((s, d), F32), jax.ShapeDtypeStruct((s, d), MXU),
                   jax.ShapeDtypeStruct((s, d // 2), jnp.int32)],
        compiler_params=_cparams(("arbitrary",)),
        name="out_proj_layernorm",
    )(merged, w_out.astype(MXU), h, ln_g.reshape(1, d), ln_b.reshape(1, d))


def _dispatch_kernel(dest_ref, hp_ref, init_ref, o_ref, sem):
    del init_ref
    tt = dest_ref.shape[1]

    def issue(r, c):
        for k in range(TOP_K):
            pltpu.make_async_copy(hp_ref.at[pl.ds(r, 1)], o_ref.at[pl.ds(dest_ref[k, r], 1)], sem).start()
        return c

    lax.fori_loop(0, tt, issue, 0)
    rows = o_ref.at[pl.ds(0, TOP_K * tt)]
    pltpu.make_async_copy(rows, rows, sem).wait()


def _dispatch(hp, dest, n_rows, *, tt=256):
    s, half = hp.shape
    return pl.pallas_call(
        _dispatch_kernel,
        grid=(s // tt,),
        in_specs=[pl.BlockSpec((TOP_K, tt), lambda i: (0, i), memory_space=pltpu.SMEM),
                  pl.BlockSpec((tt, half), lambda i: (i, 0)),
                  pl.BlockSpec(memory_space=pl.ANY)],
        out_specs=pl.BlockSpec(memory_space=pl.ANY),
        out_shape=jax.ShapeDtypeStruct((n_rows, half), jnp.int32),
        scratch_shapes=[pltpu.SemaphoreType.DMA(())],
        input_output_aliases={2: 0},
        compiler_params=_cparams(("arbitrary",)),
        name="moe_dispatch",
    )(dest, hp, jnp.zeros((n_rows, half), jnp.int32))


def _combine_kernel(dcur_ref, dnxt_ref, w_ref, h_ref, sh_ref, g_ref, b_ref, y_ref, o_ref, ob_ref, buf, sem):
    tt = dcur_ref.shape[1]
    i = pl.program_id(0)
    n = pl.num_programs(0)
    slot = i % 2

    def issue(d_ref, sl):
        def body(r, c):
            for k in range(TOP_K):
                pltpu.make_async_copy(y_ref.at[pl.ds(d_ref[k, r], 1)], buf.at[sl, pl.ds(k * tt + r, 1)],
                                      sem.at[sl]).start()
            return c

        lax.fori_loop(0, tt, body, 0)

    @pl.when(i == 0)
    def _():
        issue(dcur_ref, slot)

    @pl.when(i + 1 < n)
    def _():
        issue(dnxt_ref, 1 - slot)

    pltpu.make_async_copy(y_ref.at[pl.ds(0, TOP_K * tt)], buf.at[slot], sem.at[slot]).wait()
    w = w_ref[...]
    routed = w[:, 0:1] * buf[slot, 0:tt]
    for k in range(1, TOP_K):
        routed = routed + w[:, k:k + 1] * buf[slot, k * tt:(k + 1) * tt]
    y = _layer_norm_rows(DEEPNORM_ALPHA * h_ref[...] + (routed + sh_ref[...]), g_ref[...], b_ref[...])
    o_ref[...] = y
    ob_ref[...] = y.astype(ob_ref.dtype)


def _combine_ln(h, y_rows, dest, top_w, shared, ln_g, ln_b, *, tt=128):
    s, d = h.shape
    n = s // tt
    row = pl.BlockSpec((tt, d), lambda i: (i, 0))
    vec = pl.BlockSpec((1, d), lambda i: (0, 0))
    return pl.pallas_call(
        _combine_kernel,
        grid=(n,),
        in_specs=[pl.BlockSpec((TOP_K, tt), lambda i: (0, i), memory_space=pltpu.SMEM),
                  pl.BlockSpec((TOP_K, tt), lambda i: (0, jnp.minimum(i + 1, n - 1)), memory_space=pltpu.SMEM),
                  pl.BlockSpec((tt, TOP_K), lambda i: (i, 0)),
                  row, row, vec, vec,
                  pl.BlockSpec(memory_space=pl.ANY)],
        out_specs=[row, row],
        out_shape=[jax.ShapeDtypeStruct((s, d), F32), jax.ShapeDtypeStruct((s, d), MXU)],
        scratch_shapes=[pltpu.VMEM((2, TOP_K * tt, d), F32), pltpu.SemaphoreType.DMA((2,))],
        compiler_params=_cparams(("arbitrary",)),
        name="moe_combine_layernorm",
    )(dest, dest, top_w.T, h, shared, ln_g.reshape(1, d), ln_b.reshape(1, d), y_rows)


def _router_kernel(h_ref, wr_ref, rb_ref, e_ref, w_ref, pos_ref, cnt_ref, carry_ref):
    tq = h_ref.shape[0]
    i = pl.program_id(0)

    @pl.when(i == 0)
    def _reset():
        carry_ref[...] = jnp.zeros_like(carry_ref)

    logits = lax.dot_general(wr_ref[...].astype(MXU), h_ref[...].astype(MXU), NT, preferred_element_type=F32)
    scores = _sigmoid(logits)
    biased = scores + rb_ref[...]
    G, PG = N_EXPERT_GROUPS, EXPERTS_PER_GROUP
    sub = lax.broadcasted_iota(jnp.int32, (PG, tq), 0).astype(F32)
    gi = lax.broadcasted_iota(jnp.int32, (G, tq), 0).astype(F32)
    gs = jnp.zeros((G, tq), F32)
    for g in range(G):
        blk = biased[g * PG:(g + 1) * PG, :]
        m1 = jnp.max(blk, axis=0, keepdims=True)
        i1 = jnp.min(jnp.where(blk == m1, sub, float(PG)), axis=0, keepdims=True)
        m2 = jnp.max(jnp.where(sub == i1, -jnp.inf, blk), axis=0, keepdims=True)
        gs = jnp.where(gi == float(g), m1 + m2, gs)
    keep = jnp.zeros((G, tq), F32)
    for _ in range(TOPK_GROUPS):
        mx = jnp.max(gs, axis=0, keepdims=True)
        idx = jnp.min(jnp.where(gs == mx, gi, float(G)), axis=0, keepdims=True)
        pick = gi == idx
        keep = jnp.where(pick, 1.0, keep)
        gs = jnp.where(pick, -jnp.inf, gs)
    val = jnp.concatenate(
        [jnp.where(keep[g:g + 1, :] > 0, biased[g * PG:(g + 1) * PG, :], -jnp.inf) for g in range(G)], axis=0)
    ei = lax.broadcasted_iota(jnp.int32, (N_EXPERTS, tq), 0).astype(F32)
    picks, svals = [], []
    sel = jnp.zeros((N_EXPERTS, tq), F32)
    for r in range(TOP_K):
        mx = jnp.max(val, axis=0, keepdims=True)
        idx = jnp.min(jnp.where(val == mx, ei, float(N_EXPERTS)), axis=0, keepdims=True)
        pick = ei == idx
        picks.append(pick)
        svals.append(jnp.sum(jnp.where(pick, scores, 0.0), axis=0, keepdims=True))
        e_ref[r:r + 1, :] = idx.astype(jnp.int32)
        sel = jnp.where(pick, 1.0, sel)
        val = jnp.where(pick, -jnp.inf, val)
    wsum = svals[0]
    for r in range(1, TOP_K):
        wsum = wsum + svals[r]
    tr = lax.broadcasted_iota(jnp.int32, (tq, tq), 0)
    tc = lax.broadcasted_iota(jnp.int32, (tq, tq), 1)
    before = (tr < tc).astype(jnp.bfloat16)
    prefix = jnp.dot(sel.astype(jnp.bfloat16), before, preferred_element_type=F32)
    pos = carry_ref[:, 0:1] + prefix
    for r in range(TOP_K):
        w_ref[r:r + 1, :] = svals[r] / wsum * ROUTED_SCALE
        pos_ref[r:r + 1, :] = jnp.sum(jnp.where(picks[r], pos, 0.0), axis=0, keepdims=True).astype(jnp.int32)
    total = carry_ref[...] + jnp.sum(sel, axis=1, keepdims=True)
    carry_ref[...] = total
    cnt_ref[...] = total


def _router(hb, w_router, router_bias, *, tq=256):
    s, d = hb.shape
    row = pl.BlockSpec((TOP_K, tq), lambda i: (0, i))
    return pl.pallas_call(
        _router_kernel,
        grid=(s // tq,),
        in_specs=[pl.BlockSpec((tq, d), lambda i: (i, 0)),
                  pl.BlockSpec((N_EXPERTS, d), lambda i: (0, 0)),
                  pl.BlockSpec((N_EXPERTS, 1), lambda i: (0, 0))],
        out_specs=[row, row, row, pl.BlockSpec((N_EXPERTS, LANES), lambda i: (0, 0))],
        out_shape=[jax.ShapeDtypeStruct((TOP_K, s), jnp.int32), jax.ShapeDtypeStruct((TOP_K, s), F32),
                   jax.ShapeDtypeStruct((TOP_K, s), jnp.int32), jax.ShapeDtypeStruct((N_EXPERTS, LANES), F32)],
        scratch_shapes=[pltpu.VMEM((N_EXPERTS, LANES), F32)],
        compiler_params=_cparams(("arbitrary",)),
        name="moe_router",
    )(hb, w_router.T, router_bias.reshape(N_EXPERTS, 1))


def _expert_kernel(te_ref, nu_ref, x_ref, wg_ref, wu_ref, wd_ref, o_ref):
    i = pl.program_id(0)

    @pl.when(i < nu_ref[0])
    def _compute():
        xa, xb = _unpack_halves(x_ref[...])
        half = xa.shape[1]

        def up(w_ref):
            w = w_ref[0]
            return (jnp.dot(xa, w[:half].astype(MXU), preferred_element_type=F32)
                    + jnp.dot(xb, w[half:].astype(MXU), preferred_element_type=F32))

        hg = up(wg_ref)
        hu = up(wu_ref)
        act = (_silu(hg) * hu).astype(MXU)
        o_ref[...] = jnp.dot(act, wd_ref[0].astype(MXU), preferred_element_type=F32).astype(o_ref.dtype)

    @pl.when(i >= nu_ref[0])
    def _unused():
        o_ref[...] = jnp.zeros_like(o_ref)


def _experts(x_rows, tile_e, n_used, w_gate, w_up, w_down, *, tm):
    n_rows = x_rows.shape[0]
    d, f = w_gate.shape[1:]
    grid_spec = pltpu.PrefetchScalarGridSpec(
        num_scalar_prefetch=2,
        grid=(n_rows // tm,),
        in_specs=[pl.BlockSpec((tm, d // 2), lambda i, te, nu: (jnp.minimum(i, nu[0] - 1), 0)),
                  pl.BlockSpec((1, d, f), lambda i, te, nu: (te[i], 0, 0)),
                  pl.BlockSpec((1, d, f), lambda i, te, nu: (te[i], 0, 0)),
                  pl.BlockSpec((1, f, d), lambda i, te, nu: (te[i], 0, 0))],
        out_specs=pl.BlockSpec((tm, d), lambda i, te, nu: (i, 0)),
    )
    return pl.pallas_call(
        _expert_kernel,
        grid_spec=grid_spec,
        out_shape=jax.ShapeDtypeStruct((n_rows, d), F32),
        compiler_params=_cparams(("arbitrary",), 56),
        name="moe_experts",
    )(tile_e, n_used, x_rows, w_gate, w_up, w_down)


def _moe_ln(h, hb, hp, w_router, router_bias, w_exp_gate, w_exp_up, w_exp_down, w_sh_gate, w_sh_up, w_sh_down,
            ln_g, ln_b):
    s, d = h.shape
    tm = MOE_TILE
    top_e, top_w, top_pos, counts = _router(hb, w_router, router_bias)
    cnt = counts[:, 0].astype(jnp.int32)
    tiles_e = (cnt + tm - 1) // tm
    tile_end = jnp.cumsum(tiles_e)
    row_start = (tile_end - tiles_e) * tm
    experts = jnp.arange(N_EXPERTS, dtype=jnp.int32)
    dest = top_pos + jnp.sum(jnp.where(top_e[..., None] == experts, row_start, 0), axis=-1)
    n_tiles = s * TOP_K // tm + N_EXPERTS
    tile_e = jnp.minimum(jnp.sum(tile_end[None, :] <= jnp.arange(n_tiles, dtype=jnp.int32)[:, None], axis=1),
                         N_EXPERTS - 1).astype(jnp.int32)
    n_used = tile_end[-1:].astype(jnp.int32)
    x_rows = _dispatch(hp, dest, n_tiles * tm)
    y_rows = _experts(x_rows, tile_e, n_used, w_exp_gate, w_exp_up, w_exp_down, tm=tm)
    shared = _experts(hp, jnp.zeros((s // tm,), jnp.int32), jnp.full((1,), s // tm, jnp.int32),
                      w_sh_gate[None], w_sh_up[None], w_sh_down[None], tm=tm)
    return _combine_ln(h, y_rows, dest, top_w, shared, ln_g, ln_b)


def _project(hb, w_in_all, layer):
    n_qkv = 3 * MOBA_WIDTH
    pa = _proj_bulk(hb, w_in_all, layer, col0=0, ncols=n_qkv, tm=1024, tn=1024, out_dtype=MXU)
    ps = _proj_bulk(hb, w_in_all, layer, col0=n_qkv, ncols=W_IN_BULK - n_qkv, tm=1024, tn=1024, out_dtype=F32)
    wn, wg, w3 = _relayout_w_tail(w_in_all[layer, :, W_IN_BULK:])
    pn = _matmul(hb, wn, tm=1024, tn=1280, out_dtype=MXU)
    pg = _matmul(hb, wg, tm=1024, tn=1536, out_dtype=F32)
    p3 = _matmul(hb, w3, tm=1024, tn=P3_COLS, out_dtype=F32)
    return pa, pn, ps, pg, p3


def _mixer(hb, w_in_all, layer, conv_w, conv_b, dt_bias, a_log, d_skip, ssm_norm_g,
           cmp_pos_k, cmp_w1_k, cmp_w2_k, cmp_pos_v, cmp_w1_v, cmp_w2_v, w_br_a, w_br_b, w_br_c):
    pa, pn, ps, pg, p3 = _project(hb, w_in_all, layer)
    y_a = _moba(pa)
    y_b = _ssd(ps, p3, conv_w, conv_b, dt_bias, a_log, d_skip, ssm_norm_g)
    kvc = _nsa_compress(pn, cmp_pos_k, cmp_w1_k, cmp_w2_k, cmp_pos_v, cmp_w1_v, cmp_w2_v)
    y_c, selb = _nsa_cmp(pn, p3, kvc)
    y_c = _nsa_sel(pn, p3, selb, y_c)
    y_c = _nsa_win(pn, p3, y_c)
    return _merge(y_a, y_b, y_c, pg, w_br_a, w_br_b, w_br_c)


def kernel(x, w_in, conv_w, conv_b, dt_bias, a_log, d_skip, ssm_norm_g, cmp_pos_k, cmp_w1_k, cmp_w2_k, cmp_pos_v, cmp_w1_v, cmp_w2_v, w_br_a, w_br_b, w_br_c, w_out, ln1_g, ln1_b, w_router, router_bias, w_exp_gate, w_exp_up, w_exp_down, w_sh_gate, w_sh_up, w_sh_down, ln2_g, ln2_b):
    bsz, s, d = x.shape
    assert bsz == 1
    h = x.reshape(s, d)
    hb = h.astype(MXU)
    for l in range(w_in.shape[0]):
        merged = _mixer(hb, w_in, l, conv_w[l], conv_b[l], dt_bias[l], a_log[l], d_skip[l], ssm_norm_g[l],
                        cmp_pos_k[l], cmp_w1_k[l], cmp_w2_k[l], cmp_pos_v[l], cmp_w1_v[l], cmp_w2_v[l],
                        w_br_a[l], w_br_b[l], w_br_c[l])
        h, hb, hp = _wout_ln(merged, w_out[l], h, ln1_g[l], ln1_b[l])
        h, hb = _moe_ln(h, hb, hp, w_router[l], router_bias[l], w_exp_gate[l], w_exp_up[l], w_exp_down[l],
                        w_sh_gate[l], w_sh_up[l], w_sh_down[l], ln2_g[l], ln2_b[l])
    return h.reshape(bsz, s, d)
```

```python
import functools

import numpy as np
import jax
import jax.numpy as jnp
from jax import lax
from jax.experimental import pallas as pl
from jax.experimental.pallas import tpu as pltpu

F32 = jnp.float32
MXU = jnp.bfloat16

D_MODEL = 2048
DEPTH = 2
HEAD_DIM = 128
MOBA_HEADS = 8
MOBA_WIDTH = MOBA_HEADS * HEAD_DIM
MOBA_BLOCK = 256
MOBA_TOPK = 3
SSM_D_INNER = D_MODEL
SSM_HEAD_DIM = 64
SSM_HEADS = SSM_D_INNER // SSM_HEAD_DIM
SSM_STATE = 128
SSM_GROUPS = 8
SSM_HEADS_PER_GROUP = SSM_HEADS // SSM_GROUPS
SSM_GROUP_WIDTH = SSM_D_INNER // SSM_GROUPS
SSM_CONV = 4
SSM_CHUNK = 256
SSM_CONV_CH = SSM_D_INNER + 2 * SSM_GROUPS * SSM_STATE
NSA_HEADS = 8
NSA_KV_GROUPS = 2
NSA_HEADS_PER_GROUP = NSA_HEADS // NSA_KV_GROUPS
NSA_WIDTH = NSA_HEADS * HEAD_DIM
NSA_KV_WIDTH = NSA_KV_GROUPS * HEAD_DIM
NSA_CMP_LEN = 32
NSA_CMP_STRIDE = 16
NSA_SEL_BLOCK = 64
NSA_TOPN = 16
NSA_WINDOW = 512
N_BRANCHES = 3
N_EXPERTS = 64
N_EXPERT_GROUPS = 8
EXPERTS_PER_GROUP = N_EXPERTS // N_EXPERT_GROUPS
TOPK_GROUPS = 4
TOP_K = 8
D_EXPERT = 512
ROUTED_SCALE = 2.5
DEEPNORM_ALPHA = (2 * DEPTH) ** 0.25
LN_EPS = 1e-5
RMS_EPS = 1e-5
IN_SPLIT_SIZES = (MOBA_WIDTH, MOBA_WIDTH, MOBA_WIDTH,
                  SSM_D_INNER, SSM_CONV_CH, SSM_HEADS,
                  NSA_WIDTH, 6 * NSA_KV_WIDTH, N_BRANCHES * NSA_HEADS,
                  N_BRANCHES * D_MODEL)

LANES = 128
SUBLANES = 8
ATT_TILE = 256
MOE_TILE = 256
ATT_SCALE = HEAD_DIM ** -0.5
MASK_BIAS = -2.0 ** 30
NEG = -1e30

NT = (((1,), (1,)), ((), ()))

W_IN_BULK = 3 * MOBA_WIDTH + SSM_D_INNER + SSM_CONV_CH
P1_QA, P1_KA, P1_VA = 0, 1024, 2048
P1_QN, P1_KVN = 0, 1024
P2_Z, P2_XBC = 0, 2048
P2_GM = 0
P3_COLS = (SSM_GROUPS + NSA_KV_GROUPS) * LANES


def _cparams(semantics, vmem_mb=48):
    return pltpu.CompilerParams(dimension_semantics=semantics, vmem_limit_bytes=vmem_mb * 1024 * 1024)


def _sigmoid(x):
    return 1.0 / (1.0 + jnp.exp(-x))


def _silu(x):
    return x * _sigmoid(x)


def _split3(x):
    hi = x.astype(jnp.bfloat16)
    r1 = x - hi.astype(F32)
    mid = r1.astype(jnp.bfloat16)
    lo = (r1 - mid.astype(F32)).astype(jnp.bfloat16)
    return hi, mid, lo


def _dot3(a_exact, x, left=True):
    acc = None
    for part in _split3(x):
        t = (jnp.dot(a_exact, part, preferred_element_type=F32) if left
             else jnp.dot(part, a_exact, preferred_element_type=F32))
        acc = t if acc is None else acc + t
    return acc


def _mm_kernel(a_ref, b_ref, o_ref):
    o_ref[...] = jnp.dot(a_ref[...].astype(MXU), b_ref[...].astype(MXU),
                         preferred_element_type=F32).astype(o_ref.dtype)


def _matmul(a, b, *, tm, tn, out_dtype):
    m, k = a.shape
    n = b.shape[1]
    assert m % tm == 0 and n % tn == 0
    return pl.pallas_call(
        _mm_kernel,
        grid=(n // tn, m // tm),
        in_specs=[pl.BlockSpec((tm, k), lambda j, i: (i, 0)),
                  pl.BlockSpec((k, tn), lambda j, i: (0, j))],
        out_specs=pl.BlockSpec((tm, tn), lambda j, i: (i, j)),
        out_shape=jax.ShapeDtypeStruct((m, n), out_dtype),
        compiler_params=_cparams(("arbitrary", "arbitrary"), 56),
        name="proj_matmul",
    )(a, b)


def _mm_bulk_kernel(a_ref, b_ref, o_ref, wb_ref):
    @pl.when(pl.program_id(1) == 0)
    def _cast_weights():
        wb_ref[...] = b_ref[...].astype(MXU)

    o_ref[...] = jnp.dot(a_ref[...].astype(MXU), wb_ref[...], preferred_element_type=F32).astype(o_ref.dtype)


def _proj_bulk(a, w_all, layer, *, col0, ncols, tm, tn, out_dtype):
    m, k = a.shape
    assert m % tm == 0 and ncols % tn == 0 and col0 % tn == 0
    return pl.pallas_call(
        _mm_bulk_kernel,
        grid=(ncols // tn, m // tm),
        in_specs=[pl.BlockSpec((tm, k), lambda j, i: (i, 0)),
                  pl.BlockSpec((None, k, tn), lambda j, i: (layer, 0, col0 // tn + j))],
        out_specs=pl.BlockSpec((tm, tn), lambda j, i: (i, j)),
        out_shape=jax.ShapeDtypeStruct((m, ncols), out_dtype),
        scratch_shapes=[pltpu.VMEM((k, tn), MXU)],
        compiler_params=_cparams(("arbitrary", "arbitrary"), 56),
        name="proj_bulk_matmul",
    )(a, w_all)


def _relayout_w_tail(w_tail):
    off = np.concatenate([[0], np.cumsum(IN_SPLIT_SIZES[5:])])
    dt, qn, kvn, gn, gm = [w_tail[:, off[i]:off[i + 1]] for i in range(5)]
    d = w_tail.shape[0]
    wn = jnp.concatenate([qn, kvn], axis=1).astype(MXU)
    dtp = jnp.pad(dt.reshape(d, SSM_GROUPS, SSM_HEADS_PER_GROUP),
                  ((0, 0), (0, 0), (0, LANES - SSM_HEADS_PER_GROUP))).reshape(d, SSM_GROUPS * LANES)
    ng = NSA_HEADS_PER_GROUP * N_BRANCHES
    gnp = jnp.pad(gn.reshape(d, NSA_KV_GROUPS, ng), ((0, 0), (0, 0), (0, LANES - ng))).reshape(d, NSA_KV_GROUPS * LANES)
    w3 = jnp.concatenate([dtp, gnp], axis=1).astype(MXU)
    return wn, gm.astype(MXU), w3


def _flash_init(m_ref, acc_ref):
    m_ref[...] = jnp.full(m_ref.shape, NEG, F32)
    acc_ref[...] = jnp.zeros(acc_ref.shape, F32)


def _flash_step(s, v_aug, m_ref, acc_ref):
    hd = HEAD_DIM
    m_prev = m_ref[...]
    m_new = jnp.maximum(m_prev, jnp.max(s, axis=1, keepdims=True))
    alpha = jnp.exp(m_prev - m_new)
    p = jnp.concatenate([jnp.exp(s[:, c * hd:(c + 1) * hd] - m_new) for c in range(s.shape[1] // hd)], axis=1)
    pv = jnp.dot(p.astype(MXU), v_aug, preferred_element_type=F32)
    acc_ref[:, 0:hd] = alpha * acc_ref[:, 0:hd] + pv[:, 0:hd]
    acc_ref[:, hd:2 * hd] = alpha * acc_ref[:, hd:2 * hd] + pv[:, hd:2 * hd]
    m_ref[...] = m_new


def _flash_out(acc_ref):
    return acc_ref[:, 0:HEAD_DIM] / acc_ref[:, HEAD_DIM:2 * HEAD_DIM]


def _flash_loop(n, scores, values, sa_ref, sb_ref, m_ref, acc_ref):
    @pl.when(n > 0)
    def _first():
        sa_ref[...] = scores(0)

    def pair(u, c):
        t = 2 * u
        sb_ref[...] = scores(t + 1)
        _flash_step(sa_ref[...], values(t), m_ref, acc_ref)
        sa_ref[...] = scores(t + 2)
        _flash_step(sb_ref[...], values(t + 1), m_ref, acc_ref)
        return c

    n_pairs = jnp.maximum(n - 1, 0) // 2
    lax.fori_loop(0, n_pairs, pair, 0)
    t0 = 2 * n_pairs
    left = n - t0

    @pl.when(left == 2)
    def _last_two():
        sb_ref[...] = scores(t0 + 1)
        _flash_step(sa_ref[...], values(t0), m_ref, acc_ref)
        _flash_step(sb_ref[...], values(t0 + 1), m_ref, acc_ref)

    @pl.when(left == 1)
    def _last_one():
        _flash_step(sa_ref[...], values(t0), m_ref, acc_ref)


MOBA_KV_GROUP = 4


def _moba_kernel(q_ref, k_ref, v_ref, o_ref, kaug_ref, vaug_ref, kmean_ref, m_ref, acc_ref, sa_ref, sb_ref, *, nb):
    blk = MOBA_BLOCK
    grp = MOBA_KV_GROUP * blk
    qi = pl.program_id(1)

    @pl.when(qi == 0)
    def _build_keys():
        kmean_ref[...] = jnp.zeros_like(kmean_ref)
        lane = lax.broadcasted_iota(jnp.int32, (blk, LANES), 1)
        ones = jnp.ones((blk, HEAD_DIM), vaug_ref.dtype)

        def body(j, c):
            r0 = pl.multiple_of(j * blk, blk)
            kb = k_ref[pl.ds(r0, blk), :]
            kaug_ref[pl.ds(r0, blk), 0:HEAD_DIM] = kb.astype(kaug_ref.dtype)
            kaug_ref[pl.ds(r0, blk), HEAD_DIM:2 * HEAD_DIM] = (lane == j).astype(kaug_ref.dtype)
            vaug_ref[pl.ds(r0, blk), 0:HEAD_DIM] = v_ref[pl.ds(r0, blk), :].astype(vaug_ref.dtype)
            vaug_ref[pl.ds(r0, blk), HEAD_DIM:2 * HEAD_DIM] = ones
            kmean_ref[pl.ds(j, 1), :] = jnp.sum(kb.astype(F32), axis=0, keepdims=True) * (1.0 / blk)
            return c

        lax.fori_loop(0, nb, body, 0)

    q = q_ref[...].astype(MXU)
    gate = lax.dot_general(q, kmean_ref[...].astype(MXU), NT, preferred_element_type=F32)
    lane = lax.broadcasted_iota(jnp.int32, (blk, LANES), 1).astype(F32)
    past = lane < qi.astype(F32)
    gate = jnp.where(past, gate, -jnp.inf)
    sel = jnp.zeros((blk, LANES), jnp.bool_)
    for _ in range(MOBA_TOPK):
        mx = jnp.max(gate, axis=1, keepdims=True)
        idx = jnp.min(jnp.where((gate == mx) & past, lane, float(LANES)), axis=1, keepdims=True)
        pick = lane == idx
        sel = sel | pick
        gate = jnp.where(pick, -jnp.inf, gate)
    bias = jnp.where(sel, 0.0, MASK_BIAS).astype(MXU)
    qaug = jnp.concatenate([q, bias], axis=1)

    _flash_init(m_ref, acc_ref)

    def scores(g):
        r0 = pl.multiple_of(g * grp, grp)
        return lax.dot_general(qaug, kaug_ref[pl.ds(r0, grp), :], NT, preferred_element_type=F32) * ATT_SCALE

    def values(g):
        return vaug_ref[pl.ds(pl.multiple_of(g * grp, grp), grp), :]

    _flash_loop((qi + MOBA_KV_GROUP - 1) // MOBA_KV_GROUP, scores, values, sa_ref, sb_ref, m_ref, acc_ref)
    r0 = pl.multiple_of(qi * blk, blk)
    s = lax.dot_general(q, k_ref[pl.ds(r0, blk), :].astype(MXU), NT, preferred_element_type=F32) * ATT_SCALE
    row = lax.broadcasted_iota(jnp.int32, (blk, blk), 0)
    col = lax.broadcasted_iota(jnp.int32, (blk, blk), 1)
    _flash_step(jnp.where(col <= row, s, NEG), vaug_ref[pl.ds(r0, blk), :], m_ref, acc_ref)
    o_ref[...] = _flash_out(acc_ref).astype(o_ref.dtype)


def _moba(p1):
    s = p1.shape[0]
    nb = s // MOBA_BLOCK
    assert nb <= LANES and nb % MOBA_KV_GROUP == 0
    kcol, vcol = P1_KA // HEAD_DIM, P1_VA // HEAD_DIM
    return pl.pallas_call(
        functools.partial(_moba_kernel, nb=nb),
        grid=(MOBA_HEADS, nb),
        in_specs=[pl.BlockSpec((MOBA_BLOCK, HEAD_DIM), lambda h, i: (i, h)),
                  pl.BlockSpec((s, HEAD_DIM), lambda h, i: (0, kcol + h)),
                  pl.BlockSpec((s, HEAD_DIM), lambda h, i: (0, vcol + h))],
        out_specs=pl.BlockSpec((MOBA_BLOCK, HEAD_DIM), lambda h, i: (i, h)),
        out_shape=jax.ShapeDtypeStruct((s, MOBA_WIDTH), MXU),
        scratch_shapes=[pltpu.VMEM((s, 2 * HEAD_DIM), MXU), pltpu.VMEM((s, 2 * HEAD_DIM), MXU),
                        pltpu.VMEM((LANES, HEAD_DIM), F32),
                        pltpu.VMEM((MOBA_BLOCK, HEAD_DIM), F32), pltpu.VMEM((MOBA_BLOCK, 2 * HEAD_DIM), F32),
                        pltpu.VMEM((MOBA_BLOCK, MOBA_KV_GROUP * MOBA_BLOCK), F32),
                        pltpu.VMEM((MOBA_BLOCK, MOBA_KV_GROUP * MOBA_BLOCK), F32)],
        compiler_params=_cparams(("arbitrary", "arbitrary")),
        name="moba_attention",
    )(p1, p1, p1)


def _ssd_kernel(dtb_ref, alog_ref, dskip_ref,
                xs_ref, bm_ref, cm_ref, z_ref, dt_ref, cwx_ref, cwb_ref, cwc_ref, cbx_ref, cbb_ref, cbc_ref,
                ng_ref, o_ref, xbuf, bbuf, cbuf, state_ref, ybuf):
    L = SSM_CHUNK
    W = SSM_GROUP_WIDTH
    J = SSM_HEADS_PER_GROUP
    P = SSM_HEAD_DIM
    g = pl.program_id(0)
    c = pl.program_id(1)

    @pl.when(c == 0)
    def _reset():
        xbuf[0:SUBLANES, :] = jnp.zeros((SUBLANES, W), F32)
        bbuf[0:SUBLANES, :] = jnp.zeros((SUBLANES, SSM_STATE), F32)
        cbuf[0:SUBLANES, :] = jnp.zeros((SUBLANES, SSM_STATE), F32)
        state_ref[...] = jnp.zeros_like(state_ref)

    def conv_silu(buf, raw_ref, w_ref, b_ref):
        buf[SUBLANES:SUBLANES + L, :] = raw_ref[...]
        acc = b_ref[...]
        for i in range(SSM_CONV):
            lo = SUBLANES - (SSM_CONV - 1) + i
            acc = acc + w_ref[i:i + 1, :] * buf[lo:lo + L, :]
        buf[0:SUBLANES, :] = buf[L:L + SUBLANES, :]
        return _silu(acc)

    xs = conv_silu(xbuf, xs_ref, cwx_ref, cbx_ref)
    bm = conv_silu(bbuf, bm_ref, cwb_ref, cbb_ref)
    cm = conv_silu(cbuf, cm_ref, cwc_ref, cbc_ref)

    lane = lax.broadcasted_iota(jnp.int32, (1, LANES), 1)
    dtb = jnp.zeros((1, LANES), F32)
    alog = jnp.full((1, LANES), -jnp.inf, F32)
    for j in range(J):
        dtb = jnp.where(lane == j, dtb_ref[g * J + j], dtb)
        alog = jnp.where(lane == j, alog_ref[g * J + j], alog)
    x = dt_ref[...] + dtb
    dt = jnp.maximum(x, 0.0) + jnp.log(1.0 + jnp.exp(-jnp.abs(x)))
    dt = jnp.where(lane < J, dt, 0.0)
    a = dt * (-jnp.exp(alog))

    er = lax.broadcasted_iota(jnp.int32, (LANES, W), 0)
    ec = lax.broadcasted_iota(jnp.int32, (LANES, W), 1)
    expand = ((ec >> 6) == er).astype(jnp.bfloat16)
    tr = lax.broadcasted_iota(jnp.int32, (L, L), 0)
    tc = lax.broadcasted_iota(jnp.int32, (L, L), 1)
    tril = tr >= tc
    tril_b = tril.astype(jnp.bfloat16)
    dt_e = _dot3(expand, dt, left=False)
    a_e = _dot3(expand, a, left=False)
    acs = _dot3(tril_b, a_e, left=True)
    acs_t = acs.T
    a_last = acs[L - 1:L, :]

    xdt = xs * dt_e
    cb = lax.dot_general(cm.astype(MXU), bm.astype(MXU), NT, preferred_element_type=F32)
    for j in range(J):
        colv = acs[:, j * P:j * P + 1]
        rowv = acs_t[j * P:j * P + 1, :]
        dec = jnp.exp(jnp.where(tril, colv - rowv, -jnp.inf))
        ybuf[:, j * P:(j + 1) * P] = jnp.dot((cb * dec).astype(MXU), xdt[:, j * P:(j + 1) * P].astype(MXU),
                                             preferred_element_type=F32)
    st_old = state_ref[...]
    xdte = (xdt * jnp.exp(a_last - acs)).astype(MXU)
    st_new = jnp.dot(bm.T.astype(MXU), xdte, preferred_element_type=F32)
    y_off = jnp.dot(cm.astype(MXU), st_old.astype(MXU), preferred_element_type=F32) * jnp.exp(acs)
    state_ref[...] = st_old * jnp.exp(a_last) + st_new

    lane_w = lax.broadcasted_iota(jnp.int32, (1, W), 1)
    dsk = jnp.zeros((1, W), F32)
    for j in range(J):
        dsk = jnp.where((lane_w >> 6) == j, dskip_ref[g * J + j], dsk)
    y = ybuf[...] + y_off + xs * dsk
    y = y * _silu(z_ref[...])
    y = y * lax.rsqrt(jnp.mean(y * y, axis=1, keepdims=True) + RMS_EPS)
    o_ref[...] = (y * ng_ref[...]).astype(o_ref.dtype)


def _ssd(p2, p3, conv_w, conv_b, dt_bias, a_log, d_skip, norm_g):
    s = p2.shape[0]
    L, W, N = SSM_CHUNK, SSM_GROUP_WIDTH, SSM_STATE
    xs0 = P2_XBC // W
    bm0 = (P2_XBC + SSM_D_INNER) // N
    cm0 = bm0 + SSM_GROUPS
    z0 = P2_Z // W
    cb2 = conv_b.reshape(1, SSM_CONV_CH)
    ng2 = norm_g.reshape(1, SSM_D_INNER)
    grid_spec = pltpu.PrefetchScalarGridSpec(
        num_scalar_prefetch=3,
        grid=(SSM_GROUPS, s // L),
        in_specs=[
            pl.BlockSpec((L, W), lambda g, c, *_: (c, xs0 + g)),
            pl.BlockSpec((L, N), lambda g, c, *_: (c, bm0 + g)),
            pl.BlockSpec((L, N), lambda g, c, *_: (c, cm0 + g)),
            pl.BlockSpec((L, W), lambda g, c, *_: (c, z0 + g)),
            pl.BlockSpec((L, LANES), lambda g, c, *_: (c, g)),
            pl.BlockSpec((SSM_CONV, W), lambda g, c, *_: (0, g)),
            pl.BlockSpec((SSM_CONV, N), lambda g, c, *_: (0, SSM_D_INNER // N + g)),
            pl.BlockSpec((SSM_CONV, N), lambda g, c, *_: (0, SSM_D_INNER // N + SSM_GROUPS + g)),
            pl.BlockSpec((1, W), lambda g, c, *_: (0, g)),
            pl.BlockSpec((1, N), lambda g, c, *_: (0, SSM_D_INNER // N + g)),
            pl.BlockSpec((1, N), lambda g, c, *_: (0, SSM_D_INNER // N + SSM_GROUPS + g)),
            pl.BlockSpec((1, W), lambda g, c, *_: (0, g)),
        ],
        out_specs=pl.BlockSpec((L, W), lambda g, c, *_: (c, g)),
        scratch_shapes=[pltpu.VMEM((SUBLANES + L, W), F32), pltpu.VMEM((SUBLANES + L, N), F32),
                        pltpu.VMEM((SUBLANES + L, N), F32), pltpu.VMEM((N, W), F32), pltpu.VMEM((L, W), F32)],
    )
    return pl.pallas_call(
        _ssd_kernel,
        grid_spec=grid_spec,
        out_shape=jax.ShapeDtypeStruct((s, SSM_D_INNER), MXU),
        compiler_params=_cparams(("arbitrary", "arbitrary")),
        name="ssd_mixer",
    )(dt_bias, a_log, d_skip, p2, p2, p2, p2, p3, conv_w, conv_w, conv_w, cb2, cb2, cb2, ng2)


def _nsa_compress_kernel(t_ref, pos_ref, w1_ref, w2_ref, o_ref):
    half = NSA_CMP_STRIDE * HEAD_DIM
    t = t_ref[0].astype(F32)
    pos = pos_ref[0]
    lo = (t + pos[:, :half]).astype(MXU)
    hi = (t + pos[:, half:]).astype(MXU)
    w1 = w1_ref[0]
    a = jnp.dot(lo, w1[:half].astype(MXU), preferred_element_type=F32)
    b = jnp.dot(hi, w1[half:].astype(MXU), preferred_element_type=F32)
    n = t.shape[0]
    pre = a + pltpu.roll(b, n - 1, 0)
    act = jax.nn.gelu(pre, approximate=True)
    o_ref[0] = jnp.dot(act.astype(MXU), w2_ref[0].astype(MXU), preferred_element_type=F32).astype(o_ref.dtype)


def _nsa_compress(p1, pos_k, w1_k, w2_k, pos_v, w1_v, w2_v):
    s = p1.shape[0]
    n_str = s // NSA_CMP_STRIDE
    kv = p1[:, P1_KVN:P1_KVN + 2 * NSA_KV_WIDTH]
    t4 = kv.reshape(s, 4, HEAD_DIM).transpose(1, 0, 2).reshape(4, n_str, NSA_CMP_STRIDE * HEAD_DIM)
    pos = jnp.stack([pos_k, pos_v]).reshape(2, 1, NSA_CMP_LEN * HEAD_DIM)
    w1 = jnp.stack([w1_k, w1_v])
    w2 = jnp.stack([w2_k, w2_v])
    return pl.pallas_call(
        _nsa_compress_kernel,
        grid=(4,),
        in_specs=[pl.BlockSpec((1, n_str, NSA_CMP_STRIDE * HEAD_DIM), lambda i: (i, 0, 0)),
                  pl.BlockSpec((1, 1, NSA_CMP_LEN * HEAD_DIM), lambda i: (i // 2, 0, 0)),
                  pl.BlockSpec((1, NSA_CMP_LEN * HEAD_DIM, HEAD_DIM), lambda i: (i // 2, 0, 0)),
                  pl.BlockSpec((1, HEAD_DIM, HEAD_DIM), lambda i: (i // 2, 0, 0))],
        out_specs=pl.BlockSpec((1, n_str, HEAD_DIM), lambda i: (i, 0, 0)),
        out_shape=jax.ShapeDtypeStruct((4, n_str, HEAD_DIM), MXU),
        compiler_params=_cparams(("arbitrary",)),
        name="nsa_compress",
    )(t4, pos, w1, w2)


def _nsa_cmp_kernel(q_ref, kc_ref, vc_ref, gn_ref, o_ref, selb_ref, *, n_sel):
    tq = ATT_TILE
    qi = pl.program_id(1)
    kc = kc_ref[0].astype(MXU)
    vc = vc_ref[0].astype(MXU)
    nc = kc.shape[0]
    pos = qi * tq + lax.broadcasted_iota(jnp.int32, (tq, nc), 0)
    cidx = lax.broadcasted_iota(jnp.int32, (tq, nc), 1)
    valid = (cidx * NSA_CMP_STRIDE + NSA_CMP_LEN - 1 <= pos) & (cidx < nc - 1)
    c_start = lax.broadcasted_iota(jnp.int32, (LANES, nc), 1) * NSA_CMP_STRIDE
    s_start = lax.broadcasted_iota(jnp.int32, (LANES, nc), 0) * NSA_SEL_BLOCK
    overlap_t = ((c_start < s_start + NSA_SEL_BLOCK) & (c_start + NSA_CMP_LEN > s_start)).astype(MXU)
    gates = _sigmoid(gn_ref[...])
    imp = jnp.zeros((LANES, tq), F32)
    for j in range(NSA_HEADS_PER_GROUP):
        q = q_ref[:, j * HEAD_DIM:(j + 1) * HEAD_DIM].astype(MXU)
        s = lax.dot_general(q, kc, NT, preferred_element_type=F32) * ATT_SCALE
        s = jnp.where(valid, s, -jnp.inf)
        m = jnp.max(s, axis=1, keepdims=True)
        m = jnp.where(m > -jnp.inf, m, 0.0)
        e = jnp.exp(s - m)
        den = jnp.sum(e, axis=1, keepdims=True)
        p = (e / jnp.where(den > 0, den, 1.0)).astype(MXU)
        o = jnp.dot(p, vc, preferred_element_type=F32)
        imp = imp + lax.dot_general(overlap_t, p, NT, preferred_element_type=F32)
        o_ref[:, j * HEAD_DIM:(j + 1) * HEAD_DIM] = o * gates[:, N_BRANCHES * j:N_BRANCHES * j + 1]

    blk = lax.broadcasted_iota(jnp.int32, (LANES, tq), 0).astype(F32)
    cur = ((qi * tq + lax.broadcasted_iota(jnp.int32, (LANES, tq), 1)) >> 6).astype(F32)
    allowed = blk <= cur
    forced = (blk == 0.0) | (blk == cur) | (blk == cur - 1.0)
    val = jnp.where(forced, jnp.inf, jnp.where(allowed, imp, -jnp.inf))
    val = jnp.where(blk < float(n_sel), val, -jnp.inf)

    def pick_round(_, c):
        val, sel = c
        mx = jnp.max(val, axis=0, keepdims=True)
        idx = jnp.min(jnp.where(val == mx, blk, float(LANES)), axis=0, keepdims=True)
        pick = blk == idx
        sel = jnp.where(pick & allowed, 1.0, sel)
        val = jnp.where(pick, -jnp.inf, val)
        return val, sel

    _, sel = lax.fori_loop(0, min(NSA_TOPN, n_sel), pick_round, (val, jnp.zeros((LANES, tq), F32)))
    selb_ref[0] = jnp.where(sel.T > 0, 0.0, MASK_BIAS).astype(selb_ref.dtype)


def _nsa_cmp(p1, p3, kvc):
    s = p1.shape[0]
    n_str = s // NSA_CMP_STRIDE
    n_sel = s // NSA_SEL_BLOCK
    assert n_sel <= LANES
    gw = NSA_HEADS_PER_GROUP * HEAD_DIM
    q0 = P1_QN // gw
    return pl.pallas_call(
        functools.partial(_nsa_cmp_kernel, n_sel=n_sel),
        grid=(NSA_KV_GROUPS, s // ATT_TILE),
        in_specs=[pl.BlockSpec((ATT_TILE, gw), lambda g, i: (i, q0 + g)),
                  pl.BlockSpec((1, n_str, HEAD_DIM), lambda g, i: (g, 0, 0)),
                  pl.BlockSpec((1, n_str, HEAD_DIM), lambda g, i: (NSA_KV_GROUPS + g, 0, 0)),
                  pl.BlockSpec((ATT_TILE, LANES), lambda g, i: (i, SSM_GROUPS + g))],
        out_specs=[pl.BlockSpec((ATT_TILE, gw), lambda g, i: (i, g)),
                   pl.BlockSpec((1, ATT_TILE, LANES), lambda g, i: (g, i, 0))],
        out_shape=[jax.ShapeDtypeStruct((s, NSA_WIDTH), F32),
                   jax.ShapeDtypeStruct((NSA_KV_GROUPS, s, LANES), MXU)],
        compiler_params=_cparams(("arbitrary", "arbitrary")),
        name="nsa_compressed_attention",
    )(p1, kvc, kvc, p3)


def _nsa_sel_kernel(q_ref, selb_ref, k_ref, v_ref, gn_ref, prev_ref, o_ref, kaug_ref, vaug_ref, qaug_ref,
                    m_ref, acc_ref, sa_ref, sb_ref, *, n_tiles):
    tq = ATT_TILE
    J = NSA_HEADS_PER_GROUP
    qi = pl.program_id(1)

    @pl.when(qi == 0)
    def _build_keys():
        lane = lax.broadcasted_iota(jnp.int32, (tq, LANES), 1)
        rowi = lax.broadcasted_iota(jnp.int32, (tq, LANES), 0)
        ones = jnp.ones((tq, HEAD_DIM), vaug_ref.dtype)

        def body(t, c):
            r0 = pl.multiple_of(t * tq, tq)
            kaug_ref[pl.ds(r0, tq), 0:HEAD_DIM] = k_ref[pl.ds(r0, tq), :].astype(kaug_ref.dtype)
            kaug_ref[pl.ds(r0, tq), HEAD_DIM:2 * HEAD_DIM] = (
                lane == ((t * tq + rowi) >> 6)).astype(kaug_ref.dtype)
            vaug_ref[pl.ds(r0, tq), 0:HEAD_DIM] = v_ref[pl.ds(r0, tq), :].astype(vaug_ref.dtype)
            vaug_ref[pl.ds(r0, tq), HEAD_DIM:2 * HEAD_DIM] = ones
            return c

        lax.fori_loop(0, n_tiles, body, 0)

    selb = selb_ref[0]
    for j in range(J):
        qaug_ref[j * tq:(j + 1) * tq, 0:HEAD_DIM] = q_ref[:, j * HEAD_DIM:(j + 1) * HEAD_DIM].astype(qaug_ref.dtype)
        qaug_ref[j * tq:(j + 1) * tq, HEAD_DIM:2 * HEAD_DIM] = selb
    _flash_init(m_ref, acc_ref)

    def scores(t):
        r0 = pl.multiple_of(t * tq, tq)
        return lax.dot_general(qaug_ref[...], kaug_ref[pl.ds(r0, tq), :], NT,
                               preferred_element_type=F32) * ATT_SCALE

    def values(t):
        return vaug_ref[pl.ds(pl.multiple_of(t * tq, tq), tq), :]

    _flash_loop(qi, scores, values, sa_ref, sb_ref, m_ref, acc_ref)
    row = lax.broadcasted_iota(jnp.int32, (J * tq, tq), 0) & (tq - 1)
    col = lax.broadcasted_iota(jnp.int32, (J * tq, tq), 1)
    _flash_step(jnp.where(col <= row, scores(qi), NEG), values(qi), m_ref, acc_ref)
    o = _flash_out(acc_ref)
    gates = _sigmoid(gn_ref[...])
    for j in range(J):
        o_ref[:, j * HEAD_DIM:(j + 1) * HEAD_DIM] = (
            prev_ref[:, j * HEAD_DIM:(j + 1) * HEAD_DIM]
            + o[j * tq:(j + 1) * tq] * gates[:, N_BRANCHES * j + 1:N_BRANCHES * j + 2])


def _nsa_sel(p1, p3, selb, prev):
    s = p1.shape[0]
    gw = NSA_HEADS_PER_GROUP * HEAD_DIM
    q0 = P1_QN // gw
    k0 = (P1_KVN + 2 * NSA_KV_WIDTH) // HEAD_DIM
    v0 = k0 + NSA_KV_GROUPS
    return pl.pallas_call(
        functools.partial(_nsa_sel_kernel, n_tiles=s // ATT_TILE),
        grid=(NSA_KV_GROUPS, s // ATT_TILE),
        in_specs=[pl.BlockSpec((ATT_TILE, gw), lambda g, i: (i, q0 + g)),
                  pl.BlockSpec((1, ATT_TILE, LANES), lambda g, i: (g, i, 0)),
                  pl.BlockSpec((s, HEAD_DIM), lambda g, i: (0, k0 + g)),
                  pl.BlockSpec((s, HEAD_DIM), lambda g, i: (0, v0 + g)),
                  pl.BlockSpec((ATT_TILE, LANES), lambda g, i: (i, SSM_GROUPS + g)),
                  pl.BlockSpec((ATT_TILE, gw), lambda g, i: (i, g))],
        out_specs=pl.BlockSpec((ATT_TILE, gw), lambda g, i: (i, g)),
        out_shape=jax.ShapeDtypeStruct((s, NSA_WIDTH), F32),
        scratch_shapes=[pltpu.VMEM((s, 2 * HEAD_DIM), MXU), pltpu.VMEM((s, 2 * HEAD_DIM), MXU),
                        pltpu.VMEM((NSA_HEADS_PER_GROUP * ATT_TILE, 2 * HEAD_DIM), MXU),
                        pltpu.VMEM((NSA_HEADS_PER_GROUP * ATT_TILE, HEAD_DIM), F32),
                        pltpu.VMEM((NSA_HEADS_PER_GROUP * ATT_TILE, 2 * HEAD_DIM), F32),
                        pltpu.VMEM((NSA_HEADS_PER_GROUP * ATT_TILE, ATT_TILE), F32),
                        pltpu.VMEM((NSA_HEADS_PER_GROUP * ATT_TILE, ATT_TILE), F32)],
        compiler_params=_cparams(("arbitrary", "arbitrary")),
        name="nsa_selected_attention",
    )(p1, selb, p1, p1, p3, prev)


def _nsa_win_kernel(q_ref, k_ref, v_ref, gn_ref, prev_ref, o_ref, qst_ref, m_ref, acc_ref):
    tq = ATT_TILE
    J = NSA_HEADS_PER_GROUP
    halo = NSA_WINDOW // tq
    qi = pl.program_id(1)
    for j in range(J):
        qst_ref[j * tq:(j + 1) * tq, :] = q_ref[:, j * HEAD_DIM:(j + 1) * HEAD_DIM].astype(qst_ref.dtype)
    row = lax.broadcasted_iota(jnp.int32, (J * tq, tq), 0) & (tq - 1)
    col = lax.broadcasted_iota(jnp.int32, (J * tq, tq), 1)
    ones = jnp.ones((tq, HEAD_DIM), MXU)
    _flash_init(m_ref, acc_ref)

    def step(t, c):
        r0 = pl.multiple_of(t * tq, tq)
        s = lax.dot_general(qst_ref[...], k_ref[pl.ds(r0, tq), :].astype(MXU), NT,
                            preferred_element_type=F32) * ATT_SCALE
        diff = (qi - t) * tq + row - col
        s = jnp.where((diff >= 0) & (diff < NSA_WINDOW), s, NEG)
        v_aug = jnp.concatenate([v_ref[pl.ds(r0, tq), :].astype(MXU), ones], axis=1)
        _flash_step(s, v_aug, m_ref, acc_ref)
        return c

    lax.fori_loop(jnp.maximum(qi - halo, 0), qi + 1, step, 0)
    o = _flash_out(acc_ref)
    gates = _sigmoid(gn_ref[...])
    for j in range(J):
        o_ref[:, j * HEAD_DIM:(j + 1) * HEAD_DIM] = (
            prev_ref[:, j * HEAD_DIM:(j + 1) * HEAD_DIM]
            + o[j * tq:(j + 1) * tq] * gates[:, N_BRANCHES * j + 2:N_BRANCHES * j + 3])


def _nsa_win(p1, p3, prev):
    s = p1.shape[0]
    gw = NSA_HEADS_PER_GROUP * HEAD_DIM
    q0 = P1_QN // gw
    k0 = (P1_KVN + 4 * NSA_KV_WIDTH) // HEAD_DIM
    v0 = k0 + NSA_KV_GROUPS
    return pl.pallas_call(
        _nsa_win_kernel,
        grid=(NSA_KV_GROUPS, s // ATT_TILE),
        in_specs=[pl.BlockSpec((ATT_TILE, gw), lambda g, i: (i, q0 + g)),
                  pl.BlockSpec((s, HEAD_DIM), lambda g, i: (0, k0 + g)),
                  pl.BlockSpec((s, HEAD_DIM), lambda g, i: (0, v0 + g)),
                  pl.BlockSpec((ATT_TILE, LANES), lambda g, i: (i, SSM_GROUPS + g)),
                  pl.BlockSpec((ATT_TILE, gw), lambda g, i: (i, g))],
        out_specs=pl.BlockSpec((ATT_TILE, gw), lambda g, i: (i, g)),
        out_shape=jax.ShapeDtypeStruct((s, NSA_WIDTH), F32),
        scratch_shapes=[pltpu.VMEM((NSA_HEADS_PER_GROUP * ATT_TILE, HEAD_DIM), MXU),
                        pltpu.VMEM((NSA_HEADS_PER_GROUP * ATT_TILE, HEAD_DIM), F32),
                        pltpu.VMEM((NSA_HEADS_PER_GROUP * ATT_TILE, 2 * HEAD_DIM), F32)],
        compiler_params=_cparams(("arbitrary", "arbitrary")),
        name="nsa_window_attention",
    )(p1, p1, p1, p3, prev)


def _merge_kernel(ya_ref, yb_ref, yc_ref, g0_ref, g1_ref, g2_ref, wa_ref, wb_ref, wc_ref, o_ref):
    def branch(y_ref, w_ref, g_ref):
        prod = jnp.dot(y_ref[...].astype(MXU), w_ref[...].astype(MXU), preferred_element_type=F32)
        return _sigmoid(g_ref[...]) * prod

    o_ref[...] = (branch(ya_ref, wa_ref, g0_ref) + branch(yb_ref, wb_ref, g1_ref)
                  + branch(yc_ref, wc_ref, g2_ref)).astype(o_ref.dtype)


def _merge(ya, yb, yc, p2, wa, wb, wc, *, tm=512, tn=512):
    s = ya.shape[0]
    d = wa.shape[1]
    g0 = P2_GM // tn
    gstep = d // tn
    return pl.pallas_call(
        _merge_kernel,
        grid=(d // tn, s // tm),
        in_specs=[pl.BlockSpec((tm, ya.shape[1]), lambda j, i: (i, 0)),
                  pl.BlockSpec((tm, yb.shape[1]), lambda j, i: (i, 0)),
                  pl.BlockSpec((tm, yc.shape[1]), lambda j, i: (i, 0)),
                  pl.BlockSpec((tm, tn), lambda j, i: (i, g0 + j)),
                  pl.BlockSpec((tm, tn), lambda j, i: (i, g0 + gstep + j)),
                  pl.BlockSpec((tm, tn), lambda j, i: (i, g0 + 2 * gstep + j)),
                  pl.BlockSpec((wa.shape[0], tn), lambda j, i: (0, j)),
                  pl.BlockSpec((wb.shape[0], tn), lambda j, i: (0, j)),
                  pl.BlockSpec((wc.shape[0], tn), lambda j, i: (0, j))],
        out_specs=pl.BlockSpec((tm, tn), lambda j, i: (i, j)),
        out_shape=jax.ShapeDtypeStruct((s, d), MXU),
        compiler_params=_cparams(("arbitrary", "arbitrary")),
        name="branch_merge",
    )(ya, yb, yc, p2, p2, p2, wa, wb, wc)


def _layer_norm_rows(x, g, b):
    xc = x - jnp.mean(x, axis=1, keepdims=True)
    var = jnp.mean(xc * xc, axis=1, keepdims=True)
    return xc * lax.rsqrt(var + LN_EPS) * g + b


def _pack_halves(y):
    half = y.shape[1] // 2
    bits = lax.bitcast_convert_type(y.astype(jnp.bfloat16).astype(F32), jnp.int32)
    return ((bits[:, :half] >> 16) & jnp.int32(0xFFFF)) | (bits[:, half:] & jnp.int32(-65536))


def _unpack_halves(w):
    lo = lax.bitcast_convert_type(w << 16, F32)
    hi = lax.bitcast_convert_type(w & jnp.int32(-65536), F32)
    return lo.astype(MXU), hi.astype(MXU)


def _wout_ln_kernel(m_ref, w_ref, h_ref, g_ref, b_ref, o_ref, ob_ref, op_ref):
    acc = jnp.dot(m_ref[...].astype(MXU), w_ref[...].astype(MXU), preferred_element_type=F32)
    y = _layer_norm_rows(DEEPNORM_ALPHA * h_ref[...] + acc, g_ref[...], b_ref[...])
    o_ref[...] = y
    ob_ref[...] = y.astype(ob_ref.dtype)
    op_ref[...] = _pack_halves(y)


def _wout_ln(merged, w_out, h, ln_g, ln_b, *, tm=256):
    s, d = h.shape
    row = pl.BlockSpec((tm, d), lambda i: (i, 0))
    vec = pl.BlockSpec((1, d), lambda i: (0, 0))
    return pl.pallas_call(
        _wout_ln_kernel,
        grid=(s // tm,),
        in_specs=[row, pl.BlockSpec((d, d), lambda i: (0, 0)), row, vec, vec],
        out_specs=[row, row, pl.BlockSpec((tm, d // 2), lambda i: (i, 0))],
        out_shape=[jax.ShapeDtypeStruct((s, d), F32), jax.ShapeDtypeStruct((s, d), MXU),
                   jax.ShapeDtypeStruct((s, d // 2), jnp.int32)],
        compiler_params=_cparams(("arbitrary",)),
        name="out_proj_layernorm",
    )(merged, w_out.astype(MXU), h, ln_g.reshape(1, d), ln_b.reshape(1, d))


def _dispatch_kernel(ztile_ref, dest_ref, hp_ref, o_ref, zbuf, sem, zsem):
    tt = dest_ref.shape[1]
    tm = zbuf.shape[0]

    @pl.when(pl.program_id(0) == 0)
    def _zero_partial_tiles():
        zbuf[...] = jnp.zeros_like(zbuf)

        def tile_copy(t):
            return pltpu.make_async_copy(zbuf, o_ref.at[pl.ds(t * tm, tm)], zsem)

        def start(j, c):
            tile_copy(ztile_ref[j]).start()
            return c

        def wait(j, c):
            tile_copy(0).wait()
            return c

        lax.fori_loop(0, ztile_ref.shape[0], start, 0)
        lax.fori_loop(0, ztile_ref.shape[0], wait, 0)

    def issue(r, c):
        for k in range(TOP_K):
            pltpu.make_async_copy(hp_ref.at[pl.ds(r, 1)], o_ref.at[pl.ds(dest_ref[k, r], 1)], sem).start(
                priority=k % 2)
        return c

    lax.fori_loop(0, tt, issue, 0)
    rows = o_ref.at[pl.ds(0, TOP_K * tt)]
    pltpu.make_async_copy(rows, rows, sem).wait()


def _dispatch(hp, dest, zero_tiles, n_rows, *, tm, tt=256):
    s, half = hp.shape
    grid_spec = pltpu.PrefetchScalarGridSpec(
        num_scalar_prefetch=1,
        grid=(s // tt,),
        in_specs=[pl.BlockSpec((TOP_K, tt), lambda i, zt: (0, i), memory_space=pltpu.SMEM),
                  pl.BlockSpec((tt, half), lambda i, zt: (i, 0))],
        out_specs=pl.BlockSpec(memory_space=pl.ANY),
        scratch_shapes=[pltpu.VMEM((tm, half), jnp.int32), pltpu.SemaphoreType.DMA(()),
                        pltpu.SemaphoreType.DMA(())],
    )
    return pl.pallas_call(
        _dispatch_kernel,
        grid_spec=grid_spec,
        out_shape=jax.ShapeDtypeStruct((n_rows, half), jnp.int32),
        compiler_params=_cparams(("arbitrary",)),
        name="moe_dispatch",
    )(zero_tiles, dest, hp)


def _combine_kernel(dcur_ref, dnxt_ref, w_ref, h_ref, sh_ref, g_ref, b_ref, y_ref, o_ref, ob_ref, buf, sem):
    tt = dcur_ref.shape[1]
    i = pl.program_id(0)
    n = pl.num_programs(0)
    slot = i % 2

    def issue(d_ref, sl):
        def body(r, c):
            for k in range(TOP_K):
                pltpu.make_async_copy(y_ref.at[pl.ds(d_ref[k, r], 1)], buf.at[sl, pl.ds(k * tt + r, 1)],
                                      sem.at[sl]).start(priority=k % 2)
            return c

        lax.fori_loop(0, tt, body, 0)

    @pl.when(i == 0)
    def _():
        issue(dcur_ref, slot)

    @pl.when(i + 1 < n)
    def _():
        issue(dnxt_ref, 1 - slot)

    pltpu.make_async_copy(y_ref.at[pl.ds(0, TOP_K * tt)], buf.at[slot], sem.at[slot]).wait()
    w = w_ref[...]
    routed = w[:, 0:1] * buf[slot, 0:tt]
    for k in range(1, TOP_K):
        routed = routed + w[:, k:k + 1] * buf[slot, k * tt:(k + 1) * tt]
    y = _layer_norm_rows(DEEPNORM_ALPHA * h_ref[...] + (routed + sh_ref[...]), g_ref[...], b_ref[...])
    o_ref[...] = y
    ob_ref[...] = y.astype(ob_ref.dtype)


def _combine_ln(h, y_rows, dest, top_w, shared, ln_g, ln_b, *, tt=128):
    s, d = h.shape
    n = s // tt
    row = pl.BlockSpec((tt, d), lambda i: (i, 0))
    vec = pl.BlockSpec((1, d), lambda i: (0, 0))
    return pl.pallas_call(
        _combine_kernel,
        grid=(n,),
        in_specs=[pl.BlockSpec((TOP_K, tt), lambda i: (0, i), memory_space=pltpu.SMEM),
                  pl.BlockSpec((TOP_K, tt), lambda i: (0, jnp.minimum(i + 1, n - 1)), memory_space=pltpu.SMEM),
                  pl.BlockSpec((tt, TOP_K), lambda i: (i, 0)),
                  row, row, vec, vec,
                  pl.BlockSpec(memory_space=pl.ANY)],
        out_specs=[row, row],
        out_shape=[jax.ShapeDtypeStruct((s, d), F32), jax.ShapeDtypeStruct((s, d), MXU)],
        scratch_shapes=[pltpu.VMEM((2, TOP_K * tt, d), F32), pltpu.SemaphoreType.DMA((2,))],
        compiler_params=_cparams(("arbitrary",)),
        name="moe_combine_layernorm",
    )(dest, dest, top_w.T, h, shared, ln_g.reshape(1, d), ln_b.reshape(1, d), y_rows)


def _router_kernel(h_ref, wr_ref, rb_ref, e_ref, w_ref, pos_ref, cnt_ref, carry_ref):
    tq = h_ref.shape[0]
    i = pl.program_id(0)

    @pl.when(i == 0)
    def _reset():
        carry_ref[...] = jnp.zeros_like(carry_ref)

    logits = lax.dot_general(wr_ref[...].astype(MXU), h_ref[...].astype(MXU), NT, preferred_element_type=F32)
    scores = _sigmoid(logits)
    biased = scores + rb_ref[...]
    G, PG = N_EXPERT_GROUPS, EXPERTS_PER_GROUP
    sub = lax.broadcasted_iota(jnp.int32, (PG, tq), 0).astype(F32)
    gi = lax.broadcasted_iota(jnp.int32, (G, tq), 0).astype(F32)
    gs = jnp.zeros((G, tq), F32)
    for g in range(G):
        blk = biased[g * PG:(g + 1) * PG, :]
        m1 = jnp.max(blk, axis=0, keepdims=True)
        i1 = jnp.min(jnp.where(blk == m1, sub, float(PG)), axis=0, keepdims=True)
        m2 = jnp.max(jnp.where(sub == i1, -jnp.inf, blk), axis=0, keepdims=True)
        gs = jnp.where(gi == float(g), m1 + m2, gs)
    keep = jnp.zeros((G, tq), F32)
    for _ in range(TOPK_GROUPS):
        mx = jnp.max(gs, axis=0, keepdims=True)
        idx = jnp.min(jnp.where(gs == mx, gi, float(G)), axis=0, keepdims=True)
        pick = gi == idx
        keep = jnp.where(pick, 1.0, keep)
        gs = jnp.where(pick, -jnp.inf, gs)
    val = jnp.concatenate(
        [jnp.where(keep[g:g + 1, :] > 0, biased[g * PG:(g + 1) * PG, :], -jnp.inf) for g in range(G)], axis=0)
    ei = lax.broadcasted_iota(jnp.int32, (N_EXPERTS, tq), 0).astype(F32)
    picks, svals = [], []
    sel = jnp.zeros((N_EXPERTS, tq), F32)
    for r in range(TOP_K):
        mx = jnp.max(val, axis=0, keepdims=True)
        idx = jnp.min(jnp.where(val == mx, ei, float(N_EXPERTS)), axis=0, keepdims=True)
        pick = ei == idx
        picks.append(pick)
        svals.append(jnp.sum(jnp.where(pick, scores, 0.0), axis=0, keepdims=True))
        e_ref[r:r + 1, :] = idx.astype(jnp.int32)
        sel = jnp.where(pick, 1.0, sel)
        val = jnp.where(pick, -jnp.inf, val)
    wsum = svals[0]
    for r in range(1, TOP_K):
        wsum = wsum + svals[r]
    tr = lax.broadcasted_iota(jnp.int32, (tq, tq), 0)
    tc = lax.broadcasted_iota(jnp.int32, (tq, tq), 1)
    before = (tr < tc).astype(jnp.bfloat16)
    prefix = jnp.dot(sel.astype(jnp.bfloat16), before, preferred_element_type=F32)
    pos = carry_ref[:, 0:1] + prefix
    for r in range(TOP_K):
        w_ref[r:r + 1, :] = svals[r] / wsum * ROUTED_SCALE
        pos_ref[r:r + 1, :] = jnp.sum(jnp.where(picks[r], pos, 0.0), axis=0, keepdims=True).astype(jnp.int32)
    total = carry_ref[...] + jnp.sum(sel, axis=1, keepdims=True)
    carry_ref[...] = total
    cnt_ref[...] = total


def _router(hb, w_router, router_bias, *, tq=256):
    s, d = hb.shape
    row = pl.BlockSpec((TOP_K, tq), lambda i: (0, i))
    return pl.pallas_call(
        _router_kernel,
        grid=(s // tq,),
        in_specs=[pl.BlockSpec((tq, d), lambda i: (i, 0)),
                  pl.BlockSpec((N_EXPERTS, d), lambda i: (0, 0)),
                  pl.BlockSpec((N_EXPERTS, 1), lambda i: (0, 0))],
        out_specs=[row, row, row, pl.BlockSpec((N_EXPERTS, LANES), lambda i: (0, 0))],
        out_shape=[jax.ShapeDtypeStruct((TOP_K, s), jnp.int32), jax.ShapeDtypeStruct((TOP_K, s), F32),
                   jax.ShapeDtypeStruct((TOP_K, s), jnp.int32), jax.ShapeDtypeStruct((N_EXPERTS, LANES), F32)],
        scratch_shapes=[pltpu.VMEM((N_EXPERTS, LANES), F32)],
        compiler_params=_cparams(("arbitrary",)),
        name="moe_router",
    )(hb, w_router.T, router_bias.reshape(N_EXPERTS, 1))


def _expert_kernel(te_ref, nu_ref, x_ref, wg_ref, wu_ref, wd_ref, o_ref):
    i = pl.program_id(0)

    @pl.when(i < nu_ref[0])
    def _compute():
        xa, xb = _unpack_halves(x_ref[...])
        half = xa.shape[1]

        def up(w_ref):
            return (jnp.dot(xa, w_ref[0:half, :].astype(MXU), preferred_element_type=F32)
                    + jnp.dot(xb, w_ref[half:2 * half, :].astype(MXU), preferred_element_type=F32))

        hg = up(wg_ref)
        hu = up(wu_ref)
        act = (_silu(hg) * hu).astype(MXU)
        o_ref[...] = jnp.dot(act, wd_ref[...].astype(MXU), preferred_element_type=F32).astype(o_ref.dtype)

    @pl.when(i >= nu_ref[0])
    def _unused():
        o_ref[...] = jnp.zeros_like(o_ref)


def _experts(x_rows, tile_e, n_used, w_gate, w_up, w_down, layer, *, tm):
    n_rows = x_rows.shape[0]
    d, f = w_gate.shape[2:]
    grid_spec = pltpu.PrefetchScalarGridSpec(
        num_scalar_prefetch=2,
        grid=(n_rows // tm,),
        in_specs=[pl.BlockSpec((tm, d // 2), lambda i, te, nu: (jnp.minimum(i, nu[0] - 1), 0)),
                  pl.BlockSpec((None, None, d, f), lambda i, te, nu: (layer, te[i], 0, 0)),
                  pl.BlockSpec((None, None, d, f), lambda i, te, nu: (layer, te[i], 0, 0)),
                  pl.BlockSpec((None, None, f, d), lambda i, te, nu: (layer, te[i], 0, 0))],
        out_specs=pl.BlockSpec((tm, d), lambda i, te, nu: (i, 0)),
    )
    return pl.pallas_call(
        _expert_kernel,
        grid_spec=grid_spec,
        out_shape=jax.ShapeDtypeStruct((n_rows, d), F32),
        compiler_params=_cparams(("arbitrary",), 56),
        name="moe_experts",
    )(tile_e, n_used, x_rows, w_gate, w_up, w_down)


def _moe_ln(h, hb, hp, layer, w_router, router_bias, w_exp_gate, w_exp_up, w_exp_down, w_sh_gate, w_sh_up, w_sh_down,
            ln_g, ln_b):
    s, d = h.shape
    tm = MOE_TILE
    top_e, top_w, top_pos, counts = _router(hb, w_router, router_bias)
    cnt = counts[:, 0].astype(jnp.int32)
    tiles_e = (cnt + tm - 1) // tm
    tile_end = jnp.cumsum(tiles_e)
    row_start = (tile_end - tiles_e) * tm
    experts = jnp.arange(N_EXPERTS, dtype=jnp.int32)
    dest = top_pos + jnp.sum(jnp.where(top_e[..., None] == experts, row_start, 0), axis=-1)
    n_tiles = s * TOP_K // tm + N_EXPERTS
    tile_e = jnp.minimum(jnp.sum(tile_end[None, :] <= jnp.arange(n_tiles, dtype=jnp.int32)[:, None], axis=1),
                         N_EXPERTS - 1).astype(jnp.int32)
    n_used = tile_end[-1:].astype(jnp.int32)
    zero_tiles = jnp.concatenate([jnp.maximum(tile_end - 1, 0),
                                  jnp.minimum(n_used + experts, n_tiles - 1)]).astype(jnp.int32)
    x_rows = _dispatch(hp, dest, zero_tiles, n_tiles * tm, tm=tm)
    y_rows = _experts(x_rows, tile_e, n_used, w_exp_gate, w_exp_up, w_exp_down, layer, tm=tm)
    shared = _experts(hp, jnp.zeros((s // tm,), jnp.int32), jnp.full((1,), s // tm, jnp.int32),
                      w_sh_gate[:, None], w_sh_up[:, None], w_sh_down[:, None], layer, tm=tm)
    return _combine_ln(h, y_rows, dest, top_w, shared, ln_g, ln_b)


def _project(hb, w_in_all, layer):
    n_qkv = 3 * MOBA_WIDTH
    pa = _proj_bulk(hb, w_in_all, layer, col0=0, ncols=n_qkv, tm=1024, tn=1024, out_dtype=MXU)
    ps = _proj_bulk(hb, w_in_all, layer, col0=n_qkv, ncols=W_IN_BULK - n_qkv, tm=1024, tn=1024, out_dtype=F32)
    wn, wg, w3 = _relayout_w_tail(w_in_all[layer, :, W_IN_BULK:])
    pn = _matmul(hb, wn, tm=1024, tn=1280, out_dtype=MXU)
    pg = _matmul(hb, wg, tm=1024, tn=1536, out_dtype=F32)
    p3 = _matmul(hb, w3, tm=1024, tn=P3_COLS, out_dtype=F32)
    return pa, pn, ps, pg, p3


def _mixer(hb, w_in_all, layer, conv_w, conv_b, dt_bias, a_log, d_skip, ssm_norm_g,
           cmp_pos_k, cmp_w1_k, cmp_w2_k, cmp_pos_v, cmp_w1_v, cmp_w2_v, w_br_a, w_br_b, w_br_c):
    pa, pn, ps, pg, p3 = _project(hb, w_in_all, layer)
    y_a = _moba(pa)
    y_b = _ssd(ps, p3, conv_w, conv_b, dt_bias, a_log, d_skip, ssm_norm_g)
    kvc = _nsa_compress(pn, cmp_pos_k, cmp_w1_k, cmp_w2_k, cmp_pos_v, cmp_w1_v, cmp_w2_v)
    y_c, selb = _nsa_cmp(pn, p3, kvc)
    y_c = _nsa_sel(pn, p3, selb, y_c)
    y_c = _nsa_win(pn, p3, y_c)
    return _merge(y_a, y_b, y_c, pg, w_br_a, w_br_b, w_br_c)


def kernel(x, w_in, conv_w, conv_b, dt_bias, a_log, d_skip, ssm_norm_g, cmp_pos_k, cmp_w1_k, cmp_w2_k, cmp_pos_v, cmp_w1_v, cmp_w2_v, w_br_a, w_br_b, w_br_c, w_out, ln1_g, ln1_b, w_router, router_bias, w_exp_gate, w_exp_up, w_exp_down, w_sh_gate, w_sh_up, w_sh_down, ln2_g, ln2_b):
    bsz, s, d = x.shape
    assert bsz == 1
    h = x.reshape(s, d)
    hb = h.astype(MXU)
    for l in range(w_in.shape[0]):
        merged = _mixer(hb, w_in, l, conv_w[l], conv_b[l], dt_bias[l], a_log[l], d_skip[l], ssm_norm_g[l],
                        cmp_pos_k[l], cmp_w1_k[l], cmp_w2_k[l], cmp_pos_v[l], cmp_w1_v[l], cmp_w2_v[l],
                        w_br_a[l], w_br_b[l], w_br_c[l])
        h, hb, hp = _wout_ln(merged, w_out[l], h, ln1_g[l], ln1_b[l])
        h, hb = _moe_ln(h, hb, hp, l, w_router[l], router_bias[l], w_exp_gate, w_exp_up, w_exp_down,
                        w_sh_gate, w_sh_up, w_sh_down, ln2_g[l], ln2_b[l])
    return h.reshape(bsz, s, d)
```

```python
import functools

import numpy as np
import jax
import jax.numpy as jnp
from jax import lax
from jax.experimental import pallas as pl
from jax.experimental.pallas import tpu as pltpu

F32 = jnp.float32
MXU = jnp.bfloat16

D_MODEL = 2048
DEPTH = 2
HEAD_DIM = 128
MOBA_HEADS = 8
MOBA_WIDTH = MOBA_HEADS * HEAD_DIM
MOBA_BLOCK = 256
MOBA_TOPK = 3
SSM_D_INNER = D_MODEL
SSM_HEAD_DIM = 64
SSM_HEADS = SSM_D_INNER // SSM_HEAD_DIM
SSM_STATE = 128
SSM_GROUPS = 8
SSM_HEADS_PER_GROUP = SSM_HEADS // SSM_GROUPS
SSM_GROUP_WIDTH = SSM_D_INNER // SSM_GROUPS
SSM_CONV = 4
SSM_CHUNK = 256
SSM_CONV_CH = SSM_D_INNER + 2 * SSM_GROUPS * SSM_STATE
NSA_HEADS = 8
NSA_KV_GROUPS = 2
NSA_HEADS_PER_GROUP = NSA_HEADS // NSA_KV_GROUPS
NSA_WIDTH = NSA_HEADS * HEAD_DIM
NSA_KV_WIDTH = NSA_KV_GROUPS * HEAD_DIM
NSA_CMP_LEN = 32
NSA_CMP_STRIDE = 16
NSA_SEL_BLOCK = 64
NSA_TOPN = 16
NSA_WINDOW = 512
N_BRANCHES = 3
N_EXPERTS = 64
N_EXPERT_GROUPS = 8
EXPERTS_PER_GROUP = N_EXPERTS // N_EXPERT_GROUPS
TOPK_GROUPS = 4
TOP_K = 8
D_EXPERT = 512
ROUTED_SCALE = 2.5
DEEPNORM_ALPHA = (2 * DEPTH) ** 0.25
LN_EPS = 1e-5
RMS_EPS = 1e-5
IN_SPLIT_SIZES = (MOBA_WIDTH, MOBA_WIDTH, MOBA_WIDTH,
                  SSM_D_INNER, SSM_CONV_CH, SSM_HEADS,
                  NSA_WIDTH, 6 * NSA_KV_WIDTH, N_BRANCHES * NSA_HEADS,
                  N_BRANCHES * D_MODEL)

LANES = 128
SUBLANES = 8
ATT_TILE = 256
MOE_TILE = 256
ATT_SCALE = HEAD_DIM ** -0.5
MASK_BIAS = -2.0 ** 30
NEG = -1e30

NT = (((1,), (1,)), ((), ()))

P1_QA, P1_KA, P1_VA = 0, 1024, 2048
P1_QN, P1_KVN = 0, 1024
P2_Z, P2_XBC = 0, 2048
P2_GM = 0
P3_COLS = (SSM_GROUPS + NSA_KV_GROUPS) * LANES


def _cparams(semantics, vmem_mb=48):
    return pltpu.CompilerParams(dimension_semantics=semantics, vmem_limit_bytes=vmem_mb * 1024 * 1024)


def _sigmoid(x):
    return 1.0 / (1.0 + jnp.exp(-x))


def _silu(x):
    return x * _sigmoid(x)


def _split3(x):
    hi = x.astype(jnp.bfloat16)
    r1 = x - hi.astype(F32)
    mid = r1.astype(jnp.bfloat16)
    lo = (r1 - mid.astype(F32)).astype(jnp.bfloat16)
    return hi, mid, lo


def _dot3(a_exact, x, left=True):
    acc = None
    for part in _split3(x):
        t = (jnp.dot(a_exact, part, preferred_element_type=F32) if left
             else jnp.dot(part, a_exact, preferred_element_type=F32))
        acc = t if acc is None else acc + t
    return acc


def _mm_kernel(a_ref, b_ref, o_ref):
    o_ref[...] = jnp.dot(a_ref[...].astype(MXU), b_ref[...].astype(MXU),
                         preferred_element_type=F32).astype(o_ref.dtype)


def _matmul(a, b, *, tm, tn, out_dtype):
    m, k = a.shape
    n = b.shape[1]
    assert m % tm == 0 and n % tn == 0
    return pl.pallas_call(
        _mm_kernel,
        grid=(n // tn, m // tm),
        in_specs=[pl.BlockSpec((tm, k), lambda j, i: (i, 0)),
                  pl.BlockSpec((k, tn), lambda j, i: (0, j))],
        out_specs=pl.BlockSpec((tm, tn), lambda j, i: (i, j)),
        out_shape=jax.ShapeDtypeStruct((m, n), out_dtype),
        compiler_params=_cparams(("arbitrary", "arbitrary"), 56),
        name="proj_matmul",
    )(a, b)


def _mm_wt_kernel(a_ref, wt_ref, o_ref, wb_ref):
    @pl.when(pl.program_id(1) == 0)
    def _cast_weights():
        wb_ref[...] = wt_ref[0].astype(MXU)

    o_ref[...] = lax.dot_general(a_ref[...].astype(MXU), wb_ref[...], NT,
                                 preferred_element_type=F32).astype(o_ref.dtype)


def _proj_wt(a, wt_all, layer, *, col0, ncols, tm, tn, out_dtype):
    m, k = a.shape
    assert m % tm == 0 and ncols % tn == 0 and col0 % SUBLANES == 0
    return pl.pallas_call(
        _mm_wt_kernel,
        grid=(ncols // tn, m // tm),
        in_specs=[pl.BlockSpec((tm, k), lambda j, i: (i, 0)),
                  pl.BlockSpec((pl.Element(1), pl.Element(tn), pl.Element(k)),
                               lambda j, i: (layer, pl.multiple_of(int(col0) + j * tn, SUBLANES), 0))],
        out_specs=pl.BlockSpec((tm, tn), lambda j, i: (i, j)),
        out_shape=jax.ShapeDtypeStruct((m, ncols), out_dtype),
        scratch_shapes=[pltpu.VMEM((tn, k), MXU)],
        compiler_params=_cparams(("arbitrary", "arbitrary"), 56),
        name="proj_matmul_wt",
    )(a, wt_all)


def _small_proj_weights(wt):
    off = np.concatenate([[0], np.cumsum(IN_SPLIT_SIZES)])
    dt = wt[off[5]:off[6]].T
    gn = wt[off[8]:off[9]].T
    d = dt.shape[0]
    dtp = jnp.pad(dt.reshape(d, SSM_GROUPS, SSM_HEADS_PER_GROUP),
                  ((0, 0), (0, 0), (0, LANES - SSM_HEADS_PER_GROUP))).reshape(d, SSM_GROUPS * LANES)
    ng = NSA_HEADS_PER_GROUP * N_BRANCHES
    gnp = jnp.pad(gn.reshape(d, NSA_KV_GROUPS, ng), ((0, 0), (0, 0), (0, LANES - ng))).reshape(d, NSA_KV_GROUPS * LANES)
    return jnp.concatenate([dtp, gnp], axis=1).astype(MXU)


def _flash_init(m_ref, acc_ref):
    m_ref[...] = jnp.full(m_ref.shape, NEG, F32)
    acc_ref[...] = jnp.zeros(acc_ref.shape, F32)


def _flash_step(s, v_aug, m_ref, acc_ref):
    hd = HEAD_DIM
    m_prev = m_ref[...]
    m_new = jnp.maximum(m_prev, jnp.max(s, axis=1, keepdims=True))
    alpha = jnp.exp(m_prev - m_new)
    p = jnp.concatenate([jnp.exp(s[:, c * hd:(c + 1) * hd] - m_new) for c in range(s.shape[1] // hd)], axis=1)
    pv = jnp.dot(p.astype(MXU), v_aug, preferred_element_type=F32)
    acc_ref[:, 0:hd] = alpha * acc_ref[:, 0:hd] + pv[:, 0:hd]
    acc_ref[:, hd:2 * hd] = alpha * acc_ref[:, hd:2 * hd] + pv[:, hd:2 * hd]
    m_ref[...] = m_new


def _flash_out(acc_ref):
    return acc_ref[:, 0:HEAD_DIM] / acc_ref[:, HEAD_DIM:2 * HEAD_DIM]


def _flash_loop(n, scores, values, sa_ref, sb_ref, m_ref, acc_ref):
    @pl.when(n > 0)
    def _first():
        sa_ref[...] = scores(0)

    def pair(u, c):
        t = 2 * u
        sb_ref[...] = scores(t + 1)
        _flash_step(sa_ref[...], values(t), m_ref, acc_ref)
        sa_ref[...] = scores(t + 2)
        _flash_step(sb_ref[...], values(t + 1), m_ref, acc_ref)
        return c

    n_pairs = jnp.maximum(n - 1, 0) // 2
    lax.fori_loop(0, n_pairs, pair, 0)
    t0 = 2 * n_pairs
    left = n - t0

    @pl.when(left == 2)
    def _last_two():
        sb_ref[...] = scores(t0 + 1)
        _flash_step(sa_ref[...], values(t0), m_ref, acc_ref)
        _flash_step(sb_ref[...], values(t0 + 1), m_ref, acc_ref)

    @pl.when(left == 1)
    def _last_one():
        _flash_step(sa_ref[...], values(t0), m_ref, acc_ref)


MOBA_KV_GROUP = 4


def _moba_kernel(q_ref, k_ref, v_ref, o_ref, kaug_ref, vaug_ref, kmean_ref, m_ref, acc_ref, sa_ref, sb_ref, *, nb):
    blk = MOBA_BLOCK
    grp = MOBA_KV_GROUP * blk
    qi = pl.program_id(1)

    @pl.when(qi == 0)
    def _build_keys():
        kmean_ref[...] = jnp.zeros_like(kmean_ref)
        lane = lax.broadcasted_iota(jnp.int32, (blk, LANES), 1)
        ones = jnp.ones((blk, HEAD_DIM), vaug_ref.dtype)

        def body(j, c):
            r0 = pl.multiple_of(j * blk, blk)
            kb = k_ref[pl.ds(r0, blk), :]
            kaug_ref[pl.ds(r0, blk), 0:HEAD_DIM] = kb.astype(kaug_ref.dtype)
            kaug_ref[pl.ds(r0, blk), HEAD_DIM:2 * HEAD_DIM] = (lane == j).astype(kaug_ref.dtype)
            vaug_ref[pl.ds(r0, blk), 0:HEAD_DIM] = v_ref[pl.ds(r0, blk), :].astype(vaug_ref.dtype)
            vaug_ref[pl.ds(r0, blk), HEAD_DIM:2 * HEAD_DIM] = ones
            kmean_ref[pl.ds(j, 1), :] = jnp.sum(kb.astype(F32), axis=0, keepdims=True) * (1.0 / blk)
            return c

        lax.fori_loop(0, nb, body, 0)

    q = q_ref[...].astype(MXU)
    gate = lax.dot_general(q, kmean_ref[...].astype(MXU), NT, preferred_element_type=F32)
    lane = lax.broadcasted_iota(jnp.int32, (blk, LANES), 1).astype(F32)
    past = lane < qi.astype(F32)
    gate = jnp.where(past, gate, -jnp.inf)
    sel = jnp.zeros((blk, LANES), jnp.bool_)
    for _ in range(MOBA_TOPK):
        mx = jnp.max(gate, axis=1, keepdims=True)
        idx = jnp.min(jnp.where((gate == mx) & past, lane, float(LANES)), axis=1, keepdims=True)
        pick = lane == idx
        sel = sel | pick
        gate = jnp.where(pick, -jnp.inf, gate)
    bias = jnp.where(sel, 0.0, MASK_BIAS).astype(MXU)
    qaug = jnp.concatenate([q, bias], axis=1)

    _flash_init(m_ref, acc_ref)

    def scores(g):
        r0 = pl.multiple_of(g * grp, grp)
        return lax.dot_general(qaug, kaug_ref[pl.ds(r0, grp), :], NT, preferred_element_type=F32) * ATT_SCALE

    def values(g):
        return vaug_ref[pl.ds(pl.multiple_of(g * grp, grp), grp), :]

    _flash_loop((qi + MOBA_KV_GROUP - 1) // MOBA_KV_GROUP, scores, values, sa_ref, sb_ref, m_ref, acc_ref)
    r0 = pl.multiple_of(qi * blk, blk)
    s = lax.dot_general(q, k_ref[pl.ds(r0, blk), :].astype(MXU), NT, preferred_element_type=F32) * ATT_SCALE
    row = lax.broadcasted_iota(jnp.int32, (blk, blk), 0)
    col = lax.broadcasted_iota(jnp.int32, (blk, blk), 1)
    _flash_step(jnp.where(col <= row, s, NEG), vaug_ref[pl.ds(r0, blk), :], m_ref, acc_ref)
    o_ref[...] = _flash_out(acc_ref).astype(o_ref.dtype)


def _moba(p1):
    s = p1.shape[0]
    nb = s // MOBA_BLOCK
    assert nb <= LANES and nb % MOBA_KV_GROUP == 0
    kcol, vcol = P1_KA // HEAD_DIM, P1_VA // HEAD_DIM
    return pl.pallas_call(
        functools.partial(_moba_kernel, nb=nb),
        grid=(MOBA_HEADS, nb),
        in_specs=[pl.BlockSpec((MOBA_BLOCK, HEAD_DIM), lambda h, i: (i, h)),
                  pl.BlockSpec((s, HEAD_DIM), lambda h, i: (0, kcol + h)),
                  pl.BlockSpec((s, HEAD_DIM), lambda h, i: (0, vcol + h))],
        out_specs=pl.BlockSpec((MOBA_BLOCK, HEAD_DIM), lambda h, i: (i, h)),
        out_shape=jax.ShapeDtypeStruct((s, MOBA_WIDTH), MXU),
        scratch_shapes=[pltpu.VMEM((s, 2 * HEAD_DIM), MXU), pltpu.VMEM((s, 2 * HEAD_DIM), MXU),
                        pltpu.VMEM((LANES, HEAD_DIM), F32),
                        pltpu.VMEM((MOBA_BLOCK, HEAD_DIM), F32), pltpu.VMEM((MOBA_BLOCK, 2 * HEAD_DIM), F32),
                        pltpu.VMEM((MOBA_BLOCK, MOBA_KV_GROUP * MOBA_BLOCK), F32),
                        pltpu.VMEM((MOBA_BLOCK, MOBA_KV_GROUP * MOBA_BLOCK), F32)],
        compiler_params=_cparams(("arbitrary", "arbitrary")),
        name="moba_attention",
    )(p1, p1, p1)


def _ssd_kernel(dtb_ref, alog_ref, dskip_ref,
                xs_ref, bm_ref, cm_ref, z_ref, dt_ref, cwx_ref, cwb_ref, cwc_ref, cbx_ref, cbb_ref, cbc_ref,
                ng_ref, o_ref, xbuf, bbuf, cbuf, state_ref, ybuf):
    L = SSM_CHUNK
    W = SSM_GROUP_WIDTH
    J = SSM_HEADS_PER_GROUP
    P = SSM_HEAD_DIM
    g = pl.program_id(0)
    c = pl.program_id(1)

    @pl.when(c == 0)
    def _reset():
        xbuf[0:SUBLANES, :] = jnp.zeros((SUBLANES, W), F32)
        bbuf[0:SUBLANES, :] = jnp.zeros((SUBLANES, SSM_STATE), F32)
        cbuf[0:SUBLANES, :] = jnp.zeros((SUBLANES, SSM_STATE), F32)
        state_ref[...] = jnp.zeros_like(state_ref)

    def conv_silu(buf, raw_ref, w_ref, b_ref):
        buf[SUBLANES:SUBLANES + L, :] = raw_ref[...]
        acc = b_ref[...]
        for i in range(SSM_CONV):
            lo = SUBLANES - (SSM_CONV - 1) + i
            acc = acc + w_ref[i:i + 1, :] * buf[lo:lo + L, :]
        buf[0:SUBLANES, :] = buf[L:L + SUBLANES, :]
        return _silu(acc)

    xs = conv_silu(xbuf, xs_ref, cwx_ref, cbx_ref)
    bm = conv_silu(bbuf, bm_ref, cwb_ref, cbb_ref)
    cm = conv_silu(cbuf, cm_ref, cwc_ref, cbc_ref)

    lane = lax.broadcasted_iota(jnp.int32, (1, LANES), 1)
    dtb = jnp.zeros((1, LANES), F32)
    alog = jnp.full((1, LANES), -jnp.inf, F32)
    for j in range(J):
        dtb = jnp.where(lane == j, dtb_ref[g * J + j], dtb)
        alog = jnp.where(lane == j, alog_ref[g * J + j], alog)
    x = dt_ref[...] + dtb
    dt = jnp.maximum(x, 0.0) + jnp.log(1.0 + jnp.exp(-jnp.abs(x)))
    dt = jnp.where(lane < J, dt, 0.0)
    a = dt * (-jnp.exp(alog))

    er = lax.broadcasted_iota(jnp.int32, (LANES, W), 0)
    ec = lax.broadcasted_iota(jnp.int32, (LANES, W), 1)
    expand = ((ec >> 6) == er).astype(jnp.bfloat16)
    tr = lax.broadcasted_iota(jnp.int32, (L, L), 0)
    tc = lax.broadcasted_iota(jnp.int32, (L, L), 1)
    tril = tr >= tc
    tril_b = tril.astype(jnp.bfloat16)
    dt_e = _dot3(expand, dt, left=False)
    a_e = _dot3(expand, a, left=False)
    acs = _dot3(tril_b, a_e, left=True)
    acs_t = acs.T
    a_last = acs[L - 1:L, :]

    xdt = xs * dt_e
    cb = lax.dot_general(cm.astype(MXU), bm.astype(MXU), NT, preferred_element_type=F32)
    for j in range(J):
        colv = acs[:, j * P:j * P + 1]
        rowv = acs_t[j * P:j * P + 1, :]
        dec = jnp.exp(jnp.where(tril, colv - rowv, -jnp.inf))
        ybuf[:, j * P:(j + 1) * P] = jnp.dot((cb * dec).astype(MXU), xdt[:, j * P:(j + 1) * P].astype(MXU),
                                             preferred_element_type=F32)
    st_old = state_ref[...]
    xdte = (xdt * jnp.exp(a_last - acs)).astype(MXU)
    st_new = jnp.dot(bm.T.astype(MXU), xdte, preferred_element_type=F32)
    y_off = jnp.dot(cm.astype(MXU), st_old.astype(MXU), preferred_element_type=F32) * jnp.exp(acs)
    state_ref[...] = st_old * jnp.exp(a_last) + st_new

    lane_w = lax.broadcasted_iota(jnp.int32, (1, W), 1)
    dsk = jnp.zeros((1, W), F32)
    for j in range(J):
        dsk = jnp.where((lane_w >> 6) == j, dskip_ref[g * J + j], dsk)
    y = ybuf[...] + y_off + xs * dsk
    y = y * _silu(z_ref[...])
    y = y * lax.rsqrt(jnp.mean(y * y, axis=1, keepdims=True) + RMS_EPS)
    o_ref[...] = (y * ng_ref[...]).astype(o_ref.dtype)


def _ssd(p2, p3, conv_w, conv_b, dt_bias, a_log, d_skip, norm_g):
    s = p2.shape[0]
    L, W, N = SSM_CHUNK, SSM_GROUP_WIDTH, SSM_STATE
    xs0 = P2_XBC // W
    bm0 = (P2_XBC + SSM_D_INNER) // N
    cm0 = bm0 + SSM_GROUPS
    z0 = P2_Z // W
    cb2 = conv_b.reshape(1, SSM_CONV_CH)
    ng2 = norm_g.reshape(1, SSM_D_INNER)
    grid_spec = pltpu.PrefetchScalarGridSpec(
        num_scalar_prefetch=3,
        grid=(SSM_GROUPS, s // L),
        in_specs=[
            pl.BlockSpec((L, W), lambda g, c, *_: (c, xs0 + g)),
            pl.BlockSpec((L, N), lambda g, c, *_: (c, bm0 + g)),
            pl.BlockSpec((L, N), lambda g, c, *_: (c, cm0 + g)),
            pl.BlockSpec((L, W), lambda g, c, *_: (c, z0 + g)),
            pl.BlockSpec((L, LANES), lambda g, c, *_: (c, g)),
            pl.BlockSpec((SSM_CONV, W), lambda g, c, *_: (0, g)),
            pl.BlockSpec((SSM_CONV, N), lambda g, c, *_: (0, SSM_D_INNER // N + g)),
            pl.BlockSpec((SSM_CONV, N), lambda g, c, *_: (0, SSM_D_INNER // N + SSM_GROUPS + g)),
            pl.BlockSpec((1, W), lambda g, c, *_: (0, g)),
            pl.BlockSpec((1, N), lambda g, c, *_: (0, SSM_D_INNER // N + g)),
            pl.BlockSpec((1, N), lambda g, c, *_: (0, SSM_D_INNER // N + SSM_GROUPS + g)),
            pl.BlockSpec((1, W), lambda g, c, *_: (0, g)),
        ],
        out_specs=pl.BlockSpec((L, W), lambda g, c, *_: (c, g)),
        scratch_shapes=[pltpu.VMEM((SUBLANES + L, W), F32), pltpu.VMEM((SUBLANES + L, N), F32),
                        pltpu.VMEM((SUBLANES + L, N), F32), pltpu.VMEM((N, W), F32), pltpu.VMEM((L, W), F32)],
    )
    return pl.pallas_call(
        _ssd_kernel,
        grid_spec=grid_spec,
        out_shape=jax.ShapeDtypeStruct((s, SSM_D_INNER), MXU),
        compiler_params=_cparams(("arbitrary", "arbitrary")),
        name="ssd_mixer",
    )(dt_bias, a_log, d_skip, p2, p2, p2, p2, p3, conv_w, conv_w, conv_w, cb2, cb2, cb2, ng2)


def _nsa_compress_kernel(t_ref, pos_ref, w1_ref, w2_ref, o_ref):
    half = NSA_CMP_STRIDE * HEAD_DIM
    t = t_ref[0].astype(F32)
    pos = pos_ref[0]
    lo = (t + pos[:, :half]).astype(MXU)
    hi = (t + pos[:, half:]).astype(MXU)
    w1 = w1_ref[0]
    a = jnp.dot(lo, w1[:half].astype(MXU), preferred_element_type=F32)
    b = jnp.dot(hi, w1[half:].astype(MXU), preferred_element_type=F32)
    n = t.shape[0]
    pre = a + pltpu.roll(b, n - 1, 0)
    act = jax.nn.gelu(pre, approximate=True)
    o_ref[0] = jnp.dot(act.astype(MXU), w2_ref[0].astype(MXU), preferred_element_type=F32).astype(o_ref.dtype)


def _nsa_compress(p1, pos_k, w1_k, w2_k, pos_v, w1_v, w2_v):
    s = p1.shape[0]
    n_str = s // NSA_CMP_STRIDE
    kv = p1[:, P1_KVN:P1_KVN + 2 * NSA_KV_WIDTH]
    t4 = kv.reshape(s, 4, HEAD_DIM).transpose(1, 0, 2).reshape(4, n_str, NSA_CMP_STRIDE * HEAD_DIM)
    pos = jnp.stack([pos_k, pos_v]).reshape(2, 1, NSA_CMP_LEN * HEAD_DIM)
    w1 = jnp.stack([w1_k, w1_v])
    w2 = jnp.stack([w2_k, w2_v])
    return pl.pallas_call(
        _nsa_compress_kernel,
        grid=(4,),
        in_specs=[pl.BlockSpec((1, n_str, NSA_CMP_STRIDE * HEAD_DIM), lambda i: (i, 0, 0)),
                  pl.BlockSpec((1, 1, NSA_CMP_LEN * HEAD_DIM), lambda i: (i // 2, 0, 0)),
                  pl.BlockSpec((1, NSA_CMP_LEN * HEAD_DIM, HEAD_DIM), lambda i: (i // 2, 0, 0)),
                  pl.BlockSpec((1, HEAD_DIM, HEAD_DIM), lambda i: (i // 2, 0, 0))],
        out_specs=pl.BlockSpec((1, n_str, HEAD_DIM), lambda i: (i, 0, 0)),
        out_shape=jax.ShapeDtypeStruct((4, n_str, HEAD_DIM), MXU),
        compiler_params=_cparams(("arbitrary",)),
        name="nsa_compress",
    )(t4, pos, w1, w2)


def _nsa_cmp_kernel(q_ref, kc_ref, vc_ref, gn_ref, o_ref, selb_ref, *, n_sel):
    tq = ATT_TILE
    qi = pl.program_id(1)
    kc = kc_ref[0].astype(MXU)
    vc = vc_ref[0].astype(MXU)
    nc = kc.shape[0]
    pos = qi * tq + lax.broadcasted_iota(jnp.int32, (tq, nc), 0)
    cidx = lax.broadcasted_iota(jnp.int32, (tq, nc), 1)
    valid = (cidx * NSA_CMP_STRIDE + NSA_CMP_LEN - 1 <= pos) & (cidx < nc - 1)
    c_start = lax.broadcasted_iota(jnp.int32, (LANES, nc), 1) * NSA_CMP_STRIDE
    s_start = lax.broadcasted_iota(jnp.int32, (LANES, nc), 0) * NSA_SEL_BLOCK
    overlap_t = ((c_start < s_start + NSA_SEL_BLOCK) & (c_start + NSA_CMP_LEN > s_start)).astype(MXU)
    gates = _sigmoid(gn_ref[...])
    imp = jnp.zeros((LANES, tq), F32)
    for j in range(NSA_HEADS_PER_GROUP):
        q = q_ref[:, j * HEAD_DIM:(j + 1) * HEAD_DIM].astype(MXU)
        s = lax.dot_general(q, kc, NT, preferred_element_type=F32) * ATT_SCALE
        s = jnp.where(valid, s, -jnp.inf)
        m = jnp.max(s, axis=1, keepdims=True)
        m = jnp.where(m > -jnp.inf, m, 0.0)
        e = jnp.exp(s - m)
        den = jnp.sum(e, axis=1, keepdims=True)
        p = (e / jnp.where(den > 0, den, 1.0)).astype(MXU)
        o = jnp.dot(p, vc, preferred_element_type=F32)
        imp = imp + lax.dot_general(overlap_t, p, NT, preferred_element_type=F32)
        o_ref[:, j * HEAD_DIM:(j + 1) * HEAD_DIM] = o * gates[:, N_BRANCHES * j:N_BRANCHES * j + 1]

    blk = lax.broadcasted_iota(jnp.int32, (LANES, tq), 0).astype(F32)
    cur = ((qi * tq + lax.broadcasted_iota(jnp.int32, (LANES, tq), 1)) >> 6).astype(F32)
    allowed = blk <= cur
    forced = (blk == 0.0) | (blk == cur) | (blk == cur - 1.0)
    val = jnp.where(forced, jnp.inf, jnp.where(allowed, imp, -jnp.inf))
    val = jnp.where(blk < float(n_sel), val, -jnp.inf)

    def pick_round(_, c):
        val, sel = c
        mx = jnp.max(val, axis=0, keepdims=True)
        idx = jnp.min(jnp.where(val == mx, blk, float(LANES)), axis=0, keepdims=True)
        pick = blk == idx
        sel = jnp.where(pick & allowed, 1.0, sel)
        val = jnp.where(pick, -jnp.inf, val)
        return val, sel

    _, sel = lax.fori_loop(0, min(NSA_TOPN, n_sel), pick_round, (val, jnp.zeros((LANES, tq), F32)))
    selb_ref[0] = jnp.where(sel.T > 0, 0.0, MASK_BIAS).astype(selb_ref.dtype)


def _nsa_cmp(p1, p3, kvc):
    s = p1.shape[0]
    n_str = s // NSA_CMP_STRIDE
    n_sel = s // NSA_SEL_BLOCK
    assert n_sel <= LANES
    gw = NSA_HEADS_PER_GROUP * HEAD_DIM
    q0 = P1_QN // gw
    return pl.pallas_call(
        functools.partial(_nsa_cmp_kernel, n_sel=n_sel),
        grid=(NSA_KV_GROUPS, s // ATT_TILE),
        in_specs=[pl.BlockSpec((ATT_TILE, gw), lambda g, i: (i, q0 + g)),
                  pl.BlockSpec((1, n_str, HEAD_DIM), lambda g, i: (g, 0, 0)),
                  pl.BlockSpec((1, n_str, HEAD_DIM), lambda g, i: (NSA_KV_GROUPS + g, 0, 0)),
                  pl.BlockSpec((ATT_TILE, LANES), lambda g, i: (i, SSM_GROUPS + g))],
        out_specs=[pl.BlockSpec((ATT_TILE, gw), lambda g, i: (i, g)),
                   pl.BlockSpec((1, ATT_TILE, LANES), lambda g, i: (g, i, 0))],
        out_shape=[jax.ShapeDtypeStruct((s, NSA_WIDTH), F32),
                   jax.ShapeDtypeStruct((NSA_KV_GROUPS, s, LANES), MXU)],
        compiler_params=_cparams(("arbitrary", "arbitrary")),
        name="nsa_compressed_attention",
    )(p1, kvc, kvc, p3)


def _nsa_sel_kernel(q_ref, selb_ref, k_ref, v_ref, gn_ref, prev_ref, o_ref, kaug_ref, vaug_ref, qaug_ref,
                    m_ref, acc_ref, sa_ref, sb_ref, *, n_tiles):
    tq = ATT_TILE
    J = NSA_HEADS_PER_GROUP
    qi = pl.program_id(1)

    @pl.when(qi == 0)
    def _build_keys():
        lane = lax.broadcasted_iota(jnp.int32, (tq, LANES), 1)
        rowi = lax.broadcasted_iota(jnp.int32, (tq, LANES), 0)
        ones = jnp.ones((tq, HEAD_DIM), vaug_ref.dtype)

        def body(t, c):
            r0 = pl.multiple_of(t * tq, tq)
            kaug_ref[pl.ds(r0, tq), 0:HEAD_DIM] = k_ref[pl.ds(r0, tq), :].astype(kaug_ref.dtype)
            kaug_ref[pl.ds(r0, tq), HEAD_DIM:2 * HEAD_DIM] = (
                lane == ((t * tq + rowi) >> 6)).astype(kaug_ref.dtype)
            vaug_ref[pl.ds(r0, tq), 0:HEAD_DIM] = v_ref[pl.ds(r0, tq), :].astype(vaug_ref.dtype)
            vaug_ref[pl.ds(r0, tq), HEAD_DIM:2 * HEAD_DIM] = ones
            return c

        lax.fori_loop(0, n_tiles, body, 0)

    selb = selb_ref[0]
    for j in range(J):
        qaug_ref[j * tq:(j + 1) * tq, 0:HEAD_DIM] = q_ref[:, j * HEAD_DIM:(j + 1) * HEAD_DIM].astype(qaug_ref.dtype)
        qaug_ref[j * tq:(j + 1) * tq, HEAD_DIM:2 * HEAD_DIM] = selb
    _flash_init(m_ref, acc_ref)

    def scores(t):
        r0 = pl.multiple_of(t * tq, tq)
        return lax.dot_general(qaug_ref[...], kaug_ref[pl.ds(r0, tq), :], NT,
                               preferred_element_type=F32) * ATT_SCALE

    def values(t):
        return vaug_ref[pl.ds(pl.multiple_of(t * tq, tq), tq), :]

    _flash_loop(qi, scores, values, sa_ref, sb_ref, m_ref, acc_ref)
    row = lax.broadcasted_iota(jnp.int32, (J * tq, tq), 0) & (tq - 1)
    col = lax.broadcasted_iota(jnp.int32, (J * tq, tq), 1)
    _flash_step(jnp.where(col <= row, scores(qi), NEG), values(qi), m_ref, acc_ref)
    o = _flash_out(acc_ref)
    gates = _sigmoid(gn_ref[...])
    for j in range(J):
        o_ref[:, j * HEAD_DIM:(j + 1) * HEAD_DIM] = (
            prev_ref[:, j * HEAD_DIM:(j + 1) * HEAD_DIM]
            + o[j * tq:(j + 1) * tq] * gates[:, N_BRANCHES * j + 1:N_BRANCHES * j + 2])


def _nsa_sel(p1, p3, selb, prev):
    s = p1.shape[0]
    gw = NSA_HEADS_PER_GROUP * HEAD_DIM
    q0 = P1_QN // gw
    k0 = (P1_KVN + 2 * NSA_KV_WIDTH) // HEAD_DIM
    v0 = k0 + NSA_KV_GROUPS
    return pl.pallas_call(
        functools.partial(_nsa_sel_kernel, n_tiles=s // ATT_TILE),
        grid=(NSA_KV_GROUPS, s // ATT_TILE),
        in_specs=[pl.BlockSpec((ATT_TILE, gw), lambda g, i: (i, q0 + g)),
                  pl.BlockSpec((1, ATT_TILE, LANES), lambda g, i: (g, i, 0)),
                  pl.BlockSpec((s, HEAD_DIM), lambda g, i: (0, k0 + g)),
                  pl.BlockSpec((s, HEAD_DIM), lambda g, i: (0, v0 + g)),
                  pl.BlockSpec((ATT_TILE, LANES), lambda g, i: (i, SSM_GROUPS + g)),
                  pl.BlockSpec((ATT_TILE, gw), lambda g, i: (i, g))],
        out_specs=pl.BlockSpec((ATT_TILE, gw), lambda g, i: (i, g)),
        out_shape=jax.ShapeDtypeStruct((s, NSA_WIDTH), F32),
        scratch_shapes=[pltpu.VMEM((s, 2 * HEAD_DIM), MXU), pltpu.VMEM((s, 2 * HEAD_DIM), MXU),
                        pltpu.VMEM((NSA_HEADS_PER_GROUP * ATT_TILE, 2 * HEAD_DIM), MXU),
                        pltpu.VMEM((NSA_HEADS_PER_GROUP * ATT_TILE, HEAD_DIM), F32),
                        pltpu.VMEM((NSA_HEADS_PER_GROUP * ATT_TILE, 2 * HEAD_DIM), F32),
                        pltpu.VMEM((NSA_HEADS_PER_GROUP * ATT_TILE, ATT_TILE), F32),
                        pltpu.VMEM((NSA_HEADS_PER_GROUP * ATT_TILE, ATT_TILE), F32)],
        compiler_params=_cparams(("arbitrary", "arbitrary")),
        name="nsa_selected_attention",
    )(p1, selb, p1, p1, p3, prev)


def _nsa_win_kernel(q_ref, k_ref, v_ref, gn_ref, prev_ref, o_ref, qst_ref, m_ref, acc_ref):
    tq = ATT_TILE
    J = NSA_HEADS_PER_GROUP
    halo = NSA_WINDOW // tq
    qi = pl.program_id(1)
    for j in range(J):
        qst_ref[j * tq:(j + 1) * tq, :] = q_ref[:, j * HEAD_DIM:(j + 1) * HEAD_DIM].astype(qst_ref.dtype)
    row = lax.broadcasted_iota(jnp.int32, (J * tq, tq), 0) & (tq - 1)
    col = lax.broadcasted_iota(jnp.int32, (J * tq, tq), 1)
    ones = jnp.ones((tq, HEAD_DIM), MXU)
    _flash_init(m_ref, acc_ref)

    def step(t, c):
        r0 = pl.multiple_of(t * tq, tq)
        s = lax.dot_general(qst_ref[...], k_ref[pl.ds(r0, tq), :].astype(MXU), NT,
                            preferred_element_type=F32) * ATT_SCALE
        diff = (qi - t) * tq + row - col
        s = jnp.where((diff >= 0) & (diff < NSA_WINDOW), s, NEG)
        v_aug = jnp.concatenate([v_ref[pl.ds(r0, tq), :].astype(MXU), ones], axis=1)
        _flash_step(s, v_aug, m_ref, acc_ref)
        return c

    lax.fori_loop(jnp.maximum(qi - halo, 0), qi + 1, step, 0)
    o = _flash_out(acc_ref)
    gates = _sigmoid(gn_ref[...])
    for j in range(J):
        o_ref[:, j * HEAD_DIM:(j + 1) * HEAD_DIM] = (
            prev_ref[:, j * HEAD_DIM:(j + 1) * HEAD_DIM]
            + o[j * tq:(j + 1) * tq] * gates[:, N_BRANCHES * j + 2:N_BRANCHES * j + 3])


def _nsa_win(p1, p3, prev):
    s = p1.shape[0]
    gw = NSA_HEADS_PER_GROUP * HEAD_DIM
    q0 = P1_QN // gw
    k0 = (P1_KVN + 4 * NSA_KV_WIDTH) // HEAD_DIM
    v0 = k0 + NSA_KV_GROUPS
    return pl.pallas_call(
        _nsa_win_kernel,
        grid=(NSA_KV_GROUPS, s // ATT_TILE),
        in_specs=[pl.BlockSpec((ATT_TILE, gw), lambda g, i: (i, q0 + g)),
                  pl.BlockSpec((s, HEAD_DIM), lambda g, i: (0, k0 + g)),
                  pl.BlockSpec((s, HEAD_DIM), lambda g, i: (0, v0 + g)),
                  pl.BlockSpec((ATT_TILE, LANES), lambda g, i: (i, SSM_GROUPS + g)),
                  pl.BlockSpec((ATT_TILE, gw), lambda g, i: (i, g))],
        out_specs=pl.BlockSpec((ATT_TILE, gw), lambda g, i: (i, g)),
        out_shape=jax.ShapeDtypeStruct((s, NSA_WIDTH), F32),
        scratch_shapes=[pltpu.VMEM((NSA_HEADS_PER_GROUP * ATT_TILE, HEAD_DIM), MXU),
                        pltpu.VMEM((NSA_HEADS_PER_GROUP * ATT_TILE, HEAD_DIM), F32),
                        pltpu.VMEM((NSA_HEADS_PER_GROUP * ATT_TILE, 2 * HEAD_DIM), F32)],
        compiler_params=_cparams(("arbitrary", "arbitrary")),
        name="nsa_window_attention",
    )(p1, p1, p1, p3, prev)


def _merge_kernel(ya_ref, yb_ref, yc_ref, g0_ref, g1_ref, g2_ref, wa_ref, wb_ref, wc_ref, o_ref):
    def branch(y_ref, w_ref, g_ref):
        prod = jnp.dot(y_ref[...].astype(MXU), w_ref[...].astype(MXU), preferred_element_type=F32)
        return _sigmoid(g_ref[...]) * prod

    o_ref[...] = (branch(ya_ref, wa_ref, g0_ref) + branch(yb_ref, wb_ref, g1_ref)
                  + branch(yc_ref, wc_ref, g2_ref)).astype(o_ref.dtype)


def _merge(ya, yb, yc, p2, wa, wb, wc, *, tm=512, tn=512):
    s = ya.shape[0]
    d = wa.shape[1]
    g0 = P2_GM // tn
    gstep = d // tn
    return pl.pallas_call(
        _merge_kernel,
        grid=(d // tn, s // tm),
        in_specs=[pl.BlockSpec((tm, ya.shape[1]), lambda j, i: (i, 0)),
                  pl.BlockSpec((tm, yb.shape[1]), lambda j, i: (i, 0)),
                  pl.BlockSpec((tm, yc.shape[1]), lambda j, i: (i, 0)),
                  pl.BlockSpec((tm, tn), lambda j, i: (i, g0 + j)),
                  pl.BlockSpec((tm, tn), lambda j, i: (i, g0 + gstep + j)),
                  pl.BlockSpec((tm, tn), lambda j, i: (i, g0 + 2 * gstep + j)),
                  pl.BlockSpec((wa.shape[0], tn), lambda j, i: (0, j)),
                  pl.BlockSpec((wb.shape[0], tn), lambda j, i: (0, j)),
                  pl.BlockSpec((wc.shape[0], tn), lambda j, i: (0, j))],
        out_specs=pl.BlockSpec((tm, tn), lambda j, i: (i, j)),
        out_shape=jax.ShapeDtypeStruct((s, d), MXU),
        compiler_params=_cparams(("arbitrary", "arbitrary")),
        name="branch_merge",
    )(ya, yb, yc, p2, p2, p2, wa, wb, wc)


def _layer_norm_rows(x, g, b):
    xc = x - jnp.mean(x, axis=1, keepdims=True)
    var = jnp.mean(xc * xc, axis=1, keepdims=True)
    return xc * lax.rsqrt(var + LN_EPS) * g + b


def _pack_halves(y):
    half = y.shape[1] // 2
    bits = lax.bitcast_convert_type(y.astype(jnp.bfloat16).astype(F32), jnp.int32)
    return ((bits[:, :half] >> 16) & jnp.int32(0xFFFF)) | (bits[:, half:] & jnp.int32(-65536))


def _unpack_halves(w):
    lo = lax.bitcast_convert_type(w << 16, F32)
    hi = lax.bitcast_convert_type(w & jnp.int32(-65536), F32)
    return lo.astype(MXU), hi.astype(MXU)


def _wout_ln_kernel(m_ref, w_ref, h_ref, g_ref, b_ref, o_ref, ob_ref, op_ref):
    acc = jnp.dot(m_ref[...].astype(MXU), w_ref[...].astype(MXU), preferred_element_type=F32)
    y = _layer_norm_rows(DEEPNORM_ALPHA * h_ref[...] + acc, g_ref[...], b_ref[...])
    o_ref[...] = y
    ob_ref[...] = y.astype(ob_ref.dtype)
    op_ref[...] = _pack_halves(y)


def _wout_ln(merged, w_out, h, ln_g, ln_b, *, tm=256):
    s, d = h.shape
    row = pl.BlockSpec((tm, d), lambda i: (i, 0))
    vec = pl.BlockSpec((1, d), lambda i: (0, 0))
    return pl.pallas_call(
        _wout_ln_kernel,
        grid=(s // tm,),
        in_specs=[row, pl.BlockSpec((d, d), lambda i: (0, 0)), row, vec, vec],
        out_specs=[row, row, pl.BlockSpec((tm, d // 2), lambda i: (i, 0))],
        out_shape=[jax.ShapeDtypeStruct((s, d), F32), jax.ShapeDtypeStruct((s, d), MXU),
                   jax.ShapeDtypeStruct((s, d // 2), jnp.int32)],
        compiler_params=_cparams(("arbitrary",)),
        name="out_proj_layernorm",
    )(merged, w_out.astype(MXU), h, ln_g.reshape(1, d), ln_b.reshape(1, d))


def _dispatch_kernel(ztile_ref, dest_ref, hp_ref, o_ref, zbuf, sem, zsem):
    tt = dest_ref.shape[1]
    tm = zbuf.shape[0]

    @pl.when(pl.program_id(0) == 0)
    def _zero_partial_tiles():
        zbuf[...] = jnp.zeros_like(zbuf)

        def tile_copy(t):
            return pltpu.make_async_copy(zbuf, o_ref.at[pl.ds(t * tm, tm)], zsem)

        def start(j, c):
            @pl.when(ztile_ref[j] >= 0)
            def _():
                tile_copy(ztile_ref[j]).start()
            return c

        def wait(j, c):
            @pl.when(ztile_ref[j] >= 0)
            def _():
                tile_copy(ztile_ref[j]).wait()
            return c

        lax.fori_loop(0, ztile_ref.shape[0], start, 0)
        lax.fori_loop(0, ztile_ref.shape[0], wait, 0)

    def issue(r, c):
        for k in range(TOP_K):
            pltpu.make_async_copy(hp_ref.at[pl.ds(r, 1)], o_ref.at[pl.ds(dest_ref[k, r], 1)], sem).start()
        return c

    lax.fori_loop(0, tt, issue, 0)
    rows = o_ref.at[pl.ds(0, TOP_K * tt)]
    pltpu.make_async_copy(rows, rows, sem).wait()


def _dispatch(hp, dest, zero_tiles, n_rows, *, tm, tt=256):
    s, half = hp.shape
    grid_spec = pltpu.PrefetchScalarGridSpec(
        num_scalar_prefetch=1,
        grid=(s // tt,),
        in_specs=[pl.BlockSpec((TOP_K, tt), lambda i, zt: (0, i), memory_space=pltpu.SMEM),
                  pl.BlockSpec((tt, half), lambda i, zt: (i, 0))],
        out_specs=pl.BlockSpec(memory_space=pl.ANY),
        scratch_shapes=[pltpu.VMEM((tm, half), jnp.int32), pltpu.SemaphoreType.DMA(()),
                        pltpu.SemaphoreType.DMA(())],
    )
    return pl.pallas_call(
        _dispatch_kernel,
        grid_spec=grid_spec,
        out_shape=jax.ShapeDtypeStruct((n_rows, half), jnp.int32),
        compiler_params=_cparams(("arbitrary",)),
        name="moe_dispatch",
    )(zero_tiles, dest, hp)


def _combine_kernel(dcur_ref, dnxt_ref, w_ref, h_ref, sh_ref, g_ref, b_ref, y_ref, o_ref, ob_ref, buf, sem):
    tt = dcur_ref.shape[1]
    i = pl.program_id(0)
    n = pl.num_programs(0)
    slot = i % 2

    def issue(d_ref, sl):
        def body(r, c):
            for k in range(TOP_K):
                pltpu.make_async_copy(y_ref.at[pl.ds(d_ref[k, r], 1)], buf.at[sl, pl.ds(k * tt + r, 1)],
                                      sem.at[sl]).start()
            return c

        lax.fori_loop(0, tt, body, 0)

    @pl.when(i == 0)
    def _():
        issue(dcur_ref, slot)

    @pl.when(i + 1 < n)
    def _():
        issue(dnxt_ref, 1 - slot)

    pltpu.make_async_copy(y_ref.at[pl.ds(0, TOP_K * tt)], buf.at[slot], sem.at[slot]).wait()
    w = w_ref[...]
    routed = w[:, 0:1] * buf[slot, 0:tt]
    for k in range(1, TOP_K):
        routed = routed + w[:, k:k + 1] * buf[slot, k * tt:(k + 1) * tt]
    y = _layer_norm_rows(DEEPNORM_ALPHA * h_ref[...] + (routed + sh_ref[...]), g_ref[...], b_ref[...])
    o_ref[...] = y
    ob_ref[...] = y.astype(ob_ref.dtype)


def _combine_ln(h, y_rows, dest, top_w, shared, ln_g, ln_b, *, tt=128):
    s, d = h.shape
    n = s // tt
    row = pl.BlockSpec((tt, d), lambda i: (i, 0))
    vec = pl.BlockSpec((1, d), lambda i: (0, 0))
    return pl.pallas_call(
        _combine_kernel,
        grid=(n,),
        in_specs=[pl.BlockSpec((TOP_K, tt), lambda i: (0, i), memory_space=pltpu.SMEM),
                  pl.BlockSpec((TOP_K, tt), lambda i: (0, jnp.minimum(i + 1, n - 1)), memory_space=pltpu.SMEM),
                  pl.BlockSpec((tt, TOP_K), lambda i: (i, 0)),
                  row, row, vec, vec,
                  pl.BlockSpec(memory_space=pl.ANY)],
        out_specs=[row, row],
        out_shape=[jax.ShapeDtypeStruct((s, d), F32), jax.ShapeDtypeStruct((s, d), MXU)],
        scratch_shapes=[pltpu.VMEM((2, TOP_K * tt, d), F32), pltpu.SemaphoreType.DMA((2,))],
        compiler_params=_cparams(("arbitrary",)),
        name="moe_combine_layernorm",
    )(dest, dest, top_w.T, h, shared, ln_g.reshape(1, d), ln_b.reshape(1, d), y_rows)


def _router_kernel(h_ref, wr_ref, rb_ref, e_ref, w_ref, pos_ref, cnt_ref, carry_ref):
    tq = h_ref.shape[0]
    i = pl.program_id(0)

    @pl.when(i == 0)
    def _reset():
        carry_ref[...] = jnp.zeros_like(carry_ref)

    logits = lax.dot_general(wr_ref[...].astype(MXU), h_ref[...].astype(MXU), NT, preferred_element_type=F32)
    scores = _sigmoid(logits)
    biased = scores + rb_ref[...]
    G, PG = N_EXPERT_GROUPS, EXPERTS_PER_GROUP
    sub = lax.broadcasted_iota(jnp.int32, (PG, tq), 0).astype(F32)
    gi = lax.broadcasted_iota(jnp.int32, (G, tq), 0).astype(F32)
    gs = jnp.zeros((G, tq), F32)
    for g in range(G):
        blk = biased[g * PG:(g + 1) * PG, :]
        m1 = jnp.max(blk, axis=0, keepdims=True)
        i1 = jnp.min(jnp.where(blk == m1, sub, float(PG)), axis=0, keepdims=True)
        m2 = jnp.max(jnp.where(sub == i1, -jnp.inf, blk), axis=0, keepdims=True)
        gs = jnp.where(gi == float(g), m1 + m2, gs)
    keep = jnp.zeros((G, tq), F32)
    for _ in range(TOPK_GROUPS):
        mx = jnp.max(gs, axis=0, keepdims=True)
        idx = jnp.min(jnp.where(gs == mx, gi, float(G)), axis=0, keepdims=True)
        pick = gi == idx
        keep = jnp.where(pick, 1.0, keep)
        gs = jnp.where(pick, -jnp.inf, gs)
    val = jnp.concatenate(
        [jnp.where(keep[g:g + 1, :] > 0, biased[g * PG:(g + 1) * PG, :], -jnp.inf) for g in range(G)], axis=0)
    ei = lax.broadcasted_iota(jnp.int32, (N_EXPERTS, tq), 0).astype(F32)
    picks, svals = [], []
    sel = jnp.zeros((N_EXPERTS, tq), F32)
    for r in range(TOP_K):
        mx = jnp.max(val, axis=0, keepdims=True)
        idx = jnp.min(jnp.where(val == mx, ei, float(N_EXPERTS)), axis=0, keepdims=True)
        pick = ei == idx
        picks.append(pick)
        svals.append(jnp.sum(jnp.where(pick, scores, 0.0), axis=0, keepdims=True))
        e_ref[r:r + 1, :] = idx.astype(jnp.int32)
        sel = jnp.where(pick, 1.0, sel)
        val = jnp.where(pick, -jnp.inf, val)
    wsum = svals[0]
    for r in range(1, TOP_K):
        wsum = wsum + svals[r]
    tr = lax.broadcasted_iota(jnp.int32, (tq, tq), 0)
    tc = lax.broadcasted_iota(jnp.int32, (tq, tq), 1)
    before = (tr < tc).astype(jnp.bfloat16)
    prefix = jnp.dot(sel.astype(jnp.bfloat16), before, preferred_element_type=F32)
    pos = carry_ref[:, 0:1] + prefix
    for r in range(TOP_K):
        w_ref[r:r + 1, :] = svals[r] / wsum * ROUTED_SCALE
        pos_ref[r:r + 1, :] = jnp.sum(jnp.where(picks[r], pos, 0.0), axis=0, keepdims=True).astype(jnp.int32)
    total = carry_ref[...] + jnp.sum(sel, axis=1, keepdims=True)
    carry_ref[...] = total
    cnt_ref[...] = total


def _router(hb, w_router, router_bias, *, tq=256):
    s, d = hb.shape
    row = pl.BlockSpec((TOP_K, tq), lambda i: (0, i))
    return pl.pallas_call(
        _router_kernel,
        grid=(s // tq,),
        in_specs=[pl.BlockSpec((tq, d), lambda i: (i, 0)),
                  pl.BlockSpec((N_EXPERTS, d), lambda i: (0, 0)),
                  pl.BlockSpec((N_EXPERTS, 1), lambda i: (0, 0))],
        out_specs=[row, row, row, pl.BlockSpec((N_EXPERTS, LANES), lambda i: (0, 0))],
        out_shape=[jax.ShapeDtypeStruct((TOP_K, s), jnp.int32), jax.ShapeDtypeStruct((TOP_K, s), F32),
                   jax.ShapeDtypeStruct((TOP_K, s), jnp.int32), jax.ShapeDtypeStruct((N_EXPERTS, LANES), F32)],
        scratch_shapes=[pltpu.VMEM((N_EXPERTS, LANES), F32)],
        compiler_params=_cparams(("arbitrary",)),
        name="moe_router",
    )(hb, w_router.T, router_bias.reshape(N_EXPERTS, 1))


def _expert_kernel(te_ref, nu_ref, x_ref, wg_ref, wu_ref, wd_ref, o_ref):
    i = pl.program_id(0)

    @pl.when(i < nu_ref[0])
    def _compute():
        xa, xb = _unpack_halves(x_ref[...])
        half = xa.shape[1]

        def up(w_ref):
            return (jnp.dot(xa, w_ref[0:half, :].astype(MXU), preferred_element_type=F32)
                    + jnp.dot(xb, w_ref[half:2 * half, :].astype(MXU), preferred_element_type=F32))

        hg = up(wg_ref)
        hu = up(wu_ref)
        act = (_silu(hg) * hu).astype(MXU)
        o_ref[...] = jnp.dot(act, wd_ref[...].astype(MXU), preferred_element_type=F32).astype(o_ref.dtype)

    @pl.when(i >= nu_ref[0])
    def _unused():
        o_ref[...] = jnp.zeros_like(o_ref)


def _experts(x_rows, tile_e, n_used, w_gate, w_up, w_down, layer, *, tm):
    n_rows = x_rows.shape[0]
    d, f = w_gate.shape[2:]
    grid_spec = pltpu.PrefetchScalarGridSpec(
        num_scalar_prefetch=2,
        grid=(n_rows // tm,),
        in_specs=[pl.BlockSpec((tm, d // 2), lambda i, te, nu: (jnp.minimum(i, nu[0] - 1), 0)),
                  pl.BlockSpec((None, None, d, f), lambda i, te, nu: (layer, te[i], 0, 0)),
                  pl.BlockSpec((None, None, d, f), lambda i, te, nu: (layer, te[i], 0, 0)),
                  pl.BlockSpec((None, None, f, d), lambda i, te, nu: (layer, te[i], 0, 0))],
        out_specs=pl.BlockSpec((tm, d), lambda i, te, nu: (i, 0)),
    )
    return pl.pallas_call(
        _expert_kernel,
        grid_spec=grid_spec,
        out_shape=jax.ShapeDtypeStruct((n_rows, d), F32),
        compiler_params=_cparams(("arbitrary",), 56),
        name="moe_experts",
    )(tile_e, n_used, x_rows, w_gate, w_up, w_down)


def _moe_ln(h, hb, hp, layer, w_router, router_bias, w_exp_gate, w_exp_up, w_exp_down, w_sh_gate, w_sh_up, w_sh_down,
            ln_g, ln_b):
    s, d = h.shape
    tm = MOE_TILE
    top_e, top_w, top_pos, counts = _router(hb, w_router, router_bias)
    cnt = counts[:, 0].astype(jnp.int32)
    tiles_e = (cnt + tm - 1) // tm
    tile_end = jnp.cumsum(tiles_e)
    row_start = (tile_end - tiles_e) * tm
    experts = jnp.arange(N_EXPERTS, dtype=jnp.int32)
    dest = top_pos + jnp.sum(jnp.where(top_e[..., None] == experts, row_start, 0), axis=-1)
    n_tiles = s * TOP_K // tm + N_EXPERTS
    tile_e = jnp.minimum(jnp.sum(tile_end[None, :] <= jnp.arange(n_tiles, dtype=jnp.int32)[:, None], axis=1),
                         N_EXPERTS - 1).astype(jnp.int32)
    n_used = tile_end[-1:].astype(jnp.int32)
    zero_tiles = jnp.concatenate([jnp.where(tiles_e > 0, tile_end - 1, -1),
                                  jnp.where(n_used + experts < n_tiles, n_used + experts, -1)]).astype(jnp.int32)
    x_rows = _dispatch(hp, dest, zero_tiles, n_tiles * tm, tm=tm)
    y_rows = _experts(x_rows, tile_e, n_used, w_exp_gate, w_exp_up, w_exp_down, layer, tm=tm)
    shared = _experts(hp, jnp.zeros((s // tm,), jnp.int32), jnp.full((1,), s // tm, jnp.int32),
                      w_sh_gate[:, None], w_sh_up[:, None], w_sh_down[:, None], layer, tm=tm)
    return _combine_ln(h, y_rows, dest, top_w, shared, ln_g, ln_b)


def _project(hb, wt_all, layer):
    off = np.concatenate([[0], np.cumsum(IN_SPLIT_SIZES)])
    pa = _proj_wt(hb, wt_all, layer, col0=0, ncols=off[3], tm=1024, tn=1024, out_dtype=MXU)
    ps = _proj_wt(hb, wt_all, layer, col0=off[3], ncols=off[5] - off[3], tm=1024, tn=1024, out_dtype=F32)
    pn = _proj_wt(hb, wt_all, layer, col0=off[6], ncols=off[8] - off[6], tm=1024, tn=1280, out_dtype=MXU)
    pg = _proj_wt(hb, wt_all, layer, col0=off[9], ncols=off[10] - off[9], tm=1024, tn=1024, out_dtype=F32)
    p3 = _matmul(hb, _small_proj_weights(wt_all[layer]), tm=1024, tn=P3_COLS, out_dtype=F32)
    return pa, pn, ps, pg, p3


def _mixer(hb, wt_all, layer, conv_w, conv_b, dt_bias, a_log, d_skip, ssm_norm_g,
           cmp_pos_k, cmp_w1_k, cmp_w2_k, cmp_pos_v, cmp_w1_v, cmp_w2_v, w_br_a, w_br_b, w_br_c):
    pa, pn, ps, pg, p3 = _project(hb, wt_all, layer)
    y_a = _moba(pa)
    y_b = _ssd(ps, p3, conv_w, conv_b, dt_bias, a_log, d_skip, ssm_norm_g)
    kvc = _nsa_compress(pn, cmp_pos_k, cmp_w1_k, cmp_w2_k, cmp_pos_v, cmp_w1_v, cmp_w2_v)
    y_c, selb = _nsa_cmp(pn, p3, kvc)
    y_c = _nsa_sel(pn, p3, selb, y_c)
    y_c = _nsa_win(pn, p3, y_c)
    return _merge(y_a, y_b, y_c, pg, w_br_a, w_br_b, w_br_c)


def kernel(x, w_in, conv_w, conv_b, dt_bias, a_log, d_skip, ssm_norm_g, cmp_pos_k, cmp_w1_k, cmp_w2_k, cmp_pos_v, cmp_w1_v, cmp_w2_v, w_br_a, w_br_b, w_br_c, w_out, ln1_g, ln1_b, w_router, router_bias, w_exp_gate, w_exp_up, w_exp_down, w_sh_gate, w_sh_up, w_sh_down, ln2_g, ln2_b):
    bsz, s, d = x.shape
    assert bsz == 1
    h = x.reshape(s, d)
    hb = h.astype(MXU)
    wt_all = jnp.swapaxes(w_in, 1, 2)
    for l in range(w_in.shape[0]):
        merged = _mixer(hb, wt_all, l, conv_w[l], conv_b[l], dt_bias[l], a_log[l], d_skip[l], ssm_norm_g[l],
                        cmp_pos_k[l], cmp_w1_k[l], cmp_w2_k[l], cmp_pos_v[l], cmp_w1_v[l], cmp_w2_v[l],
                        w_br_a[l], w_br_b[l], w_br_c[l])
        h, hb, hp = _wout_ln(merged, w_out[l], h, ln1_g[l], ln1_b[l])
        h, hb = _moe_ln(h, hb, hp, l, w_router[l], router_bias[l], w_exp_gate, w_exp_up, w_exp_down,
                        w_sh_gate, w_sh_up, w_sh_down, ln2_g[l], ln2_b[l])
    return h.reshape(bsz, s, d)
```

```python
import functools

import numpy as np
import jax
import jax.numpy as jnp
from jax import lax
from jax.experimental import pallas as pl
from jax.experimental.pallas import tpu as pltpu

F32 = jnp.float32
MXU = jnp.bfloat16

D_MODEL = 2048
DEPTH = 2
HEAD_DIM = 128
MOBA_HEADS = 8
MOBA_WIDTH = MOBA_HEADS * HEAD_DIM
MOBA_BLOCK = 256
MOBA_TOPK = 3
SSM_D_INNER = D_MODEL
SSM_HEAD_DIM = 64
SSM_HEADS = SSM_D_INNER // SSM_HEAD_DIM
SSM_STATE = 128
SSM_GROUPS = 8
SSM_HEADS_PER_GROUP = SSM_HEADS // SSM_GROUPS
SSM_GROUP_WIDTH = SSM_D_INNER // SSM_GROUPS
SSM_CONV = 4
SSM_CHUNK = 256
SSM_CONV_CH = SSM_D_INNER + 2 * SSM_GROUPS * SSM_STATE
NSA_HEADS = 8
NSA_KV_GROUPS = 2
NSA_HEADS_PER_GROUP = NSA_HEADS // NSA_KV_GROUPS
NSA_WIDTH = NSA_HEADS * HEAD_DIM
NSA_KV_WIDTH = NSA_KV_GROUPS * HEAD_DIM
NSA_CMP_LEN = 32
NSA_CMP_STRIDE = 16
NSA_SEL_BLOCK = 64
NSA_TOPN = 16
NSA_WINDOW = 512
N_BRANCHES = 3
N_EXPERTS = 64
N_EXPERT_GROUPS = 8
EXPERTS_PER_GROUP = N_EXPERTS // N_EXPERT_GROUPS
TOPK_GROUPS = 4
TOP_K = 8
D_EXPERT = 512
ROUTED_SCALE = 2.5
DEEPNORM_ALPHA = (2 * DEPTH) ** 0.25
LN_EPS = 1e-5
RMS_EPS = 1e-5
IN_SPLIT_SIZES = (MOBA_WIDTH, MOBA_WIDTH, MOBA_WIDTH,
                  SSM_D_INNER, SSM_CONV_CH, SSM_HEADS,
                  NSA_WIDTH, 6 * NSA_KV_WIDTH, N_BRANCHES * NSA_HEADS,
                  N_BRANCHES * D_MODEL)

LANES = 128
SUBLANES = 8
ATT_TILE = 256
MOE_TILE = 256
ATT_SCALE = HEAD_DIM ** -0.5
MASK_BIAS = -2.0 ** 30
NEG = -1e30

NT = (((1,), (1,)), ((), ()))

P1_QA, P1_KA, P1_VA = 0, 1024, 2048
P1_QN, P1_KVN = 0, 1024
P2_Z, P2_XBC = 0, 2048
P2_GM = 0
P3_COLS = (SSM_GROUPS + NSA_KV_GROUPS) * LANES


def _cparams(semantics, vmem_mb=48):
    return pltpu.CompilerParams(dimension_semantics=semantics, vmem_limit_bytes=vmem_mb * 1024 * 1024)


def _sigmoid(x):
    return 1.0 / (1.0 + jnp.exp(-x))


def _silu(x):
    return x * _sigmoid(x)


def _split3(x):
    hi = x.astype(jnp.bfloat16)
    r1 = x - hi.astype(F32)
    mid = r1.astype(jnp.bfloat16)
    lo = (r1 - mid.astype(F32)).astype(jnp.bfloat16)
    return hi, mid, lo


def _dot3(a_exact, x, left=True):
    acc = None
    for part in _split3(x):
        t = (jnp.dot(a_exact, part, preferred_element_type=F32) if left
             else jnp.dot(part, a_exact, preferred_element_type=F32))
        acc = t if acc is None else acc + t
    return acc


def _mm_kernel(a_ref, b_ref, o_ref):
    o_ref[...] = jnp.dot(a_ref[...].astype(MXU), b_ref[...].astype(MXU),
                         preferred_element_type=F32).astype(o_ref.dtype)


def _matmul(a, b, *, tm, tn, out_dtype):
    m, k = a.shape
    n = b.shape[1]
    assert m % tm == 0 and n % tn == 0
    return pl.pallas_call(
        _mm_kernel,
        grid=(n // tn, m // tm),
        in_specs=[pl.BlockSpec((tm, k), lambda j, i: (i, 0)),
                  pl.BlockSpec((k, tn), lambda j, i: (0, j))],
        out_specs=pl.BlockSpec((tm, tn), lambda j, i: (i, j)),
        out_shape=jax.ShapeDtypeStruct((m, n), out_dtype),
        compiler_params=_cparams(("arbitrary", "arbitrary"), 56),
        name="proj_matmul",
    )(a, b)


def _mm_wt_kernel(a_ref, wt_ref, o_ref, wb_ref):
    @pl.when(pl.program_id(1) == 0)
    def _cast_weights():
        wb_ref[...] = wt_ref[0].astype(MXU)

    o_ref[...] = lax.dot_general(a_ref[...].astype(MXU), wb_ref[...], NT,
                                 preferred_element_type=F32).astype(o_ref.dtype)


def _proj_wt(a, wt_all, layer, *, col0, ncols, tm, tn, out_dtype):
    m, k = a.shape
    assert m % tm == 0 and ncols % tn == 0 and col0 % SUBLANES == 0
    return pl.pallas_call(
        _mm_wt_kernel,
        grid=(ncols // tn, m // tm),
        in_specs=[pl.BlockSpec((tm, k), lambda j, i: (i, 0)),
                  pl.BlockSpec((pl.Element(1), pl.Element(tn), pl.Element(k)),
                               lambda j, i: (layer, pl.multiple_of(int(col0) + j * tn, SUBLANES), 0))],
        out_specs=pl.BlockSpec((tm, tn), lambda j, i: (i, j)),
        out_shape=jax.ShapeDtypeStruct((m, ncols), out_dtype),
        scratch_shapes=[pltpu.VMEM((tn, k), MXU)],
        compiler_params=_cparams(("arbitrary", "arbitrary"), 56),
        name="proj_matmul_wt",
    )(a, wt_all)


def _small_proj_weights(wt):
    off = np.concatenate([[0], np.cumsum(IN_SPLIT_SIZES)])
    dt = wt[off[5]:off[6]].T
    gn = wt[off[8]:off[9]].T
    d = dt.shape[0]
    dtp = jnp.pad(dt.reshape(d, SSM_GROUPS, SSM_HEADS_PER_GROUP),
                  ((0, 0), (0, 0), (0, LANES - SSM_HEADS_PER_GROUP))).reshape(d, SSM_GROUPS * LANES)
    ng = NSA_HEADS_PER_GROUP * N_BRANCHES
    gnp = jnp.pad(gn.reshape(d, NSA_KV_GROUPS, ng), ((0, 0), (0, 0), (0, LANES - ng))).reshape(d, NSA_KV_GROUPS * LANES)
    return jnp.concatenate([dtp, gnp], axis=1).astype(MXU)


def _flash_init(m_ref, acc_ref):
    m_ref[...] = jnp.full(m_ref.shape, NEG, F32)
    acc_ref[...] = jnp.zeros(acc_ref.shape, F32)


def _flash_step(s, v_aug, m_ref, acc_ref):
    hd = HEAD_DIM
    m_prev = m_ref[...]
    m_new = jnp.maximum(m_prev, jnp.max(s, axis=1, keepdims=True))
    alpha = jnp.exp(m_prev - m_new)
    p = jnp.concatenate([jnp.exp(s[:, c * hd:(c + 1) * hd] - m_new) for c in range(s.shape[1] // hd)], axis=1)
    pv = jnp.dot(p.astype(MXU), v_aug, preferred_element_type=F32)
    acc_ref[:, 0:hd] = alpha * acc_ref[:, 0:hd] + pv[:, 0:hd]
    acc_ref[:, hd:2 * hd] = alpha * acc_ref[:, hd:2 * hd] + pv[:, hd:2 * hd]
    m_ref[...] = m_new


def _flash_out(acc_ref):
    return acc_ref[:, 0:HEAD_DIM] / acc_ref[:, HEAD_DIM:2 * HEAD_DIM]


def _flash_loop(n, scores, values, sa_ref, sb_ref, m_ref, acc_ref):
    @pl.when(n > 0)
    def _first():
        sa_ref[...] = scores(0)

    def pair(u, c):
        t = 2 * u
        sb_ref[...] = scores(t + 1)
        _flash_step(sa_ref[...], values(t), m_ref, acc_ref)
        sa_ref[...] = scores(t + 2)
        _flash_step(sb_ref[...], values(t + 1), m_ref, acc_ref)
        return c

    n_pairs = jnp.maximum(n - 1, 0) // 2
    lax.fori_loop(0, n_pairs, pair, 0)
    t0 = 2 * n_pairs
    left = n - t0

    @pl.when(left == 2)
    def _last_two():
        sb_ref[...] = scores(t0 + 1)
        _flash_step(sa_ref[...], values(t0), m_ref, acc_ref)
        _flash_step(sb_ref[...], values(t0 + 1), m_ref, acc_ref)

    @pl.when(left == 1)
    def _last_one():
        _flash_step(sa_ref[...], values(t0), m_ref, acc_ref)


MOBA_KV_GROUP = 4


def _moba_kernel(q_ref, k_ref, v_ref, o_ref, kaug_ref, vaug_ref, kmean_ref, m_ref, acc_ref, sa_ref, sb_ref, *, nb):
    blk = MOBA_BLOCK
    grp = MOBA_KV_GROUP * blk
    qi = pl.program_id(1)

    @pl.when(qi == 0)
    def _build_keys():
        kmean_ref[...] = jnp.zeros_like(kmean_ref)
        lane = lax.broadcasted_iota(jnp.int32, (blk, LANES), 1)
        ones = jnp.ones((blk, HEAD_DIM), vaug_ref.dtype)

        def body(j, c):
            r0 = pl.multiple_of(j * blk, blk)
            kb = k_ref[pl.ds(r0, blk), :]
            kaug_ref[pl.ds(r0, blk), 0:HEAD_DIM] = kb.astype(kaug_ref.dtype)
            kaug_ref[pl.ds(r0, blk), HEAD_DIM:2 * HEAD_DIM] = (lane == j).astype(kaug_ref.dtype)
            vaug_ref[pl.ds(r0, blk), 0:HEAD_DIM] = v_ref[pl.ds(r0, blk), :].astype(vaug_ref.dtype)
            vaug_ref[pl.ds(r0, blk), HEAD_DIM:2 * HEAD_DIM] = ones
            kmean_ref[pl.ds(j, 1), :] = jnp.sum(kb.astype(F32), axis=0, keepdims=True) * (1.0 / blk)
            return c

        lax.fori_loop(0, nb, body, 0)

    q = q_ref[...].astype(MXU)
    gate = lax.dot_general(q, kmean_ref[...].astype(MXU), NT, preferred_element_type=F32)
    lane = lax.broadcasted_iota(jnp.int32, (blk, LANES), 1).astype(F32)
    past = lane < qi.astype(F32)
    gate = jnp.where(past, gate, -jnp.inf)
    sel = jnp.zeros((blk, LANES), jnp.bool_)
    for _ in range(MOBA_TOPK):
        mx = jnp.max(gate, axis=1, keepdims=True)
        idx = jnp.min(jnp.where((gate == mx) & past, lane, float(LANES)), axis=1, keepdims=True)
        pick = lane == idx
        sel = sel | pick
        gate = jnp.where(pick, -jnp.inf, gate)
    bias = jnp.where(sel, 0.0, MASK_BIAS).astype(MXU)
    qaug = jnp.concatenate([q, bias], axis=1)

    _flash_init(m_ref, acc_ref)

    def scores(g):
        r0 = pl.multiple_of(g * grp, grp)
        return lax.dot_general(qaug, kaug_ref[pl.ds(r0, grp), :], NT, preferred_element_type=F32) * ATT_SCALE

    def values(g):
        return vaug_ref[pl.ds(pl.multiple_of(g * grp, grp), grp), :]

    _flash_loop((qi + MOBA_KV_GROUP - 1) // MOBA_KV_GROUP, scores, values, sa_ref, sb_ref, m_ref, acc_ref)
    r0 = pl.multiple_of(qi * blk, blk)
    s = lax.dot_general(q, k_ref[pl.ds(r0, blk), :].astype(MXU), NT, preferred_element_type=F32) * ATT_SCALE
    row = lax.broadcasted_iota(jnp.int32, (blk, blk), 0)
    col = lax.broadcasted_iota(jnp.int32, (blk, blk), 1)
    _flash_step(jnp.where(col <= row, s, NEG), vaug_ref[pl.ds(r0, blk), :], m_ref, acc_ref)
    o_ref[...] = _flash_out(acc_ref).astype(o_ref.dtype)


def _moba(p1):
    s = p1.shape[0]
    nb = s // MOBA_BLOCK
    assert nb <= LANES and nb % MOBA_KV_GROUP == 0
    kcol, vcol = P1_KA // HEAD_DIM, P1_VA // HEAD_DIM
    return pl.pallas_call(
        functools.partial(_moba_kernel, nb=nb),
        grid=(MOBA_HEADS, nb),
        in_specs=[pl.BlockSpec((MOBA_BLOCK, HEAD_DIM), lambda h, i: (i, h)),
                  pl.BlockSpec((s, HEAD_DIM), lambda h, i: (0, kcol + h)),
                  pl.BlockSpec((s, HEAD_DIM), lambda h, i: (0, vcol + h))],
        out_specs=pl.BlockSpec((MOBA_BLOCK, HEAD_DIM), lambda h, i: (i, h)),
        out_shape=jax.ShapeDtypeStruct((s, MOBA_WIDTH), MXU),
        scratch_shapes=[pltpu.VMEM((s, 2 * HEAD_DIM), MXU), pltpu.VMEM((s, 2 * HEAD_DIM), MXU),
                        pltpu.VMEM((LANES, HEAD_DIM), F32),
                        pltpu.VMEM((MOBA_BLOCK, HEAD_DIM), F32), pltpu.VMEM((MOBA_BLOCK, 2 * HEAD_DIM), F32),
                        pltpu.VMEM((MOBA_BLOCK, MOBA_KV_GROUP * MOBA_BLOCK), F32),
                        pltpu.VMEM((MOBA_BLOCK, MOBA_KV_GROUP * MOBA_BLOCK), F32)],
        compiler_params=_cparams(("arbitrary", "arbitrary")),
        name="moba_attention",
    )(p1, p1, p1)


def _ssd_kernel(dtb_ref, alog_ref, dskip_ref,
                xs_ref, bm_ref, cm_ref, z_ref, dt_ref, cwx_ref, cwb_ref, cwc_ref, cbx_ref, cbb_ref, cbc_ref,
                ng_ref, o_ref, xbuf, bbuf, cbuf, state_ref, ybuf):
    L = SSM_CHUNK
    W = SSM_GROUP_WIDTH
    J = SSM_HEADS_PER_GROUP
    P = SSM_HEAD_DIM
    g = pl.program_id(0)
    c = pl.program_id(1)

    @pl.when(c == 0)
    def _reset():
        xbuf[0:SUBLANES, :] = jnp.zeros((SUBLANES, W), F32)
        bbuf[0:SUBLANES, :] = jnp.zeros((SUBLANES, SSM_STATE), F32)
        cbuf[0:SUBLANES, :] = jnp.zeros((SUBLANES, SSM_STATE), F32)
        state_ref[...] = jnp.zeros_like(state_ref)

    def conv_silu(buf, raw_ref, w_ref, b_ref):
        buf[SUBLANES:SUBLANES + L, :] = raw_ref[...]
        acc = b_ref[...]
        for i in range(SSM_CONV):
            lo = SUBLANES - (SSM_CONV - 1) + i
            acc = acc + w_ref[i:i + 1, :] * buf[lo:lo + L, :]
        buf[0:SUBLANES, :] = buf[L:L + SUBLANES, :]
        return _silu(acc)

    xs = conv_silu(xbuf, xs_ref, cwx_ref, cbx_ref)
    bm = conv_silu(bbuf, bm_ref, cwb_ref, cbb_ref)
    cm = conv_silu(cbuf, cm_ref, cwc_ref, cbc_ref)

    lane = lax.broadcasted_iota(jnp.int32, (1, LANES), 1)
    dtb = jnp.zeros((1, LANES), F32)
    alog = jnp.full((1, LANES), -jnp.inf, F32)
    for j in range(J):
        dtb = jnp.where(lane == j, dtb_ref[g * J + j], dtb)
        alog = jnp.where(lane == j, alog_ref[g * J + j], alog)
    x = dt_ref[...] + dtb
    dt = jnp.maximum(x, 0.0) + jnp.log(1.0 + jnp.exp(-jnp.abs(x)))
    dt = jnp.where(lane < J, dt, 0.0)
    a = dt * (-jnp.exp(alog))

    er = lax.broadcasted_iota(jnp.int32, (LANES, W), 0)
    ec = lax.broadcasted_iota(jnp.int32, (LANES, W), 1)
    expand = ((ec >> 6) == er).astype(jnp.bfloat16)
    tr = lax.broadcasted_iota(jnp.int32, (L, L), 0)
    tc = lax.broadcasted_iota(jnp.int32, (L, L), 1)
    tril = tr >= tc
    tril_b = tril.astype(jnp.bfloat16)
    dt_e = _dot3(expand, dt, left=False)
    a_e = _dot3(expand, a, left=False)
    acs = _dot3(tril_b, a_e, left=True)
    acs_t = acs.T
    a_last = acs[L - 1:L, :]

    xdt = xs * dt_e
    cb = lax.dot_general(cm.astype(MXU), bm.astype(MXU), NT, preferred_element_type=F32)
    for j in range(J):
        colv = acs[:, j * P:j * P + 1]
        rowv = acs_t[j * P:j * P + 1, :]
        dec = jnp.exp(jnp.where(tril, colv - rowv, -jnp.inf))
        ybuf[:, j * P:(j + 1) * P] = jnp.dot((cb * dec).astype(MXU), xdt[:, j * P:(j + 1) * P].astype(MXU),
                                             preferred_element_type=F32)
    st_old = state_ref[...]
    xdte = (xdt * jnp.exp(a_last - acs)).astype(MXU)
    st_new = jnp.dot(bm.T.astype(MXU), xdte, preferred_element_type=F32)
    y_off = jnp.dot(cm.astype(MXU), st_old.astype(MXU), preferred_element_type=F32) * jnp.exp(acs)
    state_ref[...] = st_old * jnp.exp(a_last) + st_new

    lane_w = lax.broadcasted_iota(jnp.int32, (1, W), 1)
    dsk = jnp.zeros((1, W), F32)
    for j in range(J):
        dsk = jnp.where((lane_w >> 6) == j, dskip_ref[g * J + j], dsk)
    y = ybuf[...] + y_off + xs * dsk
    y = y * _silu(z_ref[...])
    y = y * lax.rsqrt(jnp.mean(y * y, axis=1, keepdims=True) + RMS_EPS)
    o_ref[...] = (y * ng_ref[...]).astype(o_ref.dtype)


def _ssd(p2, p3, conv_w, conv_b, dt_bias, a_log, d_skip, norm_g):
    s = p2.shape[0]
    L, W, N = SSM_CHUNK, SSM_GROUP_WIDTH, SSM_STATE
    xs0 = P2_XBC // W
    bm0 = (P2_XBC + SSM_D_INNER) // N
    cm0 = bm0 + SSM_GROUPS
    z0 = P2_Z // W
    cb2 = conv_b.reshape(1, SSM_CONV_CH)
    ng2 = norm_g.reshape(1, SSM_D_INNER)
    grid_spec = pltpu.PrefetchScalarGridSpec(
        num_scalar_prefetch=3,
        grid=(SSM_GROUPS, s // L),
        in_specs=[
            pl.BlockSpec((L, W), lambda g, c, *_: (c, xs0 + g)),
            pl.BlockSpec((L, N), lambda g, c, *_: (c, bm0 + g)),
            pl.BlockSpec((L, N), lambda g, c, *_: (c, cm0 + g)),
            pl.BlockSpec((L, W), lambda g, c, *_: (c, z0 + g)),
            pl.BlockSpec((L, LANES), lambda g, c, *_: (c, g)),
            pl.BlockSpec((SSM_CONV, W), lambda g, c, *_: (0, g)),
            pl.BlockSpec((SSM_CONV, N), lambda g, c, *_: (0, SSM_D_INNER // N + g)),
            pl.BlockSpec((SSM_CONV, N), lambda g, c, *_: (0, SSM_D_INNER // N + SSM_GROUPS + g)),
            pl.BlockSpec((1, W), lambda g, c, *_: (0, g)),
            pl.BlockSpec((1, N), lambda g, c, *_: (0, SSM_D_INNER // N + g)),
            pl.BlockSpec((1, N), lambda g, c, *_: (0, SSM_D_INNER // N + SSM_GROUPS + g)),
            pl.BlockSpec((1, W), lambda g, c, *_: (0, g)),
        ],
        out_specs=pl.BlockSpec((L, W), lambda g, c, *_: (c, g)),
        scratch_shapes=[pltpu.VMEM((SUBLANES + L, W), F32), pltpu.VMEM((SUBLANES + L, N), F32),
                        pltpu.VMEM((SUBLANES + L, N), F32), pltpu.VMEM((N, W), F32), pltpu.VMEM((L, W), F32)],
    )
    return pl.pallas_call(
        _ssd_kernel,
        grid_spec=grid_spec,
        out_shape=jax.ShapeDtypeStruct((s, SSM_D_INNER), MXU),
        compiler_params=_cparams(("arbitrary", "arbitrary")),
        name="ssd_mixer",
    )(dt_bias, a_log, d_skip, p2, p2, p2, p2, p3, conv_w, conv_w, conv_w, cb2, cb2, cb2, ng2)


def _nsa_compress_kernel(t_ref, pos_ref, w1_ref, w2_ref, o_ref):
    half = NSA_CMP_STRIDE * HEAD_DIM
    t = t_ref[0].astype(F32)
    pos = pos_ref[0]
    lo = (t + pos[:, :half]).astype(MXU)
    hi = (t + pos[:, half:]).astype(MXU)
    w1 = w1_ref[0]
    a = jnp.dot(lo, w1[:half].astype(MXU), preferred_element_type=F32)
    b = jnp.dot(hi, w1[half:].astype(MXU), preferred_element_type=F32)
    n = t.shape[0]
    pre = a + pltpu.roll(b, n - 1, 0)
    act = jax.nn.gelu(pre, approximate=True)
    o_ref[0] = jnp.dot(act.astype(MXU), w2_ref[0].astype(MXU), preferred_element_type=F32).astype(o_ref.dtype)


def _nsa_compress(p1, pos_k, w1_k, w2_k, pos_v, w1_v, w2_v):
    s = p1.shape[0]
    n_str = s // NSA_CMP_STRIDE
    kv = p1[:, P1_KVN:P1_KVN + 2 * NSA_KV_WIDTH]
    t4 = kv.reshape(s, 4, HEAD_DIM).transpose(1, 0, 2).reshape(4, n_str, NSA_CMP_STRIDE * HEAD_DIM)
    pos = jnp.stack([pos_k, pos_v]).reshape(2, 1, NSA_CMP_LEN * HEAD_DIM)
    w1 = jnp.stack([w1_k, w1_v])
    w2 = jnp.stack([w2_k, w2_v])
    return pl.pallas_call(
        _nsa_compress_kernel,
        grid=(4,),
        in_specs=[pl.BlockSpec((1, n_str, NSA_CMP_STRIDE * HEAD_DIM), lambda i: (i, 0, 0)),
                  pl.BlockSpec((1, 1, NSA_CMP_LEN * HEAD_DIM), lambda i: (i // 2, 0, 0)),
                  pl.BlockSpec((1, NSA_CMP_LEN * HEAD_DIM, HEAD_DIM), lambda i: (i // 2, 0, 0)),
                  pl.BlockSpec((1, HEAD_DIM, HEAD_DIM), lambda i: (i // 2, 0, 0))],
        out_specs=pl.BlockSpec((1, n_str, HEAD_DIM), lambda i: (i, 0, 0)),
        out_shape=jax.ShapeDtypeStruct((4, n_str, HEAD_DIM), MXU),
        compiler_params=_cparams(("arbitrary",)),
        name="nsa_compress",
    )(t4, pos, w1, w2)


def _nsa_cmp_kernel(q_ref, kc_ref, vc_ref, gn_ref, o_ref, selb_ref, *, n_sel):
    tq = ATT_TILE
    qi = pl.program_id(1)
    kc = kc_ref[0].astype(MXU)
    vc = vc_ref[0].astype(MXU)
    nc = kc.shape[0]
    pos = qi * tq + lax.broadcasted_iota(jnp.int32, (tq, nc), 0)
    cidx = lax.broadcasted_iota(jnp.int32, (tq, nc), 1)
    valid = (cidx * NSA_CMP_STRIDE + NSA_CMP_LEN - 1 <= pos) & (cidx < nc - 1)
    c_start = lax.broadcasted_iota(jnp.int32, (LANES, nc), 1) * NSA_CMP_STRIDE
    s_start = lax.broadcasted_iota(jnp.int32, (LANES, nc), 0) * NSA_SEL_BLOCK
    overlap_t = ((c_start < s_start + NSA_SEL_BLOCK) & (c_start + NSA_CMP_LEN > s_start)).astype(MXU)
    gates = _sigmoid(gn_ref[...])
    imp = jnp.zeros((LANES, tq), F32)
    for j in range(NSA_HEADS_PER_GROUP):
        q = q_ref[:, j * HEAD_DIM:(j + 1) * HEAD_DIM].astype(MXU)
        s = lax.dot_general(q, kc, NT, preferred_element_type=F32) * ATT_SCALE
        s = jnp.where(valid, s, -jnp.inf)
        m = jnp.max(s, axis=1, keepdims=True)
        m = jnp.where(m > -jnp.inf, m, 0.0)
        e = jnp.exp(s - m)
        den = jnp.sum(e, axis=1, keepdims=True)
        p = (e / jnp.where(den > 0, den, 1.0)).astype(MXU)
        o = jnp.dot(p, vc, preferred_element_type=F32)
        imp = imp + lax.dot_general(overlap_t, p, NT, preferred_element_type=F32)
        o_ref[:, j * HEAD_DIM:(j + 1) * HEAD_DIM] = o * gates[:, N_BRANCHES * j:N_BRANCHES * j + 1]

    blk = lax.broadcasted_iota(jnp.int32, (LANES, tq), 0).astype(F32)
    cur = ((qi * tq + lax.broadcasted_iota(jnp.int32, (LANES, tq), 1)) >> 6).astype(F32)
    allowed = blk <= cur
    forced = (blk == 0.0) | (blk == cur) | (blk == cur - 1.0)
    val = jnp.where(forced, jnp.inf, jnp.where(allowed, imp, -jnp.inf))
    val = jnp.where(blk < float(n_sel), val, -jnp.inf)

    def pick_round(_, c):
        val, sel = c
        mx = jnp.max(val, axis=0, keepdims=True)
        idx = jnp.min(jnp.where(val == mx, blk, float(LANES)), axis=0, keepdims=True)
        pick = blk == idx
        sel = jnp.where(pick & allowed, 1.0, sel)
        val = jnp.where(pick, -jnp.inf, val)
        return val, sel

    _, sel = lax.fori_loop(0, min(NSA_TOPN, n_sel), pick_round, (val, jnp.zeros((LANES, tq), F32)))
    selb_ref[0] = jnp.where(sel.T > 0, 0.0, MASK_BIAS).astype(selb_ref.dtype)


def _nsa_cmp(p1, p3, kvc):
    s = p1.shape[0]
    n_str = s // NSA_CMP_STRIDE
    n_sel = s // NSA_SEL_BLOCK
    assert n_sel <= LANES
    gw = NSA_HEADS_PER_GROUP * HEAD_DIM
    q0 = P1_QN // gw
    return pl.pallas_call(
        functools.partial(_nsa_cmp_kernel, n_sel=n_sel),
        grid=(NSA_KV_GROUPS, s // ATT_TILE),
        in_specs=[pl.BlockSpec((ATT_TILE, gw), lambda g, i: (i, q0 + g)),
                  pl.BlockSpec((1, n_str, HEAD_DIM), lambda g, i: (g, 0, 0)),
                  pl.BlockSpec((1, n_str, HEAD_DIM), lambda g, i: (NSA_KV_GROUPS + g, 0, 0)),
                  pl.BlockSpec((ATT_TILE, LANES), lambda g, i: (i, SSM_GROUPS + g))],
        out_specs=[pl.BlockSpec((ATT_TILE, gw), lambda g, i: (i, g)),
                   pl.BlockSpec((1, ATT_TILE, LANES), lambda g, i: (g, i, 0))],
        out_shape=[jax.ShapeDtypeStruct((s, NSA_WIDTH), F32),
                   jax.ShapeDtypeStruct((NSA_KV_GROUPS, s, LANES), MXU)],
        compiler_params=_cparams(("arbitrary", "arbitrary")),
        name="nsa_compressed_attention",
    )(p1, kvc, kvc, p3)


def _nsa_sel_kernel(q_ref, selb_ref, k_ref, v_ref, gn_ref, prev_ref, o_ref, kaug_ref, vaug_ref, qaug_ref,
                    m_ref, acc_ref, sa_ref, sb_ref, *, n_tiles):
    tq = ATT_TILE
    J = NSA_HEADS_PER_GROUP
    qi = pl.program_id(1)

    @pl.when(qi == 0)
    def _build_keys():
        lane = lax.broadcasted_iota(jnp.int32, (tq, LANES), 1)
        rowi = lax.broadcasted_iota(jnp.int32, (tq, LANES), 0)
        ones = jnp.ones((tq, HEAD_DIM), vaug_ref.dtype)

        def body(t, c):
            r0 = pl.multiple_of(t * tq, tq)
            kaug_ref[pl.ds(r0, tq), 0:HEAD_DIM] = k_ref[pl.ds(r0, tq), :].astype(kaug_ref.dtype)
            kaug_ref[pl.ds(r0, tq), HEAD_DIM:2 * HEAD_DIM] = (
                lane == ((t * tq + rowi) >> 6)).astype(kaug_ref.dtype)
            vaug_ref[pl.ds(r0, tq), 0:HEAD_DIM] = v_ref[pl.ds(r0, tq), :].astype(vaug_ref.dtype)
            vaug_ref[pl.ds(r0, tq), HEAD_DIM:2 * HEAD_DIM] = ones
            return c

        lax.fori_loop(0, n_tiles, body, 0)

    selb = selb_ref[0]
    for j in range(J):
        qaug_ref[j * tq:(j + 1) * tq, 0:HEAD_DIM] = q_ref[:, j * HEAD_DIM:(j + 1) * HEAD_DIM].astype(qaug_ref.dtype)
        qaug_ref[j * tq:(j + 1) * tq, HEAD_DIM:2 * HEAD_DIM] = selb
    _flash_init(m_ref, acc_ref)

    def scores(t):
        r0 = pl.multiple_of(t * tq, tq)
        return lax.dot_general(qaug_ref[...], kaug_ref[pl.ds(r0, tq), :], NT,
                               preferred_element_type=F32) * ATT_SCALE

    def values(t):
        return vaug_ref[pl.ds(pl.multiple_of(t * tq, tq), tq), :]

    _flash_loop(qi, scores, values, sa_ref, sb_ref, m_ref, acc_ref)
    row = lax.broadcasted_iota(jnp.int32, (J * tq, tq), 0) & (tq - 1)
    col = lax.broadcasted_iota(jnp.int32, (J * tq, tq), 1)
    _flash_step(jnp.where(col <= row, scores(qi), NEG), values(qi), m_ref, acc_ref)
    o = _flash_out(acc_ref)
    gates = _sigmoid(gn_ref[...])
    for j in range(J):
        o_ref[:, j * HEAD_DIM:(j + 1) * HEAD_DIM] = (
            prev_ref[:, j * HEAD_DIM:(j + 1) * HEAD_DIM]
            + o[j * tq:(j + 1) * tq] * gates[:, N_BRANCHES * j + 1:N_BRANCHES * j + 2])


def _nsa_sel(p1, p3, selb, prev):
    s = p1.shape[0]
    gw = NSA_HEADS_PER_GROUP * HEAD_DIM
    q0 = P1_QN // gw
    k0 = (P1_KVN + 2 * NSA_KV_WIDTH) // HEAD_DIM
    v0 = k0 + NSA_KV_GROUPS
    return pl.pallas_call(
        functools.partial(_nsa_sel_kernel, n_tiles=s // ATT_TILE),
        grid=(NSA_KV_GROUPS, s // ATT_TILE),
        in_specs=[pl.BlockSpec((ATT_TILE, gw), lambda g, i: (i, q0 + g)),
                  pl.BlockSpec((1, ATT_TILE, LANES), lambda g, i: (g, i, 0)),
                  pl.BlockSpec((s, HEAD_DIM), lambda g, i: (0, k0 + g)),
                  pl.BlockSpec((s, HEAD_DIM), lambda g, i: (0, v0 + g)),
                  pl.BlockSpec((ATT_TILE, LANES), lambda g, i: (i, SSM_GROUPS + g)),
                  pl.BlockSpec((ATT_TILE, gw), lambda g, i: (i, g))],
        out_specs=pl.BlockSpec((ATT_TILE, gw), lambda g, i: (i, g)),
        out_shape=jax.ShapeDtypeStruct((s, NSA_WIDTH), F32),
        scratch_shapes=[pltpu.VMEM((s, 2 * HEAD_DIM), MXU), pltpu.VMEM((s, 2 * HEAD_DIM), MXU),
                        pltpu.VMEM((NSA_HEADS_PER_GROUP * ATT_TILE, 2 * HEAD_DIM), MXU),
                        pltpu.VMEM((NSA_HEADS_PER_GROUP * ATT_TILE, HEAD_DIM), F32),
                        pltpu.VMEM((NSA_HEADS_PER_GROUP * ATT_TILE, 2 * HEAD_DIM), F32),
                        pltpu.VMEM((NSA_HEADS_PER_GROUP * ATT_TILE, ATT_TILE), F32),
                        pltpu.VMEM((NSA_HEADS_PER_GROUP * ATT_TILE, ATT_TILE), F32)],
        compiler_params=_cparams(("arbitrary", "arbitrary")),
        name="nsa_selected_attention",
    )(p1, selb, p1, p1, p3, prev)


def _nsa_win_kernel(q_ref, k_ref, v_ref, gn_ref, prev_ref, o_ref, qst_ref, m_ref, acc_ref):
    tq = ATT_TILE
    J = NSA_HEADS_PER_GROUP
    halo = NSA_WINDOW // tq
    qi = pl.program_id(1)
    for j in range(J):
        qst_ref[j * tq:(j + 1) * tq, :] = q_ref[:, j * HEAD_DIM:(j + 1) * HEAD_DIM].astype(qst_ref.dtype)
    row = lax.broadcasted_iota(jnp.int32, (J * tq, tq), 0) & (tq - 1)
    col = lax.broadcasted_iota(jnp.int32, (J * tq, tq), 1)
    ones = jnp.ones((tq, HEAD_DIM), MXU)
    _flash_init(m_ref, acc_ref)

    def step(t, c):
        r0 = pl.multiple_of(t * tq, tq)
        s = lax.dot_general(qst_ref[...], k_ref[pl.ds(r0, tq), :].astype(MXU), NT,
                            preferred_element_type=F32) * ATT_SCALE
        diff = (qi - t) * tq + row - col
        s = jnp.where((diff >= 0) & (diff < NSA_WINDOW), s, NEG)
        v_aug = jnp.concatenate([v_ref[pl.ds(r0, tq), :].astype(MXU), ones], axis=1)
        _flash_step(s, v_aug, m_ref, acc_ref)
        return c

    lax.fori_loop(jnp.maximum(qi - halo, 0), qi + 1, step, 0)
    o = _flash_out(acc_ref)
    gates = _sigmoid(gn_ref[...])
    for j in range(J):
        o_ref[:, j * HEAD_DIM:(j + 1) * HEAD_DIM] = (
            prev_ref[:, j * HEAD_DIM:(j + 1) * HEAD_DIM]
            + o[j * tq:(j + 1) * tq] * gates[:, N_BRANCHES * j + 2:N_BRANCHES * j + 3])


def _nsa_win(p1, p3, prev):
    s = p1.shape[0]
    gw = NSA_HEADS_PER_GROUP * HEAD_DIM
    q0 = P1_QN // gw
    k0 = (P1_KVN + 4 * NSA_KV_WIDTH) // HEAD_DIM
    v0 = k0 + NSA_KV_GROUPS
    return pl.pallas_call(
        _nsa_win_kernel,
        grid=(NSA_KV_GROUPS, s // ATT_TILE),
        in_specs=[pl.BlockSpec((ATT_TILE, gw), lambda g, i: (i, q0 + g)),
                  pl.BlockSpec((s, HEAD_DIM), lambda g, i: (0, k0 + g)),
                  pl.BlockSpec((s, HEAD_DIM), lambda g, i: (0, v0 + g)),
                  pl.BlockSpec((ATT_TILE, LANES), lambda g, i: (i, SSM_GROUPS + g)),
                  pl.BlockSpec((ATT_TILE, gw), lambda g, i: (i, g))],
        out_specs=pl.BlockSpec((ATT_TILE, gw), lambda g, i: (i, g)),
        out_shape=jax.ShapeDtypeStruct((s, NSA_WIDTH), F32),
        scratch_shapes=[pltpu.VMEM((NSA_HEADS_PER_GROUP * ATT_TILE, HEAD_DIM), MXU),
                        pltpu.VMEM((NSA_HEADS_PER_GROUP * ATT_TILE, HEAD_DIM), F32),
                        pltpu.VMEM((NSA_HEADS_PER_GROUP * ATT_TILE, 2 * HEAD_DIM), F32)],
        compiler_params=_cparams(("arbitrary", "arbitrary")),
        name="nsa_window_attention",
    )(p1, p1, p1, p3, prev)


def _merge_kernel(ya_ref, yb_ref, yc_ref, g0_ref, g1_ref, g2_ref, wa_ref, wb_ref, wc_ref, o_ref):
    def branch(y_ref, w_ref, g_ref):
        prod = jnp.dot(y_ref[...].astype(MXU), w_ref[...].astype(MXU), preferred_element_type=F32)
        return _sigmoid(g_ref[...]) * prod

    o_ref[...] = (branch(ya_ref, wa_ref, g0_ref) + branch(yb_ref, wb_ref, g1_ref)
                  + branch(yc_ref, wc_ref, g2_ref)).astype(o_ref.dtype)


def _merge(ya, yb, yc, p2, wa, wb, wc, *, tm=512, tn=512):
    s = ya.shape[0]
    d = wa.shape[1]
    g0 = P2_GM // tn
    gstep = d // tn
    return pl.pallas_call(
        _merge_kernel,
        grid=(d // tn, s // tm),
        in_specs=[pl.BlockSpec((tm, ya.shape[1]), lambda j, i: (i, 0)),
                  pl.BlockSpec((tm, yb.shape[1]), lambda j, i: (i, 0)),
                  pl.BlockSpec((tm, yc.shape[1]), lambda j, i: (i, 0)),
                  pl.BlockSpec((tm, tn), lambda j, i: (i, g0 + j)),
                  pl.BlockSpec((tm, tn), lambda j, i: (i, g0 + gstep + j)),
                  pl.BlockSpec((tm, tn), lambda j, i: (i, g0 + 2 * gstep + j)),
                  pl.BlockSpec((wa.shape[0], tn), lambda j, i: (0, j)),
                  pl.BlockSpec((wb.shape[0], tn), lambda j, i: (0, j)),
                  pl.BlockSpec((wc.shape[0], tn), lambda j, i: (0, j))],
        out_specs=pl.BlockSpec((tm, tn), lambda j, i: (i, j)),
        out_shape=jax.ShapeDtypeStruct((s, d), MXU),
        compiler_params=_cparams(("arbitrary", "arbitrary")),
        name="branch_merge",
    )(ya, yb, yc, p2, p2, p2, wa, wb, wc)


def _layer_norm_rows(x, g, b):
    xc = x - jnp.mean(x, axis=1, keepdims=True)
    var = jnp.mean(xc * xc, axis=1, keepdims=True)
    return xc * lax.rsqrt(var + LN_EPS) * g + b


def _pack_halves(y):
    half = y.shape[1] // 2
    bits = lax.bitcast_convert_type(y.astype(jnp.bfloat16).astype(F32), jnp.int32)
    return ((bits[:, :half] >> 16) & jnp.int32(0xFFFF)) | (bits[:, half:] & jnp.int32(-65536))


def _unpack_halves_f32(w):
    return lax.bitcast_convert_type(w << 16, F32), lax.bitcast_convert_type(w & jnp.int32(-65536), F32)


def _unpack_halves(w):
    lo, hi = _unpack_halves_f32(w)
    return lo.astype(MXU), hi.astype(MXU)


def _wout_ln_kernel(m_ref, w_ref, h_ref, g_ref, b_ref, o_ref, ob_ref, op_ref):
    acc = jnp.dot(m_ref[...].astype(MXU), w_ref[...].astype(MXU), preferred_element_type=F32)
    y = _layer_norm_rows(DEEPNORM_ALPHA * h_ref[...] + acc, g_ref[...], b_ref[...])
    o_ref[...] = y
    ob_ref[...] = y.astype(ob_ref.dtype)
    op_ref[...] = _pack_halves(y)


def _wout_ln(merged, w_out, h, ln_g, ln_b, *, tm=256):
    s, d = h.shape
    row = pl.BlockSpec((tm, d), lambda i: (i, 0))
    vec = pl.BlockSpec((1, d), lambda i: (0, 0))
    return pl.pallas_call(
        _wout_ln_kernel,
        grid=(s // tm,),
        in_specs=[row, pl.BlockSpec((d, d), lambda i: (0, 0)), row, vec, vec],
        out_specs=[row, row, pl.BlockSpec((tm, d // 2), lambda i: (i, 0))],
        out_shape=[jax.ShapeDtypeStruct((s, d), F32), jax.ShapeDtypeStruct((s, d), MXU),
                   jax.ShapeDtypeStruct((s, d // 2), jnp.int32)],
        compiler_params=_cparams(("arbitrary",)),
        name="out_proj_layernorm",
    )(merged, w_out.astype(MXU), h, ln_g.reshape(1, d), ln_b.reshape(1, d))


def _dispatch_kernel(ztile_ref, dest_ref, hp_ref, o_ref, zbuf, sem, zsem):
    tt = dest_ref.shape[1]
    tm = zbuf.shape[0]

    @pl.when(pl.program_id(0) == 0)
    def _zero_partial_tiles():
        zbuf[...] = jnp.zeros_like(zbuf)

        def tile_copy(t):
            return pltpu.make_async_copy(zbuf, o_ref.at[pl.ds(t * tm, tm)], zsem)

        def start(j, c):
            @pl.when(ztile_ref[j] >= 0)
            def _():
                tile_copy(ztile_ref[j]).start()
            return c

        def wait(j, c):
            @pl.when(ztile_ref[j] >= 0)
            def _():
                tile_copy(ztile_ref[j]).wait()
            return c

        lax.fori_loop(0, ztile_ref.shape[0], start, 0)
        lax.fori_loop(0, ztile_ref.shape[0], wait, 0)

    def issue(r, c):
        for k in range(TOP_K):
            pltpu.make_async_copy(hp_ref.at[pl.ds(r, 1)], o_ref.at[pl.ds(dest_ref[k, r], 1)], sem).start()
        return c

    lax.fori_loop(0, tt, issue, 0)
    rows = o_ref.at[pl.ds(0, TOP_K * tt)]
    pltpu.make_async_copy(rows, rows, sem).wait()


def _dispatch(hp, dest, zero_tiles, n_rows, *, tm, tt=256):
    s, half = hp.shape
    grid_spec = pltpu.PrefetchScalarGridSpec(
        num_scalar_prefetch=1,
        grid=(s // tt,),
        in_specs=[pl.BlockSpec((TOP_K, tt), lambda i, zt: (0, i), memory_space=pltpu.SMEM),
                  pl.BlockSpec((tt, half), lambda i, zt: (i, 0))],
        out_specs=pl.BlockSpec(memory_space=pl.ANY),
        scratch_shapes=[pltpu.VMEM((tm, half), jnp.int32), pltpu.SemaphoreType.DMA(()),
                        pltpu.SemaphoreType.DMA(())],
    )
    return pl.pallas_call(
        _dispatch_kernel,
        grid_spec=grid_spec,
        out_shape=jax.ShapeDtypeStruct((n_rows, half), jnp.int32),
        compiler_params=_cparams(("arbitrary",)),
        name="moe_dispatch",
    )(zero_tiles, dest, hp)


def _combine_kernel(dcur_ref, dnxt_ref, w_ref, h_ref, sh_ref, g_ref, b_ref, y_ref, o_ref, ob_ref, buf, sem):
    tt = dcur_ref.shape[1]
    i = pl.program_id(0)
    n = pl.num_programs(0)
    slot = i % 2

    def issue(d_ref, sl):
        def body(r, c):
            for k in range(TOP_K):
                pltpu.make_async_copy(y_ref.at[pl.ds(d_ref[k, r], 1)], buf.at[sl, pl.ds(k * tt + r, 1)],
                                      sem.at[sl]).start()
            return c

        lax.fori_loop(0, tt, body, 0)

    @pl.when(i == 0)
    def _():
        issue(dcur_ref, slot)

    @pl.when(i + 1 < n)
    def _():
        issue(dnxt_ref, 1 - slot)

    pltpu.make_async_copy(y_ref.at[pl.ds(0, TOP_K * tt)], buf.at[slot], sem.at[slot]).wait()
    w = w_ref[...]
    lo = hi = None
    for k in range(TOP_K):
        ya, yb = _unpack_halves_f32(buf[slot, k * tt:(k + 1) * tt])
        ta, tb = w[:, k:k + 1] * ya, w[:, k:k + 1] * yb
        lo, hi = (ta, tb) if lo is None else (lo + ta, hi + tb)
    routed = jnp.concatenate([lo, hi], axis=1)
    y = _layer_norm_rows(DEEPNORM_ALPHA * h_ref[...] + (routed + sh_ref[...]), g_ref[...], b_ref[...])
    o_ref[...] = y
    ob_ref[...] = y.astype(ob_ref.dtype)


def _combine_ln(h, y_rows, dest, top_w, shared, ln_g, ln_b, *, tt=128):
    s, d = h.shape
    n = s // tt
    row = pl.BlockSpec((tt, d), lambda i: (i, 0))
    vec = pl.BlockSpec((1, d), lambda i: (0, 0))
    return pl.pallas_call(
        _combine_kernel,
        grid=(n,),
        in_specs=[pl.BlockSpec((TOP_K, tt), lambda i: (0, i), memory_space=pltpu.SMEM),
                  pl.BlockSpec((TOP_K, tt), lambda i: (0, jnp.minimum(i + 1, n - 1)), memory_space=pltpu.SMEM),
                  pl.BlockSpec((tt, TOP_K), lambda i: (i, 0)),
                  row, row, vec, vec,
                  pl.BlockSpec(memory_space=pl.ANY)],
        out_specs=[row, row],
        out_shape=[jax.ShapeDtypeStruct((s, d), F32), jax.ShapeDtypeStruct((s, d), MXU)],
        scratch_shapes=[pltpu.VMEM((2, TOP_K * tt, d // 2), jnp.int32), pltpu.SemaphoreType.DMA((2,))],
        compiler_params=_cparams(("arbitrary",)),
        name="moe_combine_layernorm",
    )(dest, dest, top_w.T, h, shared, ln_g.reshape(1, d), ln_b.reshape(1, d), y_rows)


def _router_kernel(h_ref, wr_ref, rb_ref, e_ref, w_ref, pos_ref, cnt_ref, carry_ref):
    tq = h_ref.shape[0]
    i = pl.program_id(0)

    @pl.when(i == 0)
    def _reset():
        carry_ref[...] = jnp.zeros_like(carry_ref)

    logits = lax.dot_general(wr_ref[...].astype(MXU), h_ref[...].astype(MXU), NT, preferred_element_type=F32)
    scores = _sigmoid(logits)
    biased = scores + rb_ref[...]
    G, PG = N_EXPERT_GROUPS, EXPERTS_PER_GROUP
    sub = lax.broadcasted_iota(jnp.int32, (PG, tq), 0).astype(F32)
    gi = lax.broadcasted_iota(jnp.int32, (G, tq), 0).astype(F32)
    gs = jnp.zeros((G, tq), F32)
    for g in range(G):
        blk = biased[g * PG:(g + 1) * PG, :]
        m1 = jnp.max(blk, axis=0, keepdims=True)
        i1 = jnp.min(jnp.where(blk == m1, sub, float(PG)), axis=0, keepdims=True)
        m2 = jnp.max(jnp.where(sub == i1, -jnp.inf, blk), axis=0, keepdims=True)
        gs = jnp.where(gi == float(g), m1 + m2, gs)
    keep = jnp.zeros((G, tq), F32)
    for _ in range(TOPK_GROUPS):
        mx = jnp.max(gs, axis=0, keepdims=True)
        idx = jnp.min(jnp.where(gs == mx, gi, float(G)), axis=0, keepdims=True)
        pick = gi == idx
        keep = jnp.where(pick, 1.0, keep)
        gs = jnp.where(pick, -jnp.inf, gs)
    val = jnp.concatenate(
        [jnp.where(keep[g:g + 1, :] > 0, biased[g * PG:(g + 1) * PG, :], -jnp.inf) for g in range(G)], axis=0)
    ei = lax.broadcasted_iota(jnp.int32, (N_EXPERTS, tq), 0).astype(F32)
    picks, svals = [], []
    sel = jnp.zeros((N_EXPERTS, tq), F32)
    for r in range(TOP_K):
        mx = jnp.max(val, axis=0, keepdims=True)
        idx = jnp.min(jnp.where(val == mx, ei, float(N_EXPERTS)), axis=0, keepdims=True)
        pick = ei == idx
        picks.append(pick)
        svals.append(jnp.sum(jnp.where(pick, scores, 0.0), axis=0, keepdims=True))
        e_ref[r:r + 1, :] = idx.astype(jnp.int32)
        sel = jnp.where(pick, 1.0, sel)
        val = jnp.where(pick, -jnp.inf, val)
    wsum = svals[0]
    for r in range(1, TOP_K):
        wsum = wsum + svals[r]
    tr = lax.broadcasted_iota(jnp.int32, (tq, tq), 0)
    tc = lax.broadcasted_iota(jnp.int32, (tq, tq), 1)
    before = (tr < tc).astype(jnp.bfloat16)
    prefix = jnp.dot(sel.astype(jnp.bfloat16), before, preferred_element_type=F32)
    pos = carry_ref[:, 0:1] + prefix
    for r in range(TOP_K):
        w_ref[r:r + 1, :] = svals[r] / wsum * ROUTED_SCALE
        pos_ref[r:r + 1, :] = jnp.sum(jnp.where(picks[r], pos, 0.0), axis=0, keepdims=True).astype(jnp.int32)
    total = carry_ref[...] + jnp.sum(sel, axis=1, keepdims=True)
    carry_ref[...] = total
    cnt_ref[...] = total


def _router(hb, w_router, router_bias, *, tq=256):
    s, d = hb.shape
    row = pl.BlockSpec((TOP_K, tq), lambda i: (0, i))
    return pl.pallas_call(
        _router_kernel,
        grid=(s // tq,),
        in_specs=[pl.BlockSpec((tq, d), lambda i: (i, 0)),
                  pl.BlockSpec((N_EXPERTS, d), lambda i: (0, 0)),
                  pl.BlockSpec((N_EXPERTS, 1), lambda i: (0, 0))],
        out_specs=[row, row, row, pl.BlockSpec((N_EXPERTS, LANES), lambda i: (0, 0))],
        out_shape=[jax.ShapeDtypeStruct((TOP_K, s), jnp.int32), jax.ShapeDtypeStruct((TOP_K, s), F32),
                   jax.ShapeDtypeStruct((TOP_K, s), jnp.int32), jax.ShapeDtypeStruct((N_EXPERTS, LANES), F32)],
        scratch_shapes=[pltpu.VMEM((N_EXPERTS, LANES), F32)],
        compiler_params=_cparams(("arbitrary",)),
        name="moe_router",
    )(hb, w_router.T, router_bias.reshape(N_EXPERTS, 1))


def _expert_kernel(te_ref, nu_ref, first_ref, slot_ref, nxt_ref, x_ref, wg_hbm, wu_hbm, wd_hbm, o_ref,
                   wg_buf, wu_buf, wd_buf, sem, *, layer, pack_out):
    i = pl.program_id(0)

    def weight_copies(e, sl):
        return (pltpu.make_async_copy(wg_hbm.at[layer, e], wg_buf.at[sl], sem.at[sl]),
                pltpu.make_async_copy(wu_hbm.at[layer, e], wu_buf.at[sl], sem.at[sl]),
                pltpu.make_async_copy(wd_hbm.at[layer, e], wd_buf.at[sl], sem.at[sl]))

    @pl.when(i < nu_ref[0])
    def _compute():
        sl = slot_ref[i]

        @pl.when(i == 0)
        def _first_expert():
            for c in weight_copies(te_ref[i], sl):
                c.start()

        @pl.when(first_ref[i] == 1)
        def _expert_changed():
            for c in weight_copies(te_ref[i], sl):
                c.wait()

            @pl.when(nxt_ref[i] >= 0)
            def _prefetch_next_expert():
                for c in weight_copies(nxt_ref[i], 1 - sl):
                    c.start()

        xa, xb = _unpack_halves(x_ref[...])
        half = xa.shape[1]

        def up(w_buf):
            return (jnp.dot(xa, w_buf[sl, 0:half, :].astype(MXU), preferred_element_type=F32)
                    + jnp.dot(xb, w_buf[sl, half:2 * half, :].astype(MXU), preferred_element_type=F32))

        hg = up(wg_buf)
        hu = up(wu_buf)
        act = (_silu(hg) * hu).astype(MXU)
        y = jnp.dot(act, wd_buf[sl].astype(MXU), preferred_element_type=F32)
        o_ref[...] = _pack_halves(y) if pack_out else y

    @pl.when(i >= nu_ref[0])
    def _unused():
        o_ref[...] = jnp.zeros_like(o_ref)


def _experts(x_rows, tile_e, n_used, w_gate, w_up, w_down, layer, *, tm, pack_out):
    n_rows = x_rows.shape[0]
    n_tiles = n_rows // tm
    d, f = w_gate.shape[2:]
    d_out, out_dtype = (d // 2, jnp.int32) if pack_out else (d, F32)
    idx = jnp.arange(n_tiles, dtype=jnp.int32)
    first = jnp.concatenate([jnp.ones((1,), jnp.int32), (tile_e[1:] != tile_e[:-1]).astype(jnp.int32)])
    slot = ((jnp.cumsum(first) - 1) % 2).astype(jnp.int32)
    used = idx < n_used[0]
    run_end = jnp.sum((tile_e[None, :] <= tile_e[:, None]) & used[None, :], axis=1).astype(jnp.int32)
    nxt = jnp.where(run_end < n_used[0], tile_e[jnp.minimum(run_end, n_tiles - 1)], -1).astype(jnp.int32)
    grid_spec = pltpu.PrefetchScalarGridSpec(
        num_scalar_prefetch=5,
        grid=(n_tiles,),
        in_specs=[pl.BlockSpec((tm, d // 2), lambda i, te, nu, *_: (jnp.minimum(i, nu[0] - 1), 0)),
                  pl.BlockSpec(memory_space=pl.ANY), pl.BlockSpec(memory_space=pl.ANY),
                  pl.BlockSpec(memory_space=pl.ANY)],
        out_specs=pl.BlockSpec((tm, d_out), lambda i, *_: (i, 0)),
        scratch_shapes=[pltpu.VMEM((2, d, f), w_gate.dtype), pltpu.VMEM((2, d, f), w_up.dtype),
                        pltpu.VMEM((2, f, d), w_down.dtype), pltpu.SemaphoreType.DMA((2,))],
    )
    return pl.pallas_call(
        functools.partial(_expert_kernel, layer=layer, pack_out=pack_out),
        grid_spec=grid_spec,
        out_shape=jax.ShapeDtypeStruct((n_rows, d_out), out_dtype),
        compiler_params=_cparams(("arbitrary",), 56),
        name="moe_experts",
    )(tile_e, n_used, first, slot, nxt, x_rows, w_gate, w_up, w_down)


def _moe_ln(h, hb, hp, layer, w_router, router_bias, w_exp_gate, w_exp_up, w_exp_down, w_sh_gate, w_sh_up, w_sh_down,
            ln_g, ln_b):
    s, d = h.shape
    tm = MOE_TILE
    top_e, top_w, top_pos, counts = _router(hb, w_router, router_bias)
    cnt = counts[:, 0].astype(jnp.int32)
    tiles_e = (cnt + tm - 1) // tm
    tile_end = jnp.cumsum(tiles_e)
    row_start = (tile_end - tiles_e) * tm
    experts = jnp.arange(N_EXPERTS, dtype=jnp.int32)
    dest = top_pos + jnp.sum(jnp.where(top_e[..., None] == experts, row_start, 0), axis=-1)
    n_tiles = s * TOP_K // tm + N_EXPERTS
    tile_e = jnp.minimum(jnp.sum(tile_end[None, :] <= jnp.arange(n_tiles, dtype=jnp.int32)[:, None], axis=1),
                         N_EXPERTS - 1).astype(jnp.int32)
    n_used = tile_end[-1:].astype(jnp.int32)
    zero_tiles = jnp.concatenate([jnp.where(tiles_e > 0, tile_end - 1, -1),
                                  jnp.where(n_used + experts < n_tiles, n_used + experts, -1)]).astype(jnp.int32)
    x_rows = _dispatch(hp, dest, zero_tiles, n_tiles * tm, tm=tm)
    y_rows = _experts(x_rows, tile_e, n_used, w_exp_gate, w_exp_up, w_exp_down, layer, tm=tm, pack_out=True)
    shared = _experts(hp, jnp.zeros((s // tm,), jnp.int32), jnp.full((1,), s // tm, jnp.int32),
                      w_sh_gate[:, None], w_sh_up[:, None], w_sh_down[:, None], layer, tm=tm, pack_out=False)
    return _combine_ln(h, y_rows, dest, top_w, shared, ln_g, ln_b)


def _project(hb, wt_all, layer):
    off = np.concatenate([[0], np.cumsum(IN_SPLIT_SIZES)])
    pa = _proj_wt(hb, wt_all, layer, col0=0, ncols=off[3], tm=1024, tn=1024, out_dtype=MXU)
    ps = _proj_wt(hb, wt_all, layer, col0=off[3], ncols=off[5] - off[3], tm=1024, tn=1024, out_dtype=F32)
    pn = _proj_wt(hb, wt_all, layer, col0=off[6], ncols=off[8] - off[6], tm=1024, tn=1280, out_dtype=MXU)
    pg = _proj_wt(hb, wt_all, layer, col0=off[9], ncols=off[10] - off[9], tm=1024, tn=1024, out_dtype=F32)
    p3 = _matmul(hb, _small_proj_weights(wt_all[layer]), tm=1024, tn=P3_COLS, out_dtype=F32)
    return pa, pn, ps, pg, p3


def _mixer(hb, wt_all, layer, conv_w, conv_b, dt_bias, a_log, d_skip, ssm_norm_g,
           cmp_pos_k, cmp_w1_k, cmp_w2_k, cmp_pos_v, cmp_w1_v, cmp_w2_v, w_br_a, w_br_b, w_br_c):
    pa, pn, ps, pg, p3 = _project(hb, wt_all, layer)
    y_a = _moba(pa)
    y_b = _ssd(ps, p3, conv_w, conv_b, dt_bias, a_log, d_skip, ssm_norm_g)
    kvc = _nsa_compress(pn, cmp_pos_k, cmp_w1_k, cmp_w2_k, cmp_pos_v, cmp_w1_v, cmp_w2_v)
    y_c, selb = _nsa_cmp(pn, p3, kvc)
    y_c = _nsa_sel(pn, p3, selb, y_c)
    y_c = _nsa_win(pn, p3, y_c)
    return _merge(y_a, y_b, y_c, pg, w_br_a, w_br_b, w_br_c)


def kernel(x, w_in, conv_w, conv_b, dt_bias, a_log, d_skip, ssm_norm_g, cmp_pos_k, cmp_w1_k, cmp_w2_k, cmp_pos_v, cmp_w1_v, cmp_w2_v, w_br_a, w_br_b, w_br_c, w_out, ln1_g, ln1_b, w_router, router_bias, w_exp_gate, w_exp_up, w_exp_down, w_sh_gate, w_sh_up, w_sh_down, ln2_g, ln2_b):
    bsz, s, d = x.shape
    assert bsz == 1
    h = x.reshape(s, d)
    hb = h.astype(MXU)
    wt_all = jnp.swapaxes(w_in, 1, 2)
    for l in range(w_in.shape[0]):
        merged = _mixer(hb, wt_all, l, conv_w[l], conv_b[l], dt_bias[l], a_log[l], d_skip[l], ssm_norm_g[l],
                        cmp_pos_k[l], cmp_w1_k[l], cmp_w2_k[l], cmp_pos_v[l], cmp_w1_v[l], cmp_w2_v[l],
                        w_br_a[l], w_br_b[l], w_br_c[l])
        h, hb, hp = _wout_ln(merged, w_out[l], h, ln1_g[l], ln1_b[l])
        h, hb = _moe_ln(h, hb, hp, l, w_router[l], router_bias[l], w_exp_gate, w_exp_up, w_exp_down,
                        w_sh_gate, w_sh_up, w_sh_down, ln2_g[l], ln2_b[l])
    return h.reshape(bsz, s, d)
```

```python
import functools

import numpy as np
import jax
import jax.numpy as jnp
from jax import lax
from jax.experimental import pallas as pl
from jax.experimental.pallas import tpu as pltpu

F32 = jnp.float32
MXU = jnp.bfloat16

D_MODEL = 2048
DEPTH = 2
HEAD_DIM = 128
MOBA_HEADS = 8
MOBA_WIDTH = MOBA_HEADS * HEAD_DIM
MOBA_BLOCK = 256
MOBA_TOPK = 3
SSM_D_INNER = D_MODEL
SSM_HEAD_DIM = 64
SSM_HEADS = SSM_D_INNER // SSM_HEAD_DIM
SSM_STATE = 128
SSM_GROUPS = 8
SSM_HEADS_PER_GROUP = SSM_HEADS // SSM_GROUPS
SSM_GROUP_WIDTH = SSM_D_INNER // SSM_GROUPS
SSM_CONV = 4
SSM_CHUNK = 256
SSM_CONV_CH = SSM_D_INNER + 2 * SSM_GROUPS * SSM_STATE
NSA_HEADS = 8
NSA_KV_GROUPS = 2
NSA_HEADS_PER_GROUP = NSA_HEADS // NSA_KV_GROUPS
NSA_WIDTH = NSA_HEADS * HEAD_DIM
NSA_KV_WIDTH = NSA_KV_GROUPS * HEAD_DIM
NSA_CMP_LEN = 32
NSA_CMP_STRIDE = 16
NSA_SEL_BLOCK = 64
NSA_TOPN = 16
NSA_WINDOW = 512
N_BRANCHES = 3
N_EXPERTS = 64
N_EXPERT_GROUPS = 8
EXPERTS_PER_GROUP = N_EXPERTS // N_EXPERT_GROUPS
TOPK_GROUPS = 4
TOP_K = 8
D_EXPERT = 512
ROUTED_SCALE = 2.5
DEEPNORM_ALPHA = (2 * DEPTH) ** 0.25
LN_EPS = 1e-5
RMS_EPS = 1e-5
IN_SPLIT_SIZES = (MOBA_WIDTH, MOBA_WIDTH, MOBA_WIDTH,
                  SSM_D_INNER, SSM_CONV_CH, SSM_HEADS,
                  NSA_WIDTH, 6 * NSA_KV_WIDTH, N_BRANCHES * NSA_HEADS,
                  N_BRANCHES * D_MODEL)

LANES = 128
SUBLANES = 8
ATT_TILE = 256
MOE_TILE = 256
ATT_SCALE = HEAD_DIM ** -0.5
MASK_BIAS = -2.0 ** 30
NEG = -1e30

NT = (((1,), (1,)), ((), ()))

P1_QA, P1_KA, P1_VA = 0, 1024, 2048
P1_QN, P1_KVN = 0, 1024
P2_Z, P2_XBC = 0, 2048
P2_GM = 0
P3_COLS = (SSM_GROUPS + NSA_KV_GROUPS) * LANES


def _cparams(semantics, vmem_mb=48):
    return pltpu.CompilerParams(dimension_semantics=semantics, vmem_limit_bytes=vmem_mb * 1024 * 1024)


def _sigmoid(x):
    return 1.0 / (1.0 + jnp.exp(-x))


def _silu(x):
    return x * _sigmoid(x)


def _split3(x):
    hi = x.astype(jnp.bfloat16)
    r1 = x - hi.astype(F32)
    mid = r1.astype(jnp.bfloat16)
    lo = (r1 - mid.astype(F32)).astype(jnp.bfloat16)
    return hi, mid, lo


def _dot3(a_exact, x, left=True):
    acc = None
    for part in _split3(x):
        t = (jnp.dot(a_exact, part, preferred_element_type=F32) if left
             else jnp.dot(part, a_exact, preferred_element_type=F32))
        acc = t if acc is None else acc + t
    return acc


def _mm_kernel(a_ref, b_ref, o_ref):
    o_ref[...] = jnp.dot(a_ref[...].astype(MXU), b_ref[...].astype(MXU),
                         preferred_element_type=F32).astype(o_ref.dtype)


def _matmul(a, b, *, tm, tn, out_dtype):
    m, k = a.shape
    n = b.shape[1]
    assert m % tm == 0 and n % tn == 0
    return pl.pallas_call(
        _mm_kernel,
        grid=(n // tn, m // tm),
        in_specs=[pl.BlockSpec((tm, k), lambda j, i: (i, 0)),
                  pl.BlockSpec((k, tn), lambda j, i: (0, j))],
        out_specs=pl.BlockSpec((tm, tn), lambda j, i: (i, j)),
        out_shape=jax.ShapeDtypeStruct((m, n), out_dtype),
        compiler_params=_cparams(("arbitrary", "arbitrary"), 56),
        name="proj_matmul",
    )(a, b)


def _mm_wt_kernel(a_ref, wt_ref, o_ref, wb_ref):
    @pl.when(pl.program_id(1) == 0)
    def _cast_weights():
        wb_ref[...] = wt_ref[0].astype(MXU)

    o_ref[...] = lax.dot_general(a_ref[...].astype(MXU), wb_ref[...], NT,
                                 preferred_element_type=F32).astype(o_ref.dtype)


def _proj_wt(a, wt_all, layer, *, col0, ncols, tm, tn, out_dtype):
    m, k = a.shape
    assert m % tm == 0 and ncols % tn == 0 and col0 % SUBLANES == 0
    return pl.pallas_call(
        _mm_wt_kernel,
        grid=(ncols // tn, m // tm),
        in_specs=[pl.BlockSpec((tm, k), lambda j, i: (i, 0)),
                  pl.BlockSpec((pl.Element(1), pl.Element(tn), pl.Element(k)),
                               lambda j, i: (layer, pl.multiple_of(int(col0) + j * tn, SUBLANES), 0))],
        out_specs=pl.BlockSpec((tm, tn), lambda j, i: (i, j)),
        out_shape=jax.ShapeDtypeStruct((m, ncols), out_dtype),
        scratch_shapes=[pltpu.VMEM((tn, k), MXU)],
        compiler_params=_cparams(("arbitrary", "arbitrary"), 56),
        name="proj_matmul_wt",
    )(a, wt_all)


def _small_proj_weights(wt):
    off = np.concatenate([[0], np.cumsum(IN_SPLIT_SIZES)])
    dt = wt[off[5]:off[6]].T
    gn = wt[off[8]:off[9]].T
    d = dt.shape[0]
    dtp = jnp.pad(dt.reshape(d, SSM_GROUPS, SSM_HEADS_PER_GROUP),
                  ((0, 0), (0, 0), (0, LANES - SSM_HEADS_PER_GROUP))).reshape(d, SSM_GROUPS * LANES)
    ng = NSA_HEADS_PER_GROUP * N_BRANCHES
    gnp = jnp.pad(gn.reshape(d, NSA_KV_GROUPS, ng), ((0, 0), (0, 0), (0, LANES - ng))).reshape(d, NSA_KV_GROUPS * LANES)
    return jnp.concatenate([dtp, gnp], axis=1).astype(MXU)


def _flash_init(m_ref, acc_ref):
    m_ref[...] = jnp.full(m_ref.shape, NEG, F32)
    acc_ref[...] = jnp.zeros(acc_ref.shape, F32)


def _flash_step(s, v_aug, m_ref, acc_ref):
    hd = HEAD_DIM
    m_prev = m_ref[...]
    m_new = jnp.maximum(m_prev, jnp.max(s, axis=1, keepdims=True))
    alpha = jnp.exp(m_prev - m_new)
    p = jnp.concatenate([jnp.exp(s[:, c * hd:(c + 1) * hd] - m_new) for c in range(s.shape[1] // hd)], axis=1)
    pv = jnp.dot(p.astype(MXU), v_aug, preferred_element_type=F32)
    acc_ref[:, 0:hd] = alpha * acc_ref[:, 0:hd] + pv[:, 0:hd]
    acc_ref[:, hd:2 * hd] = alpha * acc_ref[:, hd:2 * hd] + pv[:, hd:2 * hd]
    m_ref[...] = m_new


def _flash_out(acc_ref):
    return acc_ref[:, 0:HEAD_DIM] / acc_ref[:, HEAD_DIM:2 * HEAD_DIM]


def _flash_loop(n, scores, values, sa_ref, sb_ref, m_ref, acc_ref):
    @pl.when(n > 0)
    def _first():
        sa_ref[...] = scores(0)

    def pair(u, c):
        t = 2 * u
        sb_ref[...] = scores(t + 1)
        _flash_step(sa_ref[...], values(t), m_ref, acc_ref)
        sa_ref[...] = scores(t + 2)
        _flash_step(sb_ref[...], values(t + 1), m_ref, acc_ref)
        return c

    n_pairs = jnp.maximum(n - 1, 0) // 2
    lax.fori_loop(0, n_pairs, pair, 0)
    t0 = 2 * n_pairs
    left = n - t0

    @pl.when(left == 2)
    def _last_two():
        sb_ref[...] = scores(t0 + 1)
        _flash_step(sa_ref[...], values(t0), m_ref, acc_ref)
        _flash_step(sb_ref[...], values(t0 + 1), m_ref, acc_ref)

    @pl.when(left == 1)
    def _last_one():
        _flash_step(sa_ref[...], values(t0), m_ref, acc_ref)


MOBA_KV_GROUP = 4


def _moba_kernel(q_ref, k_ref, v_ref, o_ref, kaug_ref, vaug_ref, kmean_ref, m_ref, acc_ref, sa_ref, sb_ref, *, nb):
    blk = MOBA_BLOCK
    grp = MOBA_KV_GROUP * blk
    qi = pl.program_id(1)

    @pl.when(qi == 0)
    def _build_keys():
        kmean_ref[...] = jnp.zeros_like(kmean_ref)
        lane = lax.broadcasted_iota(jnp.int32, (blk, LANES), 1)
        ones = jnp.ones((blk, HEAD_DIM), vaug_ref.dtype)

        def body(j, c):
            r0 = pl.multiple_of(j * blk, blk)
            kb = k_ref[pl.ds(r0, blk), :]
            kaug_ref[pl.ds(r0, blk), 0:HEAD_DIM] = kb.astype(kaug_ref.dtype)
            kaug_ref[pl.ds(r0, blk), HEAD_DIM:2 * HEAD_DIM] = (lane == j).astype(kaug_ref.dtype)
            vaug_ref[pl.ds(r0, blk), 0:HEAD_DIM] = v_ref[pl.ds(r0, blk), :].astype(vaug_ref.dtype)
            vaug_ref[pl.ds(r0, blk), HEAD_DIM:2 * HEAD_DIM] = ones
            kmean_ref[pl.ds(j, 1), :] = jnp.sum(kb.astype(F32), axis=0, keepdims=True) * (1.0 / blk)
            return c

        lax.fori_loop(0, nb, body, 0)

    q = q_ref[...].astype(MXU)
    gate = lax.dot_general(kmean_ref[...].astype(MXU), q, NT, preferred_element_type=F32)
    kblk = lax.broadcasted_iota(jnp.int32, (LANES, blk), 0).astype(F32)
    past = kblk < qi.astype(F32)
    gate = jnp.where(past, gate, -jnp.inf)
    sel = jnp.zeros((LANES, blk), F32)
    for _ in range(MOBA_TOPK):
        mx = jnp.max(gate, axis=0, keepdims=True)
        idx = jnp.min(jnp.where((gate == mx) & past, kblk, float(LANES)), axis=0, keepdims=True)
        pick = kblk == idx
        sel = jnp.where(pick, 1.0, sel)
        gate = jnp.where(pick, -jnp.inf, gate)
    bias = jnp.where(sel.T > 0, 0.0, MASK_BIAS).astype(MXU)
    qaug = jnp.concatenate([q, bias], axis=1)

    _flash_init(m_ref, acc_ref)

    def scores(g):
        r0 = pl.multiple_of(g * grp, grp)
        return lax.dot_general(qaug, kaug_ref[pl.ds(r0, grp), :], NT, preferred_element_type=F32) * ATT_SCALE

    def values(g):
        return vaug_ref[pl.ds(pl.multiple_of(g * grp, grp), grp), :]

    _flash_loop((qi + MOBA_KV_GROUP - 1) // MOBA_KV_GROUP, scores, values, sa_ref, sb_ref, m_ref, acc_ref)
    r0 = pl.multiple_of(qi * blk, blk)
    s = lax.dot_general(q, k_ref[pl.ds(r0, blk), :].astype(MXU), NT, preferred_element_type=F32) * ATT_SCALE
    row = lax.broadcasted_iota(jnp.int32, (blk, blk), 0)
    col = lax.broadcasted_iota(jnp.int32, (blk, blk), 1)
    _flash_step(jnp.where(col <= row, s, NEG), vaug_ref[pl.ds(r0, blk), :], m_ref, acc_ref)
    o_ref[...] = _flash_out(acc_ref).astype(o_ref.dtype)


def _moba(p1):
    s = p1.shape[0]
    nb = s // MOBA_BLOCK
    assert nb <= LANES and nb % MOBA_KV_GROUP == 0
    kcol, vcol = P1_KA // HEAD_DIM, P1_VA // HEAD_DIM
    return pl.pallas_call(
        functools.partial(_moba_kernel, nb=nb),
        grid=(MOBA_HEADS, nb),
        in_specs=[pl.BlockSpec((MOBA_BLOCK, HEAD_DIM), lambda h, i: (i, h)),
                  pl.BlockSpec((s, HEAD_DIM), lambda h, i: (0, kcol + h)),
                  pl.BlockSpec((s, HEAD_DIM), lambda h, i: (0, vcol + h))],
        out_specs=pl.BlockSpec((MOBA_BLOCK, HEAD_DIM), lambda h, i: (i, h)),
        out_shape=jax.ShapeDtypeStruct((s, MOBA_WIDTH), MXU),
        scratch_shapes=[pltpu.VMEM((s, 2 * HEAD_DIM), MXU), pltpu.VMEM((s, 2 * HEAD_DIM), MXU),
                        pltpu.VMEM((LANES, HEAD_DIM), F32),
                        pltpu.VMEM((MOBA_BLOCK, HEAD_DIM), F32), pltpu.VMEM((MOBA_BLOCK, 2 * HEAD_DIM), F32),
                        pltpu.VMEM((MOBA_BLOCK, MOBA_KV_GROUP * MOBA_BLOCK), F32),
                        pltpu.VMEM((MOBA_BLOCK, MOBA_KV_GROUP * MOBA_BLOCK), F32)],
        compiler_params=_cparams(("arbitrary", "arbitrary")),
        name="moba_attention",
    )(p1, p1, p1)


def _ssd_kernel(dtb_ref, alog_ref, dskip_ref,
                xs_ref, bm_ref, cm_ref, z_ref, dt_ref, cwx_ref, cwb_ref, cwc_ref, cbx_ref, cbb_ref, cbc_ref,
                ng_ref, o_ref, xbuf, bbuf, cbuf, state_ref, ybuf):
    L = SSM_CHUNK
    W = SSM_GROUP_WIDTH
    J = SSM_HEADS_PER_GROUP
    P = SSM_HEAD_DIM
    g = pl.program_id(0)
    c = pl.program_id(1)

    @pl.when(c == 0)
    def _reset():
        xbuf[0:SUBLANES, :] = jnp.zeros((SUBLANES, W), F32)
        bbuf[0:SUBLANES, :] = jnp.zeros((SUBLANES, SSM_STATE), F32)
        cbuf[0:SUBLANES, :] = jnp.zeros((SUBLANES, SSM_STATE), F32)
        state_ref[...] = jnp.zeros_like(state_ref)

    def conv_silu(buf, raw_ref, w_ref, b_ref):
        buf[SUBLANES:SUBLANES + L, :] = raw_ref[...]
        acc = b_ref[...]
        for i in range(SSM_CONV):
            lo = SUBLANES - (SSM_CONV - 1) + i
            acc = acc + w_ref[i:i + 1, :] * buf[lo:lo + L, :]
        buf[0:SUBLANES, :] = buf[L:L + SUBLANES, :]
        return _silu(acc)

    xs = conv_silu(xbuf, xs_ref, cwx_ref, cbx_ref)
    bm = conv_silu(bbuf, bm_ref, cwb_ref, cbb_ref)
    cm = conv_silu(cbuf, cm_ref, cwc_ref, cbc_ref)

    lane = lax.broadcasted_iota(jnp.int32, (1, LANES), 1)
    dtb = jnp.zeros((1, LANES), F32)
    alog = jnp.full((1, LANES), -jnp.inf, F32)
    for j in range(J):
        dtb = jnp.where(lane == j, dtb_ref[g * J + j], dtb)
        alog = jnp.where(lane == j, alog_ref[g * J + j], alog)
    x = dt_ref[...] + dtb
    dt = jnp.maximum(x, 0.0) + jnp.log(1.0 + jnp.exp(-jnp.abs(x)))
    dt = jnp.where(lane < J, dt, 0.0)
    a = dt * (-jnp.exp(alog))

    er = lax.broadcasted_iota(jnp.int32, (LANES, W), 0)
    ec = lax.broadcasted_iota(jnp.int32, (LANES, W), 1)
    expand = ((ec >> 6) == er).astype(jnp.bfloat16)
    tr = lax.broadcasted_iota(jnp.int32, (L, L), 0)
    tc = lax.broadcasted_iota(jnp.int32, (L, L), 1)
    tril = tr >= tc
    tril_b = tril.astype(jnp.bfloat16)
    dt_e = _dot3(expand, dt, left=False)
    a_e = _dot3(expand, a, left=False)
    acs = _dot3(tril_b, a_e, left=True)
    acs_t = acs.T
    a_last = acs[L - 1:L, :]

    xdt = xs * dt_e
    cb = lax.dot_general(cm.astype(MXU), bm.astype(MXU), NT, preferred_element_type=F32)
    for j in range(J):
        colv = acs[:, j * P:j * P + 1]
        rowv = acs_t[j * P:j * P + 1, :]
        dec = jnp.exp(jnp.where(tril, colv - rowv, -jnp.inf))
        ybuf[:, j * P:(j + 1) * P] = jnp.dot((cb * dec).astype(MXU), xdt[:, j * P:(j + 1) * P].astype(MXU),
                                             preferred_element_type=F32)
    st_old = state_ref[...]
    xdte = (xdt * jnp.exp(a_last - acs)).astype(MXU)
    st_new = jnp.dot(bm.T.astype(MXU), xdte, preferred_element_type=F32)
    y_off = jnp.dot(cm.astype(MXU), st_old.astype(MXU), preferred_element_type=F32) * jnp.exp(acs)
    state_ref[...] = st_old * jnp.exp(a_last) + st_new

    lane_w = lax.broadcasted_iota(jnp.int32, (1, W), 1)
    dsk = jnp.zeros((1, W), F32)
    for j in range(J):
        dsk = jnp.where((lane_w >> 6) == j, dskip_ref[g * J + j], dsk)
    y = ybuf[...] + y_off + xs * dsk
    y = y * _silu(z_ref[...])
    y = y * lax.rsqrt(jnp.mean(y * y, axis=1, keepdims=True) + RMS_EPS)
    o_ref[...] = (y * ng_ref[...]).astype(o_ref.dtype)


def _ssd(p2, p3, conv_w, conv_b, dt_bias, a_log, d_skip, norm_g):
    s = p2.shape[0]
    L, W, N = SSM_CHUNK, SSM_GROUP_WIDTH, SSM_STATE
    xs0 = P2_XBC // W
    bm0 = (P2_XBC + SSM_D_INNER) // N
    cm0 = bm0 + SSM_GROUPS
    z0 = P2_Z // W
    cb2 = conv_b.reshape(1, SSM_CONV_CH)
    ng2 = norm_g.reshape(1, SSM_D_INNER)
    grid_spec = pltpu.PrefetchScalarGridSpec(
        num_scalar_prefetch=3,
        grid=(SSM_GROUPS, s // L),
        in_specs=[
            pl.BlockSpec((L, W), lambda g, c, *_: (c, xs0 + g)),
            pl.BlockSpec((L, N), lambda g, c, *_: (c, bm0 + g)),
            pl.BlockSpec((L, N), lambda g, c, *_: (c, cm0 + g)),
            pl.BlockSpec((L, W), lambda g, c, *_: (c, z0 + g)),
            pl.BlockSpec((L, LANES), lambda g, c, *_: (c, g)),
            pl.BlockSpec((SSM_CONV, W), lambda g, c, *_: (0, g)),
            pl.BlockSpec((SSM_CONV, N), lambda g, c, *_: (0, SSM_D_INNER // N + g)),
            pl.BlockSpec((SSM_CONV, N), lambda g, c, *_: (0, SSM_D_INNER // N + SSM_GROUPS + g)),
            pl.BlockSpec((1, W), lambda g, c, *_: (0, g)),
            pl.BlockSpec((1, N), lambda g, c, *_: (0, SSM_D_INNER // N + g)),
            pl.BlockSpec((1, N), lambda g, c, *_: (0, SSM_D_INNER // N + SSM_GROUPS + g)),
            pl.BlockSpec((1, W), lambda g, c, *_: (0, g)),
        ],
        out_specs=pl.BlockSpec((L, W), lambda g, c, *_: (c, g)),
        scratch_shapes=[pltpu.VMEM((SUBLANES + L, W), F32), pltpu.VMEM((SUBLANES + L, N), F32),
                        pltpu.VMEM((SUBLANES + L, N), F32), pltpu.VMEM((N, W), F32), pltpu.VMEM((L, W), F32)],
    )
    return pl.pallas_call(
        _ssd_kernel,
        grid_spec=grid_spec,
        out_shape=jax.ShapeDtypeStruct((s, SSM_D_INNER), MXU),
        compiler_params=_cparams(("arbitrary", "arbitrary")),
        name="ssd_mixer",
    )(dt_bias, a_log, d_skip, p2, p2, p2, p2, p3, conv_w, conv_w, conv_w, cb2, cb2, cb2, ng2)


def _nsa_compress_kernel(t_ref, pos_ref, w1_ref, w2_ref, o_ref):
    half = NSA_CMP_STRIDE * HEAD_DIM
    t = t_ref[0].astype(F32)
    pos = pos_ref[0]
    lo = (t + pos[:, :half]).astype(MXU)
    hi = (t + pos[:, half:]).astype(MXU)
    w1 = w1_ref[0]
    a = jnp.dot(lo, w1[:half].astype(MXU), preferred_element_type=F32)
    b = jnp.dot(hi, w1[half:].astype(MXU), preferred_element_type=F32)
    n = t.shape[0]
    pre = a + pltpu.roll(b, n - 1, 0)
    act = jax.nn.gelu(pre, approximate=True)
    o_ref[0] = jnp.dot(act.astype(MXU), w2_ref[0].astype(MXU), preferred_element_type=F32).astype(o_ref.dtype)


def _nsa_compress(p1, pos_k, w1_k, w2_k, pos_v, w1_v, w2_v):
    s = p1.shape[0]
    n_str = s // NSA_CMP_STRIDE
    kv = p1[:, P1_KVN:P1_KVN + 2 * NSA_KV_WIDTH]
    t4 = kv.reshape(s, 4, HEAD_DIM).transpose(1, 0, 2).reshape(4, n_str, NSA_CMP_STRIDE * HEAD_DIM)
    pos = jnp.stack([pos_k, pos_v]).reshape(2, 1, NSA_CMP_LEN * HEAD_DIM)
    w1 = jnp.stack([w1_k, w1_v])
    w2 = jnp.stack([w2_k, w2_v])
    return pl.pallas_call(
        _nsa_compress_kernel,
        grid=(4,),
        in_specs=[pl.BlockSpec((1, n_str, NSA_CMP_STRIDE * HEAD_DIM), lambda i: (i, 0, 0)),
                  pl.BlockSpec((1, 1, NSA_CMP_LEN * HEAD_DIM), lambda i: (i // 2, 0, 0)),
                  pl.BlockSpec((1, NSA_CMP_LEN * HEAD_DIM, HEAD_DIM), lambda i: (i // 2, 0, 0)),
                  pl.BlockSpec((1, HEAD_DIM, HEAD_DIM), lambda i: (i // 2, 0, 0))],
        out_specs=pl.BlockSpec((1, n_str, HEAD_DIM), lambda i: (i, 0, 0)),
        out_shape=jax.ShapeDtypeStruct((4, n_str, HEAD_DIM), MXU),
        compiler_params=_cparams(("arbitrary",)),
        name="nsa_compress",
    )(t4, pos, w1, w2)


def _nsa_cmp_kernel(q_ref, kc_ref, vc_ref, gn_ref, o_ref, selb_ref, *, n_sel):
    tq = ATT_TILE
    qi = pl.program_id(1)
    kc = kc_ref[0].astype(MXU)
    vc = vc_ref[0].astype(MXU)
    nc = kc.shape[0]
    pos = qi * tq + lax.broadcasted_iota(jnp.int32, (tq, nc), 0)
    cidx = lax.broadcasted_iota(jnp.int32, (tq, nc), 1)
    valid = (cidx * NSA_CMP_STRIDE + NSA_CMP_LEN - 1 <= pos) & (cidx < nc - 1)
    c_start = lax.broadcasted_iota(jnp.int32, (LANES, nc), 1) * NSA_CMP_STRIDE
    s_start = lax.broadcasted_iota(jnp.int32, (LANES, nc), 0) * NSA_SEL_BLOCK
    overlap_t = ((c_start < s_start + NSA_SEL_BLOCK) & (c_start + NSA_CMP_LEN > s_start)).astype(MXU)
    gates = _sigmoid(gn_ref[...])
    imp = jnp.zeros((LANES, tq), F32)
    for j in range(NSA_HEADS_PER_GROUP):
        q = q_ref[:, j * HEAD_DIM:(j + 1) * HEAD_DIM].astype(MXU)
        s = lax.dot_general(q, kc, NT, preferred_element_type=F32) * ATT_SCALE
        s = jnp.where(valid, s, -jnp.inf)
        m = jnp.max(s, axis=1, keepdims=True)
        m = jnp.where(m > -jnp.inf, m, 0.0)
        e = jnp.exp(s - m)
        den = jnp.sum(e, axis=1, keepdims=True)
        p = (e / jnp.where(den > 0, den, 1.0)).astype(MXU)
        o = jnp.dot(p, vc, preferred_element_type=F32)
        imp = imp + lax.dot_general(overlap_t, p, NT, preferred_element_type=F32)
        o_ref[:, j * HEAD_DIM:(j + 1) * HEAD_DIM] = o * gates[:, N_BRANCHES * j:N_BRANCHES * j + 1]

    blk = lax.broadcasted_iota(jnp.int32, (LANES, tq), 0).astype(F32)
    cur = ((qi * tq + lax.broadcasted_iota(jnp.int32, (LANES, tq), 1)) >> 6).astype(F32)
    allowed = blk <= cur
    forced = (blk == 0.0) | (blk == cur) | (blk == cur - 1.0)
    val = jnp.where(forced, jnp.inf, jnp.where(allowed, imp, -jnp.inf))
    val = jnp.where(blk < float(n_sel), val, -jnp.inf)

    def pick_round(_, c):
        val, sel = c
        mx = jnp.max(val, axis=0, keepdims=True)
        idx = jnp.min(jnp.where(val == mx, blk, float(LANES)), axis=0, keepdims=True)
        pick = blk == idx
        sel = jnp.where(pick & allowed, 1.0, sel)
        val = jnp.where(pick, -jnp.inf, val)
        return val, sel

    _, sel = lax.fori_loop(0, min(NSA_TOPN, n_sel), pick_round, (val, jnp.zeros((LANES, tq), F32)))
    selb_ref[0] = jnp.where(sel.T > 0, 0.0, MASK_BIAS).astype(selb_ref.dtype)


def _nsa_cmp(p1, p3, kvc):
    s = p1.shape[0]
    n_str = s // NSA_CMP_STRIDE
    n_sel = s // NSA_SEL_BLOCK
    assert n_sel <= LANES
    gw = NSA_HEADS_PER_GROUP * HEAD_DIM
    q0 = P1_QN // gw
    return pl.pallas_call(
        functools.partial(_nsa_cmp_kernel, n_sel=n_sel),
        grid=(NSA_KV_GROUPS, s // ATT_TILE),
        in_specs=[pl.BlockSpec((ATT_TILE, gw), lambda g, i: (i, q0 + g)),
                  pl.BlockSpec((1, n_str, HEAD_DIM), lambda g, i: (g, 0, 0)),
                  pl.BlockSpec((1, n_str, HEAD_DIM), lambda g, i: (NSA_KV_GROUPS + g, 0, 0)),
                  pl.BlockSpec((ATT_TILE, LANES), lambda g, i: (i, SSM_GROUPS + g))],
        out_specs=[pl.BlockSpec((ATT_TILE, gw), lambda g, i: (i, g)),
                   pl.BlockSpec((1, ATT_TILE, LANES), lambda g, i: (g, i, 0))],
        out_shape=[jax.ShapeDtypeStruct((s, NSA_WIDTH), F32),
                   jax.ShapeDtypeStruct((NSA_KV_GROUPS, s, LANES), MXU)],
        compiler_params=_cparams(("arbitrary", "arbitrary")),
        name="nsa_compressed_attention",
    )(p1, kvc, kvc, p3)


def _nsa_sel_kernel(q_ref, selb_ref, k_ref, v_ref, gn_ref, prev_ref, o_ref, kaug_ref, vaug_ref, qaug_ref,
                    m_ref, acc_ref, sa_ref, sb_ref, *, n_tiles):
    tq = ATT_TILE
    J = NSA_HEADS_PER_GROUP
    qi = pl.program_id(1)

    @pl.when(qi == 0)
    def _build_keys():
        lane = lax.broadcasted_iota(jnp.int32, (tq, LANES), 1)
        rowi = lax.broadcasted_iota(jnp.int32, (tq, LANES), 0)
        ones = jnp.ones((tq, HEAD_DIM), vaug_ref.dtype)

        def body(t, c):
            r0 = pl.multiple_of(t * tq, tq)
            kaug_ref[pl.ds(r0, tq), 0:HEAD_DIM] = k_ref[pl.ds(r0, tq), :].astype(kaug_ref.dtype)
            kaug_ref[pl.ds(r0, tq), HEAD_DIM:2 * HEAD_DIM] = (
                lane == ((t * tq + rowi) >> 6)).astype(kaug_ref.dtype)
            vaug_ref[pl.ds(r0, tq), 0:HEAD_DIM] = v_ref[pl.ds(r0, tq), :].astype(vaug_ref.dtype)
            vaug_ref[pl.ds(r0, tq), HEAD_DIM:2 * HEAD_DIM] = ones
            return c

        lax.fori_loop(0, n_tiles, body, 0)

    selb = selb_ref[0]
    for j in range(J):
        qaug_ref[j * tq:(j + 1) * tq, 0:HEAD_DIM] = q_ref[:, j * HEAD_DIM:(j + 1) * HEAD_DIM].astype(qaug_ref.dtype)
        qaug_ref[j * tq:(j + 1) * tq, HEAD_DIM:2 * HEAD_DIM] = selb
    _flash_init(m_ref, acc_ref)

    def scores(t):
        r0 = pl.multiple_of(t * tq, tq)
        return lax.dot_general(qaug_ref[...], kaug_ref[pl.ds(r0, tq), :], NT,
                               preferred_element_type=F32) * ATT_SCALE

    def values(t):
        return vaug_ref[pl.ds(pl.multiple_of(t * tq, tq), tq), :]

    _flash_loop(qi, scores, values, sa_ref, sb_ref, m_ref, acc_ref)
    row = lax.broadcasted_iota(jnp.int32, (J * tq, tq), 0) & (tq - 1)
    col = lax.broadcasted_iota(jnp.int32, (J * tq, tq), 1)
    _flash_step(jnp.where(col <= row, scores(qi), NEG), values(qi), m_ref, acc_ref)
    o = _flash_out(acc_ref)
    gates = _sigmoid(gn_ref[...])
    for j in range(J):
        o_ref[:, j * HEAD_DIM:(j + 1) * HEAD_DIM] = (
            prev_ref[:, j * HEAD_DIM:(j + 1) * HEAD_DIM]
            + o[j * tq:(j + 1) * tq] * gates[:, N_BRANCHES * j + 1:N_BRANCHES * j + 2])


def _nsa_sel(p1, p3, selb, prev):
    s = p1.shape[0]
    gw = NSA_HEADS_PER_GROUP * HEAD_DIM
    q0 = P1_QN // gw
    k0 = (P1_KVN + 2 * NSA_KV_WIDTH) // HEAD_DIM
    v0 = k0 + NSA_KV_GROUPS
    return pl.pallas_call(
        functools.partial(_nsa_sel_kernel, n_tiles=s // ATT_TILE),
        grid=(NSA_KV_GROUPS, s // ATT_TILE),
        in_specs=[pl.BlockSpec((ATT_TILE, gw), lambda g, i: (i, q0 + g)),
                  pl.BlockSpec((1, ATT_TILE, LANES), lambda g, i: (g, i, 0)),
                  pl.BlockSpec((s, HEAD_DIM), lambda g, i: (0, k0 + g)),
                  pl.BlockSpec((s, HEAD_DIM), lambda g, i: (0, v0 + g)),
                  pl.BlockSpec((ATT_TILE, LANES), lambda g, i: (i, SSM_GROUPS + g)),
                  pl.BlockSpec((ATT_TILE, gw), lambda g, i: (i, g))],
        out_specs=pl.BlockSpec((ATT_TILE, gw), lambda g, i: (i, g)),
        out_shape=jax.ShapeDtypeStruct((s, NSA_WIDTH), F32),
        scratch_shapes=[pltpu.VMEM((s, 2 * HEAD_DIM), MXU), pltpu.VMEM((s, 2 * HEAD_DIM), MXU),
                        pltpu.VMEM((NSA_HEADS_PER_GROUP * ATT_TILE, 2 * HEAD_DIM), MXU),
                        pltpu.VMEM((NSA_HEADS_PER_GROUP * ATT_TILE, HEAD_DIM), F32),
                        pltpu.VMEM((NSA_HEADS_PER_GROUP * ATT_TILE, 2 * HEAD_DIM), F32),
                        pltpu.VMEM((NSA_HEADS_PER_GROUP * ATT_TILE, ATT_TILE), F32),
                        pltpu.VMEM((NSA_HEADS_PER_GROUP * ATT_TILE, ATT_TILE), F32)],
        compiler_params=_cparams(("arbitrary", "arbitrary")),
        name="nsa_selected_attention",
    )(p1, selb, p1, p1, p3, prev)


def _nsa_win_kernel(q_ref, k_ref, v_ref, gn_ref, prev_ref, o_ref, qst_ref, m_ref, acc_ref, sa_ref, sb_ref):
    tq = ATT_TILE
    J = NSA_HEADS_PER_GROUP
    halo = NSA_WINDOW // tq
    qi = pl.program_id(1)
    for j in range(J):
        qst_ref[j * tq:(j + 1) * tq, :] = q_ref[:, j * HEAD_DIM:(j + 1) * HEAD_DIM].astype(qst_ref.dtype)
    row = lax.broadcasted_iota(jnp.int32, (J * tq, tq), 0) & (tq - 1)
    col = lax.broadcasted_iota(jnp.int32, (J * tq, tq), 1)
    ones = jnp.ones((tq, HEAD_DIM), MXU)
    _flash_init(m_ref, acc_ref)
    first = jnp.maximum(qi - halo, 0)

    def scores(t):
        kt = first + t
        r0 = pl.multiple_of(kt * tq, tq)
        s = lax.dot_general(qst_ref[...], k_ref[pl.ds(r0, tq), :].astype(MXU), NT,
                            preferred_element_type=F32) * ATT_SCALE
        diff = (qi - kt) * tq + row - col
        return jnp.where((diff >= 0) & (diff < NSA_WINDOW), s, NEG)

    def values(t):
        r0 = pl.multiple_of((first + t) * tq, tq)
        return jnp.concatenate([v_ref[pl.ds(r0, tq), :].astype(MXU), ones], axis=1)

    _flash_loop(qi + 1 - first, scores, values, sa_ref, sb_ref, m_ref, acc_ref)
    o = _flash_out(acc_ref)
    gates = _sigmoid(gn_ref[...])
    for j in range(J):
        o_ref[:, j * HEAD_DIM:(j + 1) * HEAD_DIM] = (
            prev_ref[:, j * HEAD_DIM:(j + 1) * HEAD_DIM]
            + o[j * tq:(j + 1) * tq] * gates[:, N_BRANCHES * j + 2:N_BRANCHES * j + 3])


def _nsa_win(p1, p3, prev):
    s = p1.shape[0]
    gw = NSA_HEADS_PER_GROUP * HEAD_DIM
    q0 = P1_QN // gw
    k0 = (P1_KVN + 4 * NSA_KV_WIDTH) // HEAD_DIM
    v0 = k0 + NSA_KV_GROUPS
    return pl.pallas_call(
        _nsa_win_kernel,
        grid=(NSA_KV_GROUPS, s // ATT_TILE),
        in_specs=[pl.BlockSpec((ATT_TILE, gw), lambda g, i: (i, q0 + g)),
                  pl.BlockSpec((s, HEAD_DIM), lambda g, i: (0, k0 + g)),
                  pl.BlockSpec((s, HEAD_DIM), lambda g, i: (0, v0 + g)),
                  pl.BlockSpec((ATT_TILE, LANES), lambda g, i: (i, SSM_GROUPS + g)),
                  pl.BlockSpec((ATT_TILE, gw), lambda g, i: (i, g))],
        out_specs=pl.BlockSpec((ATT_TILE, gw), lambda g, i: (i, g)),
        out_shape=jax.ShapeDtypeStruct((s, NSA_WIDTH), F32),
        scratch_shapes=[pltpu.VMEM((NSA_HEADS_PER_GROUP * ATT_TILE, HEAD_DIM), MXU),
                        pltpu.VMEM((NSA_HEADS_PER_GROUP * ATT_TILE, HEAD_DIM), F32),
                        pltpu.VMEM((NSA_HEADS_PER_GROUP * ATT_TILE, 2 * HEAD_DIM), F32),
                        pltpu.VMEM((NSA_HEADS_PER_GROUP * ATT_TILE, ATT_TILE), F32),
                        pltpu.VMEM((NSA_HEADS_PER_GROUP * ATT_TILE, ATT_TILE), F32)],
        compiler_params=_cparams(("arbitrary", "arbitrary")),
        name="nsa_window_attention",
    )(p1, p1, p1, p3, prev)


def _merge_kernel(ya_ref, yb_ref, yc_ref, g0_ref, g1_ref, g2_ref, wa_ref, wb_ref, wc_ref, o_ref):
    def branch(y_ref, w_ref, g_ref):
        prod = jnp.dot(y_ref[...].astype(MXU), w_ref[...].astype(MXU), preferred_element_type=F32)
        return _sigmoid(g_ref[...]) * prod

    o_ref[...] = (branch(ya_ref, wa_ref, g0_ref) + branch(yb_ref, wb_ref, g1_ref)
                  + branch(yc_ref, wc_ref, g2_ref)).astype(o_ref.dtype)


def _merge(ya, yb, yc, p2, wa, wb, wc, *, tm=512, tn=512):
    s = ya.shape[0]
    d = wa.shape[1]
    g0 = P2_GM // tn
    gstep = d // tn
    return pl.pallas_call(
        _merge_kernel,
        grid=(d // tn, s // tm),
        in_specs=[pl.BlockSpec((tm, ya.shape[1]), lambda j, i: (i, 0)),
                  pl.BlockSpec((tm, yb.shape[1]), lambda j, i: (i, 0)),
                  pl.BlockSpec((tm, yc.shape[1]), lambda j, i: (i, 0)),
                  pl.BlockSpec((tm, tn), lambda j, i: (i, g0 + j)),
                  pl.BlockSpec((tm, tn), lambda j, i: (i, g0 + gstep + j)),
                  pl.BlockSpec((tm, tn), lambda j, i: (i, g0 + 2 * gstep + j)),
                  pl.BlockSpec((wa.shape[0], tn), lambda j, i: (0, j)),
                  pl.BlockSpec((wb.shape[0], tn), lambda j, i: (0, j)),
                  pl.BlockSpec((wc.shape[0], tn), lambda j, i: (0, j))],
        out_specs=pl.BlockSpec((tm, tn), lambda j, i: (i, j)),
        out_shape=jax.ShapeDtypeStruct((s, d), MXU),
        compiler_params=_cparams(("arbitrary", "arbitrary")),
        name="branch_merge",
    )(ya, yb, yc, p2, p2, p2, wa, wb, wc)


def _layer_norm_rows(x, g, b):
    xc = x - jnp.mean(x, axis=1, keepdims=True)
    var = jnp.mean(xc * xc, axis=1, keepdims=True)
    return xc * lax.rsqrt(var + LN_EPS) * g + b


def _pack_halves(y):
    half = y.shape[1] // 2
    bits = lax.bitcast_convert_type(y.astype(jnp.bfloat16).astype(F32), jnp.int32)
    return ((bits[:, :half] >> 16) & jnp.int32(0xFFFF)) | (bits[:, half:] & jnp.int32(-65536))


def _unpack_halves_f32(w):
    return lax.bitcast_convert_type(w << 16, F32), lax.bitcast_convert_type(w & jnp.int32(-65536), F32)


def _unpack_halves(w):
    lo, hi = _unpack_halves_f32(w)
    return lo.astype(MXU), hi.astype(MXU)


def _wout_ln_kernel(m_ref, w_ref, h_ref, g_ref, b_ref, o_ref, ob_ref, op_ref):
    acc = jnp.dot(m_ref[...].astype(MXU), w_ref[...].astype(MXU), preferred_element_type=F32)
    y = _layer_norm_rows(DEEPNORM_ALPHA * h_ref[...] + acc, g_ref[...], b_ref[...])
    o_ref[...] = y
    ob_ref[...] = y.astype(ob_ref.dtype)
    op_ref[...] = _pack_halves(y)


def _wout_ln(merged, w_out, h, ln_g, ln_b, *, tm=256):
    s, d = h.shape
    row = pl.BlockSpec((tm, d), lambda i: (i, 0))
    vec = pl.BlockSpec((1, d), lambda i: (0, 0))
    return pl.pallas_call(
        _wout_ln_kernel,
        grid=(s // tm,),
        in_specs=[row, pl.BlockSpec((d, d), lambda i: (0, 0)), row, vec, vec],
        out_specs=[row, row, pl.BlockSpec((tm, d // 2), lambda i: (i, 0))],
        out_shape=[jax.ShapeDtypeStruct((s, d), F32), jax.ShapeDtypeStruct((s, d), MXU),
                   jax.ShapeDtypeStruct((s, d // 2), jnp.int32)],
        compiler_params=_cparams(("arbitrary",)),
        name="out_proj_layernorm",
    )(merged, w_out.astype(MXU), h, ln_g.reshape(1, d), ln_b.reshape(1, d))


def _dispatch_kernel(ztile_ref, dest_ref, hp_ref, o_ref, zbuf, sem, zsem):
    tt = dest_ref.shape[1]
    tm = zbuf.shape[0]

    @pl.when(pl.program_id(0) == 0)
    def _zero_partial_tiles():
        zbuf[...] = jnp.zeros_like(zbuf)

        def tile_copy(t):
            return pltpu.make_async_copy(zbuf, o_ref.at[pl.ds(t * tm, tm)], zsem)

        def start(j, c):
            @pl.when(ztile_ref[j] >= 0)
            def _():
                tile_copy(ztile_ref[j]).start()
            return c

        def wait(j, c):
            @pl.when(ztile_ref[j] >= 0)
            def _():
                tile_copy(ztile_ref[j]).wait()
            return c

        lax.fori_loop(0, ztile_ref.shape[0], start, 0)
        lax.fori_loop(0, ztile_ref.shape[0], wait, 0)

    def issue(g, c):
        base = pl.multiple_of(g * SUBLANES, SUBLANES)
        for rr in range(SUBLANES):
            for k in range(TOP_K):
                pltpu.make_async_copy(hp_ref.at[pl.ds(base + rr, 1)], o_ref.at[pl.ds(dest_ref[k, base + rr], 1)],
                                      sem).start()
        return c

    lax.fori_loop(0, tt // SUBLANES, issue, 0)
    rows = o_ref.at[pl.ds(0, TOP_K * tt)]
    pltpu.make_async_copy(rows, rows, sem).wait()


def _dispatch(hp, dest, zero_tiles, n_rows, *, tm, tt=256):
    s, half = hp.shape
    grid_spec = pltpu.PrefetchScalarGridSpec(
        num_scalar_prefetch=1,
        grid=(s // tt,),
        in_specs=[pl.BlockSpec((TOP_K, tt), lambda i, zt: (0, i), memory_space=pltpu.SMEM),
                  pl.BlockSpec((tt, half), lambda i, zt: (i, 0))],
        out_specs=pl.BlockSpec(memory_space=pl.ANY),
        scratch_shapes=[pltpu.VMEM((tm, half), jnp.int32), pltpu.SemaphoreType.DMA(()),
                        pltpu.SemaphoreType.DMA(())],
    )
    return pl.pallas_call(
        _dispatch_kernel,
        grid_spec=grid_spec,
        out_shape=jax.ShapeDtypeStruct((n_rows, half), jnp.int32),
        compiler_params=_cparams(("arbitrary",)),
        name="moe_dispatch",
    )(zero_tiles, dest, hp)


def _combine_kernel(dcur_ref, dnxt_ref, w_ref, h_ref, sh_ref, g_ref, b_ref, y_ref, o_ref, ob_ref, buf, sem):
    tt = dcur_ref.shape[1]
    i = pl.program_id(0)
    n = pl.num_programs(0)
    slot = i % 2

    def issue(d_ref, sl):
        def body(g, c):
            base = pl.multiple_of(g * SUBLANES, SUBLANES)
            for rr in range(SUBLANES):
                for k in range(TOP_K):
                    pltpu.make_async_copy(y_ref.at[pl.ds(d_ref[k, base + rr], 1)],
                                          buf.at[sl, pl.ds(k * tt + base + rr, 1)], sem.at[sl]).start()
            return c

        lax.fori_loop(0, tt // SUBLANES, body, 0)

    @pl.when(i == 0)
    def _():
        issue(dcur_ref, slot)

    @pl.when(i + 1 < n)
    def _():
        issue(dnxt_ref, 1 - slot)

    pltpu.make_async_copy(y_ref.at[pl.ds(0, TOP_K * tt)], buf.at[slot], sem.at[slot]).wait()
    w = w_ref[...]
    lo = hi = None
    for k in range(TOP_K):
        ya, yb = _unpack_halves_f32(buf[slot, k * tt:(k + 1) * tt])
        ta, tb = w[:, k:k + 1] * ya, w[:, k:k + 1] * yb
        lo, hi = (ta, tb) if lo is None else (lo + ta, hi + tb)
    routed = jnp.concatenate([lo, hi], axis=1)
    y = _layer_norm_rows(DEEPNORM_ALPHA * h_ref[...] + (routed + sh_ref[...]), g_ref[...], b_ref[...])
    o_ref[...] = y
    ob_ref[...] = y.astype(ob_ref.dtype)


def _combine_ln(h, y_rows, dest, top_w, shared, ln_g, ln_b, *, tt=128):
    s, d = h.shape
    n = s // tt
    row = pl.BlockSpec((tt, d), lambda i: (i, 0))
    vec = pl.BlockSpec((1, d), lambda i: (0, 0))
    return pl.pallas_call(
        _combine_kernel,
        grid=(n,),
        in_specs=[pl.BlockSpec((TOP_K, tt), lambda i: (0, i), memory_space=pltpu.SMEM),
                  pl.BlockSpec((TOP_K, tt), lambda i: (0, jnp.minimum(i + 1, n - 1)), memory_space=pltpu.SMEM),
                  pl.BlockSpec((tt, TOP_K), lambda i: (i, 0)),
                  row, row, vec, vec,
                  pl.BlockSpec(memory_space=pl.ANY)],
        out_specs=[row, row],
        out_shape=[jax.ShapeDtypeStruct((s, d), F32), jax.ShapeDtypeStruct((s, d), MXU)],
        scratch_shapes=[pltpu.VMEM((2, TOP_K * tt, d // 2), jnp.int32), pltpu.SemaphoreType.DMA((2,))],
        compiler_params=_cparams(("arbitrary",)),
        name="moe_combine_layernorm",
    )(dest, dest, top_w.T, h, shared, ln_g.reshape(1, d), ln_b.reshape(1, d), y_rows)


def _router_kernel(h_ref, wr_ref, rb_ref, e_ref, w_ref, pos_ref, cnt_ref, carry_ref):
    tq = h_ref.shape[0]
    i = pl.program_id(0)

    @pl.when(i == 0)
    def _reset():
        carry_ref[...] = jnp.zeros_like(carry_ref)

    logits = lax.dot_general(wr_ref[...].astype(MXU), h_ref[...].astype(MXU), NT, preferred_element_type=F32)
    scores = _sigmoid(logits)
    biased = scores + rb_ref[...]
    G, PG = N_EXPERT_GROUPS, EXPERTS_PER_GROUP
    sub = lax.broadcasted_iota(jnp.int32, (PG, tq), 0).astype(F32)
    gi = lax.broadcasted_iota(jnp.int32, (G, tq), 0).astype(F32)
    gs = jnp.zeros((G, tq), F32)
    for g in range(G):
        blk = biased[g * PG:(g + 1) * PG, :]
        m1 = jnp.max(blk, axis=0, keepdims=True)
        i1 = jnp.min(jnp.where(blk == m1, sub, float(PG)), axis=0, keepdims=True)
        m2 = jnp.max(jnp.where(sub == i1, -jnp.inf, blk), axis=0, keepdims=True)
        gs = jnp.where(gi == float(g), m1 + m2, gs)
    keep = jnp.zeros((G, tq), F32)
    for _ in range(TOPK_GROUPS):
        mx = jnp.max(gs, axis=0, keepdims=True)
        idx = jnp.min(jnp.where(gs == mx, gi, float(G)), axis=0, keepdims=True)
        pick = gi == idx
        keep = jnp.where(pick, 1.0, keep)
        gs = jnp.where(pick, -jnp.inf, gs)
    val = jnp.concatenate(
        [jnp.where(keep[g:g + 1, :] > 0, biased[g * PG:(g + 1) * PG, :], -jnp.inf) for g in range(G)], axis=0)
    ei = lax.broadcasted_iota(jnp.int32, (N_EXPERTS, tq), 0).astype(F32)
    picks, svals = [], []
    sel = jnp.zeros((N_EXPERTS, tq), F32)
    for r in range(TOP_K):
        mx = jnp.max(val, axis=0, keepdims=True)
        idx = jnp.min(jnp.where(val == mx, ei, float(N_EXPERTS)), axis=0, keepdims=True)
        pick = ei == idx
        picks.append(pick)
        svals.append(jnp.sum(jnp.where(pick, scores, 0.0), axis=0, keepdims=True))
        e_ref[r:r + 1, :] = idx.astype(jnp.int32)
        sel = jnp.where(pick, 1.0, sel)
        val = jnp.where(pick, -jnp.inf, val)
    wsum = svals[0]
    for r in range(1, TOP_K):
        wsum = wsum + svals[r]
    tr = lax.broadcasted_iota(jnp.int32, (tq, tq), 0)
    tc = lax.broadcasted_iota(jnp.int32, (tq, tq), 1)
    before = (tr < tc).astype(jnp.bfloat16)
    prefix = jnp.dot(sel.astype(jnp.bfloat16), before, preferred_element_type=F32)
    pos = carry_ref[:, 0:1] + prefix
    for r in range(TOP_K):
        w_ref[r:r + 1, :] = svals[r] / wsum * ROUTED_SCALE
        pos_ref[r:r + 1, :] = jnp.sum(jnp.where(picks[r], pos, 0.0), axis=0, keepdims=True).astype(jnp.int32)
    total = carry_ref[...] + jnp.sum(sel, axis=1, keepdims=True)
    carry_ref[...] = total
    cnt_ref[...] = total


def _router(hb, w_router, router_bias, *, tq=256):
    s, d = hb.shape
    row = pl.BlockSpec((TOP_K, tq), lambda i: (0, i))
    return pl.pallas_call(
        _router_kernel,
        grid=(s // tq,),
        in_specs=[pl.BlockSpec((tq, d), lambda i: (i, 0)),
                  pl.BlockSpec((N_EXPERTS, d), lambda i: (0, 0)),
                  pl.BlockSpec((N_EXPERTS, 1), lambda i: (0, 0))],
        out_specs=[row, row, row, pl.BlockSpec((N_EXPERTS, LANES), lambda i: (0, 0))],
        out_shape=[jax.ShapeDtypeStruct((TOP_K, s), jnp.int32), jax.ShapeDtypeStruct((TOP_K, s), F32),
                   jax.ShapeDtypeStruct((TOP_K, s), jnp.int32), jax.ShapeDtypeStruct((N_EXPERTS, LANES), F32)],
        scratch_shapes=[pltpu.VMEM((N_EXPERTS, LANES), F32)],
        compiler_params=_cparams(("arbitrary",)),
        name="moe_router",
    )(hb, w_router.T, router_bias.reshape(N_EXPERTS, 1))


def _expert_kernel(te_ref, nu_ref, first_ref, slot_ref, nxt_ref, x_ref, wg_hbm, wu_hbm, wd_hbm, o_ref,
                   wg_buf, wu_buf, wd_buf, sem, *, layer, pack_out):
    i = pl.program_id(0)

    def weight_copies(e, sl):
        return (pltpu.make_async_copy(wg_hbm.at[layer, e], wg_buf.at[sl], sem.at[sl]),
                pltpu.make_async_copy(wu_hbm.at[layer, e], wu_buf.at[sl], sem.at[sl]),
                pltpu.make_async_copy(wd_hbm.at[layer, e], wd_buf.at[sl], sem.at[sl]))

    @pl.when(i < nu_ref[0])
    def _compute():
        sl = slot_ref[i]

        @pl.when(i == 0)
        def _first_expert():
            for c in weight_copies(te_ref[i], sl):
                c.start()

        @pl.when(first_ref[i] == 1)
        def _expert_changed():
            for c in weight_copies(te_ref[i], sl):
                c.wait()

            @pl.when(nxt_ref[i] >= 0)
            def _prefetch_next_expert():
                for c in weight_copies(nxt_ref[i], 1 - sl):
                    c.start()

        xa, xb = _unpack_halves(x_ref[...])
        half = xa.shape[1]

        def up(w_buf):
            return (jnp.dot(xa, w_buf[sl, 0:half, :].astype(MXU), preferred_element_type=F32)
                    + jnp.dot(xb, w_buf[sl, half:2 * half, :].astype(MXU), preferred_element_type=F32))

        hg = up(wg_buf)
        hu = up(wu_buf)
        act = (_silu(hg) * hu).astype(MXU)
        y = jnp.dot(act, wd_buf[sl].astype(MXU), preferred_element_type=F32)
        o_ref[...] = _pack_halves(y) if pack_out else y

    @pl.when(i >= nu_ref[0])
    def _unused():
        o_ref[...] = jnp.zeros_like(o_ref)


def _experts(x_rows, tile_e, n_used, w_gate, w_up, w_down, layer, *, tm, pack_out):
    n_rows = x_rows.shape[0]
    n_tiles = n_rows // tm
    d, f = w_gate.shape[2:]
    d_out, out_dtype = (d // 2, jnp.int32) if pack_out else (d, F32)
    idx = jnp.arange(n_tiles, dtype=jnp.int32)
    first = jnp.concatenate([jnp.ones((1,), jnp.int32), (tile_e[1:] != tile_e[:-1]).astype(jnp.int32)])
    slot = ((jnp.cumsum(first) - 1) % 2).astype(jnp.int32)
    used = idx < n_used[0]
    run_end = jnp.sum((tile_e[None, :] <= tile_e[:, None]) & used[None, :], axis=1).astype(jnp.int32)
    nxt = jnp.where(run_end < n_used[0], tile_e[jnp.minimum(run_end, n_tiles - 1)], -1).astype(jnp.int32)
    grid_spec = pltpu.PrefetchScalarGridSpec(
        num_scalar_prefetch=5,
        grid=(n_tiles,),
        in_specs=[pl.BlockSpec((tm, d // 2), lambda i, te, nu, *_: (jnp.minimum(i, nu[0] - 1), 0)),
                  pl.BlockSpec(memory_space=pl.ANY), pl.BlockSpec(memory_space=pl.ANY),
                  pl.BlockSpec(memory_space=pl.ANY)],
        out_specs=pl.BlockSpec((tm, d_out), lambda i, *_: (i, 0)),
        scratch_shapes=[pltpu.VMEM((2, d, f), w_gate.dtype), pltpu.VMEM((2, d, f), w_up.dtype),
                        pltpu.VMEM((2, f, d), w_down.dtype), pltpu.SemaphoreType.DMA((2,))],
    )
    return pl.pallas_call(
        functools.partial(_expert_kernel, layer=layer, pack_out=pack_out),
        grid_spec=grid_spec,
        out_shape=jax.ShapeDtypeStruct((n_rows, d_out), out_dtype),
        compiler_params=_cparams(("arbitrary",), 56),
        name="moe_experts",
    )(tile_e, n_used, first, slot, nxt, x_rows, w_gate, w_up, w_down)


def _moe_ln(h, hb, hp, layer, w_router, router_bias, w_exp_gate, w_exp_up, w_exp_down, w_sh_gate, w_sh_up, w_sh_down,
            ln_g, ln_b):
    s, d = h.shape
    tm = MOE_TILE
    top_e, top_w, top_pos, counts = _router(hb, w_router, router_bias)
    cnt = counts[:, 0].astype(jnp.int32)
    tiles_e = (cnt + tm - 1) // tm
    tile_end = jnp.cumsum(tiles_e)
    row_start = (tile_end - tiles_e) * tm
    experts = jnp.arange(N_EXPERTS, dtype=jnp.int32)
    dest = top_pos + jnp.sum(jnp.where(top_e[..., None] == experts, row_start, 0), axis=-1)
    n_tiles = s * TOP_K // tm + N_EXPERTS
    tile_e = jnp.minimum(jnp.sum(tile_end[None, :] <= jnp.arange(n_tiles, dtype=jnp.int32)[:, None], axis=1),
                         N_EXPERTS - 1).astype(jnp.int32)
    n_used = tile_end[-1:].astype(jnp.int32)
    zero_tiles = jnp.concatenate([jnp.where(tiles_e > 0, tile_end - 1, -1),
                                  jnp.where(n_used + experts < n_tiles, n_used + experts, -1)]).astype(jnp.int32)
    x_rows = _dispatch(hp, dest, zero_tiles, n_tiles * tm, tm=tm)
    y_rows = _experts(x_rows, tile_e, n_used, w_exp_gate, w_exp_up, w_exp_down, layer, tm=tm, pack_out=True)
    shared = _experts(hp, jnp.zeros((s // tm,), jnp.int32), jnp.full((1,), s // tm, jnp.int32),
                      w_sh_gate[:, None], w_sh_up[:, None], w_sh_down[:, None], layer, tm=tm, pack_out=False)
    return _combine_ln(h, y_rows, dest, top_w, shared, ln_g, ln_b)


def _project(hb, wt_all, layer):
    off = np.concatenate([[0], np.cumsum(IN_SPLIT_SIZES)])
    pa = _proj_wt(hb, wt_all, layer, col0=0, ncols=off[3], tm=1024, tn=1024, out_dtype=MXU)
    ps = _proj_wt(hb, wt_all, layer, col0=off[3], ncols=off[5] - off[3], tm=1024, tn=1024, out_dtype=F32)
    pn = _proj_wt(hb, wt_all, layer, col0=off[6], ncols=off[8] - off[6], tm=1024, tn=1280, out_dtype=MXU)
    pg = _proj_wt(hb, wt_all, layer, col0=off[9], ncols=off[10] - off[9], tm=1024, tn=1024, out_dtype=F32)
    p3 = _matmul(hb, _small_proj_weights(wt_all[layer]), tm=1024, tn=P3_COLS, out_dtype=F32)
    return pa, pn, ps, pg, p3


def _mixer(hb, wt_all, layer, conv_w, conv_b, dt_bias, a_log, d_skip, ssm_norm_g,
           cmp_pos_k, cmp_w1_k, cmp_w2_k, cmp_pos_v, cmp_w1_v, cmp_w2_v, w_br_a, w_br_b, w_br_c):
    pa, pn, ps, pg, p3 = _project(hb, wt_all, layer)
    y_a = _moba(pa)
    y_b = _ssd(ps, p3, conv_w, conv_b, dt_bias, a_log, d_skip, ssm_norm_g)
    kvc = _nsa_compress(pn, cmp_pos_k, cmp_w1_k, cmp_w2_k, cmp_pos_v, cmp_w1_v, cmp_w2_v)
    y_c, selb = _nsa_cmp(pn, p3, kvc)
    y_c = _nsa_sel(pn, p3, selb, y_c)
    y_c = _nsa_win(pn, p3, y_c)
    return _merge(y_a, y_b, y_c, pg, w_br_a, w_br_b, w_br_c)


def kernel(x, w_in, conv_w, conv_b, dt_bias, a_log, d_skip, ssm_norm_g, cmp_pos_k, cmp_w1_k, cmp_w2_k, cmp_pos_v, cmp_w1_v, cmp_w2_v, w_br_a, w_br_b, w_br_c, w_out, ln1_g, ln1_b, w_router, router_bias, w_exp_gate, w_exp_up, w_exp_down, w_sh_gate, w_sh_up, w_sh_down, ln2_g, ln2_b):
    bsz, s, d = x.shape
    assert bsz == 1
    h = x.reshape(s, d)
    hb = h.astype(MXU)
    wt_all = jnp.swapaxes(w_in, 1, 2)
    for l in range(w_in.shape[0]):
        merged = _mixer(hb, wt_all, l, conv_w[l], conv_b[l], dt_bias[l], a_log[l], d_skip[l], ssm_norm_g[l],
                        cmp_pos_k[l], cmp_w1_k[l], cmp_w2_k[l], cmp_pos_v[l], cmp_w1_v[l], cmp_w2_v[l],
                        w_br_a[l], w_br_b[l], w_br_c[l])
        h, hb, hp = _wout_ln(merged, w_out[l], h, ln1_g[l], ln1_b[l])
        h, hb = _moe_ln(h, hb, hp, l, w_router[l], router_bias[l], w_exp_gate, w_exp_up, w_exp_down,
                        w_sh_gate, w_sh_up, w_sh_down, ln2_g[l], ln2_b[l])
    return h.reshape(bsz, s, d)
```

```python
import functools

import numpy as np
import jax
import jax.numpy as jnp
from jax import lax
from jax.experimental import pallas as pl
from jax.experimental.pallas import tpu as pltpu

F32 = jnp.float32
MXU = jnp.bfloat16

D_MODEL = 2048
DEPTH = 2
HEAD_DIM = 128
MOBA_HEADS = 8
MOBA_WIDTH = MOBA_HEADS * HEAD_DIM
MOBA_BLOCK = 256
MOBA_TOPK = 3
SSM_D_INNER = D_MODEL
SSM_HEAD_DIM = 64
SSM_HEADS = SSM_D_INNER // SSM_HEAD_DIM
SSM_STATE = 128
SSM_GROUPS = 8
SSM_HEADS_PER_GROUP = SSM_HEADS // SSM_GROUPS
SSM_GROUP_WIDTH = SSM_D_INNER // SSM_GROUPS
SSM_CONV = 4
SSM_CHUNK = 256
SSM_CONV_CH = SSM_D_INNER + 2 * SSM_GROUPS * SSM_STATE
NSA_HEADS = 8
NSA_KV_GROUPS = 2
NSA_HEADS_PER_GROUP = NSA_HEADS // NSA_KV_GROUPS
NSA_WIDTH = NSA_HEADS * HEAD_DIM
NSA_KV_WIDTH = NSA_KV_GROUPS * HEAD_DIM
NSA_CMP_LEN = 32
NSA_CMP_STRIDE = 16
NSA_SEL_BLOCK = 64
NSA_TOPN = 16
NSA_WINDOW = 512
N_BRANCHES = 3
N_EXPERTS = 64
N_EXPERT_GROUPS = 8
EXPERTS_PER_GROUP = N_EXPERTS // N_EXPERT_GROUPS
TOPK_GROUPS = 4
TOP_K = 8
D_EXPERT = 512
ROUTED_SCALE = 2.5
DEEPNORM_ALPHA = (2 * DEPTH) ** 0.25
LN_EPS = 1e-5
RMS_EPS = 1e-5
IN_SPLIT_SIZES = (MOBA_WIDTH, MOBA_WIDTH, MOBA_WIDTH,
                  SSM_D_INNER, SSM_CONV_CH, SSM_HEADS,
                  NSA_WIDTH, 6 * NSA_KV_WIDTH, N_BRANCHES * NSA_HEADS,
                  N_BRANCHES * D_MODEL)

LANES = 128
SUBLANES = 8
ATT_TILE = 256
MOE_TILE = 256
ATT_SCALE = HEAD_DIM ** -0.5
Q_PRESCALE = ATT_SCALE * 1.4426950408889634
MASK_BIAS = -2.0 ** 30
NEG = -1e30

NT = (((1,), (1,)), ((), ()))

P1_QA, P1_KA, P1_VA = 0, 1024, 2048
P1_QN, P1_KVN = 0, 1024
P2_Z, P2_XBC = 0, 2048
P2_GM = 0
P3_COLS = (SSM_GROUPS + NSA_KV_GROUPS) * LANES


def _cparams(semantics, vmem_mb=48):
    return pltpu.CompilerParams(dimension_semantics=semantics, vmem_limit_bytes=vmem_mb * 1024 * 1024)


def _sigmoid(x):
    return 1.0 / (1.0 + jnp.exp(-x))


def _silu(x):
    return x * _sigmoid(x)


def _split3(x):
    hi = x.astype(jnp.bfloat16)
    r1 = x - hi.astype(F32)
    mid = r1.astype(jnp.bfloat16)
    lo = (r1 - mid.astype(F32)).astype(jnp.bfloat16)
    return hi, mid, lo


def _dot3(a_exact, x, left=True):
    acc = None
    for part in _split3(x):
        t = (jnp.dot(a_exact, part, preferred_element_type=F32) if left
             else jnp.dot(part, a_exact, preferred_element_type=F32))
        acc = t if acc is None else acc + t
    return acc


def _mm_kernel(a_ref, b_ref, o_ref):
    o_ref[...] = jnp.dot(a_ref[...].astype(MXU), b_ref[...].astype(MXU),
                         preferred_element_type=F32).astype(o_ref.dtype)


def _matmul(a, b, *, tm, tn, out_dtype):
    m, k = a.shape
    n = b.shape[1]
    assert m % tm == 0 and n % tn == 0
    return pl.pallas_call(
        _mm_kernel,
        grid=(n // tn, m // tm),
        in_specs=[pl.BlockSpec((tm, k), lambda j, i: (i, 0)),
                  pl.BlockSpec((k, tn), lambda j, i: (0, j))],
        out_specs=pl.BlockSpec((tm, tn), lambda j, i: (i, j)),
        out_shape=jax.ShapeDtypeStruct((m, n), out_dtype),
        compiler_params=_cparams(("arbitrary", "arbitrary"), 56),
        name="proj_matmul",
    )(a, b)


def _mm_wt_kernel(a_ref, wt_ref, o_ref, wb_ref, *, q_blocks):
    @pl.when(pl.program_id(1) == 0)
    def _cast_weights():
        wb_ref[...] = wt_ref[0].astype(MXU)

    acc = lax.dot_general(a_ref[...].astype(MXU), wb_ref[...], NT, preferred_element_type=F32)
    if q_blocks:
        acc = acc * jnp.where(pl.program_id(0) < q_blocks, Q_PRESCALE, 1.0)
    o_ref[...] = acc.astype(o_ref.dtype)


def _proj_wt(a, wt_all, layer, *, col0, ncols, tm, tn, out_dtype, q_cols=0):
    m, k = a.shape
    assert m % tm == 0 and ncols % tn == 0 and col0 % SUBLANES == 0 and q_cols % tn == 0
    return pl.pallas_call(
        functools.partial(_mm_wt_kernel, q_blocks=q_cols // tn),
        grid=(ncols // tn, m // tm),
        in_specs=[pl.BlockSpec((tm, k), lambda j, i: (i, 0)),
                  pl.BlockSpec((pl.Element(1), pl.Element(tn), pl.Element(k)),
                               lambda j, i: (layer, pl.multiple_of(int(col0) + j * tn, SUBLANES), 0))],
        out_specs=pl.BlockSpec((tm, tn), lambda j, i: (i, j)),
        out_shape=jax.ShapeDtypeStruct((m, ncols), out_dtype),
        scratch_shapes=[pltpu.VMEM((tn, k), MXU)],
        compiler_params=_cparams(("arbitrary", "arbitrary"), 56),
        name="proj_matmul_wt",
    )(a, wt_all)


def _small_proj_weights(wt):
    off = np.concatenate([[0], np.cumsum(IN_SPLIT_SIZES)])
    dt = wt[off[5]:off[6]].T
    gn = wt[off[8]:off[9]].T
    d = dt.shape[0]
    dtp = jnp.pad(dt.reshape(d, SSM_GROUPS, SSM_HEADS_PER_GROUP),
                  ((0, 0), (0, 0), (0, LANES - SSM_HEADS_PER_GROUP))).reshape(d, SSM_GROUPS * LANES)
    ng = NSA_HEADS_PER_GROUP * N_BRANCHES
    gnp = jnp.pad(gn.reshape(d, NSA_KV_GROUPS, ng), ((0, 0), (0, 0), (0, LANES - ng))).reshape(d, NSA_KV_GROUPS * LANES)
    return jnp.concatenate([dtp, gnp], axis=1).astype(MXU)


def _flash_init(m_ref, acc_ref):
    m_ref[...] = jnp.full(m_ref.shape, NEG, F32)
    acc_ref[...] = jnp.zeros(acc_ref.shape, F32)


def _flash_step(s, v_aug, m_ref, acc_ref):
    hd = HEAD_DIM
    m_prev = m_ref[...]
    m_new = jnp.maximum(m_prev, jnp.max(s, axis=1, keepdims=True))
    alpha = jnp.exp2(m_prev - m_new)
    p = jnp.concatenate([jnp.exp2(s[:, c * hd:(c + 1) * hd] - m_new) for c in range(s.shape[1] // hd)], axis=1)
    pv = jnp.dot(p.astype(MXU), v_aug, preferred_element_type=F32)
    acc_ref[:, 0:hd] = alpha * acc_ref[:, 0:hd] + pv[:, 0:hd]
    acc_ref[:, hd:2 * hd] = alpha * acc_ref[:, hd:2 * hd] + pv[:, hd:2 * hd]
    m_ref[...] = m_new


def _flash_out(acc_ref):
    return acc_ref[:, 0:HEAD_DIM] / acc_ref[:, HEAD_DIM:2 * HEAD_DIM]


def _flash_loop(n, scores, values, sa_ref, sb_ref, m_ref, acc_ref):
    @pl.when(n > 0)
    def _first():
        sa_ref[...] = scores(0)

    def pair(u, c):
        t = 2 * u
        sb_ref[...] = scores(t + 1)
        _flash_step(sa_ref[...], values(t), m_ref, acc_ref)
        sa_ref[...] = scores(t + 2)
        _flash_step(sb_ref[...], values(t + 1), m_ref, acc_ref)
        return c

    n_pairs = jnp.maximum(n - 1, 0) // 2
    lax.fori_loop(0, n_pairs, pair, 0)
    t0 = 2 * n_pairs
    left = n - t0

    @pl.when(left == 2)
    def _last_two():
        sb_ref[...] = scores(t0 + 1)
        _flash_step(sa_ref[...], values(t0), m_ref, acc_ref)
        _flash_step(sb_ref[...], values(t0 + 1), m_ref, acc_ref)

    @pl.when(left == 1)
    def _last_one():
        _flash_step(sa_ref[...], values(t0), m_ref, acc_ref)


MOBA_KV_GROUP = 4


def _moba_kernel(q_ref, k_ref, v_ref, o_ref, kaug_ref, vaug_ref, kmean_ref, m_ref, acc_ref, sa_ref, sb_ref, *, nb):
    blk = MOBA_BLOCK
    grp = MOBA_KV_GROUP * blk
    qi = pl.program_id(1)

    @pl.when(qi == 0)
    def _build_keys():
        kmean_ref[...] = jnp.zeros_like(kmean_ref)
        lane = lax.broadcasted_iota(jnp.int32, (blk, LANES), 1)
        ones = jnp.ones((blk, HEAD_DIM), vaug_ref.dtype)

        def body(j, c):
            r0 = pl.multiple_of(j * blk, blk)
            kb = k_ref[pl.ds(r0, blk), :]
            kaug_ref[pl.ds(r0, blk), 0:HEAD_DIM] = kb.astype(kaug_ref.dtype)
            kaug_ref[pl.ds(r0, blk), HEAD_DIM:2 * HEAD_DIM] = (lane == j).astype(kaug_ref.dtype)
            vaug_ref[pl.ds(r0, blk), 0:HEAD_DIM] = v_ref[pl.ds(r0, blk), :].astype(vaug_ref.dtype)
            vaug_ref[pl.ds(r0, blk), HEAD_DIM:2 * HEAD_DIM] = ones
            kmean_ref[pl.ds(j, 1), :] = jnp.sum(kb.astype(F32), axis=0, keepdims=True) * (1.0 / blk)
            return c

        lax.fori_loop(0, nb, body, 0)

    q = q_ref[...].astype(MXU)
    gate = lax.dot_general(kmean_ref[...].astype(MXU), q, NT, preferred_element_type=F32)
    kblk = lax.broadcasted_iota(jnp.int32, (LANES, blk), 0).astype(F32)
    past = kblk < qi.astype(F32)
    gate = jnp.where(past, gate, -jnp.inf)
    sel = jnp.zeros((LANES, blk), F32)
    for _ in range(MOBA_TOPK):
        mx = jnp.max(gate, axis=0, keepdims=True)
        idx = jnp.min(jnp.where((gate == mx) & past, kblk, float(LANES)), axis=0, keepdims=True)
        pick = kblk == idx
        sel = jnp.where(pick, 1.0, sel)
        gate = jnp.where(pick, -jnp.inf, gate)
    bias = jnp.where(sel.T > 0, 0.0, MASK_BIAS).astype(MXU)
    qaug = jnp.concatenate([q, bias], axis=1)

    _flash_init(m_ref, acc_ref)

    def scores(g):
        r0 = pl.multiple_of(g * grp, grp)
        return lax.dot_general(qaug, kaug_ref[pl.ds(r0, grp), :], NT, preferred_element_type=F32)

    def values(g):
        return vaug_ref[pl.ds(pl.multiple_of(g * grp, grp), grp), :]

    _flash_loop((qi + MOBA_KV_GROUP - 1) // MOBA_KV_GROUP, scores, values, sa_ref, sb_ref, m_ref, acc_ref)
    r0 = pl.multiple_of(qi * blk, blk)
    s = lax.dot_general(q, k_ref[pl.ds(r0, blk), :].astype(MXU), NT, preferred_element_type=F32)
    row = lax.broadcasted_iota(jnp.int32, (blk, blk), 0)
    col = lax.broadcasted_iota(jnp.int32, (blk, blk), 1)
    _flash_step(jnp.where(col <= row, s, NEG), vaug_ref[pl.ds(r0, blk), :], m_ref, acc_ref)
    o_ref[...] = _flash_out(acc_ref).astype(o_ref.dtype)


def _moba(p1):
    s = p1.shape[0]
    nb = s // MOBA_BLOCK
    assert nb <= LANES and nb % MOBA_KV_GROUP == 0
    kcol, vcol = P1_KA // HEAD_DIM, P1_VA // HEAD_DIM
    return pl.pallas_call(
        functools.partial(_moba_kernel, nb=nb),
        grid=(MOBA_HEADS, nb),
        in_specs=[pl.BlockSpec((MOBA_BLOCK, HEAD_DIM), lambda h, i: (i, h)),
                  pl.BlockSpec((s, HEAD_DIM), lambda h, i: (0, kcol + h)),
                  pl.BlockSpec((s, HEAD_DIM), lambda h, i: (0, vcol + h))],
        out_specs=pl.BlockSpec((MOBA_BLOCK, HEAD_DIM), lambda h, i: (i, h)),
        out_shape=jax.ShapeDtypeStruct((s, MOBA_WIDTH), MXU),
        scratch_shapes=[pltpu.VMEM((s, 2 * HEAD_DIM), MXU), pltpu.VMEM((s, 2 * HEAD_DIM), MXU),
                        pltpu.VMEM((LANES, HEAD_DIM), F32),
                        pltpu.VMEM((MOBA_BLOCK, HEAD_DIM), F32), pltpu.VMEM((MOBA_BLOCK, 2 * HEAD_DIM), F32),
                        pltpu.VMEM((MOBA_BLOCK, MOBA_KV_GROUP * MOBA_BLOCK), F32),
                        pltpu.VMEM((MOBA_BLOCK, MOBA_KV_GROUP * MOBA_BLOCK), F32)],
        compiler_params=_cparams(("arbitrary", "arbitrary")),
        name="moba_attention",
    )(p1, p1, p1)


def _ssd_kernel(dtb_ref, alog_ref, dskip_ref,
                xs_ref, bm_ref, cm_ref, z_ref, dt_ref, cwx_ref, cwb_ref, cwc_ref, cbx_ref, cbb_ref, cbc_ref,
                ng_ref, o_ref, xbuf, bbuf, cbuf, state_ref, ybuf):
    L = SSM_CHUNK
    W = SSM_GROUP_WIDTH
    J = SSM_HEADS_PER_GROUP
    P = SSM_HEAD_DIM
    g = pl.program_id(0)
    c = pl.program_id(1)

    @pl.when(c == 0)
    def _reset():
        xbuf[0:SUBLANES, :] = jnp.zeros((SUBLANES, W), F32)
        bbuf[0:SUBLANES, :] = jnp.zeros((SUBLANES, SSM_STATE), F32)
        cbuf[0:SUBLANES, :] = jnp.zeros((SUBLANES, SSM_STATE), F32)
        state_ref[...] = jnp.zeros_like(state_ref)

    def conv_silu(buf, raw_ref, w_ref, b_ref):
        buf[SUBLANES:SUBLANES + L, :] = raw_ref[...]
        acc = b_ref[...]
        for i in range(SSM_CONV):
            lo = SUBLANES - (SSM_CONV - 1) + i
            acc = acc + w_ref[i:i + 1, :] * buf[lo:lo + L, :]
        buf[0:SUBLANES, :] = buf[L:L + SUBLANES, :]
        return _silu(acc)

    xs = conv_silu(xbuf, xs_ref, cwx_ref, cbx_ref)
    bm = conv_silu(bbuf, bm_ref, cwb_ref, cbb_ref)
    cm = conv_silu(cbuf, cm_ref, cwc_ref, cbc_ref)

    lane = lax.broadcasted_iota(jnp.int32, (1, LANES), 1)
    dtb = jnp.zeros((1, LANES), F32)
    alog = jnp.full((1, LANES), -jnp.inf, F32)
    for j in range(J):
        dtb = jnp.where(lane == j, dtb_ref[g * J + j], dtb)
        alog = jnp.where(lane == j, alog_ref[g * J + j], alog)
    x = dt_ref[...] + dtb
    dt = jnp.maximum(x, 0.0) + jnp.log(1.0 + jnp.exp(-jnp.abs(x)))
    dt = jnp.where(lane < J, dt, 0.0)
    a = dt * (-jnp.exp(alog))

    er = lax.broadcasted_iota(jnp.int32, (LANES, W), 0)
    ec = lax.broadcasted_iota(jnp.int32, (LANES, W), 1)
    expand = ((ec >> 6) == er).astype(jnp.bfloat16)
    tr = lax.broadcasted_iota(jnp.int32, (L, L), 0)
    tc = lax.broadcasted_iota(jnp.int32, (L, L), 1)
    tril = tr >= tc
    tril_b = tril.astype(jnp.bfloat16)
    dt_e = _dot3(expand, dt, left=False)
    a_e = _dot3(expand, a, left=False)
    acs = _dot3(tril_b, a_e, left=True)
    acs_t = acs.T
    a_last = acs[L - 1:L, :]

    xdt = xs * dt_e
    cb = lax.dot_general(cm.astype(MXU), bm.astype(MXU), NT, preferred_element_type=F32)
    for j in range(J):
        colv = acs[:, j * P:j * P + 1]
        rowv = acs_t[j * P:j * P + 1, :]
        dec = jnp.exp(jnp.where(tril, colv - rowv, -jnp.inf))
        ybuf[:, j * P:(j + 1) * P] = jnp.dot((cb * dec).astype(MXU), xdt[:, j * P:(j + 1) * P].astype(MXU),
                                             preferred_element_type=F32)
    st_old = state_ref[...]
    xdte = (xdt * jnp.exp(a_last - acs)).astype(MXU)
    st_new = jnp.dot(bm.T.astype(MXU), xdte, preferred_element_type=F32)
    y_off = jnp.dot(cm.astype(MXU), st_old.astype(MXU), preferred_element_type=F32) * jnp.exp(acs)
    state_ref[...] = st_old * jnp.exp(a_last) + st_new

    lane_w = lax.broadcasted_iota(jnp.int32, (1, W), 1)
    dsk = jnp.zeros((1, W), F32)
    for j in range(J):
        dsk = jnp.where((lane_w >> 6) == j, dskip_ref[g * J + j], dsk)
    y = ybuf[...] + y_off + xs * dsk
    y = y * _silu(z_ref[...])
    y = y * lax.rsqrt(jnp.mean(y * y, axis=1, keepdims=True) + RMS_EPS)
    o_ref[...] = (y * ng_ref[...]).astype(o_ref.dtype)


def _ssd(p2, p3, conv_w, conv_b, dt_bias, a_log, d_skip, norm_g):
    s = p2.shape[0]
    L, W, N = SSM_CHUNK, SSM_GROUP_WIDTH, SSM_STATE
    xs0 = P2_XBC // W
    bm0 = (P2_XBC + SSM_D_INNER) // N
    cm0 = bm0 + SSM_GROUPS
    z0 = P2_Z // W
    cb2 = conv_b.reshape(1, SSM_CONV_CH)
    ng2 = norm_g.reshape(1, SSM_D_INNER)
    grid_spec = pltpu.PrefetchScalarGridSpec(
        num_scalar_prefetch=3,
        grid=(SSM_GROUPS, s // L),
        in_specs=[
            pl.BlockSpec((L, W), lambda g, c, *_: (c, xs0 + g)),
            pl.BlockSpec((L, N), lambda g, c, *_: (c, bm0 + g)),
            pl.BlockSpec((L, N), lambda g, c, *_: (c, cm0 + g)),
            pl.BlockSpec((L, W), lambda g, c, *_: (c, z0 + g)),
            pl.BlockSpec((L, LANES), lambda g, c, *_: (c, g)),
            pl.BlockSpec((SSM_CONV, W), lambda g, c, *_: (0, g)),
            pl.BlockSpec((SSM_CONV, N), lambda g, c, *_: (0, SSM_D_INNER // N + g)),
            pl.BlockSpec((SSM_CONV, N), lambda g, c, *_: (0, SSM_D_INNER // N + SSM_GROUPS + g)),
            pl.BlockSpec((1, W), lambda g, c, *_: (0, g)),
            pl.BlockSpec((1, N), lambda g, c, *_: (0, SSM_D_INNER // N + g)),
            pl.BlockSpec((1, N), lambda g, c, *_: (0, SSM_D_INNER // N + SSM_GROUPS + g)),
            pl.BlockSpec((1, W), lambda g, c, *_: (0, g)),
        ],
        out_specs=pl.BlockSpec((L, W), lambda g, c, *_: (c, g)),
        scratch_shapes=[pltpu.VMEM((SUBLANES + L, W), F32), pltpu.VMEM((SUBLANES + L, N), F32),
                        pltpu.VMEM((SUBLANES + L, N), F32), pltpu.VMEM((N, W), F32), pltpu.VMEM((L, W), F32)],
    )
    return pl.pallas_call(
        _ssd_kernel,
        grid_spec=grid_spec,
        out_shape=jax.ShapeDtypeStruct((s, SSM_D_INNER), MXU),
        compiler_params=_cparams(("arbitrary", "arbitrary")),
        name="ssd_mixer",
    )(dt_bias, a_log, d_skip, p2, p2, p2, p2, p3, conv_w, conv_w, conv_w, cb2, cb2, cb2, ng2)


def _nsa_compress_kernel(t_ref, pos_ref, w1_ref, w2_ref, o_ref):
    half = NSA_CMP_STRIDE * HEAD_DIM
    t = t_ref[0].astype(F32)
    pos = pos_ref[0]
    lo = (t + pos[:, :half]).astype(MXU)
    hi = (t + pos[:, half:]).astype(MXU)
    w1 = w1_ref[0]
    a = jnp.dot(lo, w1[:half].astype(MXU), preferred_element_type=F32)
    b = jnp.dot(hi, w1[half:].astype(MXU), preferred_element_type=F32)
    n = t.shape[0]
    pre = a + pltpu.roll(b, n - 1, 0)
    act = jax.nn.gelu(pre, approximate=True)
    o_ref[0] = jnp.dot(act.astype(MXU), w2_ref[0].astype(MXU), preferred_element_type=F32).astype(o_ref.dtype)


def _nsa_compress(p1, pos_k, w1_k, w2_k, pos_v, w1_v, w2_v):
    s = p1.shape[0]
    n_str = s // NSA_CMP_STRIDE
    kv = p1[:, P1_KVN:P1_KVN + 2 * NSA_KV_WIDTH]
    t4 = kv.reshape(s, 4, HEAD_DIM).transpose(1, 0, 2).reshape(4, n_str, NSA_CMP_STRIDE * HEAD_DIM)
    pos = jnp.stack([pos_k, pos_v]).reshape(2, 1, NSA_CMP_LEN * HEAD_DIM)
    w1 = jnp.stack([w1_k, w1_v])
    w2 = jnp.stack([w2_k, w2_v])
    return pl.pallas_call(
        _nsa_compress_kernel,
        grid=(4,),
        in_specs=[pl.BlockSpec((1, n_str, NSA_CMP_STRIDE * HEAD_DIM), lambda i: (i, 0, 0)),
                  pl.BlockSpec((1, 1, NSA_CMP_LEN * HEAD_DIM), lambda i: (i // 2, 0, 0)),
                  pl.BlockSpec((1, NSA_CMP_LEN * HEAD_DIM, HEAD_DIM), lambda i: (i // 2, 0, 0)),
                  pl.BlockSpec((1, HEAD_DIM, HEAD_DIM), lambda i: (i // 2, 0, 0))],
        out_specs=pl.BlockSpec((1, n_str, HEAD_DIM), lambda i: (i, 0, 0)),
        out_shape=jax.ShapeDtypeStruct((4, n_str, HEAD_DIM), MXU),
        compiler_params=_cparams(("arbitrary",)),
        name="nsa_compress",
    )(t4, pos, w1, w2)


def _nsa_cmp_kernel(q_ref, kc_ref, vc_ref, gn_ref, o_ref, selb_ref, *, n_sel):
    tq = ATT_TILE
    qi = pl.program_id(1)
    kc = kc_ref[0].astype(MXU)
    vc = vc_ref[0].astype(MXU)
    nc = kc.shape[0]
    pos = qi * tq + lax.broadcasted_iota(jnp.int32, (tq, nc), 0)
    cidx = lax.broadcasted_iota(jnp.int32, (tq, nc), 1)
    valid = (cidx * NSA_CMP_STRIDE + NSA_CMP_LEN - 1 <= pos) & (cidx < nc - 1)
    c_start = lax.broadcasted_iota(jnp.int32, (LANES, nc), 1) * NSA_CMP_STRIDE
    s_start = lax.broadcasted_iota(jnp.int32, (LANES, nc), 0) * NSA_SEL_BLOCK
    overlap_t = ((c_start < s_start + NSA_SEL_BLOCK) & (c_start + NSA_CMP_LEN > s_start)).astype(MXU)
    gates = _sigmoid(gn_ref[...])
    imp = jnp.zeros((LANES, tq), F32)
    for j in range(NSA_HEADS_PER_GROUP):
        q = q_ref[:, j * HEAD_DIM:(j + 1) * HEAD_DIM].astype(MXU)
        s = lax.dot_general(q, kc, NT, preferred_element_type=F32)
        s = jnp.where(valid, s, -jnp.inf)
        m = jnp.max(s, axis=1, keepdims=True)
        m = jnp.where(m > -jnp.inf, m, 0.0)
        e = jnp.exp2(s - m)
        den = jnp.sum(e, axis=1, keepdims=True)
        p = (e / jnp.where(den > 0, den, 1.0)).astype(MXU)
        o = jnp.dot(p, vc, preferred_element_type=F32)
        imp = imp + lax.dot_general(overlap_t, p, NT, preferred_element_type=F32)
        o_ref[:, j * HEAD_DIM:(j + 1) * HEAD_DIM] = o * gates[:, N_BRANCHES * j:N_BRANCHES * j + 1]

    blk = lax.broadcasted_iota(jnp.int32, (LANES, tq), 0).astype(F32)
    cur = ((qi * tq + lax.broadcasted_iota(jnp.int32, (LANES, tq), 1)) >> 6).astype(F32)
    allowed = blk <= cur
    forced = (blk == 0.0) | (blk == cur) | (blk == cur - 1.0)
    val = jnp.where(forced, jnp.inf, jnp.where(allowed, imp, -jnp.inf))
    val = jnp.where(blk < float(n_sel), val, -jnp.inf)

    def pick_round(_, c):
        val, sel = c
        mx = jnp.max(val, axis=0, keepdims=True)
        idx = jnp.min(jnp.where(val == mx, blk, float(LANES)), axis=0, keepdims=True)
        pick = blk == idx
        sel = jnp.where(pick & allowed, 1.0, sel)
        val = jnp.where(pick, -jnp.inf, val)
        return val, sel

    _, sel = lax.fori_loop(0, min(NSA_TOPN, n_sel), pick_round, (val, jnp.zeros((LANES, tq), F32)))
    selb_ref[0] = jnp.where(sel.T > 0, 0.0, MASK_BIAS).astype(selb_ref.dtype)


def _nsa_cmp(p1, p3, kvc):
    s = p1.shape[0]
    n_str = s // NSA_CMP_STRIDE
    n_sel = s // NSA_SEL_BLOCK
    assert n_sel <= LANES
    gw = NSA_HEADS_PER_GROUP * HEAD_DIM
    q0 = P1_QN // gw
    return pl.pallas_call(
        functools.partial(_nsa_cmp_kernel, n_sel=n_sel),
        grid=(NSA_KV_GROUPS, s // ATT_TILE),
        in_specs=[pl.BlockSpec((ATT_TILE, gw), lambda g, i: (i, q0 + g)),
                  pl.BlockSpec((1, n_str, HEAD_DIM), lambda g, i: (g, 0, 0)),
                  pl.BlockSpec((1, n_str, HEAD_DIM), lambda g, i: (NSA_KV_GROUPS + g, 0, 0)),
                  pl.BlockSpec((ATT_TILE, LANES), lambda g, i: (i, SSM_GROUPS + g))],
        out_specs=[pl.BlockSpec((ATT_TILE, gw), lambda g, i: (i, g)),
                   pl.BlockSpec((1, ATT_TILE, LANES), lambda g, i: (g, i, 0))],
        out_shape=[jax.ShapeDtypeStruct((s, NSA_WIDTH), F32),
                   jax.ShapeDtypeStruct((NSA_KV_GROUPS, s, LANES), MXU)],
        compiler_params=_cparams(("arbitrary", "arbitrary")),
        name="nsa_compressed_attention",
    )(p1, kvc, kvc, p3)


def _nsa_sel_kernel(q_ref, selb_ref, k_ref, v_ref, gn_ref, prev_ref, o_ref, kaug_ref, vaug_ref, qaug_ref,
                    m_ref, acc_ref, sa_ref, sb_ref, *, n_tiles):
    tq = ATT_TILE
    J = NSA_HEADS_PER_GROUP
    qi = pl.program_id(1)

    @pl.when(qi == 0)
    def _build_keys():
        lane = lax.broadcasted_iota(jnp.int32, (tq, LANES), 1)
        rowi = lax.broadcasted_iota(jnp.int32, (tq, LANES), 0)
        ones = jnp.ones((tq, HEAD_DIM), vaug_ref.dtype)

        def body(t, c):
            r0 = pl.multiple_of(t * tq, tq)
            kaug_ref[pl.ds(r0, tq), 0:HEAD_DIM] = k_ref[pl.ds(r0, tq), :].astype(kaug_ref.dtype)
            kaug_ref[pl.ds(r0, tq), HEAD_DIM:2 * HEAD_DIM] = (
                lane == ((t * tq + rowi) >> 6)).astype(kaug_ref.dtype)
            vaug_ref[pl.ds(r0, tq), 0:HEAD_DIM] = v_ref[pl.ds(r0, tq), :].astype(vaug_ref.dtype)
            vaug_ref[pl.ds(r0, tq), HEAD_DIM:2 * HEAD_DIM] = ones
            return c

        lax.fori_loop(0, n_tiles, body, 0)

    selb = selb_ref[0]
    for j in range(J):
        qaug_ref[j * tq:(j + 1) * tq, 0:HEAD_DIM] = q_ref[:, j * HEAD_DIM:(j + 1) * HEAD_DIM].astype(qaug_ref.dtype)
        qaug_ref[j * tq:(j + 1) * tq, HEAD_DIM:2 * HEAD_DIM] = selb
    _flash_init(m_ref, acc_ref)

    def scores(t):
        r0 = pl.multiple_of(t * tq, tq)
        return lax.dot_general(qaug_ref[...], kaug_ref[pl.ds(r0, tq), :], NT, preferred_element_type=F32)

    def values(t):
        return vaug_ref[pl.ds(pl.multiple_of(t * tq, tq), tq), :]

    _flash_loop(qi, scores, values, sa_ref, sb_ref, m_ref, acc_ref)
    row = lax.broadcasted_iota(jnp.int32, (J * tq, tq), 0) & (tq - 1)
    col = lax.broadcasted_iota(jnp.int32, (J * tq, tq), 1)
    _flash_step(jnp.where(col <= row, scores(qi), NEG), values(qi), m_ref, acc_ref)
    o = _flash_out(acc_ref)
    gates = _sigmoid(gn_ref[...])
    for j in range(J):
        o_ref[:, j * HEAD_DIM:(j + 1) * HEAD_DIM] = (
            prev_ref[:, j * HEAD_DIM:(j + 1) * HEAD_DIM]
            + o[j * tq:(j + 1) * tq] * gates[:, N_BRANCHES * j + 1:N_BRANCHES * j + 2])


def _nsa_sel(p1, p3, selb, prev):
    s = p1.shape[0]
    gw = NSA_HEADS_PER_GROUP * HEAD_DIM
    q0 = P1_QN // gw
    k0 = (P1_KVN + 2 * NSA_KV_WIDTH) // HEAD_DIM
    v0 = k0 + NSA_KV_GROUPS
    return pl.pallas_call(
        functools.partial(_nsa_sel_kernel, n_tiles=s // ATT_TILE),
        grid=(NSA_KV_GROUPS, s // ATT_TILE),
        in_specs=[pl.BlockSpec((ATT_TILE, gw), lambda g, i: (i, q0 + g)),
                  pl.BlockSpec((1, ATT_TILE, LANES), lambda g, i: (g, i, 0)),
                  pl.BlockSpec((s, HEAD_DIM), lambda g, i: (0, k0 + g)),
                  pl.BlockSpec((s, HEAD_DIM), lambda g, i: (0, v0 + g)),
                  pl.BlockSpec((ATT_TILE, LANES), lambda g, i: (i, SSM_GROUPS + g)),
                  pl.BlockSpec((ATT_TILE, gw), lambda g, i: (i, g))],
        out_specs=pl.BlockSpec((ATT_TILE, gw), lambda g, i: (i, g)),
        out_shape=jax.ShapeDtypeStruct((s, NSA_WIDTH), F32),
        scratch_shapes=[pltpu.VMEM((s, 2 * HEAD_DIM), MXU), pltpu.VMEM((s, 2 * HEAD_DIM), MXU),
                        pltpu.VMEM((NSA_HEADS_PER_GROUP * ATT_TILE, 2 * HEAD_DIM), MXU),
                        pltpu.VMEM((NSA_HEADS_PER_GROUP * ATT_TILE, HEAD_DIM), F32),
                        pltpu.VMEM((NSA_HEADS_PER_GROUP * ATT_TILE, 2 * HEAD_DIM), F32),
                        pltpu.VMEM((NSA_HEADS_PER_GROUP * ATT_TILE, ATT_TILE), F32),
                        pltpu.VMEM((NSA_HEADS_PER_GROUP * ATT_TILE, ATT_TILE), F32)],
        compiler_params=_cparams(("arbitrary", "arbitrary")),
        name="nsa_selected_attention",
    )(p1, selb, p1, p1, p3, prev)


def _nsa_win_kernel(q_ref, k_ref, v_ref, gn_ref, prev_ref, o_ref, qst_ref, m_ref, acc_ref, sa_ref, sb_ref):
    tq = ATT_TILE
    J = NSA_HEADS_PER_GROUP
    halo = NSA_WINDOW // tq
    qi = pl.program_id(1)
    for j in range(J):
        qst_ref[j * tq:(j + 1) * tq, :] = q_ref[:, j * HEAD_DIM:(j + 1) * HEAD_DIM].astype(qst_ref.dtype)
    row = lax.broadcasted_iota(jnp.int32, (J * tq, tq), 0) & (tq - 1)
    col = lax.broadcasted_iota(jnp.int32, (J * tq, tq), 1)
    ones = jnp.ones((tq, HEAD_DIM), MXU)
    _flash_init(m_ref, acc_ref)
    first = jnp.maximum(qi - halo, 0)

    def scores(t):
        kt = first + t
        r0 = pl.multiple_of(kt * tq, tq)
        s = lax.dot_general(qst_ref[...], k_ref[pl.ds(r0, tq), :].astype(MXU), NT, preferred_element_type=F32)
        diff = (qi - kt) * tq + row - col
        return jnp.where((diff >= 0) & (diff < NSA_WINDOW), s, NEG)

    def values(t):
        r0 = pl.multiple_of((first + t) * tq, tq)
        return jnp.concatenate([v_ref[pl.ds(r0, tq), :].astype(MXU), ones], axis=1)

    _flash_loop(qi + 1 - first, scores, values, sa_ref, sb_ref, m_ref, acc_ref)
    o = _flash_out(acc_ref)
    gates = _sigmoid(gn_ref[...])
    for j in range(J):
        o_ref[:, j * HEAD_DIM:(j + 1) * HEAD_DIM] = (
            prev_ref[:, j * HEAD_DIM:(j + 1) * HEAD_DIM]
            + o[j * tq:(j + 1) * tq] * gates[:, N_BRANCHES * j + 2:N_BRANCHES * j + 3])


def _nsa_win(p1, p3, prev):
    s = p1.shape[0]
    gw = NSA_HEADS_PER_GROUP * HEAD_DIM
    q0 = P1_QN // gw
    k0 = (P1_KVN + 4 * NSA_KV_WIDTH) // HEAD_DIM
    v0 = k0 + NSA_KV_GROUPS
    return pl.pallas_call(
        _nsa_win_kernel,
        grid=(NSA_KV_GROUPS, s // ATT_TILE),
        in_specs=[pl.BlockSpec((ATT_TILE, gw), lambda g, i: (i, q0 + g)),
                  pl.BlockSpec((s, HEAD_DIM), lambda g, i: (0, k0 + g)),
                  pl.BlockSpec((s, HEAD_DIM), lambda g, i: (0, v0 + g)),
                  pl.BlockSpec((ATT_TILE, LANES), lambda g, i: (i, SSM_GROUPS + g)),
                  pl.BlockSpec((ATT_TILE, gw), lambda g, i: (i, g))],
        out_specs=pl.BlockSpec((ATT_TILE, gw), lambda g, i: (i, g)),
        out_shape=jax.ShapeDtypeStruct((s, NSA_WIDTH), F32),
        scratch_shapes=[pltpu.VMEM((NSA_HEADS_PER_GROUP * ATT_TILE, HEAD_DIM), MXU),
                        pltpu.VMEM((NSA_HEADS_PER_GROUP * ATT_TILE, HEAD_DIM), F32),
                        pltpu.VMEM((NSA_HEADS_PER_GROUP * ATT_TILE, 2 * HEAD_DIM), F32),
                        pltpu.VMEM((NSA_HEADS_PER_GROUP * ATT_TILE, ATT_TILE), F32),
                        pltpu.VMEM((NSA_HEADS_PER_GROUP * ATT_TILE, ATT_TILE), F32)],
        compiler_params=_cparams(("arbitrary", "arbitrary")),
        name="nsa_window_attention",
    )(p1, p1, p1, p3, prev)


def _merge_kernel(ya_ref, yb_ref, yc_ref, g0_ref, g1_ref, g2_ref, wa_ref, wb_ref, wc_ref, o_ref):
    def branch(y_ref, w_ref, g_ref):
        prod = jnp.dot(y_ref[...].astype(MXU), w_ref[...].astype(MXU), preferred_element_type=F32)
        return _sigmoid(g_ref[...]) * prod

    o_ref[...] = (branch(ya_ref, wa_ref, g0_ref) + branch(yb_ref, wb_ref, g1_ref)
                  + branch(yc_ref, wc_ref, g2_ref)).astype(o_ref.dtype)


def _merge(ya, yb, yc, p2, wa, wb, wc, *, tm=512, tn=512):
    s = ya.shape[0]
    d = wa.shape[1]
    g0 = P2_GM // tn
    gstep = d // tn
    return pl.pallas_call(
        _merge_kernel,
        grid=(d // tn, s // tm),
        in_specs=[pl.BlockSpec((tm, ya.shape[1]), lambda j, i: (i, 0)),
                  pl.BlockSpec((tm, yb.shape[1]), lambda j, i: (i, 0)),
                  pl.BlockSpec((tm, yc.shape[1]), lambda j, i: (i, 0)),
                  pl.BlockSpec((tm, tn), lambda j, i: (i, g0 + j)),
                  pl.BlockSpec((tm, tn), lambda j, i: (i, g0 + gstep + j)),
                  pl.BlockSpec((tm, tn), lambda j, i: (i, g0 + 2 * gstep + j)),
                  pl.BlockSpec((wa.shape[0], tn), lambda j, i: (0, j)),
                  pl.BlockSpec((wb.shape[0], tn), lambda j, i: (0, j)),
                  pl.BlockSpec((wc.shape[0], tn), lambda j, i: (0, j))],
        out_specs=pl.BlockSpec((tm, tn), lambda j, i: (i, j)),
        out_shape=jax.ShapeDtypeStruct((s, d), MXU),
        compiler_params=_cparams(("arbitrary", "arbitrary")),
        name="branch_merge",
    )(ya, yb, yc, p2, p2, p2, wa, wb, wc)


def _layer_norm_rows(x, g, b):
    xc = x - jnp.mean(x, axis=1, keepdims=True)
    var = jnp.mean(xc * xc, axis=1, keepdims=True)
    return xc * lax.rsqrt(var + LN_EPS) * g + b


def _pack_halves(y):
    half = y.shape[1] // 2
    bits = lax.bitcast_convert_type(y.astype(jnp.bfloat16).astype(F32), jnp.int32)
    return ((bits[:, :half] >> 16) & jnp.int32(0xFFFF)) | (bits[:, half:] & jnp.int32(-65536))


def _unpack_halves_f32(w):
    return lax.bitcast_convert_type(w << 16, F32), lax.bitcast_convert_type(w & jnp.int32(-65536), F32)


def _unpack_halves(w):
    lo, hi = _unpack_halves_f32(w)
    return lo.astype(MXU), hi.astype(MXU)


def _wout_ln_kernel(m_ref, w_ref, h_ref, g_ref, b_ref, o_ref, ob_ref, op_ref):
    acc = jnp.dot(m_ref[...].astype(MXU), w_ref[...].astype(MXU), preferred_element_type=F32)
    y = _layer_norm_rows(DEEPNORM_ALPHA * h_ref[...] + acc, g_ref[...], b_ref[...])
    o_ref[...] = y
    ob_ref[...] = y.astype(ob_ref.dtype)
    op_ref[...] = _pack_halves(y)


def _wout_ln(merged, w_out, h, ln_g, ln_b, *, tm=256):
    s, d = h.shape
    row = pl.BlockSpec((tm, d), lambda i: (i, 0))
    vec = pl.BlockSpec((1, d), lambda i: (0, 0))
    return pl.pallas_call(
        _wout_ln_kernel,
        grid=(s // tm,),
        in_specs=[row, pl.BlockSpec((d, d), lambda i: (0, 0)), row, vec, vec],
        out_specs=[row, row, pl.BlockSpec((tm, d // 2), lambda i: (i, 0))],
        out_shape=[jax.ShapeDtypeStruct((s, d), F32), jax.ShapeDtypeStruct((s, d), MXU),
                   jax.ShapeDtypeStruct((s, d // 2), jnp.int32)],
        compiler_params=_cparams(("arbitrary",)),
        name="out_proj_layernorm",
    )(merged, w_out.astype(MXU), h, ln_g.reshape(1, d), ln_b.reshape(1, d))


def _dispatch_kernel(ztile_ref, dest_ref, hp_ref, o_ref, zbuf, sem, zsem):
    tt = dest_ref.shape[1]
    tm = zbuf.shape[0]

    @pl.when(pl.program_id(0) == 0)
    def _zero_partial_tiles():
        zbuf[...] = jnp.zeros_like(zbuf)

        def tile_copy(t):
            return pltpu.make_async_copy(zbuf, o_ref.at[pl.ds(t * tm, tm)], zsem)

        def start(j, c):
            @pl.when(ztile_ref[j] >= 0)
            def _():
                tile_copy(ztile_ref[j]).start()
            return c

        def wait(j, c):
            @pl.when(ztile_ref[j] >= 0)
            def _():
                tile_copy(ztile_ref[j]).wait()
            return c

        lax.fori_loop(0, ztile_ref.shape[0], start, 0)
        lax.fori_loop(0, ztile_ref.shape[0], wait, 0)

    def issue(g, c):
        base = pl.multiple_of(g * SUBLANES, SUBLANES)
        for rr in range(SUBLANES):
            for k in range(TOP_K):
                pltpu.make_async_copy(hp_ref.at[pl.ds(base + rr, 1)], o_ref.at[pl.ds(dest_ref[k, base + rr], 1)],
                                      sem).start()
        return c

    lax.fori_loop(0, tt // SUBLANES, issue, 0)
    rows = o_ref.at[pl.ds(0, TOP_K * tt)]
    pltpu.make_async_copy(rows, rows, sem).wait()


def _dispatch(hp, dest, zero_tiles, n_rows, *, tm, tt=256):
    s, half = hp.shape
    grid_spec = pltpu.PrefetchScalarGridSpec(
        num_scalar_prefetch=1,
        grid=(s // tt,),
        in_specs=[pl.BlockSpec((TOP_K, tt), lambda i, zt: (0, i), memory_space=pltpu.SMEM),
                  pl.BlockSpec((tt, half), lambda i, zt: (i, 0))],
        out_specs=pl.BlockSpec(memory_space=pl.ANY),
        scratch_shapes=[pltpu.VMEM((tm, half), jnp.int32), pltpu.SemaphoreType.DMA(()),
                        pltpu.SemaphoreType.DMA(())],
    )
    return pl.pallas_call(
        _dispatch_kernel,
        grid_spec=grid_spec,
        out_shape=jax.ShapeDtypeStruct((n_rows, half), jnp.int32),
        compiler_params=_cparams(("arbitrary",)),
        name="moe_dispatch",
    )(zero_tiles, dest, hp)


def _combine_kernel(dcur_ref, dnxt_ref, w_ref, h_ref, sh_ref, g_ref, b_ref, y_ref, o_ref, ob_ref, buf, sem):
    tt = dcur_ref.shape[1]
    i = pl.program_id(0)
    n = pl.num_programs(0)
    slot = i % 2

    def issue(d_ref, sl):
        def body(g, c):
            base = pl.multiple_of(g * SUBLANES, SUBLANES)
            for rr in range(SUBLANES):
                for k in range(TOP_K):
                    pltpu.make_async_copy(y_ref.at[pl.ds(d_ref[k, base + rr], 1)],
                                          buf.at[sl, pl.ds(k * tt + base + rr, 1)], sem.at[sl]).start()
            return c

        lax.fori_loop(0, tt // SUBLANES, body, 0)

    @pl.when(i == 0)
    def _():
        issue(dcur_ref, slot)

    @pl.when(i + 1 < n)
    def _():
        issue(dnxt_ref, 1 - slot)

    pltpu.make_async_copy(y_ref.at[pl.ds(0, TOP_K * tt)], buf.at[slot], sem.at[slot]).wait()
    w = w_ref[...]
    lo = hi = None
    for k in range(TOP_K):
        ya, yb = _unpack_halves_f32(buf[slot, k * tt:(k + 1) * tt])
        ta, tb = w[:, k:k + 1] * ya, w[:, k:k + 1] * yb
        lo, hi = (ta, tb) if lo is None else (lo + ta, hi + tb)
    routed = jnp.concatenate([lo, hi], axis=1)
    y = _layer_norm_rows(DEEPNORM_ALPHA * h_ref[...] + (routed + sh_ref[...]), g_ref[...], b_ref[...])
    o_ref[...] = y
    ob_ref[...] = y.astype(ob_ref.dtype)


def _combine_ln(h, y_rows, dest, top_w, shared, ln_g, ln_b, *, tt=128):
    s, d = h.shape
    n = s // tt
    row = pl.BlockSpec((tt, d), lambda i: (i, 0))
    vec = pl.BlockSpec((1, d), lambda i: (0, 0))
    return pl.pallas_call(
        _combine_kernel,
        grid=(n,),
        in_specs=[pl.BlockSpec((TOP_K, tt), lambda i: (0, i), memory_space=pltpu.SMEM),
                  pl.BlockSpec((TOP_K, tt), lambda i: (0, jnp.minimum(i + 1, n - 1)), memory_space=pltpu.SMEM),
                  pl.BlockSpec((tt, TOP_K), lambda i: (i, 0)),
                  row, row, vec, vec,
                  pl.BlockSpec(memory_space=pl.ANY)],
        out_specs=[row, row],
        out_shape=[jax.ShapeDtypeStruct((s, d), F32), jax.ShapeDtypeStruct((s, d), MXU)],
        scratch_shapes=[pltpu.VMEM((2, TOP_K * tt, d // 2), jnp.int32), pltpu.SemaphoreType.DMA((2,))],
        compiler_params=_cparams(("arbitrary",)),
        name="moe_combine_layernorm",
    )(dest, dest, top_w.T, h, shared, ln_g.reshape(1, d), ln_b.reshape(1, d), y_rows)


def _router_kernel(h_ref, wr_ref, rb_ref, e_ref, w_ref, pos_ref, cnt_ref, carry_ref):
    tq = h_ref.shape[0]
    i = pl.program_id(0)

    @pl.when(i == 0)
    def _reset():
        carry_ref[...] = jnp.zeros_like(carry_ref)

    logits = lax.dot_general(wr_ref[...].astype(MXU), h_ref[...].astype(MXU), NT, preferred_element_type=F32)
    scores = _sigmoid(logits)
    biased = scores + rb_ref[...]
    G, PG = N_EXPERT_GROUPS, EXPERTS_PER_GROUP
    sub = lax.broadcasted_iota(jnp.int32, (PG, tq), 0).astype(F32)
    gi = lax.broadcasted_iota(jnp.int32, (G, tq), 0).astype(F32)
    gs = jnp.zeros((G, tq), F32)
    for g in range(G):
        blk = biased[g * PG:(g + 1) * PG, :]
        m1 = jnp.max(blk, axis=0, keepdims=True)
        i1 = jnp.min(jnp.where(blk == m1, sub, float(PG)), axis=0, keepdims=True)
        m2 = jnp.max(jnp.where(sub == i1, -jnp.inf, blk), axis=0, keepdims=True)
        gs = jnp.where(gi == float(g), m1 + m2, gs)
    keep = jnp.zeros((G, tq), F32)
    for _ in range(TOPK_GROUPS):
        mx = jnp.max(gs, axis=0, keepdims=True)
        idx = jnp.min(jnp.where(gs == mx, gi, float(G)), axis=0, keepdims=True)
        pick = gi == idx
        keep = jnp.where(pick, 1.0, keep)
        gs = jnp.where(pick, -jnp.inf, gs)
    val = jnp.concatenate(
        [jnp.where(keep[g:g + 1, :] > 0, biased[g * PG:(g + 1) * PG, :], -jnp.inf) for g in range(G)], axis=0)
    ei = lax.broadcasted_iota(jnp.int32, (N_EXPERTS, tq), 0).astype(F32)
    picks, svals = [], []
    sel = jnp.zeros((N_EXPERTS, tq), F32)
    for r in range(TOP_K):
        mx = jnp.max(val, axis=0, keepdims=True)
        idx = jnp.min(jnp.where(val == mx, ei, float(N_EXPERTS)), axis=0, keepdims=True)
        pick = ei == idx
        picks.append(pick)
        svals.append(jnp.sum(jnp.where(pick, scores, 0.0), axis=0, keepdims=True))
        e_ref[r:r + 1, :] = idx.astype(jnp.int32)
        sel = jnp.where(pick, 1.0, sel)
        val = jnp.where(pick, -jnp.inf, val)
    wsum = svals[0]
    for r in range(1, TOP_K):
        wsum = wsum + svals[r]
    tr = lax.broadcasted_iota(jnp.int32, (tq, tq), 0)
    tc = lax.broadcasted_iota(jnp.int32, (tq, tq), 1)
    before = (tr < tc).astype(jnp.bfloat16)
    prefix = jnp.dot(sel.astype(jnp.bfloat16), before, preferred_element_type=F32)
    pos = carry_ref[:, 0:1] + prefix
    for r in range(TOP_K):
        w_ref[r:r + 1, :] = svals[r] / wsum * ROUTED_SCALE
        pos_ref[r:r + 1, :] = jnp.sum(jnp.where(picks[r], pos, 0.0), axis=0, keepdims=True).astype(jnp.int32)
    total = carry_ref[...] + jnp.sum(sel, axis=1, keepdims=True)
    carry_ref[...] = total
    cnt_ref[...] = total


def _router(hb, w_router, router_bias, *, tq=256):
    s, d = hb.shape
    row = pl.BlockSpec((TOP_K, tq), lambda i: (0, i))
    return pl.pallas_call(
        _router_kernel,
        grid=(s // tq,),
        in_specs=[pl.BlockSpec((tq, d), lambda i: (i, 0)),
                  pl.BlockSpec((N_EXPERTS, d), lambda i: (0, 0)),
                  pl.BlockSpec((N_EXPERTS, 1), lambda i: (0, 0))],
        out_specs=[row, row, row, pl.BlockSpec((N_EXPERTS, LANES), lambda i: (0, 0))],
        out_shape=[jax.ShapeDtypeStruct((TOP_K, s), jnp.int32), jax.ShapeDtypeStruct((TOP_K, s), F32),
                   jax.ShapeDtypeStruct((TOP_K, s), jnp.int32), jax.ShapeDtypeStruct((N_EXPERTS, LANES), F32)],
        scratch_shapes=[pltpu.VMEM((N_EXPERTS, LANES), F32)],
        compiler_params=_cparams(("arbitrary",)),
        name="moe_router",
    )(hb, w_router.T, router_bias.reshape(N_EXPERTS, 1))


def _expert_kernel(te_ref, nu_ref, first_ref, slot_ref, nxt_ref, x_ref, wg_hbm, wu_hbm, wd_hbm, o_ref,
                   wg_buf, wu_buf, wd_buf, sem, *, layer, pack_out):
    i = pl.program_id(0)

    def weight_copies(e, sl):
        return (pltpu.make_async_copy(wg_hbm.at[layer, e], wg_buf.at[sl], sem.at[sl]),
                pltpu.make_async_copy(wu_hbm.at[layer, e], wu_buf.at[sl], sem.at[sl]),
                pltpu.make_async_copy(wd_hbm.at[layer, e], wd_buf.at[sl], sem.at[sl]))

    @pl.when(i < nu_ref[0])
    def _compute():
        sl = slot_ref[i]

        @pl.when(i == 0)
        def _first_expert():
            for c in weight_copies(te_ref[i], sl):
                c.start()

        @pl.when(first_ref[i] == 1)
        def _expert_changed():
            for c in weight_copies(te_ref[i], sl):
                c.wait()

            @pl.when(nxt_ref[i] >= 0)
            def _prefetch_next_expert():
                for c in weight_copies(nxt_ref[i], 1 - sl):
                    c.start()

        xa, xb = _unpack_halves(x_ref[...])
        half = xa.shape[1]

        def up(w_buf):
            return (jnp.dot(xa, w_buf[sl, 0:half, :].astype(MXU), preferred_element_type=F32)
                    + jnp.dot(xb, w_buf[sl, half:2 * half, :].astype(MXU), preferred_element_type=F32))

        hg = up(wg_buf)
        hu = up(wu_buf)
        act = (_silu(hg) * hu).astype(MXU)
        y = jnp.dot(act, wd_buf[sl].astype(MXU), preferred_element_type=F32)
        o_ref[...] = _pack_halves(y) if pack_out else y

    @pl.when(i >= nu_ref[0])
    def _unused():
        o_ref[...] = jnp.zeros_like(o_ref)


def _experts(x_rows, tile_e, n_used, w_gate, w_up, w_down, layer, *, tm, pack_out):
    n_rows = x_rows.shape[0]
    n_tiles = n_rows // tm
    d, f = w_gate.shape[2:]
    d_out, out_dtype = (d // 2, jnp.int32) if pack_out else (d, F32)
    idx = jnp.arange(n_tiles, dtype=jnp.int32)
    first = jnp.concatenate([jnp.ones((1,), jnp.int32), (tile_e[1:] != tile_e[:-1]).astype(jnp.int32)])
    slot = ((jnp.cumsum(first) - 1) % 2).astype(jnp.int32)
    used = idx < n_used[0]
    run_end = jnp.sum((tile_e[None, :] <= tile_e[:, None]) & used[None, :], axis=1).astype(jnp.int32)
    nxt = jnp.where(run_end < n_used[0], tile_e[jnp.minimum(run_end, n_tiles - 1)], -1).astype(jnp.int32)
    grid_spec = pltpu.PrefetchScalarGridSpec(
        num_scalar_prefetch=5,
        grid=(n_tiles,),
        in_specs=[pl.BlockSpec((tm, d // 2), lambda i, te, nu, *_: (jnp.minimum(i, nu[0] - 1), 0)),
                  pl.BlockSpec(memory_space=pl.ANY), pl.BlockSpec(memory_space=pl.ANY),
                  pl.BlockSpec(memory_space=pl.ANY)],
        out_specs=pl.BlockSpec((tm, d_out), lambda i, *_: (i, 0)),
        scratch_shapes=[pltpu.VMEM((2, d, f), w_gate.dtype), pltpu.VMEM((2, d, f), w_up.dtype),
                        pltpu.VMEM((2, f, d), w_down.dtype), pltpu.SemaphoreType.DMA((2,))],
    )
    return pl.pallas_call(
        functools.partial(_expert_kernel, layer=layer, pack_out=pack_out),
        grid_spec=grid_spec,
        out_shape=jax.ShapeDtypeStruct((n_rows, d_out), out_dtype),
        compiler_params=_cparams(("arbitrary",), 56),
        name="moe_experts",
    )(tile_e, n_used, first, slot, nxt, x_rows, w_gate, w_up, w_down)


def _moe_ln(h, hb, hp, layer, w_router, router_bias, w_exp_gate, w_exp_up, w_exp_down, w_sh_gate, w_sh_up, w_sh_down,
            ln_g, ln_b):
    s, d = h.shape
    tm = MOE_TILE
    top_e, top_w, top_pos, counts = _router(hb, w_router, router_bias)
    cnt = counts[:, 0].astype(jnp.int32)
    tiles_e = (cnt + tm - 1) // tm
    tile_end = jnp.cumsum(tiles_e)
    row_start = (tile_end - tiles_e) * tm
    experts = jnp.arange(N_EXPERTS, dtype=jnp.int32)
    dest = top_pos + jnp.sum(jnp.where(top_e[..., None] == experts, row_start, 0), axis=-1)
    n_tiles = s * TOP_K // tm + N_EXPERTS
    tile_e = jnp.minimum(jnp.sum(tile_end[None, :] <= jnp.arange(n_tiles, dtype=jnp.int32)[:, None], axis=1),
                         N_EXPERTS - 1).astype(jnp.int32)
    n_used = tile_end[-1:].astype(jnp.int32)
    zero_tiles = jnp.concatenate([jnp.where(tiles_e > 0, tile_end - 1, -1),
                                  jnp.where(n_used + experts < n_tiles, n_used + experts, -1)]).astype(jnp.int32)
    x_rows = _dispatch(hp, dest, zero_tiles, n_tiles * tm, tm=tm)
    y_rows = _experts(x_rows, tile_e, n_used, w_exp_gate, w_exp_up, w_exp_down, layer, tm=tm, pack_out=True)
    shared = _experts(hp, jnp.zeros((s // tm,), jnp.int32), jnp.full((1,), s // tm, jnp.int32),
                      w_sh_gate[:, None], w_sh_up[:, None], w_sh_down[:, None], layer, tm=tm, pack_out=False)
    return _combine_ln(h, y_rows, dest, top_w, shared, ln_g, ln_b)


def _project(hb, wt_all, layer):
    off = np.concatenate([[0], np.cumsum(IN_SPLIT_SIZES)])
    pa = _proj_wt(hb, wt_all, layer, col0=0, ncols=off[3], tm=1024, tn=1024, out_dtype=MXU, q_cols=MOBA_WIDTH)
    ps = _proj_wt(hb, wt_all, layer, col0=off[3], ncols=off[5] - off[3], tm=1024, tn=1024, out_dtype=F32)
    pn = _proj_wt(hb, wt_all, layer, col0=off[6], ncols=off[8] - off[6], tm=1024, tn=512, out_dtype=MXU,
                  q_cols=NSA_WIDTH)
    pg = _proj_wt(hb, wt_all, layer, col0=off[9], ncols=off[10] - off[9], tm=1024, tn=1024, out_dtype=F32)
    p3 = _matmul(hb, _small_proj_weights(wt_all[layer]), tm=1024, tn=P3_COLS, out_dtype=F32)
    return pa, pn, ps, pg, p3


def _mixer(hb, wt_all, layer, conv_w, conv_b, dt_bias, a_log, d_skip, ssm_norm_g,
           cmp_pos_k, cmp_w1_k, cmp_w2_k, cmp_pos_v, cmp_w1_v, cmp_w2_v, w_br_a, w_br_b, w_br_c):
    pa, pn, ps, pg, p3 = _project(hb, wt_all, layer)
    y_a = _moba(pa)
    y_b = _ssd(ps, p3, conv_w, conv_b, dt_bias, a_log, d_skip, ssm_norm_g)
    kvc = _nsa_compress(pn, cmp_pos_k, cmp_w1_k, cmp_w2_k, cmp_pos_v, cmp_w1_v, cmp_w2_v)
    y_c, selb = _nsa_cmp(pn, p3, kvc)
    y_c = _nsa_sel(pn, p3, selb, y_c)
    y_c = _nsa_win(pn, p3, y_c)
    return _merge(y_a, y_b, y_c, pg, w_br_a, w_br_b, w_br_c)


def kernel(x, w_in, conv_w, conv_b, dt_bias, a_log, d_skip, ssm_norm_g, cmp_pos_k, cmp_w1_k, cmp_w2_k, cmp_pos_v, cmp_w1_v, cmp_w2_v, w_br_a, w_br_b, w_br_c, w_out, ln1_g, ln1_b, w_router, router_bias, w_exp_gate, w_exp_up, w_exp_down, w_sh_gate, w_sh_up, w_sh_down, ln2_g, ln2_b):
    bsz, s, d = x.shape
    assert bsz == 1
    h = x.reshape(s, d)
    hb = h.astype(MXU)
    wt_all = jnp.swapaxes(w_in, 1, 2)
    for l in range(w_in.shape[0]):
        merged = _mixer(hb, wt_all, l, conv_w[l], conv_b[l], dt_bias[l], a_log[l], d_skip[l], ssm_norm_g[l],
                        cmp_pos_k[l], cmp_w1_k[l], cmp_w2_k[l], cmp_pos_v[l], cmp_w1_v[l], cmp_w2_v[l],
                        w_br_a[l], w_br_b[l], w_br_c[l])
        h, hb, hp = _wout_ln(merged, w_out[l], h, ln1_g[l], ln1_b[l])
        h, hb = _moe_ln(h, hb, hp, l, w_router[l], router_bias[l], w_exp_gate, w_exp_up, w_exp_down,
                        w_sh_gate, w_sh_up, w_sh_down, ln2_g[l], ln2_b[l])
    return h.reshape(bsz, s, d)
```

```python
import functools

import numpy as np
import jax
import jax.numpy as jnp
from jax import lax
from jax.experimental import pallas as pl
from jax.experimental.pallas import tpu as pltpu

F32 = jnp.float32
MXU = jnp.bfloat16

D_MODEL = 2048
DEPTH = 2
HEAD_DIM = 128
MOBA_HEADS = 8
MOBA_WIDTH = MOBA_HEADS * HEAD_DIM
MOBA_BLOCK = 256
MOBA_TOPK = 3
SSM_D_INNER = D_MODEL
SSM_HEAD_DIM = 64
SSM_HEADS = SSM_D_INNER // SSM_HEAD_DIM
SSM_STATE = 128
SSM_GROUPS = 8
SSM_HEADS_PER_GROUP = SSM_HEADS // SSM_GROUPS
SSM_GROUP_WIDTH = SSM_D_INNER // SSM_GROUPS
SSM_CONV = 4
SSM_CHUNK = 256
SSM_CONV_CH = SSM_D_INNER + 2 * SSM_GROUPS * SSM_STATE
NSA_HEADS = 8
NSA_KV_GROUPS = 2
NSA_HEADS_PER_GROUP = NSA_HEADS // NSA_KV_GROUPS
NSA_WIDTH = NSA_HEADS * HEAD_DIM
NSA_KV_WIDTH = NSA_KV_GROUPS * HEAD_DIM
NSA_CMP_LEN = 32
NSA_CMP_STRIDE = 16
NSA_SEL_BLOCK = 64
NSA_TOPN = 16
NSA_WINDOW = 512
N_BRANCHES = 3
N_EXPERTS = 64
N_EXPERT_GROUPS = 8
EXPERTS_PER_GROUP = N_EXPERTS // N_EXPERT_GROUPS
TOPK_GROUPS = 4
TOP_K = 8
D_EXPERT = 512
ROUTED_SCALE = 2.5
DEEPNORM_ALPHA = (2 * DEPTH) ** 0.25
LN_EPS = 1e-5
RMS_EPS = 1e-5
IN_SPLIT_SIZES = (MOBA_WIDTH, MOBA_WIDTH, MOBA_WIDTH,
                  SSM_D_INNER, SSM_CONV_CH, SSM_HEADS,
                  NSA_WIDTH, 6 * NSA_KV_WIDTH, N_BRANCHES * NSA_HEADS,
                  N_BRANCHES * D_MODEL)

LANES = 128
SUBLANES = 8
ATT_TILE = 256
MOE_TILE = 256
ATT_SCALE = HEAD_DIM ** -0.5
Q_PRESCALE = ATT_SCALE * 1.4426950408889634
MASK_BIAS = -2.0 ** 30
NEG = -1e30

NT = (((1,), (1,)), ((), ()))

P1_QA, P1_KA, P1_VA = 0, 1024, 2048
P1_QN, P1_KVN = 0, 1024
P2_Z, P2_XBC = 0, 2048
P2_GM = 0
P3_COLS = (SSM_GROUPS + NSA_KV_GROUPS) * LANES


def _cparams(semantics, vmem_mb=48):
    return pltpu.CompilerParams(dimension_semantics=semantics, vmem_limit_bytes=vmem_mb * 1024 * 1024)


def _sigmoid(x):
    return 1.0 / (1.0 + jnp.exp(-x))


def _silu(x):
    return x * _sigmoid(x)


def _split3(x):
    hi = x.astype(jnp.bfloat16)
    r1 = x - hi.astype(F32)
    mid = r1.astype(jnp.bfloat16)
    lo = (r1 - mid.astype(F32)).astype(jnp.bfloat16)
    return hi, mid, lo


def _dot3(a_exact, x, left=True):
    acc = None
    for part in _split3(x):
        t = (jnp.dot(a_exact, part, preferred_element_type=F32) if left
             else jnp.dot(part, a_exact, preferred_element_type=F32))
        acc = t if acc is None else acc + t
    return acc


def _mm_kernel(a_ref, b_ref, o_ref):
    o_ref[...] = jnp.dot(a_ref[...].astype(MXU), b_ref[...].astype(MXU),
                         preferred_element_type=F32).astype(o_ref.dtype)


def _matmul(a, b, *, tm, tn, out_dtype):
    m, k = a.shape
    n = b.shape[1]
    assert m % tm == 0 and n % tn == 0
    return pl.pallas_call(
        _mm_kernel,
        grid=(n // tn, m // tm),
        in_specs=[pl.BlockSpec((tm, k), lambda j, i: (i, 0)),
                  pl.BlockSpec((k, tn), lambda j, i: (0, j))],
        out_specs=pl.BlockSpec((tm, tn), lambda j, i: (i, j)),
        out_shape=jax.ShapeDtypeStruct((m, n), out_dtype),
        compiler_params=_cparams(("arbitrary", "arbitrary"), 56),
        name="proj_matmul",
    )(a, b)


def _mm_wt_kernel(a_ref, wt_ref, o_ref, wb_ref, *, q_blocks):
    @pl.when(pl.program_id(1) == 0)
    def _cast_weights():
        wb_ref[...] = wt_ref[0].astype(MXU)

    acc = lax.dot_general(a_ref[...].astype(MXU), wb_ref[...], NT, preferred_element_type=F32)
    if q_blocks:
        acc = acc * jnp.where(pl.program_id(0) < q_blocks, Q_PRESCALE, 1.0)
    o_ref[...] = acc.astype(o_ref.dtype)


def _proj_wt(a, wt_all, layer, *, col0, ncols, tm, tn, out_dtype, q_cols=0):
    m, k = a.shape
    assert m % tm == 0 and ncols % tn == 0 and col0 % SUBLANES == 0 and q_cols % tn == 0
    return pl.pallas_call(
        functools.partial(_mm_wt_kernel, q_blocks=q_cols // tn),
        grid=(ncols // tn, m // tm),
        in_specs=[pl.BlockSpec((tm, k), lambda j, i: (i, 0)),
                  pl.BlockSpec((pl.Element(1), pl.Element(tn), pl.Element(k)),
                               lambda j, i: (layer, pl.multiple_of(int(col0) + j * tn, SUBLANES), 0))],
        out_specs=pl.BlockSpec((tm, tn), lambda j, i: (i, j)),
        out_shape=jax.ShapeDtypeStruct((m, ncols), out_dtype),
        scratch_shapes=[pltpu.VMEM((tn, k), MXU)],
        compiler_params=_cparams(("arbitrary", "arbitrary"), 56),
        name="proj_matmul_wt",
    )(a, wt_all)


def _small_proj_weights(wt):
    off = np.concatenate([[0], np.cumsum(IN_SPLIT_SIZES)])
    dt = wt[off[5]:off[6]].T
    gn = wt[off[8]:off[9]].T
    d = dt.shape[0]
    dtp = jnp.pad(dt.reshape(d, SSM_GROUPS, SSM_HEADS_PER_GROUP),
                  ((0, 0), (0, 0), (0, LANES - SSM_HEADS_PER_GROUP))).reshape(d, SSM_GROUPS * LANES)
    ng = NSA_HEADS_PER_GROUP * N_BRANCHES
    gnp = jnp.pad(gn.reshape(d, NSA_KV_GROUPS, ng), ((0, 0), (0, 0), (0, LANES - ng))).reshape(d, NSA_KV_GROUPS * LANES)
    return jnp.concatenate([dtp, gnp], axis=1).astype(MXU)


def _flash_init(m_ref, acc_ref):
    m_ref[...] = jnp.full(m_ref.shape, NEG, F32)
    acc_ref[...] = jnp.zeros(acc_ref.shape, F32)


def _flash_step(s, v_aug, m_ref, acc_ref):
    hd = HEAD_DIM
    m_prev = m_ref[...]
    m_new = jnp.maximum(m_prev, jnp.max(s, axis=1, keepdims=True))
    alpha = jnp.exp2(m_prev - m_new)
    p = jnp.concatenate([jnp.exp2(s[:, c * hd:(c + 1) * hd] - m_new) for c in range(s.shape[1] // hd)], axis=1)
    pv = jnp.dot(p.astype(MXU), v_aug, preferred_element_type=F32)
    acc_ref[:, 0:hd] = alpha * acc_ref[:, 0:hd] + pv[:, 0:hd]
    acc_ref[:, hd:2 * hd] = alpha * acc_ref[:, hd:2 * hd] + pv[:, hd:2 * hd]
    m_ref[...] = m_new


def _flash_out(acc_ref):
    return acc_ref[:, 0:HEAD_DIM] / acc_ref[:, HEAD_DIM:2 * HEAD_DIM]


def _flash_loop(n, scores, values, sa_ref, sb_ref, m_ref, acc_ref):
    @pl.when(n > 0)
    def _first():
        sa_ref[...] = scores(0)

    def pair(u, c):
        t = 2 * u
        sb_ref[...] = scores(t + 1)
        _flash_step(sa_ref[...], values(t), m_ref, acc_ref)
        sa_ref[...] = scores(t + 2)
        _flash_step(sb_ref[...], values(t + 1), m_ref, acc_ref)
        return c

    n_pairs = jnp.maximum(n - 1, 0) // 2
    lax.fori_loop(0, n_pairs, pair, 0)
    t0 = 2 * n_pairs
    left = n - t0

    @pl.when(left == 2)
    def _last_two():
        sb_ref[...] = scores(t0 + 1)
        _flash_step(sa_ref[...], values(t0), m_ref, acc_ref)
        _flash_step(sb_ref[...], values(t0 + 1), m_ref, acc_ref)

    @pl.when(left == 1)
    def _last_one():
        _flash_step(sa_ref[...], values(t0), m_ref, acc_ref)


MOBA_KV_GROUP = 4


def _moba_kernel(q_ref, k_ref, v_ref, o_ref, kaug_ref, vaug_ref, kmean_ref, m_ref, acc_ref, sa_ref, sb_ref, *, nb):
    blk = MOBA_BLOCK
    grp = MOBA_KV_GROUP * blk
    qi = pl.program_id(1)

    @pl.when(qi == 0)
    def _build_keys():
        kmean_ref[...] = jnp.zeros_like(kmean_ref)
        lane = lax.broadcasted_iota(jnp.int32, (blk, LANES), 1)
        ones = jnp.ones((blk, HEAD_DIM), vaug_ref.dtype)

        def body(j, c):
            r0 = pl.multiple_of(j * blk, blk)
            kb = k_ref[pl.ds(r0, blk), :]
            kaug_ref[pl.ds(r0, blk), 0:HEAD_DIM] = kb.astype(kaug_ref.dtype)
            kaug_ref[pl.ds(r0, blk), HEAD_DIM:2 * HEAD_DIM] = (lane == j).astype(kaug_ref.dtype)
            vaug_ref[pl.ds(r0, blk), 0:HEAD_DIM] = v_ref[pl.ds(r0, blk), :].astype(vaug_ref.dtype)
            vaug_ref[pl.ds(r0, blk), HEAD_DIM:2 * HEAD_DIM] = ones
            kmean_ref[pl.ds(j, 1), :] = jnp.sum(kb.astype(F32), axis=0, keepdims=True) * (1.0 / blk)
            return c

        lax.fori_loop(0, nb, body, 0)

    q = q_ref[...].astype(MXU)
    gate = lax.dot_general(kmean_ref[...].astype(MXU), q, NT, preferred_element_type=F32)
    kblk = lax.broadcasted_iota(jnp.int32, (LANES, blk), 0).astype(F32)
    past = kblk < qi.astype(F32)
    gate = jnp.where(past, gate, -jnp.inf)
    sel = jnp.where(kblk == qi.astype(F32), 1.0, 0.0)
    for _ in range(MOBA_TOPK):
        mx = jnp.max(gate, axis=0, keepdims=True)
        idx = jnp.min(jnp.where((gate == mx) & past, kblk, float(LANES)), axis=0, keepdims=True)
        pick = kblk == idx
        sel = jnp.where(pick, 1.0, sel)
        gate = jnp.where(pick, -jnp.inf, gate)
    bias = jnp.where(sel.T > 0, 0.0, MASK_BIAS).astype(MXU)
    qaug = jnp.concatenate([q, bias], axis=1)
    rel = (qi * blk + lax.broadcasted_iota(jnp.int32, (blk, grp), 0)
           - lax.broadcasted_iota(jnp.int32, (blk, grp), 1))

    _flash_init(m_ref, acc_ref)

    def scores(g):
        r0 = pl.multiple_of(g * grp, grp)
        s = lax.dot_general(qaug, kaug_ref[pl.ds(r0, grp), :], NT, preferred_element_type=F32)
        return jnp.where(rel >= g * grp, s, NEG)

    def values(g):
        return vaug_ref[pl.ds(pl.multiple_of(g * grp, grp), grp), :]

    _flash_loop(qi // MOBA_KV_GROUP + 1, scores, values, sa_ref, sb_ref, m_ref, acc_ref)
    o_ref[...] = _flash_out(acc_ref).astype(o_ref.dtype)


def _moba(p1):
    s = p1.shape[0]
    nb = s // MOBA_BLOCK
    assert nb <= LANES and nb % MOBA_KV_GROUP == 0
    kcol, vcol = P1_KA // HEAD_DIM, P1_VA // HEAD_DIM
    return pl.pallas_call(
        functools.partial(_moba_kernel, nb=nb),
        grid=(MOBA_HEADS, nb),
        in_specs=[pl.BlockSpec((MOBA_BLOCK, HEAD_DIM), lambda h, i: (i, h)),
                  pl.BlockSpec((s, HEAD_DIM), lambda h, i: (0, kcol + h)),
                  pl.BlockSpec((s, HEAD_DIM), lambda h, i: (0, vcol + h))],
        out_specs=pl.BlockSpec((MOBA_BLOCK, HEAD_DIM), lambda h, i: (i, h)),
        out_shape=jax.ShapeDtypeStruct((s, MOBA_WIDTH), MXU),
        scratch_shapes=[pltpu.VMEM((s, 2 * HEAD_DIM), MXU), pltpu.VMEM((s, 2 * HEAD_DIM), MXU),
                        pltpu.VMEM((LANES, HEAD_DIM), F32),
                        pltpu.VMEM((MOBA_BLOCK, HEAD_DIM), F32), pltpu.VMEM((MOBA_BLOCK, 2 * HEAD_DIM), F32),
                        pltpu.VMEM((MOBA_BLOCK, MOBA_KV_GROUP * MOBA_BLOCK), F32),
                        pltpu.VMEM((MOBA_BLOCK, MOBA_KV_GROUP * MOBA_BLOCK), F32)],
        compiler_params=_cparams(("arbitrary", "arbitrary")),
        name="moba_attention",
    )(p1, p1, p1)


def _ssd_kernel(dtb_ref, alog_ref, dskip_ref,
                xs_ref, bm_ref, cm_ref, z_ref, dt_ref, cwx_ref, cwb_ref, cwc_ref, cbx_ref, cbb_ref, cbc_ref,
                ng_ref, o_ref, xbuf, bbuf, cbuf, state_ref, ybuf):
    L = SSM_CHUNK
    W = SSM_GROUP_WIDTH
    J = SSM_HEADS_PER_GROUP
    P = SSM_HEAD_DIM
    g = pl.program_id(0)
    c = pl.program_id(1)

    @pl.when(c == 0)
    def _reset():
        xbuf[0:SUBLANES, :] = jnp.zeros((SUBLANES, W), F32)
        bbuf[0:SUBLANES, :] = jnp.zeros((SUBLANES, SSM_STATE), F32)
        cbuf[0:SUBLANES, :] = jnp.zeros((SUBLANES, SSM_STATE), F32)
        state_ref[...] = jnp.zeros_like(state_ref)

    def conv_silu(buf, raw_ref, w_ref, b_ref):
        buf[SUBLANES:SUBLANES + L, :] = raw_ref[...]
        acc = b_ref[...]
        for i in range(SSM_CONV):
            lo = SUBLANES - (SSM_CONV - 1) + i
            acc = acc + w_ref[i:i + 1, :] * buf[lo:lo + L, :]
        buf[0:SUBLANES, :] = buf[L:L + SUBLANES, :]
        return _silu(acc)

    xs = conv_silu(xbuf, xs_ref, cwx_ref, cbx_ref)
    bm = conv_silu(bbuf, bm_ref, cwb_ref, cbb_ref)
    cm = conv_silu(cbuf, cm_ref, cwc_ref, cbc_ref)

    lane = lax.broadcasted_iota(jnp.int32, (1, LANES), 1)
    dtb = jnp.zeros((1, LANES), F32)
    alog = jnp.full((1, LANES), -jnp.inf, F32)
    for j in range(J):
        dtb = jnp.where(lane == j, dtb_ref[g * J + j], dtb)
        alog = jnp.where(lane == j, alog_ref[g * J + j], alog)
    x = dt_ref[...] + dtb
    dt = jnp.maximum(x, 0.0) + jnp.log(1.0 + jnp.exp(-jnp.abs(x)))
    dt = jnp.where(lane < J, dt, 0.0)
    a = dt * (-jnp.exp(alog))

    er = lax.broadcasted_iota(jnp.int32, (LANES, W), 0)
    ec = lax.broadcasted_iota(jnp.int32, (LANES, W), 1)
    expand = ((ec >> 6) == er).astype(jnp.bfloat16)
    tr = lax.broadcasted_iota(jnp.int32, (L, L), 0)
    tc = lax.broadcasted_iota(jnp.int32, (L, L), 1)
    tril = tr >= tc
    tril_b = tril.astype(jnp.bfloat16)
    dt_e = _dot3(expand, dt, left=False)
    a_e = _dot3(expand, a, left=False)
    acs = _dot3(tril_b, a_e, left=True)
    acs_t = acs.T
    a_last = acs[L - 1:L, :]

    xdt = xs * dt_e
    cb = lax.dot_general(cm.astype(MXU), bm.astype(MXU), NT, preferred_element_type=F32)
    for j in range(J):
        colv = acs[:, j * P:j * P + 1]
        rowv = acs_t[j * P:j * P + 1, :]
        dec = jnp.exp(jnp.where(tril, colv - rowv, -jnp.inf))
        ybuf[:, j * P:(j + 1) * P] = jnp.dot((cb * dec).astype(MXU), xdt[:, j * P:(j + 1) * P].astype(MXU),
                                             preferred_element_type=F32)
    st_old = state_ref[...]
    xdte = (xdt * jnp.exp(a_last - acs)).astype(MXU)
    st_new = jnp.dot(bm.T.astype(MXU), xdte, preferred_element_type=F32)
    y_off = jnp.dot(cm.astype(MXU), st_old.astype(MXU), preferred_element_type=F32) * jnp.exp(acs)
    state_ref[...] = st_old * jnp.exp(a_last) + st_new

    lane_w = lax.broadcasted_iota(jnp.int32, (1, W), 1)
    dsk = jnp.zeros((1, W), F32)
    for j in range(J):
        dsk = jnp.where((lane_w >> 6) == j, dskip_ref[g * J + j], dsk)
    y = ybuf[...] + y_off + xs * dsk
    y = y * _silu(z_ref[...])
    y = y * lax.rsqrt(jnp.mean(y * y, axis=1, keepdims=True) + RMS_EPS)
    o_ref[...] = (y * ng_ref[...]).astype(o_ref.dtype)


def _ssd(p2, p3, conv_w, conv_b, dt_bias, a_log, d_skip, norm_g):
    s = p2.shape[0]
    L, W, N = SSM_CHUNK, SSM_GROUP_WIDTH, SSM_STATE
    xs0 = P2_XBC // W
    bm0 = (P2_XBC + SSM_D_INNER) // N
    cm0 = bm0 + SSM_GROUPS
    z0 = P2_Z // W
    cb2 = conv_b.reshape(1, SSM_CONV_CH)
    ng2 = norm_g.reshape(1, SSM_D_INNER)
    grid_spec = pltpu.PrefetchScalarGridSpec(
        num_scalar_prefetch=3,
        grid=(SSM_GROUPS, s // L),
        in_specs=[
            pl.BlockSpec((L, W), lambda g, c, *_: (c, xs0 + g)),
            pl.BlockSpec((L, N), lambda g, c, *_: (c, bm0 + g)),
            pl.BlockSpec((L, N), lambda g, c, *_: (c, cm0 + g)),
            pl.BlockSpec((L, W), lambda g, c, *_: (c, z0 + g)),
            pl.BlockSpec((L, LANES), lambda g, c, *_: (c, g)),
            pl.BlockSpec((SSM_CONV, W), lambda g, c, *_: (0, g)),
            pl.BlockSpec((SSM_CONV, N), lambda g, c, *_: (0, SSM_D_INNER // N + g)),
            pl.BlockSpec((SSM_CONV, N), lambda g, c, *_: (0, SSM_D_INNER // N + SSM_GROUPS + g)),
            pl.BlockSpec((1, W), lambda g, c, *_: (0, g)),
            pl.BlockSpec((1, N), lambda g, c, *_: (0, SSM_D_INNER // N + g)),
            pl.BlockSpec((1, N), lambda g, c, *_: (0, SSM_D_INNER // N + SSM_GROUPS + g)),
            pl.BlockSpec((1, W), lambda g, c, *_: (0, g)),
        ],
        out_specs=pl.BlockSpec((L, W), lambda g, c, *_: (c, g)),
        scratch_shapes=[pltpu.VMEM((SUBLANES + L, W), F32), pltpu.VMEM((SUBLANES + L, N), F32),
                        pltpu.VMEM((SUBLANES + L, N), F32), pltpu.VMEM((N, W), F32), pltpu.VMEM((L, W), F32)],
    )
    return pl.pallas_call(
        _ssd_kernel,
        grid_spec=grid_spec,
        out_shape=jax.ShapeDtypeStruct((s, SSM_D_INNER), MXU),
        compiler_params=_cparams(("arbitrary", "arbitrary")),
        name="ssd_mixer",
    )(dt_bias, a_log, d_skip, p2, p2, p2, p2, p3, conv_w, conv_w, conv_w, cb2, cb2, cb2, ng2)


def _nsa_compress_kernel(t_ref, pos_ref, w1_ref, w2_ref, o_ref):
    half = NSA_CMP_STRIDE * HEAD_DIM
    t = t_ref[0].astype(F32)
    pos = pos_ref[0]
    lo = (t + pos[:, :half]).astype(MXU)
    hi = (t + pos[:, half:]).astype(MXU)
    w1 = w1_ref[0]
    a = jnp.dot(lo, w1[:half].astype(MXU), preferred_element_type=F32)
    b = jnp.dot(hi, w1[half:].astype(MXU), preferred_element_type=F32)
    n = t.shape[0]
    pre = a + pltpu.roll(b, n - 1, 0)
    act = jax.nn.gelu(pre, approximate=True)
    o_ref[0] = jnp.dot(act.astype(MXU), w2_ref[0].astype(MXU), preferred_element_type=F32).astype(o_ref.dtype)


def _nsa_compress(p1, pos_k, w1_k, w2_k, pos_v, w1_v, w2_v):
    s = p1.shape[0]
    n_str = s // NSA_CMP_STRIDE
    kv = p1[:, P1_KVN:P1_KVN + 2 * NSA_KV_WIDTH]
    t4 = kv.reshape(s, 4, HEAD_DIM).transpose(1, 0, 2).reshape(4, n_str, NSA_CMP_STRIDE * HEAD_DIM)
    pos = jnp.stack([pos_k, pos_v]).reshape(2, 1, NSA_CMP_LEN * HEAD_DIM)
    w1 = jnp.stack([w1_k, w1_v])
    w2 = jnp.stack([w2_k, w2_v])
    return pl.pallas_call(
        _nsa_compress_kernel,
        grid=(4,),
        in_specs=[pl.BlockSpec((1, n_str, NSA_CMP_STRIDE * HEAD_DIM), lambda i: (i, 0, 0)),
                  pl.BlockSpec((1, 1, NSA_CMP_LEN * HEAD_DIM), lambda i: (i // 2, 0, 0)),
                  pl.BlockSpec((1, NSA_CMP_LEN * HEAD_DIM, HEAD_DIM), lambda i: (i // 2, 0, 0)),
                  pl.BlockSpec((1, HEAD_DIM, HEAD_DIM), lambda i: (i // 2, 0, 0))],
        out_specs=pl.BlockSpec((1, n_str, HEAD_DIM), lambda i: (i, 0, 0)),
        out_shape=jax.ShapeDtypeStruct((4, n_str, HEAD_DIM), MXU),
        compiler_params=_cparams(("arbitrary",)),
        name="nsa_compress",
    )(t4, pos, w1, w2)


def _nsa_cmp_kernel(q_ref, kc_ref, vc_ref, gn_ref, o_ref, selb_ref, *, n_sel):
    tq = ATT_TILE
    qi = pl.program_id(1)
    kc = kc_ref[0].astype(MXU)
    vc = vc_ref[0].astype(MXU)
    nc = kc.shape[0]
    pos = qi * tq + lax.broadcasted_iota(jnp.int32, (tq, nc), 0)
    cidx = lax.broadcasted_iota(jnp.int32, (tq, nc), 1)
    valid = (cidx * NSA_CMP_STRIDE + NSA_CMP_LEN - 1 <= pos) & (cidx < nc - 1)
    c_start = lax.broadcasted_iota(jnp.int32, (LANES, nc), 1) * NSA_CMP_STRIDE
    s_start = lax.broadcasted_iota(jnp.int32, (LANES, nc), 0) * NSA_SEL_BLOCK
    overlap_t = ((c_start < s_start + NSA_SEL_BLOCK) & (c_start + NSA_CMP_LEN > s_start)).astype(MXU)
    gates = _sigmoid(gn_ref[...])
    imp = jnp.zeros((LANES, tq), F32)
    for j in range(NSA_HEADS_PER_GROUP):
        q = q_ref[:, j * HEAD_DIM:(j + 1) * HEAD_DIM].astype(MXU)
        s = lax.dot_general(q, kc, NT, preferred_element_type=F32)
        s = jnp.where(valid, s, -jnp.inf)
        m = jnp.max(s, axis=1, keepdims=True)
        m = jnp.where(m > -jnp.inf, m, 0.0)
        e = jnp.exp2(s - m)
        den = jnp.sum(e, axis=1, keepdims=True)
        p = (e * (1.0 / jnp.where(den > 0, den, 1.0))).astype(MXU)
        o = jnp.dot(p, vc, preferred_element_type=F32)
        imp = imp + lax.dot_general(overlap_t, p, NT, preferred_element_type=F32)
        o_ref[:, j * HEAD_DIM:(j + 1) * HEAD_DIM] = o * gates[:, N_BRANCHES * j:N_BRANCHES * j + 1]

    blk = lax.broadcasted_iota(jnp.int32, (LANES, tq), 0).astype(F32)
    cur = ((qi * tq + lax.broadcasted_iota(jnp.int32, (LANES, tq), 1)) >> 6).astype(F32)
    allowed = blk <= cur
    forced = (blk == 0.0) | (blk == cur) | (blk == cur - 1.0)
    val = jnp.where(forced, jnp.inf, jnp.where(allowed, imp, -jnp.inf))
    val = jnp.where(blk < float(n_sel), val, -jnp.inf)

    def pick_round(_, c):
        val, sel = c
        mx = jnp.max(val, axis=0, keepdims=True)
        idx = jnp.min(jnp.where(val == mx, blk, float(LANES)), axis=0, keepdims=True)
        pick = blk == idx
        sel = jnp.where(pick & allowed, 1.0, sel)
        val = jnp.where(pick, -jnp.inf, val)
        return val, sel

    _, sel = lax.fori_loop(0, min(NSA_TOPN, n_sel), pick_round, (val, jnp.zeros((LANES, tq), F32)))
    selb_ref[0] = jnp.where(sel.T > 0, 0.0, MASK_BIAS).astype(selb_ref.dtype)


def _nsa_cmp(p1, p3, kvc):
    s = p1.shape[0]
    n_str = s // NSA_CMP_STRIDE
    n_sel = s // NSA_SEL_BLOCK
    assert n_sel <= LANES
    gw = NSA_HEADS_PER_GROUP * HEAD_DIM
    q0 = P1_QN // gw
    return pl.pallas_call(
        functools.partial(_nsa_cmp_kernel, n_sel=n_sel),
        grid=(NSA_KV_GROUPS, s // ATT_TILE),
        in_specs=[pl.BlockSpec((ATT_TILE, gw), lambda g, i: (i, q0 + g)),
                  pl.BlockSpec((1, n_str, HEAD_DIM), lambda g, i: (g, 0, 0)),
                  pl.BlockSpec((1, n_str, HEAD_DIM), lambda g, i: (NSA_KV_GROUPS + g, 0, 0)),
                  pl.BlockSpec((ATT_TILE, LANES), lambda g, i: (i, SSM_GROUPS + g))],
        out_specs=[pl.BlockSpec((ATT_TILE, gw), lambda g, i: (i, g)),
                   pl.BlockSpec((1, ATT_TILE, LANES), lambda g, i: (g, i, 0))],
        out_shape=[jax.ShapeDtypeStruct((s, NSA_WIDTH), F32),
                   jax.ShapeDtypeStruct((NSA_KV_GROUPS, s, LANES), MXU)],
        compiler_params=_cparams(("arbitrary", "arbitrary")),
        name="nsa_compressed_attention",
    )(p1, kvc, kvc, p3)


def _nsa_sel_kernel(q_ref, selb_ref, k_ref, v_ref, gn_ref, prev_ref, o_ref, kaug_ref, vaug_ref, qaug_ref,
                    m_ref, acc_ref, sa_ref, sb_ref, *, n_tiles):
    tq = ATT_TILE
    J = NSA_HEADS_PER_GROUP
    qi = pl.program_id(1)

    @pl.when(qi == 0)
    def _build_keys():
        lane = lax.broadcasted_iota(jnp.int32, (tq, LANES), 1)
        rowi = lax.broadcasted_iota(jnp.int32, (tq, LANES), 0)
        ones = jnp.ones((tq, HEAD_DIM), vaug_ref.dtype)

        def body(t, c):
            r0 = pl.multiple_of(t * tq, tq)
            kaug_ref[pl.ds(r0, tq), 0:HEAD_DIM] = k_ref[pl.ds(r0, tq), :].astype(kaug_ref.dtype)
            kaug_ref[pl.ds(r0, tq), HEAD_DIM:2 * HEAD_DIM] = (
                lane == ((t * tq + rowi) >> 6)).astype(kaug_ref.dtype)
            vaug_ref[pl.ds(r0, tq), 0:HEAD_DIM] = v_ref[pl.ds(r0, tq), :].astype(vaug_ref.dtype)
            vaug_ref[pl.ds(r0, tq), HEAD_DIM:2 * HEAD_DIM] = ones
            return c

        lax.fori_loop(0, n_tiles, body, 0)

    selb = selb_ref[0]
    for j in range(J):
        qaug_ref[j * tq:(j + 1) * tq, 0:HEAD_DIM] = q_ref[:, j * HEAD_DIM:(j + 1) * HEAD_DIM].astype(qaug_ref.dtype)
        qaug_ref[j * tq:(j + 1) * tq, HEAD_DIM:2 * HEAD_DIM] = selb
    _flash_init(m_ref, acc_ref)

    def scores(t):
        r0 = pl.multiple_of(t * tq, tq)
        return lax.dot_general(qaug_ref[...], kaug_ref[pl.ds(r0, tq), :], NT, preferred_element_type=F32)

    def values(t):
        return vaug_ref[pl.ds(pl.multiple_of(t * tq, tq), tq), :]

    _flash_loop(qi, scores, values, sa_ref, sb_ref, m_ref, acc_ref)
    row = lax.broadcasted_iota(jnp.int32, (J * tq, tq), 0) & (tq - 1)
    col = lax.broadcasted_iota(jnp.int32, (J * tq, tq), 1)
    _flash_step(jnp.where(col <= row, scores(qi), NEG), values(qi), m_ref, acc_ref)
    o = _flash_out(acc_ref)
    gates = _sigmoid(gn_ref[...])
    for j in range(J):
        o_ref[:, j * HEAD_DIM:(j + 1) * HEAD_DIM] = (
            prev_ref[:, j * HEAD_DIM:(j + 1) * HEAD_DIM]
            + o[j * tq:(j + 1) * tq] * gates[:, N_BRANCHES * j + 1:N_BRANCHES * j + 2])


def _nsa_sel(p1, p3, selb, prev):
    s = p1.shape[0]
    gw = NSA_HEADS_PER_GROUP * HEAD_DIM
    q0 = P1_QN // gw
    k0 = (P1_KVN + 2 * NSA_KV_WIDTH) // HEAD_DIM
    v0 = k0 + NSA_KV_GROUPS
    return pl.pallas_call(
        functools.partial(_nsa_sel_kernel, n_tiles=s // ATT_TILE),
        grid=(NSA_KV_GROUPS, s // ATT_TILE),
        in_specs=[pl.BlockSpec((ATT_TILE, gw), lambda g, i: (i, q0 + g)),
                  pl.BlockSpec((1, ATT_TILE, LANES), lambda g, i: (g, i, 0)),
                  pl.BlockSpec((s, HEAD_DIM), lambda g, i: (0, k0 + g)),
                  pl.BlockSpec((s, HEAD_DIM), lambda g, i: (0, v0 + g)),
                  pl.BlockSpec((ATT_TILE, LANES), lambda g, i: (i, SSM_GROUPS + g)),
                  pl.BlockSpec((ATT_TILE, gw), lambda g, i: (i, g))],
        out_specs=pl.BlockSpec((ATT_TILE, gw), lambda g, i: (i, g)),
        out_shape=jax.ShapeDtypeStruct((s, NSA_WIDTH), F32),
        scratch_shapes=[pltpu.VMEM((s, 2 * HEAD_DIM), MXU), pltpu.VMEM((s, 2 * HEAD_DIM), MXU),
                        pltpu.VMEM((NSA_HEADS_PER_GROUP * ATT_TILE, 2 * HEAD_DIM), MXU),
                        pltpu.VMEM((NSA_HEADS_PER_GROUP * ATT_TILE, HEAD_DIM), F32),
                        pltpu.VMEM((NSA_HEADS_PER_GROUP * ATT_TILE, 2 * HEAD_DIM), F32),
                        pltpu.VMEM((NSA_HEADS_PER_GROUP * ATT_TILE, ATT_TILE), F32),
                        pltpu.VMEM((NSA_HEADS_PER_GROUP * ATT_TILE, ATT_TILE), F32)],
        compiler_params=_cparams(("arbitrary", "arbitrary")),
        name="nsa_selected_attention",
    )(p1, selb, p1, p1, p3, prev)


def _nsa_win_kernel(q_ref, k_ref, v_ref, gn_ref, prev_ref, o_ref, qst_ref, m_ref, acc_ref, sa_ref, sb_ref):
    tq = ATT_TILE
    J = NSA_HEADS_PER_GROUP
    halo = NSA_WINDOW // tq
    qi = pl.program_id(1)
    for j in range(J):
        qst_ref[j * tq:(j + 1) * tq, :] = q_ref[:, j * HEAD_DIM:(j + 1) * HEAD_DIM].astype(qst_ref.dtype)
    row = lax.broadcasted_iota(jnp.int32, (J * tq, tq), 0) & (tq - 1)
    col = lax.broadcasted_iota(jnp.int32, (J * tq, tq), 1)
    ones = jnp.ones((tq, HEAD_DIM), MXU)
    _flash_init(m_ref, acc_ref)
    first = jnp.maximum(qi - halo, 0)

    def scores(t):
        kt = first + t
        r0 = pl.multiple_of(kt * tq, tq)
        s = lax.dot_general(qst_ref[...], k_ref[pl.ds(r0, tq), :].astype(MXU), NT, preferred_element_type=F32)
        diff = (qi - kt) * tq + row - col
        return jnp.where((diff >= 0) & (diff < NSA_WINDOW), s, NEG)

    def values(t):
        r0 = pl.multiple_of((first + t) * tq, tq)
        return jnp.concatenate([v_ref[pl.ds(r0, tq), :].astype(MXU), ones], axis=1)

    _flash_loop(qi + 1 - first, scores, values, sa_ref, sb_ref, m_ref, acc_ref)
    o = _flash_out(acc_ref)
    gates = _sigmoid(gn_ref[...])
    for j in range(J):
        o_ref[:, j * HEAD_DIM:(j + 1) * HEAD_DIM] = (
            prev_ref[:, j * HEAD_DIM:(j + 1) * HEAD_DIM]
            + o[j * tq:(j + 1) * tq] * gates[:, N_BRANCHES * j + 2:N_BRANCHES * j + 3])


def _nsa_win(p1, p3, prev):
    s = p1.shape[0]
    gw = NSA_HEADS_PER_GROUP * HEAD_DIM
    q0 = P1_QN // gw
    k0 = (P1_KVN + 4 * NSA_KV_WIDTH) // HEAD_DIM
    v0 = k0 + NSA_KV_GROUPS
    return pl.pallas_call(
        _nsa_win_kernel,
        grid=(NSA_KV_GROUPS, s // ATT_TILE),
        in_specs=[pl.BlockSpec((ATT_TILE, gw), lambda g, i: (i, q0 + g)),
                  pl.BlockSpec((s, HEAD_DIM), lambda g, i: (0, k0 + g)),
                  pl.BlockSpec((s, HEAD_DIM), lambda g, i: (0, v0 + g)),
                  pl.BlockSpec((ATT_TILE, LANES), lambda g, i: (i, SSM_GROUPS + g)),
                  pl.BlockSpec((ATT_TILE, gw), lambda g, i: (i, g))],
        out_specs=pl.BlockSpec((ATT_TILE, gw), lambda g, i: (i, g)),
        out_shape=jax.ShapeDtypeStruct((s, NSA_WIDTH), F32),
        scratch_shapes=[pltpu.VMEM((NSA_HEADS_PER_GROUP * ATT_TILE, HEAD_DIM), MXU),
                        pltpu.VMEM((NSA_HEADS_PER_GROUP * ATT_TILE, HEAD_DIM), F32),
                        pltpu.VMEM((NSA_HEADS_PER_GROUP * ATT_TILE, 2 * HEAD_DIM), F32),
                        pltpu.VMEM((NSA_HEADS_PER_GROUP * ATT_TILE, ATT_TILE), F32),
                        pltpu.VMEM((NSA_HEADS_PER_GROUP * ATT_TILE, ATT_TILE), F32)],
        compiler_params=_cparams(("arbitrary", "arbitrary")),
        name="nsa_window_attention",
    )(p1, p1, p1, p3, prev)


def _merge_kernel(ya_ref, yb_ref, yc_ref, g0_ref, g1_ref, g2_ref, wa_ref, wb_ref, wc_ref, o_ref):
    def branch(y_ref, w_ref, g_ref):
        prod = jnp.dot(y_ref[...].astype(MXU), w_ref[...].astype(MXU), preferred_element_type=F32)
        return _sigmoid(g_ref[...]) * prod

    o_ref[...] = (branch(ya_ref, wa_ref, g0_ref) + branch(yb_ref, wb_ref, g1_ref)
                  + branch(yc_ref, wc_ref, g2_ref)).astype(o_ref.dtype)


def _merge(ya, yb, yc, p2, wa, wb, wc, *, tm=512, tn=512):
    s = ya.shape[0]
    d = wa.shape[1]
    g0 = P2_GM // tn
    gstep = d // tn
    return pl.pallas_call(
        _merge_kernel,
        grid=(d // tn, s // tm),
        in_specs=[pl.BlockSpec((tm, ya.shape[1]), lambda j, i: (i, 0)),
                  pl.BlockSpec((tm, yb.shape[1]), lambda j, i: (i, 0)),
                  pl.BlockSpec((tm, yc.shape[1]), lambda j, i: (i, 0)),
                  pl.BlockSpec((tm, tn), lambda j, i: (i, g0 + j)),
                  pl.BlockSpec((tm, tn), lambda j, i: (i, g0 + gstep + j)),
                  pl.BlockSpec((tm, tn), lambda j, i: (i, g0 + 2 * gstep + j)),
                  pl.BlockSpec((wa.shape[0], tn), lambda j, i: (0, j)),
                  pl.BlockSpec((wb.shape[0], tn), lambda j, i: (0, j)),
                  pl.BlockSpec((wc.shape[0], tn), lambda j, i: (0, j))],
        out_specs=pl.BlockSpec((tm, tn), lambda j, i: (i, j)),
        out_shape=jax.ShapeDtypeStruct((s, d), MXU),
        compiler_params=_cparams(("arbitrary", "arbitrary")),
        name="branch_merge",
    )(ya, yb, yc, p2, p2, p2, wa, wb, wc)


def _layer_norm_rows(x, g, b):
    xc = x - jnp.mean(x, axis=1, keepdims=True)
    var = jnp.mean(xc * xc, axis=1, keepdims=True)
    return xc * lax.rsqrt(var + LN_EPS) * g + b


def _pack_halves(y):
    half = y.shape[1] // 2
    bits = lax.bitcast_convert_type(y.astype(jnp.bfloat16).astype(F32), jnp.int32)
    return ((bits[:, :half] >> 16) & jnp.int32(0xFFFF)) | (bits[:, half:] & jnp.int32(-65536))


def _unpack_halves_f32(w):
    return lax.bitcast_convert_type(w << 16, F32), lax.bitcast_convert_type(w & jnp.int32(-65536), F32)


def _unpack_halves(w):
    lo, hi = _unpack_halves_f32(w)
    return lo.astype(MXU), hi.astype(MXU)


def _wout_ln_kernel(m_ref, w_ref, h_ref, g_ref, b_ref, o_ref, ob_ref, op_ref):
    acc = jnp.dot(m_ref[...].astype(MXU), w_ref[...].astype(MXU), preferred_element_type=F32)
    y = _layer_norm_rows(DEEPNORM_ALPHA * h_ref[...] + acc, g_ref[...], b_ref[...])
    o_ref[...] = y
    ob_ref[...] = y.astype(ob_ref.dtype)
    op_ref[...] = _pack_halves(y)


def _wout_ln(merged, w_out, h, ln_g, ln_b, *, tm=256):
    s, d = h.shape
    row = pl.BlockSpec((tm, d), lambda i: (i, 0))
    vec = pl.BlockSpec((1, d), lambda i: (0, 0))
    return pl.pallas_call(
        _wout_ln_kernel,
        grid=(s // tm,),
        in_specs=[row, pl.BlockSpec((d, d), lambda i: (0, 0)), row, vec, vec],
        out_specs=[row, row, pl.BlockSpec((tm, d // 2), lambda i: (i, 0))],
        out_shape=[jax.ShapeDtypeStruct((s, d), F32), jax.ShapeDtypeStruct((s, d), MXU),
                   jax.ShapeDtypeStruct((s, d // 2), jnp.int32)],
        compiler_params=_cparams(("arbitrary",)),
        name="out_proj_layernorm",
    )(merged, w_out.astype(MXU), h, ln_g.reshape(1, d), ln_b.reshape(1, d))


def _swiglu_packed(x_packed, wg, wu, wd):
    xa, xb = _unpack_halves(x_packed)
    half = xa.shape[1]

    def up(w):
        return (jnp.dot(xa, w[0:half, :].astype(MXU), preferred_element_type=F32)
                + jnp.dot(xb, w[half:2 * half, :].astype(MXU), preferred_element_type=F32))

    act = (_silu(up(wg)) * up(wu)).astype(MXU)
    return jnp.dot(act, wd[...].astype(MXU), preferred_element_type=F32)


def _dispatch_kernel(ztile_ref, dest_ref, hp_ref, wg_ref, wu_ref, wd_ref, o_ref, sh_ref, zbuf, sem, zsem):
    tt = dest_ref.shape[1]
    tm = zbuf.shape[0]

    @pl.when(pl.program_id(0) == 0)
    def _zero_partial_tiles():
        zbuf[...] = jnp.zeros_like(zbuf)

        def tile_copy(t):
            return pltpu.make_async_copy(zbuf, o_ref.at[pl.ds(t * tm, tm)], zsem)

        def start(j, c):
            @pl.when(ztile_ref[j] >= 0)
            def _():
                tile_copy(ztile_ref[j]).start()
            return c

        def wait(j, c):
            @pl.when(ztile_ref[j] >= 0)
            def _():
                tile_copy(ztile_ref[j]).wait()
            return c

        lax.fori_loop(0, ztile_ref.shape[0], start, 0)
        lax.fori_loop(0, ztile_ref.shape[0], wait, 0)

    def issue(g, c):
        base = pl.multiple_of(g * SUBLANES, SUBLANES)
        for rr in range(SUBLANES):
            for k in range(TOP_K):
                pltpu.make_async_copy(hp_ref.at[pl.ds(base + rr, 1)], o_ref.at[pl.ds(dest_ref[k, base + rr], 1)],
                                      sem).start()
        return c

    lax.fori_loop(0, tt // SUBLANES, issue, 0)
    sh_ref[...] = _swiglu_packed(hp_ref[...], wg_ref, wu_ref, wd_ref)
    rows = o_ref.at[pl.ds(0, TOP_K * tt)]
    pltpu.make_async_copy(rows, rows, sem).wait()


def _dispatch_shared(hp, dest, zero_tiles, n_rows, w_sh_gate, w_sh_up, w_sh_down, layer, *, tm, tt=256):
    s, half = hp.shape
    d, f = w_sh_gate.shape[1:]
    grid_spec = pltpu.PrefetchScalarGridSpec(
        num_scalar_prefetch=1,
        grid=(s // tt,),
        in_specs=[pl.BlockSpec((TOP_K, tt), lambda i, zt: (0, i), memory_space=pltpu.SMEM),
                  pl.BlockSpec((tt, half), lambda i, zt: (i, 0)),
                  pl.BlockSpec((None, d, f), lambda i, zt: (layer, 0, 0)),
                  pl.BlockSpec((None, d, f), lambda i, zt: (layer, 0, 0)),
                  pl.BlockSpec((None, f, d), lambda i, zt: (layer, 0, 0))],
        out_specs=[pl.BlockSpec(memory_space=pl.ANY), pl.BlockSpec((tt, d), lambda i, zt: (i, 0))],
        scratch_shapes=[pltpu.VMEM((tm, half), jnp.int32), pltpu.SemaphoreType.DMA(()),
                        pltpu.SemaphoreType.DMA(())],
    )
    return pl.pallas_call(
        _dispatch_kernel,
        grid_spec=grid_spec,
        out_shape=[jax.ShapeDtypeStruct((n_rows, half), jnp.int32), jax.ShapeDtypeStruct((s, d), F32)],
        compiler_params=_cparams(("arbitrary",), 56),
        name="moe_dispatch_shared",
    )(zero_tiles, dest, hp, w_sh_gate, w_sh_up, w_sh_down)


def _combine_kernel(dcur_ref, dnxt_ref, w_ref, h_ref, sh_ref, g_ref, b_ref, y_ref, o_ref, ob_ref, buf, sem):
    tt = dcur_ref.shape[1]
    i = pl.program_id(0)
    n = pl.num_programs(0)
    slot = i % 2

    def issue(d_ref, sl):
        def body(g, c):
            base = pl.multiple_of(g * SUBLANES, SUBLANES)
            for rr in range(SUBLANES):
                for k in range(TOP_K):
                    pltpu.make_async_copy(y_ref.at[pl.ds(d_ref[k, base + rr], 1)],
                                          buf.at[sl, pl.ds(k * tt + base + rr, 1)], sem.at[sl]).start()
            return c

        lax.fori_loop(0, tt // SUBLANES, body, 0)

    @pl.when(i == 0)
    def _():
        issue(dcur_ref, slot)

    @pl.when(i + 1 < n)
    def _():
        issue(dnxt_ref, 1 - slot)

    pltpu.make_async_copy(y_ref.at[pl.ds(0, TOP_K * tt)], buf.at[slot], sem.at[slot]).wait()
    w = w_ref[...]
    lo = hi = None
    for k in range(TOP_K):
        ya, yb = _unpack_halves_f32(buf[slot, k * tt:(k + 1) * tt])
        ta, tb = w[:, k:k + 1] * ya, w[:, k:k + 1] * yb
        lo, hi = (ta, tb) if lo is None else (lo + ta, hi + tb)
    routed = jnp.concatenate([lo, hi], axis=1)
    y = _layer_norm_rows(DEEPNORM_ALPHA * h_ref[...] + (routed + sh_ref[...]), g_ref[...], b_ref[...])
    o_ref[...] = y
    ob_ref[...] = y.astype(ob_ref.dtype)


def _combine_ln(h, y_rows, dest, top_w, shared, ln_g, ln_b, *, tt=128):
    s, d = h.shape
    n = s // tt
    row = pl.BlockSpec((tt, d), lambda i: (i, 0))
    vec = pl.BlockSpec((1, d), lambda i: (0, 0))
    return pl.pallas_call(
        _combine_kernel,
        grid=(n,),
        in_specs=[pl.BlockSpec((TOP_K, tt), lambda i: (0, i), memory_space=pltpu.SMEM),
                  pl.BlockSpec((TOP_K, tt), lambda i: (0, jnp.minimum(i + 1, n - 1)), memory_space=pltpu.SMEM),
                  pl.BlockSpec((tt, TOP_K), lambda i: (i, 0)),
                  row, row, vec, vec,
                  pl.BlockSpec(memory_space=pl.ANY)],
        out_specs=[row, row],
        out_shape=[jax.ShapeDtypeStruct((s, d), F32), jax.ShapeDtypeStruct((s, d), MXU)],
        scratch_shapes=[pltpu.VMEM((2, TOP_K * tt, d // 2), jnp.int32), pltpu.SemaphoreType.DMA((2,))],
        compiler_params=_cparams(("arbitrary",)),
        name="moe_combine_layernorm",
    )(dest, dest, top_w.T, h, shared, ln_g.reshape(1, d), ln_b.reshape(1, d), y_rows)


def _router_kernel(h_ref, wr_ref, rb_ref, e_ref, w_ref, pos_ref, cnt_ref, carry_ref):
    tq = h_ref.shape[0]
    i = pl.program_id(0)

    @pl.when(i == 0)
    def _reset():
        carry_ref[...] = jnp.zeros_like(carry_ref)

    logits = lax.dot_general(wr_ref[...].astype(MXU), h_ref[...].astype(MXU), NT, preferred_element_type=F32)
    scores = _sigmoid(logits)
    biased = scores + rb_ref[...]
    G, PG = N_EXPERT_GROUPS, EXPERTS_PER_GROUP
    sub = lax.broadcasted_iota(jnp.int32, (PG, tq), 0).astype(F32)
    gi = lax.broadcasted_iota(jnp.int32, (G, tq), 0).astype(F32)
    gs = jnp.zeros((G, tq), F32)
    for g in range(G):
        blk = biased[g * PG:(g + 1) * PG, :]
        m1 = jnp.max(blk, axis=0, keepdims=True)
        i1 = jnp.min(jnp.where(blk == m1, sub, float(PG)), axis=0, keepdims=True)
        m2 = jnp.max(jnp.where(sub == i1, -jnp.inf, blk), axis=0, keepdims=True)
        gs = jnp.where(gi == float(g), m1 + m2, gs)
    keep = jnp.zeros((G, tq), F32)
    for _ in range(TOPK_GROUPS):
        mx = jnp.max(gs, axis=0, keepdims=True)
        idx = jnp.min(jnp.where(gs == mx, gi, float(G)), axis=0, keepdims=True)
        pick = gi == idx
        keep = jnp.where(pick, 1.0, keep)
        gs = jnp.where(pick, -jnp.inf, gs)
    val = jnp.concatenate(
        [jnp.where(keep[g:g + 1, :] > 0, biased[g * PG:(g + 1) * PG, :], -jnp.inf) for g in range(G)], axis=0)
    ei = lax.broadcasted_iota(jnp.int32, (N_EXPERTS, tq), 0).astype(F32)
    picks, svals = [], []
    sel = jnp.zeros((N_EXPERTS, tq), F32)
    for r in range(TOP_K):
        mx = jnp.max(val, axis=0, keepdims=True)
        idx = jnp.min(jnp.where(val == mx, ei, float(N_EXPERTS)), axis=0, keepdims=True)
        pick = ei == idx
        picks.append(pick)
        svals.append(jnp.sum(jnp.where(pick, scores, 0.0), axis=0, keepdims=True))
        e_ref[r:r + 1, :] = idx.astype(jnp.int32)
        sel = jnp.where(pick, 1.0, sel)
        val = jnp.where(pick, -jnp.inf, val)
    wsum = svals[0]
    for r in range(1, TOP_K):
        wsum = wsum + svals[r]
    tr = lax.broadcasted_iota(jnp.int32, (tq, tq), 0)
    tc = lax.broadcasted_iota(jnp.int32, (tq, tq), 1)
    before = (tr < tc).astype(jnp.bfloat16)
    prefix = jnp.dot(sel.astype(jnp.bfloat16), before, preferred_element_type=F32)
    pos = carry_ref[:, 0:1] + prefix
    for r in range(TOP_K):
        w_ref[r:r + 1, :] = svals[r] / wsum * ROUTED_SCALE
        pos_ref[r:r + 1, :] = jnp.sum(jnp.where(picks[r], pos, 0.0), axis=0, keepdims=True).astype(jnp.int32)
    total = carry_ref[...] + jnp.sum(sel, axis=1, keepdims=True)
    carry_ref[...] = total
    cnt_ref[...] = total


def _router(hb, w_router, router_bias, *, tq=256):
    s, d = hb.shape
    row = pl.BlockSpec((TOP_K, tq), lambda i: (0, i))
    return pl.pallas_call(
        _router_kernel,
        grid=(s // tq,),
        in_specs=[pl.BlockSpec((tq, d), lambda i: (i, 0)),
                  pl.BlockSpec((N_EXPERTS, d), lambda i: (0, 0)),
                  pl.BlockSpec((N_EXPERTS, 1), lambda i: (0, 0))],
        out_specs=[row, row, row, pl.BlockSpec((N_EXPERTS, LANES), lambda i: (0, 0))],
        out_shape=[jax.ShapeDtypeStruct((TOP_K, s), jnp.int32), jax.ShapeDtypeStruct((TOP_K, s), F32),
                   jax.ShapeDtypeStruct((TOP_K, s), jnp.int32), jax.ShapeDtypeStruct((N_EXPERTS, LANES), F32)],
        scratch_shapes=[pltpu.VMEM((N_EXPERTS, LANES), F32)],
        compiler_params=_cparams(("arbitrary",)),
        name="moe_router",
    )(hb, w_router.T, router_bias.reshape(N_EXPERTS, 1))


def _expert_kernel(te_ref, nu_ref, first_ref, slot_ref, nxt_ref, x_ref, wg_hbm, wu_hbm, wd_hbm, o_ref,
                   wg_buf, wu_buf, wd_buf, sem, *, layer):
    i = pl.program_id(0)

    def weight_copies(e, sl):
        return (pltpu.make_async_copy(wg_hbm.at[layer, e], wg_buf.at[sl], sem.at[sl]),
                pltpu.make_async_copy(wu_hbm.at[layer, e], wu_buf.at[sl], sem.at[sl]),
                pltpu.make_async_copy(wd_hbm.at[layer, e], wd_buf.at[sl], sem.at[sl]))

    @pl.when(i < nu_ref[0])
    def _compute():
        sl = slot_ref[i]

        @pl.when(i == 0)
        def _first_expert():
            for c in weight_copies(te_ref[i], sl):
                c.start()

        @pl.when(first_ref[i] == 1)
        def _expert_changed():
            for c in weight_copies(te_ref[i], sl):
                c.wait()

            @pl.when(nxt_ref[i] >= 0)
            def _prefetch_next_expert():
                for c in weight_copies(nxt_ref[i], 1 - sl):
                    c.start()

        y = _swiglu_packed(x_ref[...], wg_buf.at[sl], wu_buf.at[sl], wd_buf.at[sl])
        o_ref[...] = _pack_halves(y)

    @pl.when(i >= nu_ref[0])
    def _unused():
        o_ref[...] = jnp.zeros_like(o_ref)


def _experts(x_rows, tile_e, n_used, w_gate, w_up, w_down, layer, *, tm):
    n_rows = x_rows.shape[0]
    n_tiles = n_rows // tm
    d, f = w_gate.shape[2:]
    d_out, out_dtype = d // 2, jnp.int32
    idx = jnp.arange(n_tiles, dtype=jnp.int32)
    first = jnp.concatenate([jnp.ones((1,), jnp.int32), (tile_e[1:] != tile_e[:-1]).astype(jnp.int32)])
    slot = ((jnp.cumsum(first) - 1) % 2).astype(jnp.int32)
    used = idx < n_used[0]
    run_end = jnp.sum((tile_e[None, :] <= tile_e[:, None]) & used[None, :], axis=1).astype(jnp.int32)
    nxt = jnp.where(run_end < n_used[0], tile_e[jnp.minimum(run_end, n_tiles - 1)], -1).astype(jnp.int32)
    grid_spec = pltpu.PrefetchScalarGridSpec(
        num_scalar_prefetch=5,
        grid=(n_tiles,),
        in_specs=[pl.BlockSpec((tm, d // 2), lambda i, te, nu, *_: (jnp.minimum(i, nu[0] - 1), 0)),
                  pl.BlockSpec(memory_space=pl.ANY), pl.BlockSpec(memory_space=pl.ANY),
                  pl.BlockSpec(memory_space=pl.ANY)],
        out_specs=pl.BlockSpec((tm, d_out), lambda i, *_: (i, 0)),
        scratch_shapes=[pltpu.VMEM((2, d, f), w_gate.dtype), pltpu.VMEM((2, d, f), w_up.dtype),
                        pltpu.VMEM((2, f, d), w_down.dtype), pltpu.SemaphoreType.DMA((2,))],
    )
    return pl.pallas_call(
        functools.partial(_expert_kernel, layer=layer),
        grid_spec=grid_spec,
        out_shape=jax.ShapeDtypeStruct((n_rows, d_out), out_dtype),
        compiler_params=_cparams(("arbitrary",), 56),
        name="moe_experts",
    )(tile_e, n_used, first, slot, nxt, x_rows, w_gate, w_up, w_down)


def _moe_ln(h, hb, hp, layer, w_router, router_bias, w_exp_gate, w_exp_up, w_exp_down, w_sh_gate, w_sh_up, w_sh_down,
            ln_g, ln_b):
    s, d = h.shape
    tm = MOE_TILE
    top_e, top_w, top_pos, counts = _router(hb, w_router, router_bias)
    cnt = counts[:, 0].astype(jnp.int32)
    tiles_e = (cnt + tm - 1) // tm
    tile_end = jnp.cumsum(tiles_e)
    row_start = (tile_end - tiles_e) * tm
    experts = jnp.arange(N_EXPERTS, dtype=jnp.int32)
    dest = top_pos + jnp.sum(jnp.where(top_e[..., None] == experts, row_start, 0), axis=-1)
    n_tiles = s * TOP_K // tm + N_EXPERTS
    tile_e = jnp.minimum(jnp.sum(tile_end[None, :] <= jnp.arange(n_tiles, dtype=jnp.int32)[:, None], axis=1),
                         N_EXPERTS - 1).astype(jnp.int32)
    n_used = tile_end[-1:].astype(jnp.int32)
    zero_tiles = jnp.concatenate([jnp.where(tiles_e > 0, tile_end - 1, -1),
                                  jnp.where(n_used + experts < n_tiles, n_used + experts, -1)]).astype(jnp.int32)
    x_rows, shared = _dispatch_shared(hp, dest, zero_tiles, n_tiles * tm, w_sh_gate, w_sh_up, w_sh_down, layer, tm=tm)
    y_rows = _experts(x_rows, tile_e, n_used, w_exp_gate, w_exp_up, w_exp_down, layer, tm=tm)
    return _combine_ln(h, y_rows, dest, top_w, shared, ln_g, ln_b)


def _project(hb, wt_all, layer):
    off = np.concatenate([[0], np.cumsum(IN_SPLIT_SIZES)])
    pa = _proj_wt(hb, wt_all, layer, col0=0, ncols=off[3], tm=1024, tn=1024, out_dtype=MXU, q_cols=MOBA_WIDTH)
    ps = _proj_wt(hb, wt_all, layer, col0=off[3], ncols=off[5] - off[3], tm=1024, tn=1024, out_dtype=F32)
    pn = _proj_wt(hb, wt_all, layer, col0=off[6], ncols=off[8] - off[6], tm=1024, tn=512, out_dtype=MXU,
                  q_cols=NSA_WIDTH)
    pg = _proj_wt(hb, wt_all, layer, col0=off[9], ncols=off[10] - off[9], tm=1024, tn=1024, out_dtype=F32)
    p3 = _matmul(hb, _small_proj_weights(wt_all[layer]), tm=1024, tn=P3_COLS, out_dtype=F32)
    return pa, pn, ps, pg, p3


def _mixer(hb, wt_all, layer, conv_w, conv_b, dt_bias, a_log, d_skip, ssm_norm_g,
           cmp_pos_k, cmp_w1_k, cmp_w2_k, cmp_pos_v, cmp_w1_v, cmp_w2_v, w_br_a, w_br_b, w_br_c):
    pa, pn, ps, pg, p3 = _project(hb, wt_all, layer)
    y_a = _moba(pa)
    y_b = _ssd(ps, p3, conv_w, conv_b, dt_bias, a_log, d_skip, ssm_norm_g)
    kvc = _nsa_compress(pn, cmp_pos_k, cmp_w1_k, cmp_w2_k, cmp_pos_v, cmp_w1_v, cmp_w2_v)
    y_c, selb = _nsa_cmp(pn, p3, kvc)
    y_c = _nsa_sel(pn, p3, selb, y_c)
    y_c = _nsa_win(pn, p3, y_c)
    return _merge(y_a, y_b, y_c, pg, w_br_a, w_br_b, w_br_c)


def kernel(x, w_in, conv_w, conv_b, dt_bias, a_log, d_skip, ssm_norm_g, cmp_pos_k, cmp_w1_k, cmp_w2_k, cmp_pos_v, cmp_w1_v, cmp_w2_v, w_br_a, w_br_b, w_br_c, w_out, ln1_g, ln1_b, w_router, router_bias, w_exp_gate, w_exp_up, w_exp_down, w_sh_gate, w_sh_up, w_sh_down, ln2_g, ln2_b):
    bsz, s, d = x.shape
    assert bsz == 1
    h = x.reshape(s, d)
    hb = h.astype(MXU)
    wt_all = jnp.swapaxes(w_in, 1, 2)
    for l in range(w_in.shape[0]):
        merged = _mixer(hb, wt_all, l, conv_w[l], conv_b[l], dt_bias[l], a_log[l], d_skip[l], ssm_norm_g[l],
                        cmp_pos_k[l], cmp_w1_k[l], cmp_w2_k[l], cmp_pos_v[l], cmp_w1_v[l], cmp_w2_v[l],
                        w_br_a[l], w_br_b[l], w_br_c[l])
        h, hb, hp = _wout_ln(merged, w_out[l], h, ln1_g[l], ln1_b[l])
        h, hb = _moe_ln(h, hb, hp, l, w_router[l], router_bias[l], w_exp_gate, w_exp_up, w_exp_down,
                        w_sh_gate, w_sh_up, w_sh_down, ln2_g[l], ln2_b[l])
    return h.reshape(bsz, s, d)
```

```python
import functools

import numpy as np
import jax
import jax.numpy as jnp
from jax import lax
from jax.experimental import pallas as pl
from jax.experimental.pallas import tpu as pltpu

F32 = jnp.float32
MXU = jnp.bfloat16

D_MODEL = 2048
DEPTH = 2
HEAD_DIM = 128
MOBA_HEADS = 8
MOBA_WIDTH = MOBA_HEADS * HEAD_DIM
MOBA_BLOCK = 256
MOBA_TOPK = 3
SSM_D_INNER = D_MODEL
SSM_HEAD_DIM = 64
SSM_HEADS = SSM_D_INNER // SSM_HEAD_DIM
SSM_STATE = 128
SSM_GROUPS = 8
SSM_HEADS_PER_GROUP = SSM_HEADS // SSM_GROUPS
SSM_GROUP_WIDTH = SSM_D_INNER // SSM_GROUPS
SSM_CONV = 4
SSM_CHUNK = 256
SSM_CONV_CH = SSM_D_INNER + 2 * SSM_GROUPS * SSM_STATE
NSA_HEADS = 8
NSA_KV_GROUPS = 2
NSA_HEADS_PER_GROUP = NSA_HEADS // NSA_KV_GROUPS
NSA_WIDTH = NSA_HEADS * HEAD_DIM
NSA_KV_WIDTH = NSA_KV_GROUPS * HEAD_DIM
NSA_CMP_LEN = 32
NSA_CMP_STRIDE = 16
NSA_SEL_BLOCK = 64
NSA_TOPN = 16
NSA_WINDOW = 512
N_BRANCHES = 3
N_EXPERTS = 64
N_EXPERT_GROUPS = 8
EXPERTS_PER_GROUP = N_EXPERTS // N_EXPERT_GROUPS
TOPK_GROUPS = 4
TOP_K = 8
D_EXPERT = 512
ROUTED_SCALE = 2.5
DEEPNORM_ALPHA = (2 * DEPTH) ** 0.25
LN_EPS = 1e-5
RMS_EPS = 1e-5
IN_SPLIT_SIZES = (MOBA_WIDTH, MOBA_WIDTH, MOBA_WIDTH,
                  SSM_D_INNER, SSM_CONV_CH, SSM_HEADS,
                  NSA_WIDTH, 6 * NSA_KV_WIDTH, N_BRANCHES * NSA_HEADS,
                  N_BRANCHES * D_MODEL)

LANES = 128
SUBLANES = 8
ATT_TILE = 256
MOE_TILE = 256
ATT_SCALE = HEAD_DIM ** -0.5
Q_PRESCALE = ATT_SCALE * 1.4426950408889634
MASK_BIAS = -2.0 ** 30
NEG = -1e30

NT = (((1,), (1,)), ((), ()))

P1_QA, P1_KA, P1_VA = 0, 1024, 2048
P1_QN, P1_KVN = 0, 1024
P2_Z, P2_XBC = 0, 2048
P2_GM = 0
P3_COLS = (SSM_GROUPS + NSA_KV_GROUPS) * LANES


def _cparams(semantics, vmem_mb=48):
    return pltpu.CompilerParams(dimension_semantics=semantics, vmem_limit_bytes=vmem_mb * 1024 * 1024)


def _sigmoid(x):
    return 1.0 / (1.0 + jnp.exp(-x))


def _silu(x):
    return x * _sigmoid(x)


def _split3(x):
    hi = x.astype(jnp.bfloat16)
    r1 = x - hi.astype(F32)
    mid = r1.astype(jnp.bfloat16)
    lo = (r1 - mid.astype(F32)).astype(jnp.bfloat16)
    return hi, mid, lo


def _dot3(a_exact, x, left=True):
    acc = None
    for part in _split3(x):
        t = (jnp.dot(a_exact, part, preferred_element_type=F32) if left
             else jnp.dot(part, a_exact, preferred_element_type=F32))
        acc = t if acc is None else acc + t
    return acc


def _mm_kernel(a_ref, b_ref, o_ref):
    o_ref[...] = jnp.dot(a_ref[...].astype(MXU), b_ref[...].astype(MXU),
                         preferred_element_type=F32).astype(o_ref.dtype)


def _matmul(a, b, *, tm, tn, out_dtype):
    m, k = a.shape
    n = b.shape[1]
    assert m % tm == 0 and n % tn == 0
    return pl.pallas_call(
        _mm_kernel,
        grid=(n // tn, m // tm),
        in_specs=[pl.BlockSpec((tm, k), lambda j, i: (i, 0)),
                  pl.BlockSpec((k, tn), lambda j, i: (0, j))],
        out_specs=pl.BlockSpec((tm, tn), lambda j, i: (i, j)),
        out_shape=jax.ShapeDtypeStruct((m, n), out_dtype),
        compiler_params=_cparams(("arbitrary", "arbitrary"), 56),
        name="proj_matmul",
    )(a, b)


def _mm_wt_kernel(a_ref, wt_ref, o_ref, wb_ref, *, q_blocks):
    @pl.when(pl.program_id(1) == 0)
    def _cast_weights():
        wb_ref[...] = wt_ref[0].astype(MXU)

    acc = lax.dot_general(a_ref[...].astype(MXU), wb_ref[...], NT, preferred_element_type=F32)
    if q_blocks:
        acc = acc * jnp.where(pl.program_id(0) < q_blocks, Q_PRESCALE, 1.0)
    o_ref[...] = acc.astype(o_ref.dtype)


def _proj_wt(a, wt_all, layer, *, col0, ncols, tm, tn, out_dtype, q_cols=0):
    m, k = a.shape
    assert m % tm == 0 and ncols % tn == 0 and col0 % SUBLANES == 0 and q_cols % tn == 0
    return pl.pallas_call(
        functools.partial(_mm_wt_kernel, q_blocks=q_cols // tn),
        grid=(ncols // tn, m // tm),
        in_specs=[pl.BlockSpec((tm, k), lambda j, i: (i, 0)),
                  pl.BlockSpec((pl.Element(1), pl.Element(tn), pl.Element(k)),
                               lambda j, i: (layer, pl.multiple_of(int(col0) + j * tn, SUBLANES), 0))],
        out_specs=pl.BlockSpec((tm, tn), lambda j, i: (i, j)),
        out_shape=jax.ShapeDtypeStruct((m, ncols), out_dtype),
        scratch_shapes=[pltpu.VMEM((tn, k), MXU)],
        compiler_params=_cparams(("arbitrary", "arbitrary"), 56),
        name="proj_matmul_wt",
    )(a, wt_all)


def _small_proj_weights(wt_all, layer):
    off = np.concatenate([[0], np.cumsum(IN_SPLIT_SIZES)])
    k = wt_all.shape[2]
    dt = lax.slice(wt_all, (layer, int(off[5]), 0), (layer + 1, int(off[6]), k))[0].T
    gn = lax.slice(wt_all, (layer, int(off[8]), 0), (layer + 1, int(off[9]), k))[0].T
    d = dt.shape[0]
    dtp = jnp.pad(dt.reshape(d, SSM_GROUPS, SSM_HEADS_PER_GROUP),
                  ((0, 0), (0, 0), (0, LANES - SSM_HEADS_PER_GROUP))).reshape(d, SSM_GROUPS * LANES)
    ng = NSA_HEADS_PER_GROUP * N_BRANCHES
    gnp = jnp.pad(gn.reshape(d, NSA_KV_GROUPS, ng), ((0, 0), (0, 0), (0, LANES - ng))).reshape(d, NSA_KV_GROUPS * LANES)
    return jnp.concatenate([dtp, gnp], axis=1)


def _flash_init(m_ref, acc_ref):
    m_ref[...] = jnp.full(m_ref.shape, NEG, F32)
    acc_ref[...] = jnp.zeros(acc_ref.shape, F32)


def _flash_step(s, v_aug, m_ref, acc_ref):
    hd = HEAD_DIM
    m_prev = m_ref[...]
    m_new = jnp.maximum(m_prev, jnp.max(s, axis=1, keepdims=True))
    alpha = jnp.exp2(m_prev - m_new)
    p = jnp.concatenate([jnp.exp2(s[:, c * hd:(c + 1) * hd] - m_new) for c in range(s.shape[1] // hd)], axis=1)
    pv = jnp.dot(p.astype(MXU), v_aug, preferred_element_type=F32)
    acc_ref[:, 0:hd] = alpha * acc_ref[:, 0:hd] + pv[:, 0:hd]
    acc_ref[:, hd:2 * hd] = alpha * acc_ref[:, hd:2 * hd] + pv[:, hd:2 * hd]
    m_ref[...] = m_new


def _flash_out(acc_ref):
    return acc_ref[:, 0:HEAD_DIM] / acc_ref[:, HEAD_DIM:2 * HEAD_DIM]


def _flash_loop(n, scores, values, sa_ref, sb_ref, m_ref, acc_ref):
    @pl.when(n > 0)
    def _first():
        sa_ref[...] = scores(0)

    def pair(u, c):
        t = 2 * u
        sb_ref[...] = scores(t + 1)
        _flash_step(sa_ref[...], values(t), m_ref, acc_ref)
        sa_ref[...] = scores(t + 2)
        _flash_step(sb_ref[...], values(t + 1), m_ref, acc_ref)
        return c

    n_pairs = jnp.maximum(n - 1, 0) // 2
    lax.fori_loop(0, n_pairs, pair, 0)
    t0 = 2 * n_pairs
    left = n - t0

    @pl.when(left == 2)
    def _last_two():
        sb_ref[...] = scores(t0 + 1)
        _flash_step(sa_ref[...], values(t0), m_ref, acc_ref)
        _flash_step(sb_ref[...], values(t0 + 1), m_ref, acc_ref)

    @pl.when(left == 1)
    def _last_one():
        _flash_step(sa_ref[...], values(t0), m_ref, acc_ref)


MOBA_KV_GROUP = 4


def _moba_kernel(q_ref, k_ref, v_ref, o_ref, kaug_ref, vaug_ref, kmean_ref, m_ref, acc_ref, sa_ref, sb_ref, *, nb):
    blk = MOBA_BLOCK
    grp = MOBA_KV_GROUP * blk
    qi = pl.program_id(1)

    @pl.when(qi == 0)
    def _build_keys():
        kmean_ref[...] = jnp.zeros_like(kmean_ref)
        lane = lax.broadcasted_iota(jnp.int32, (blk, LANES), 1)
        ones = jnp.ones((blk, HEAD_DIM), vaug_ref.dtype)

        def body(j, c):
            r0 = pl.multiple_of(j * blk, blk)
            kb = k_ref[pl.ds(r0, blk), :]
            kaug_ref[pl.ds(r0, blk), 0:HEAD_DIM] = kb.astype(kaug_ref.dtype)
            kaug_ref[pl.ds(r0, blk), HEAD_DIM:2 * HEAD_DIM] = (lane == j).astype(kaug_ref.dtype)
            vaug_ref[pl.ds(r0, blk), 0:HEAD_DIM] = v_ref[pl.ds(r0, blk), :].astype(vaug_ref.dtype)
            vaug_ref[pl.ds(r0, blk), HEAD_DIM:2 * HEAD_DIM] = ones
            kmean_ref[pl.ds(j, 1), :] = jnp.sum(kb.astype(F32), axis=0, keepdims=True) * (1.0 / blk)
            return c

        lax.fori_loop(0, nb, body, 0)

    q = q_ref[...].astype(MXU)
    gate = lax.dot_general(kmean_ref[...].astype(MXU), q, NT, preferred_element_type=F32)
    kblk = lax.broadcasted_iota(jnp.int32, (LANES, blk), 0).astype(F32)
    past = kblk < qi.astype(F32)
    gate = jnp.where(past, gate, -jnp.inf)
    sel = jnp.where(kblk == qi.astype(F32), 1.0, 0.0)
    for _ in range(MOBA_TOPK):
        mx = jnp.max(gate, axis=0, keepdims=True)
        idx = jnp.min(jnp.where((gate == mx) & past, kblk, float(LANES)), axis=0, keepdims=True)
        pick = kblk == idx
        sel = jnp.where(pick, 1.0, sel)
        gate = jnp.where(pick, -jnp.inf, gate)
    bias = jnp.where(sel.T > 0, 0.0, MASK_BIAS).astype(MXU)
    qaug = jnp.concatenate([q, bias], axis=1)
    rel = (qi * blk + lax.broadcasted_iota(jnp.int32, (blk, grp), 0)
           - lax.broadcasted_iota(jnp.int32, (blk, grp), 1))

    _flash_init(m_ref, acc_ref)

    def scores(g):
        r0 = pl.multiple_of(g * grp, grp)
        s = lax.dot_general(qaug, kaug_ref[pl.ds(r0, grp), :], NT, preferred_element_type=F32)
        return jnp.where(rel >= g * grp, s, NEG)

    def values(g):
        return vaug_ref[pl.ds(pl.multiple_of(g * grp, grp), grp), :]

    _flash_loop(qi // MOBA_KV_GROUP + 1, scores, values, sa_ref, sb_ref, m_ref, acc_ref)
    o_ref[...] = _flash_out(acc_ref).astype(o_ref.dtype)


def _moba(p1):
    s = p1.shape[0]
    nb = s // MOBA_BLOCK
    assert nb <= LANES and nb % MOBA_KV_GROUP == 0
    kcol, vcol = P1_KA // HEAD_DIM, P1_VA // HEAD_DIM
    return pl.pallas_call(
        functools.partial(_moba_kernel, nb=nb),
        grid=(MOBA_HEADS, nb),
        in_specs=[pl.BlockSpec((MOBA_BLOCK, HEAD_DIM), lambda h, i: (i, h)),
                  pl.BlockSpec((s, HEAD_DIM), lambda h, i: (0, kcol + h)),
                  pl.BlockSpec((s, HEAD_DIM), lambda h, i: (0, vcol + h))],
        out_specs=pl.BlockSpec((MOBA_BLOCK, HEAD_DIM), lambda h, i: (i, h)),
        out_shape=jax.ShapeDtypeStruct((s, MOBA_WIDTH), MXU),
        scratch_shapes=[pltpu.VMEM((s, 2 * HEAD_DIM), MXU), pltpu.VMEM((s, 2 * HEAD_DIM), MXU),
                        pltpu.VMEM((LANES, HEAD_DIM), F32),
                        pltpu.VMEM((MOBA_BLOCK, HEAD_DIM), F32), pltpu.VMEM((MOBA_BLOCK, 2 * HEAD_DIM), F32),
                        pltpu.VMEM((MOBA_BLOCK, MOBA_KV_GROUP * MOBA_BLOCK), F32),
                        pltpu.VMEM((MOBA_BLOCK, MOBA_KV_GROUP * MOBA_BLOCK), F32)],
        compiler_params=_cparams(("arbitrary", "arbitrary")),
        name="moba_attention",
    )(p1, p1, p1)


def _ssd_kernel(dtb_ref, alog_ref, dskip_ref,
                xs_ref, bm_ref, cm_ref, z_ref, dt_ref, cwx_ref, cwb_ref, cwc_ref, cbx_ref, cbb_ref, cbc_ref,
                ng_ref, o_ref, xbuf, bbuf, cbuf, state_ref, ybuf):
    L = SSM_CHUNK
    W = SSM_GROUP_WIDTH
    J = SSM_HEADS_PER_GROUP
    P = SSM_HEAD_DIM
    g = pl.program_id(0)
    c = pl.program_id(1)

    @pl.when(c == 0)
    def _reset():
        xbuf[0:SUBLANES, :] = jnp.zeros((SUBLANES, W), F32)
        bbuf[0:SUBLANES, :] = jnp.zeros((SUBLANES, SSM_STATE), F32)
        cbuf[0:SUBLANES, :] = jnp.zeros((SUBLANES, SSM_STATE), F32)
        state_ref[...] = jnp.zeros_like(state_ref)

    def conv_silu(buf, raw_ref, w_ref, b_ref):
        buf[SUBLANES:SUBLANES + L, :] = raw_ref[...]
        acc = b_ref[...]
        for i in range(SSM_CONV):
            lo = SUBLANES - (SSM_CONV - 1) + i
            acc = acc + w_ref[i:i + 1, :] * buf[lo:lo + L, :]
        buf[0:SUBLANES, :] = buf[L:L + SUBLANES, :]
        return _silu(acc)

    xs = conv_silu(xbuf, xs_ref, cwx_ref, cbx_ref)
    bm = conv_silu(bbuf, bm_ref, cwb_ref, cbb_ref)
    cm = conv_silu(cbuf, cm_ref, cwc_ref, cbc_ref)

    lane = lax.broadcasted_iota(jnp.int32, (1, LANES), 1)
    dtb = jnp.zeros((1, LANES), F32)
    alog = jnp.full((1, LANES), -jnp.inf, F32)
    for j in range(J):
        dtb = jnp.where(lane == j, dtb_ref[g * J + j], dtb)
        alog = jnp.where(lane == j, alog_ref[g * J + j], alog)
    x = dt_ref[...] + dtb
    dt = jnp.maximum(x, 0.0) + jnp.log(1.0 + jnp.exp(-jnp.abs(x)))
    dt = jnp.where(lane < J, dt, 0.0)
    a = dt * (-jnp.exp(alog))

    er = lax.broadcasted_iota(jnp.int32, (LANES, W), 0)
    ec = lax.broadcasted_iota(jnp.int32, (LANES, W), 1)
    expand = ((ec >> 6) == er).astype(jnp.bfloat16)
    tr = lax.broadcasted_iota(jnp.int32, (L, L), 0)
    tc = lax.broadcasted_iota(jnp.int32, (L, L), 1)
    tril = tr >= tc
    tril_b = tril.astype(jnp.bfloat16)
    dt_e = _dot3(expand, dt, left=False)
    a_e = _dot3(expand, a, left=False)
    acs = _dot3(tril_b, a_e, left=True)
    acs_t = acs.T
    a_last = acs[L - 1:L, :]

    xdt = xs * dt_e
    cb = lax.dot_general(cm.astype(MXU), bm.astype(MXU), NT, preferred_element_type=F32)
    for j in range(J):
        colv = acs[:, j * P:j * P + 1]
        rowv = acs_t[j * P:j * P + 1, :]
        dec = jnp.exp(jnp.where(tril, colv - rowv, -jnp.inf))
        ybuf[:, j * P:(j + 1) * P] = jnp.dot((cb * dec).astype(MXU), xdt[:, j * P:(j + 1) * P].astype(MXU),
                                             preferred_element_type=F32)
    st_old = state_ref[...]
    xdte = (xdt * jnp.exp(a_last - acs)).astype(MXU)
    st_new = jnp.dot(bm.T.astype(MXU), xdte, preferred_element_type=F32)
    y_off = jnp.dot(cm.astype(MXU), st_old.astype(MXU), preferred_element_type=F32) * jnp.exp(acs)
    state_ref[...] = st_old * jnp.exp(a_last) + st_new

    lane_w = lax.broadcasted_iota(jnp.int32, (1, W), 1)
    dsk = jnp.zeros((1, W), F32)
    for j in range(J):
        dsk = jnp.where((lane_w >> 6) == j, dskip_ref[g * J + j], dsk)
    y = ybuf[...] + y_off + xs * dsk
    y = y * _silu(z_ref[...])
    y = y * lax.rsqrt(jnp.mean(y * y, axis=1, keepdims=True) + RMS_EPS)
    o_ref[...] = (y * ng_ref[...]).astype(o_ref.dtype)


def _ssd(p2, p3, conv_w, conv_b, dt_bias, a_log, d_skip, norm_g):
    s = p2.shape[0]
    L, W, N = SSM_CHUNK, SSM_GROUP_WIDTH, SSM_STATE
    xs0 = P2_XBC // W
    bm0 = (P2_XBC + SSM_D_INNER) // N
    cm0 = bm0 + SSM_GROUPS
    z0 = P2_Z // W
    cb2 = conv_b.reshape(1, SSM_CONV_CH)
    ng2 = norm_g.reshape(1, SSM_D_INNER)
    grid_spec = pltpu.PrefetchScalarGridSpec(
        num_scalar_prefetch=3,
        grid=(SSM_GROUPS, s // L),
        in_specs=[
            pl.BlockSpec((L, W), lambda g, c, *_: (c, xs0 + g)),
            pl.BlockSpec((L, N), lambda g, c, *_: (c, bm0 + g)),
            pl.BlockSpec((L, N), lambda g, c, *_: (c, cm0 + g)),
            pl.BlockSpec((L, W), lambda g, c, *_: (c, z0 + g)),
            pl.BlockSpec((L, LANES), lambda g, c, *_: (c, g)),
            pl.BlockSpec((SSM_CONV, W), lambda g, c, *_: (0, g)),
            pl.BlockSpec((SSM_CONV, N), lambda g, c, *_: (0, SSM_D_INNER // N + g)),
            pl.BlockSpec((SSM_CONV, N), lambda g, c, *_: (0, SSM_D_INNER // N + SSM_GROUPS + g)),
            pl.BlockSpec((1, W), lambda g, c, *_: (0, g)),
            pl.BlockSpec((1, N), lambda g, c, *_: (0, SSM_D_INNER // N + g)),
            pl.BlockSpec((1, N), lambda g, c, *_: (0, SSM_D_INNER // N + SSM_GROUPS + g)),
            pl.BlockSpec((1, W), lambda g, c, *_: (0, g)),
        ],
        out_specs=pl.BlockSpec((L, W), lambda g, c, *_: (c, g)),
        scratch_shapes=[pltpu.VMEM((SUBLANES + L, W), F32), pltpu.VMEM((SUBLANES + L, N), F32),
                        pltpu.VMEM((SUBLANES + L, N), F32), pltpu.VMEM((N, W), F32), pltpu.VMEM((L, W), F32)],
    )
    return pl.pallas_call(
        _ssd_kernel,
        grid_spec=grid_spec,
        out_shape=jax.ShapeDtypeStruct((s, SSM_D_INNER), MXU),
        compiler_params=_cparams(("arbitrary", "arbitrary")),
        name="ssd_mixer",
    )(dt_bias, a_log, d_skip, p2, p2, p2, p2, p3, conv_w, conv_w, conv_w, cb2, cb2, cb2, ng2)


def _nsa_compress_kernel(t_ref, pos_ref, w1_ref, w2_ref, o_ref):
    half = NSA_CMP_STRIDE * HEAD_DIM
    t = t_ref[0].astype(F32)
    pos = pos_ref[0]
    lo = (t + pos[:, :half]).astype(MXU)
    hi = (t + pos[:, half:]).astype(MXU)
    w1 = w1_ref[0]
    a = jnp.dot(lo, w1[:half].astype(MXU), preferred_element_type=F32)
    b = jnp.dot(hi, w1[half:].astype(MXU), preferred_element_type=F32)
    n = t.shape[0]
    pre = a + pltpu.roll(b, n - 1, 0)
    act = jax.nn.gelu(pre, approximate=True)
    o_ref[0] = jnp.dot(act.astype(MXU), w2_ref[0].astype(MXU), preferred_element_type=F32).astype(o_ref.dtype)


def _nsa_compress(p1, pos_k, w1_k, w2_k, pos_v, w1_v, w2_v):
    s = p1.shape[0]
    n_str = s // NSA_CMP_STRIDE
    kv = p1[:, P1_KVN:P1_KVN + 2 * NSA_KV_WIDTH]
    t4 = kv.reshape(s, 4, HEAD_DIM).transpose(1, 0, 2).reshape(4, n_str, NSA_CMP_STRIDE * HEAD_DIM)
    pos = jnp.stack([pos_k, pos_v]).reshape(2, 1, NSA_CMP_LEN * HEAD_DIM)
    w1 = jnp.stack([w1_k, w1_v])
    w2 = jnp.stack([w2_k, w2_v])
    return pl.pallas_call(
        _nsa_compress_kernel,
        grid=(4,),
        in_specs=[pl.BlockSpec((1, n_str, NSA_CMP_STRIDE * HEAD_DIM), lambda i: (i, 0, 0)),
                  pl.BlockSpec((1, 1, NSA_CMP_LEN * HEAD_DIM), lambda i: (i // 2, 0, 0)),
                  pl.BlockSpec((1, NSA_CMP_LEN * HEAD_DIM, HEAD_DIM), lambda i: (i // 2, 0, 0)),
                  pl.BlockSpec((1, HEAD_DIM, HEAD_DIM), lambda i: (i // 2, 0, 0))],
        out_specs=pl.BlockSpec((1, n_str, HEAD_DIM), lambda i: (i, 0, 0)),
        out_shape=jax.ShapeDtypeStruct((4, n_str, HEAD_DIM), MXU),
        compiler_params=_cparams(("arbitrary",)),
        name="nsa_compress",
    )(t4, pos, w1, w2)


def _nsa_cmp_kernel(q_ref, kc_ref, vc_ref, gn_ref, o_ref, selb_ref, *, n_sel):
    tq = ATT_TILE
    qi = pl.program_id(1)
    kc = kc_ref[0].astype(MXU)
    vc = vc_ref[0].astype(MXU)
    nc = kc.shape[0]
    pos = qi * tq + lax.broadcasted_iota(jnp.int32, (tq, nc), 0)
    cidx = lax.broadcasted_iota(jnp.int32, (tq, nc), 1)
    valid = (cidx * NSA_CMP_STRIDE + NSA_CMP_LEN - 1 <= pos) & (cidx < nc - 1)
    c_start = lax.broadcasted_iota(jnp.int32, (LANES, nc), 1) * NSA_CMP_STRIDE
    s_start = lax.broadcasted_iota(jnp.int32, (LANES, nc), 0) * NSA_SEL_BLOCK
    overlap_t = ((c_start < s_start + NSA_SEL_BLOCK) & (c_start + NSA_CMP_LEN > s_start)).astype(MXU)
    gates = _sigmoid(gn_ref[...])
    imp = jnp.zeros((LANES, tq), F32)
    for j in range(NSA_HEADS_PER_GROUP):
        q = q_ref[:, j * HEAD_DIM:(j + 1) * HEAD_DIM].astype(MXU)
        s = lax.dot_general(q, kc, NT, preferred_element_type=F32)
        s = jnp.where(valid, s, -jnp.inf)
        m = jnp.max(s, axis=1, keepdims=True)
        m = jnp.where(m > -jnp.inf, m, 0.0)
        e = jnp.exp2(s - m)
        den = jnp.sum(e, axis=1, keepdims=True)
        p = (e * (1.0 / jnp.where(den > 0, den, 1.0))).astype(MXU)
        o = jnp.dot(p, vc, preferred_element_type=F32)
        imp = imp + lax.dot_general(overlap_t, p, NT, preferred_element_type=F32)
        o_ref[:, j * HEAD_DIM:(j + 1) * HEAD_DIM] = o * gates[:, N_BRANCHES * j:N_BRANCHES * j + 1]

    blk = lax.broadcasted_iota(jnp.int32, (LANES, tq), 0).astype(F32)
    cur = ((qi * tq + lax.broadcasted_iota(jnp.int32, (LANES, tq), 1)) >> 6).astype(F32)
    allowed = blk <= cur
    forced = (blk == 0.0) | (blk == cur) | (blk == cur - 1.0)
    val = jnp.where(forced, jnp.inf, jnp.where(allowed, imp, -jnp.inf))
    val = jnp.where(blk < float(n_sel), val, -jnp.inf)

    def pick_round(_, c):
        val, sel = c
        mx = jnp.max(val, axis=0, keepdims=True)
        idx = jnp.min(jnp.where(val == mx, blk, float(LANES)), axis=0, keepdims=True)
        pick = blk == idx
        sel = jnp.where(pick & allowed, 1.0, sel)
        val = jnp.where(pick, -jnp.inf, val)
        return val, sel

    _, sel = lax.fori_loop(0, min(NSA_TOPN, n_sel), pick_round, (val, jnp.zeros((LANES, tq), F32)))
    selb_ref[0] = jnp.where(sel.T > 0, 0.0, MASK_BIAS).astype(selb_ref.dtype)


def _nsa_cmp(p1, p3, kvc):
    s = p1.shape[0]
    n_str = s // NSA_CMP_STRIDE
    n_sel = s // NSA_SEL_BLOCK
    assert n_sel <= LANES
    gw = NSA_HEADS_PER_GROUP * HEAD_DIM
    q0 = P1_QN // gw
    return pl.pallas_call(
        functools.partial(_nsa_cmp_kernel, n_sel=n_sel),
        grid=(NSA_KV_GROUPS, s // ATT_TILE),
        in_specs=[pl.BlockSpec((ATT_TILE, gw), lambda g, i: (i, q0 + g)),
                  pl.BlockSpec((1, n_str, HEAD_DIM), lambda g, i: (g, 0, 0)),
                  pl.BlockSpec((1, n_str, HEAD_DIM), lambda g, i: (NSA_KV_GROUPS + g, 0, 0)),
                  pl.BlockSpec((ATT_TILE, LANES), lambda g, i: (i, SSM_GROUPS + g))],
        out_specs=[pl.BlockSpec((ATT_TILE, gw), lambda g, i: (i, g)),
                   pl.BlockSpec((1, ATT_TILE, LANES), lambda g, i: (g, i, 0))],
        out_shape=[jax.ShapeDtypeStruct((s, NSA_WIDTH), F32),
                   jax.ShapeDtypeStruct((NSA_KV_GROUPS, s, LANES), MXU)],
        compiler_params=_cparams(("arbitrary", "arbitrary")),
        name="nsa_compressed_attention",
    )(p1, kvc, kvc, p3)


def _nsa_sel_kernel(q_ref, selb_ref, k_ref, v_ref, gn_ref, prev_ref, o_ref, kaug_ref, vaug_ref, qaug_ref,
                    m_ref, acc_ref, sa_ref, sb_ref, *, n_tiles):
    tq = ATT_TILE
    J = NSA_HEADS_PER_GROUP
    qi = pl.program_id(1)

    @pl.when(qi == 0)
    def _build_keys():
        lane = lax.broadcasted_iota(jnp.int32, (tq, LANES), 1)
        rowi = lax.broadcasted_iota(jnp.int32, (tq, LANES), 0)
        ones = jnp.ones((tq, HEAD_DIM), vaug_ref.dtype)

        def body(t, c):
            r0 = pl.multiple_of(t * tq, tq)
            kaug_ref[pl.ds(r0, tq), 0:HEAD_DIM] = k_ref[pl.ds(r0, tq), :].astype(kaug_ref.dtype)
            kaug_ref[pl.ds(r0, tq), HEAD_DIM:2 * HEAD_DIM] = (
                lane == ((t * tq + rowi) >> 6)).astype(kaug_ref.dtype)
            vaug_ref[pl.ds(r0, tq), 0:HEAD_DIM] = v_ref[pl.ds(r0, tq), :].astype(vaug_ref.dtype)
            vaug_ref[pl.ds(r0, tq), HEAD_DIM:2 * HEAD_DIM] = ones
            return c

        lax.fori_loop(0, n_tiles, body, 0)

    selb = selb_ref[0]
    for j in range(J):
        qaug_ref[j * tq:(j + 1) * tq, 0:HEAD_DIM] = q_ref[:, j * HEAD_DIM:(j + 1) * HEAD_DIM].astype(qaug_ref.dtype)
        qaug_ref[j * tq:(j + 1) * tq, HEAD_DIM:2 * HEAD_DIM] = selb
    _flash_init(m_ref, acc_ref)

    def scores(t):
        r0 = pl.multiple_of(t * tq, tq)
        return lax.dot_general(qaug_ref[...], kaug_ref[pl.ds(r0, tq), :], NT, preferred_element_type=F32)

    def values(t):
        return vaug_ref[pl.ds(pl.multiple_of(t * tq, tq), tq), :]

    _flash_loop(qi, scores, values, sa_ref, sb_ref, m_ref, acc_ref)
    row = lax.broadcasted_iota(jnp.int32, (J * tq, tq), 0) & (tq - 1)
    col = lax.broadcasted_iota(jnp.int32, (J * tq, tq), 1)
    _flash_step(jnp.where(col <= row, scores(qi), NEG), values(qi), m_ref, acc_ref)
    o = _flash_out(acc_ref)
    gates = _sigmoid(gn_ref[...])
    for j in range(J):
        o_ref[:, j * HEAD_DIM:(j + 1) * HEAD_DIM] = (
            prev_ref[:, j * HEAD_DIM:(j + 1) * HEAD_DIM]
            + o[j * tq:(j + 1) * tq] * gates[:, N_BRANCHES * j + 1:N_BRANCHES * j + 2])


def _nsa_sel(p1, p3, selb, prev):
    s = p1.shape[0]
    gw = NSA_HEADS_PER_GROUP * HEAD_DIM
    q0 = P1_QN // gw
    k0 = (P1_KVN + 2 * NSA_KV_WIDTH) // HEAD_DIM
    v0 = k0 + NSA_KV_GROUPS
    return pl.pallas_call(
        functools.partial(_nsa_sel_kernel, n_tiles=s // ATT_TILE),
        grid=(NSA_KV_GROUPS, s // ATT_TILE),
        in_specs=[pl.BlockSpec((ATT_TILE, gw), lambda g, i: (i, q0 + g)),
                  pl.BlockSpec((1, ATT_TILE, LANES), lambda g, i: (g, i, 0)),
                  pl.BlockSpec((s, HEAD_DIM), lambda g, i: (0, k0 + g)),
                  pl.BlockSpec((s, HEAD_DIM), lambda g, i: (0, v0 + g)),
                  pl.BlockSpec((ATT_TILE, LANES), lambda g, i: (i, SSM_GROUPS + g)),
                  pl.BlockSpec((ATT_TILE, gw), lambda g, i: (i, g))],
        out_specs=pl.BlockSpec((ATT_TILE, gw), lambda g, i: (i, g)),
        out_shape=jax.ShapeDtypeStruct((s, NSA_WIDTH), F32),
        scratch_shapes=[pltpu.VMEM((s, 2 * HEAD_DIM), MXU), pltpu.VMEM((s, 2 * HEAD_DIM), MXU),
                        pltpu.VMEM((NSA_HEADS_PER_GROUP * ATT_TILE, 2 * HEAD_DIM), MXU),
                        pltpu.VMEM((NSA_HEADS_PER_GROUP * ATT_TILE, HEAD_DIM), F32),
                        pltpu.VMEM((NSA_HEADS_PER_GROUP * ATT_TILE, 2 * HEAD_DIM), F32),
                        pltpu.VMEM((NSA_HEADS_PER_GROUP * ATT_TILE, ATT_TILE), F32),
                        pltpu.VMEM((NSA_HEADS_PER_GROUP * ATT_TILE, ATT_TILE), F32)],
        compiler_params=_cparams(("arbitrary", "arbitrary")),
        name="nsa_selected_attention",
    )(p1, selb, p1, p1, p3, prev)


def _nsa_win_kernel(q_ref, k_ref, v_ref, gn_ref, prev_ref, o_ref, qst_ref, m_ref, acc_ref, sa_ref, sb_ref):
    tq = ATT_TILE
    J = NSA_HEADS_PER_GROUP
    halo = NSA_WINDOW // tq
    qi = pl.program_id(1)
    for j in range(J):
        qst_ref[j * tq:(j + 1) * tq, :] = q_ref[:, j * HEAD_DIM:(j + 1) * HEAD_DIM].astype(qst_ref.dtype)
    row = lax.broadcasted_iota(jnp.int32, (J * tq, tq), 0) & (tq - 1)
    col = lax.broadcasted_iota(jnp.int32, (J * tq, tq), 1)
    ones = jnp.ones((tq, HEAD_DIM), MXU)
    _flash_init(m_ref, acc_ref)
    first = jnp.maximum(qi - halo, 0)

    def scores(t):
        kt = first + t
        r0 = pl.multiple_of(kt * tq, tq)
        s = lax.dot_general(qst_ref[...], k_ref[pl.ds(r0, tq), :].astype(MXU), NT, preferred_element_type=F32)
        diff = (qi - kt) * tq + row - col
        return jnp.where((diff >= 0) & (diff < NSA_WINDOW), s, NEG)

    def values(t):
        r0 = pl.multiple_of((first + t) * tq, tq)
        return jnp.concatenate([v_ref[pl.ds(r0, tq), :].astype(MXU), ones], axis=1)

    _flash_loop(qi + 1 - first, scores, values, sa_ref, sb_ref, m_ref, acc_ref)
    o = _flash_out(acc_ref)
    gates = _sigmoid(gn_ref[...])
    for j in range(J):
        o_ref[:, j * HEAD_DIM:(j + 1) * HEAD_DIM] = (
            prev_ref[:, j * HEAD_DIM:(j + 1) * HEAD_DIM]
            + o[j * tq:(j + 1) * tq] * gates[:, N_BRANCHES * j + 2:N_BRANCHES * j + 3])


def _nsa_win(p1, p3, prev):
    s = p1.shape[0]
    gw = NSA_HEADS_PER_GROUP * HEAD_DIM
    q0 = P1_QN // gw
    k0 = (P1_KVN + 4 * NSA_KV_WIDTH) // HEAD_DIM
    v0 = k0 + NSA_KV_GROUPS
    return pl.pallas_call(
        _nsa_win_kernel,
        grid=(NSA_KV_GROUPS, s // ATT_TILE),
        in_specs=[pl.BlockSpec((ATT_TILE, gw), lambda g, i: (i, q0 + g)),
                  pl.BlockSpec((s, HEAD_DIM), lambda g, i: (0, k0 + g)),
                  pl.BlockSpec((s, HEAD_DIM), lambda g, i: (0, v0 + g)),
                  pl.BlockSpec((ATT_TILE, LANES), lambda g, i: (i, SSM_GROUPS + g)),
                  pl.BlockSpec((ATT_TILE, gw), lambda g, i: (i, g))],
        out_specs=pl.BlockSpec((ATT_TILE, gw), lambda g, i: (i, g)),
        out_shape=jax.ShapeDtypeStruct((s, NSA_WIDTH), F32),
        scratch_shapes=[pltpu.VMEM((NSA_HEADS_PER_GROUP * ATT_TILE, HEAD_DIM), MXU),
                        pltpu.VMEM((NSA_HEADS_PER_GROUP * ATT_TILE, HEAD_DIM), F32),
                        pltpu.VMEM((NSA_HEADS_PER_GROUP * ATT_TILE, 2 * HEAD_DIM), F32),
                        pltpu.VMEM((NSA_HEADS_PER_GROUP * ATT_TILE, ATT_TILE), F32),
                        pltpu.VMEM((NSA_HEADS_PER_GROUP * ATT_TILE, ATT_TILE), F32)],
        compiler_params=_cparams(("arbitrary", "arbitrary")),
        name="nsa_window_attention",
    )(p1, p1, p1, p3, prev)


def _merge_kernel(ya_ref, yb_ref, yc_ref, g0_ref, g1_ref, g2_ref, wa_ref, wb_ref, wc_ref, o_ref):
    def branch(y_ref, w_ref, g_ref):
        prod = jnp.dot(y_ref[...].astype(MXU), w_ref[...].astype(MXU), preferred_element_type=F32)
        return _sigmoid(g_ref[...]) * prod

    o_ref[...] = (branch(ya_ref, wa_ref, g0_ref) + branch(yb_ref, wb_ref, g1_ref)
                  + branch(yc_ref, wc_ref, g2_ref)).astype(o_ref.dtype)


def _merge(ya, yb, yc, p2, wa, wb, wc, *, tm=512, tn=512):
    s = ya.shape[0]
    d = wa.shape[1]
    g0 = P2_GM // tn
    gstep = d // tn
    return pl.pallas_call(
        _merge_kernel,
        grid=(d // tn, s // tm),
        in_specs=[pl.BlockSpec((tm, ya.shape[1]), lambda j, i: (i, 0)),
                  pl.BlockSpec((tm, yb.shape[1]), lambda j, i: (i, 0)),
                  pl.BlockSpec((tm, yc.shape[1]), lambda j, i: (i, 0)),
                  pl.BlockSpec((tm, tn), lambda j, i: (i, g0 + j)),
                  pl.BlockSpec((tm, tn), lambda j, i: (i, g0 + gstep + j)),
                  pl.BlockSpec((tm, tn), lambda j, i: (i, g0 + 2 * gstep + j)),
                  pl.BlockSpec((wa.shape[0], tn), lambda j, i: (0, j)),
                  pl.BlockSpec((wb.shape[0], tn), lambda j, i: (0, j)),
                  pl.BlockSpec((wc.shape[0], tn), lambda j, i: (0, j))],
        out_specs=pl.BlockSpec((tm, tn), lambda j, i: (i, j)),
        out_shape=jax.ShapeDtypeStruct((s, d), MXU),
        compiler_params=_cparams(("arbitrary", "arbitrary")),
        name="branch_merge",
    )(ya, yb, yc, p2, p2, p2, wa, wb, wc)


def _layer_norm_rows(x, g, b):
    xc = x - jnp.mean(x, axis=1, keepdims=True)
    var = jnp.mean(xc * xc, axis=1, keepdims=True)
    return xc * lax.rsqrt(var + LN_EPS) * g + b


def _pack_halves(y):
    half = y.shape[1] // 2
    bits = lax.bitcast_convert_type(y.astype(jnp.bfloat16).astype(F32), jnp.int32)
    return ((bits[:, :half] >> 16) & jnp.int32(0xFFFF)) | (bits[:, half:] & jnp.int32(-65536))


def _unpack_halves_f32(w):
    return lax.bitcast_convert_type(w << 16, F32), lax.bitcast_convert_type(w & jnp.int32(-65536), F32)


def _unpack_halves(w):
    lo, hi = _unpack_halves_f32(w)
    return lo.astype(MXU), hi.astype(MXU)


def _wout_ln_kernel(m_ref, w_ref, h_ref, g_ref, b_ref, o_ref, ob_ref, op_ref):
    acc = jnp.dot(m_ref[...].astype(MXU), w_ref[...].astype(MXU), preferred_element_type=F32)
    y = _layer_norm_rows(DEEPNORM_ALPHA * h_ref[...] + acc, g_ref[...], b_ref[...])
    o_ref[...] = y
    ob_ref[...] = y.astype(ob_ref.dtype)
    op_ref[...] = _pack_halves(y)


def _wout_ln(merged, w_out, h, ln_g, ln_b, *, tm=256):
    s, d = h.shape
    row = pl.BlockSpec((tm, d), lambda i: (i, 0))
    vec = pl.BlockSpec((1, d), lambda i: (0, 0))
    return pl.pallas_call(
        _wout_ln_kernel,
        grid=(s // tm,),
        in_specs=[row, pl.BlockSpec((d, d), lambda i: (0, 0)), row, vec, vec],
        out_specs=[row, row, pl.BlockSpec((tm, d // 2), lambda i: (i, 0))],
        out_shape=[jax.ShapeDtypeStruct((s, d), F32), jax.ShapeDtypeStruct((s, d), MXU),
                   jax.ShapeDtypeStruct((s, d // 2), jnp.int32)],
        compiler_params=_cparams(("arbitrary",)),
        name="out_proj_layernorm",
    )(merged, w_out.astype(MXU), h, ln_g.reshape(1, d), ln_b.reshape(1, d))


def _swiglu_packed(x_packed, wg, wu, wd):
    xa, xb = _unpack_halves(x_packed)
    half = xa.shape[1]

    def up(w):
        return (jnp.dot(xa, w[0:half, :].astype(MXU), preferred_element_type=F32)
                + jnp.dot(xb, w[half:2 * half, :].astype(MXU), preferred_element_type=F32))

    act = (_silu(up(wg)) * up(wu)).astype(MXU)
    return jnp.dot(act, wd[...].astype(MXU), preferred_element_type=F32)


def _dispatch_kernel(ztile_ref, dest_ref, hp_ref, wg_ref, wu_ref, wd_ref, o_ref, sh_ref, zbuf, sem, zsem):
    tt = dest_ref.shape[1]
    tm = zbuf.shape[0]

    @pl.when(pl.program_id(0) == 0)
    def _zero_partial_tiles():
        zbuf[...] = jnp.zeros_like(zbuf)

        def tile_copy(t):
            return pltpu.make_async_copy(zbuf, o_ref.at[pl.ds(t * tm, tm)], zsem)

        def start(j, c):
            @pl.when(ztile_ref[j] >= 0)
            def _():
                tile_copy(ztile_ref[j]).start()
            return c

        def wait(j, c):
            @pl.when(ztile_ref[j] >= 0)
            def _():
                tile_copy(ztile_ref[j]).wait()
            return c

        lax.fori_loop(0, ztile_ref.shape[0], start, 0)
        lax.fori_loop(0, ztile_ref.shape[0], wait, 0)

    def issue(g, c):
        base = pl.multiple_of(g * SUBLANES, SUBLANES)
        for rr in range(SUBLANES):
            for k in range(TOP_K):
                pltpu.make_async_copy(hp_ref.at[pl.ds(base + rr, 1)], o_ref.at[pl.ds(dest_ref[k, base + rr], 1)],
                                      sem).start()
        return c

    lax.fori_loop(0, tt // SUBLANES, issue, 0)
    sh_ref[...] = _swiglu_packed(hp_ref[...], wg_ref, wu_ref, wd_ref)
    rows = o_ref.at[pl.ds(0, TOP_K * tt)]
    pltpu.make_async_copy(rows, rows, sem).wait()


def _dispatch_shared(hp, dest, zero_tiles, n_rows, w_sh_gate, w_sh_up, w_sh_down, layer, *, tm, tt=256):
    s, half = hp.shape
    d, f = w_sh_gate.shape[1:]
    grid_spec = pltpu.PrefetchScalarGridSpec(
        num_scalar_prefetch=1,
        grid=(s // tt,),
        in_specs=[pl.BlockSpec((TOP_K, tt), lambda i, zt: (0, i), memory_space=pltpu.SMEM),
                  pl.BlockSpec((tt, half), lambda i, zt: (i, 0)),
                  pl.BlockSpec((None, d, f), lambda i, zt: (layer, 0, 0)),
                  pl.BlockSpec((None, d, f), lambda i, zt: (layer, 0, 0)),
                  pl.BlockSpec((None, f, d), lambda i, zt: (layer, 0, 0))],
        out_specs=[pl.BlockSpec(memory_space=pl.ANY), pl.BlockSpec((tt, d), lambda i, zt: (i, 0))],
        scratch_shapes=[pltpu.VMEM((tm, half), jnp.int32), pltpu.SemaphoreType.DMA(()),
                        pltpu.SemaphoreType.DMA(())],
    )
    return pl.pallas_call(
        _dispatch_kernel,
        grid_spec=grid_spec,
        out_shape=[jax.ShapeDtypeStruct((n_rows, half), jnp.int32), jax.ShapeDtypeStruct((s, d), F32)],
        compiler_params=_cparams(("arbitrary",), 56),
        name="moe_dispatch_shared",
    )(zero_tiles, dest, hp, w_sh_gate, w_sh_up, w_sh_down)


def _combine_kernel(dcur_ref, dnxt_ref, w_ref, h_ref, sh_ref, g_ref, b_ref, y_ref, o_ref, ob_ref, buf, sem):
    tt = dcur_ref.shape[1]
    i = pl.program_id(0)
    n = pl.num_programs(0)
    slot = i % 2

    def issue(d_ref, sl):
        def body(g, c):
            base = pl.multiple_of(g * SUBLANES, SUBLANES)
            for rr in range(SUBLANES):
                for k in range(TOP_K):
                    pltpu.make_async_copy(y_ref.at[pl.ds(d_ref[k, base + rr], 1)],
                                          buf.at[sl, pl.ds(k * tt + base + rr, 1)], sem.at[sl]).start()
            return c

        lax.fori_loop(0, tt // SUBLANES, body, 0)

    @pl.when(i == 0)
    def _():
        issue(dcur_ref, slot)

    @pl.when(i + 1 < n)
    def _():
        issue(dnxt_ref, 1 - slot)

    pltpu.make_async_copy(y_ref.at[pl.ds(0, TOP_K * tt)], buf.at[slot], sem.at[slot]).wait()
    w = w_ref[...]
    lo = hi = None
    for k in range(TOP_K):
        ya, yb = _unpack_halves_f32(buf[slot, k * tt:(k + 1) * tt])
        ta, tb = w[:, k:k + 1] * ya, w[:, k:k + 1] * yb
        lo, hi = (ta, tb) if lo is None else (lo + ta, hi + tb)
    routed = jnp.concatenate([lo, hi], axis=1)
    y = _layer_norm_rows(DEEPNORM_ALPHA * h_ref[...] + (routed + sh_ref[...]), g_ref[...], b_ref[...])
    o_ref[...] = y
    ob_ref[...] = y.astype(ob_ref.dtype)


def _combine_ln(h, y_rows, dest, top_w, shared, ln_g, ln_b, *, tt=128):
    s, d = h.shape
    n = s // tt
    row = pl.BlockSpec((tt, d), lambda i: (i, 0))
    vec = pl.BlockSpec((1, d), lambda i: (0, 0))
    return pl.pallas_call(
        _combine_kernel,
        grid=(n,),
        in_specs=[pl.BlockSpec((TOP_K, tt), lambda i: (0, i), memory_space=pltpu.SMEM),
                  pl.BlockSpec((TOP_K, tt), lambda i: (0, jnp.minimum(i + 1, n - 1)), memory_space=pltpu.SMEM),
                  pl.BlockSpec((tt, TOP_K), lambda i: (i, 0)),
                  row, row, vec, vec,
                  pl.BlockSpec(memory_space=pl.ANY)],
        out_specs=[row, row],
        out_shape=[jax.ShapeDtypeStruct((s, d), F32), jax.ShapeDtypeStruct((s, d), MXU)],
        scratch_shapes=[pltpu.VMEM((2, TOP_K * tt, d // 2), jnp.int32), pltpu.SemaphoreType.DMA((2,))],
        compiler_params=_cparams(("arbitrary",)),
        name="moe_combine_layernorm",
    )(dest, dest, top_w.T, h, shared, ln_g.reshape(1, d), ln_b.reshape(1, d), y_rows)


def _router_kernel(h_ref, wr_ref, rb_ref, e_ref, w_ref, pos_ref, cnt_ref, carry_ref):
    tq = h_ref.shape[0]
    i = pl.program_id(0)

    @pl.when(i == 0)
    def _reset():
        carry_ref[...] = jnp.zeros_like(carry_ref)

    logits = lax.dot_general(wr_ref[...].astype(MXU), h_ref[...].astype(MXU), NT, preferred_element_type=F32)
    scores = _sigmoid(logits)
    biased = scores + rb_ref[...]
    G, PG = N_EXPERT_GROUPS, EXPERTS_PER_GROUP
    sub = lax.broadcasted_iota(jnp.int32, (PG, tq), 0).astype(F32)
    gi = lax.broadcasted_iota(jnp.int32, (G, tq), 0).astype(F32)
    gs = jnp.zeros((G, tq), F32)
    for g in range(G):
        blk = biased[g * PG:(g + 1) * PG, :]
        m1 = jnp.max(blk, axis=0, keepdims=True)
        i1 = jnp.min(jnp.where(blk == m1, sub, float(PG)), axis=0, keepdims=True)
        m2 = jnp.max(jnp.where(sub == i1, -jnp.inf, blk), axis=0, keepdims=True)
        gs = jnp.where(gi == float(g), m1 + m2, gs)
    keep = jnp.zeros((G, tq), F32)
    for _ in range(TOPK_GROUPS):
        mx = jnp.max(gs, axis=0, keepdims=True)
        idx = jnp.min(jnp.where(gs == mx, gi, float(G)), axis=0, keepdims=True)
        pick = gi == idx
        keep = jnp.where(pick, 1.0, keep)
        gs = jnp.where(pick, -jnp.inf, gs)
    val = jnp.concatenate(
        [jnp.where(keep[g:g + 1, :] > 0, biased[g * PG:(g + 1) * PG, :], -jnp.inf) for g in range(G)], axis=0)
    ei = lax.broadcasted_iota(jnp.int32, (N_EXPERTS, tq), 0).astype(F32)
    picks, svals = [], []
    sel = jnp.zeros((N_EXPERTS, tq), F32)
    for r in range(TOP_K):
        mx = jnp.max(val, axis=0, keepdims=True)
        idx = jnp.min(jnp.where(val == mx, ei, float(N_EXPERTS)), axis=0, keepdims=True)
        pick = ei == idx
        picks.append(pick)
        svals.append(jnp.sum(jnp.where(pick, scores, 0.0), axis=0, keepdims=True))
        e_ref[r:r + 1, :] = idx.astype(jnp.int32)
        sel = jnp.where(pick, 1.0, sel)
        val = jnp.where(pick, -jnp.inf, val)
    wsum = svals[0]
    for r in range(1, TOP_K):
        wsum = wsum + svals[r]
    tr = lax.broadcasted_iota(jnp.int32, (tq, tq), 0)
    tc = lax.broadcasted_iota(jnp.int32, (tq, tq), 1)
    before = (tr < tc).astype(jnp.bfloat16)
    prefix = jnp.dot(sel.astype(jnp.bfloat16), before, preferred_element_type=F32)
    pos = carry_ref[:, 0:1] + prefix
    for r in range(TOP_K):
        w_ref[r:r + 1, :] = svals[r] / wsum * ROUTED_SCALE
        pos_ref[r:r + 1, :] = jnp.sum(jnp.where(picks[r], pos, 0.0), axis=0, keepdims=True).astype(jnp.int32)
    total = carry_ref[...] + jnp.sum(sel, axis=1, keepdims=True)
    carry_ref[...] = total
    cnt_ref[...] = total


def _router(hb, w_router, router_bias, *, tq=256):
    s, d = hb.shape
    row = pl.BlockSpec((TOP_K, tq), lambda i: (0, i))
    return pl.pallas_call(
        _router_kernel,
        grid=(s // tq,),
        in_specs=[pl.BlockSpec((tq, d), lambda i: (i, 0)),
                  pl.BlockSpec((N_EXPERTS, d), lambda i: (0, 0)),
                  pl.BlockSpec((N_EXPERTS, 1), lambda i: (0, 0))],
        out_specs=[row, row, row, pl.BlockSpec((N_EXPERTS, LANES), lambda i: (0, 0))],
        out_shape=[jax.ShapeDtypeStruct((TOP_K, s), jnp.int32), jax.ShapeDtypeStruct((TOP_K, s), F32),
                   jax.ShapeDtypeStruct((TOP_K, s), jnp.int32), jax.ShapeDtypeStruct((N_EXPERTS, LANES), F32)],
        scratch_shapes=[pltpu.VMEM((N_EXPERTS, LANES), F32)],
        compiler_params=_cparams(("arbitrary",)),
        name="moe_router",
    )(hb, w_router.T, router_bias.reshape(N_EXPERTS, 1))


def _expert_kernel(te_ref, nu_ref, first_ref, slot_ref, nxt_ref, x_ref, wg_hbm, wu_hbm, wd_hbm, o_ref,
                   wg_buf, wu_buf, wd_buf, sem, *, layer):
    i = pl.program_id(0)

    def weight_copies(e, sl):
        return (pltpu.make_async_copy(wg_hbm.at[layer, e], wg_buf.at[sl], sem.at[sl]),
                pltpu.make_async_copy(wu_hbm.at[layer, e], wu_buf.at[sl], sem.at[sl]),
                pltpu.make_async_copy(wd_hbm.at[layer, e], wd_buf.at[sl], sem.at[sl]))

    @pl.when(i < nu_ref[0])
    def _compute():
        sl = slot_ref[i]

        @pl.when(i == 0)
        def _first_expert():
            for c in weight_copies(te_ref[i], sl):
                c.start()

        @pl.when(first_ref[i] == 1)
        def _expert_changed():
            for c in weight_copies(te_ref[i], sl):
                c.wait()

            @pl.when(nxt_ref[i] >= 0)
            def _prefetch_next_expert():
                for c in weight_copies(nxt_ref[i], 1 - sl):
                    c.start()

        y = _swiglu_packed(x_ref[...], wg_buf.at[sl], wu_buf.at[sl], wd_buf.at[sl])
        o_ref[...] = _pack_halves(y)

    @pl.when(i >= nu_ref[0])
    def _unused():
        o_ref[...] = jnp.zeros_like(o_ref)


def _experts(x_rows, tile_e, n_used, w_gate, w_up, w_down, layer, *, tm):
    n_rows = x_rows.shape[0]
    n_tiles = n_rows // tm
    d, f = w_gate.shape[2:]
    d_out, out_dtype = d // 2, jnp.int32
    idx = jnp.arange(n_tiles, dtype=jnp.int32)
    first = jnp.concatenate([jnp.ones((1,), jnp.int32), (tile_e[1:] != tile_e[:-1]).astype(jnp.int32)])
    slot = ((jnp.cumsum(first) - 1) % 2).astype(jnp.int32)
    used = idx < n_used[0]
    run_end = jnp.sum((tile_e[None, :] <= tile_e[:, None]) & used[None, :], axis=1).astype(jnp.int32)
    nxt = jnp.where(run_end < n_used[0], tile_e[jnp.minimum(run_end, n_tiles - 1)], -1).astype(jnp.int32)
    grid_spec = pltpu.PrefetchScalarGridSpec(
        num_scalar_prefetch=5,
        grid=(n_tiles,),
        in_specs=[pl.BlockSpec((tm, d // 2), lambda i, te, nu, *_: (jnp.minimum(i, nu[0] - 1), 0)),
                  pl.BlockSpec(memory_space=pl.ANY), pl.BlockSpec(memory_space=pl.ANY),
                  pl.BlockSpec(memory_space=pl.ANY)],
        out_specs=pl.BlockSpec((tm, d_out), lambda i, *_: (i, 0)),
        scratch_shapes=[pltpu.VMEM((2, d, f), w_gate.dtype), pltpu.VMEM((2, d, f), w_up.dtype),
                        pltpu.VMEM((2, f, d), w_down.dtype), pltpu.SemaphoreType.DMA((2,))],
    )
    return pl.pallas_call(
        functools.partial(_expert_kernel, layer=layer),
        grid_spec=grid_spec,
        out_shape=jax.ShapeDtypeStruct((n_rows, d_out), out_dtype),
        compiler_params=_cparams(("arbitrary",), 56),
        name="moe_experts",
    )(tile_e, n_used, first, slot, nxt, x_rows, w_gate, w_up, w_down)


def _moe_ln(h, hb, hp, layer, w_router, router_bias, w_exp_gate, w_exp_up, w_exp_down, w_sh_gate, w_sh_up, w_sh_down,
            ln_g, ln_b):
    s, d = h.shape
    tm = MOE_TILE
    top_e, top_w, top_pos, counts = _router(hb, w_router, router_bias)
    cnt = counts[:, 0].astype(jnp.int32)
    tiles_e = (cnt + tm - 1) // tm
    tile_end = jnp.cumsum(tiles_e)
    row_start = (tile_end - tiles_e) * tm
    experts = jnp.arange(N_EXPERTS, dtype=jnp.int32)
    dest = top_pos + jnp.sum(jnp.where(top_e[..., None] == experts, row_start, 0), axis=-1)
    n_tiles = s * TOP_K // tm + N_EXPERTS
    tile_e = jnp.minimum(jnp.sum(tile_end[None, :] <= jnp.arange(n_tiles, dtype=jnp.int32)[:, None], axis=1),
                         N_EXPERTS - 1).astype(jnp.int32)
    n_used = tile_end[-1:].astype(jnp.int32)
    zero_tiles = jnp.concatenate([jnp.where(tiles_e > 0, tile_end - 1, -1),
                                  jnp.where(n_used + experts < n_tiles, n_used + experts, -1)]).astype(jnp.int32)
    x_rows, shared = _dispatch_shared(hp, dest, zero_tiles, n_tiles * tm, w_sh_gate, w_sh_up, w_sh_down, layer, tm=tm)
    y_rows = _experts(x_rows, tile_e, n_used, w_exp_gate, w_exp_up, w_exp_down, layer, tm=tm)
    return _combine_ln(h, y_rows, dest, top_w, shared, ln_g, ln_b)


def _project(hb, wt_all, layer):
    off = np.concatenate([[0], np.cumsum(IN_SPLIT_SIZES)])
    pa = _proj_wt(hb, wt_all, layer, col0=0, ncols=off[3], tm=1024, tn=1024, out_dtype=MXU, q_cols=MOBA_WIDTH)
    ps = _proj_wt(hb, wt_all, layer, col0=off[3], ncols=off[5] - off[3], tm=1024, tn=1024, out_dtype=F32)
    pn = _proj_wt(hb, wt_all, layer, col0=off[6], ncols=off[8] - off[6], tm=1024, tn=512, out_dtype=MXU,
                  q_cols=NSA_WIDTH)
    pg = _proj_wt(hb, wt_all, layer, col0=off[9], ncols=off[10] - off[9], tm=1024, tn=1024, out_dtype=F32)
    p3 = _matmul(hb, _small_proj_weights(wt_all, layer), tm=1024, tn=P3_COLS, out_dtype=F32)
    return pa, pn, ps, pg, p3


def _mixer(hb, wt_all, layer, conv_w, conv_b, dt_bias, a_log, d_skip, ssm_norm_g,
           cmp_pos_k, cmp_w1_k, cmp_w2_k, cmp_pos_v, cmp_w1_v, cmp_w2_v, w_br_a, w_br_b, w_br_c):
    pa, pn, ps, pg, p3 = _project(hb, wt_all, layer)
    y_a = _moba(pa)
    y_b = _ssd(ps, p3, conv_w, conv_b, dt_bias, a_log, d_skip, ssm_norm_g)
    kvc = _nsa_compress(pn, cmp_pos_k, cmp_w1_k, cmp_w2_k, cmp_pos_v, cmp_w1_v, cmp_w2_v)
    y_c, selb = _nsa_cmp(pn, p3, kvc)
    y_c = _nsa_sel(pn, p3, selb, y_c)
    y_c = _nsa_win(pn, p3, y_c)
    return _merge(y_a, y_b, y_c, pg, w_br_a, w_br_b, w_br_c)


def kernel(x, w_in, conv_w, conv_b, dt_bias, a_log, d_skip, ssm_norm_g, cmp_pos_k, cmp_w1_k, cmp_w2_k, cmp_pos_v, cmp_w1_v, cmp_w2_v, w_br_a, w_br_b, w_br_c, w_out, ln1_g, ln1_b, w_router, router_bias, w_exp_gate, w_exp_up, w_exp_down, w_sh_gate, w_sh_up, w_sh_down, ln2_g, ln2_b):
    bsz, s, d = x.shape
    assert bsz == 1
    h = x.reshape(s, d)
    hb = h.astype(MXU)
    wt_all = jnp.swapaxes(w_in, 1, 2)
    for l in range(w_in.shape[0]):
        merged = _mixer(hb, wt_all, l, conv_w[l], conv_b[l], dt_bias[l], a_log[l], d_skip[l], ssm_norm_g[l],
                        cmp_pos_k[l], cmp_w1_k[l], cmp_w2_k[l], cmp_pos_v[l], cmp_w1_v[l], cmp_w2_v[l],
                        w_br_a[l], w_br_b[l], w_br_c[l])
        h, hb, hp = _wout_ln(merged, w_out[l], h, ln1_g[l], ln1_b[l])
        h, hb = _moe_ln(h, hb, hp, l, w_router[l], router_bias[l], w_exp_gate, w_exp_up, w_exp_down,
                        w_sh_gate, w_sh_up, w_sh_down, ln2_g[l], ln2_b[l])
    return h.reshape(bsz, s, d)
```
